```python
import math
import jax
import jax.numpy as jnp
from jax import lax
import numpy as np

D_MODEL = 1024
BATCH = 8
SEQ = 4096
DEPTH = 4

D_MIX = D_MODEL
EPS = 1e-6
CONV_WIDTH = 4
CHUNK = 64
S5_WIDTH = D_MIX // 4
S5_CH = 16
S5_GROUPS = S5_WIDTH // S5_CH
S5_STATE = 64
GDN_WIDTH = (3 * D_MIX) // 8
GDN_HEAD_DIM = 64
GDN_HEADS = GDN_WIDTH // GDN_HEAD_DIM
GDN_CONV_DIM = 3 * GDN_WIDTH
SSD_WIDTH = D_MIX - S5_WIDTH - GDN_WIDTH
SSD_HEAD_DIM = 64
SSD_HEADS = SSD_WIDTH // SSD_HEAD_DIM
SSD_GROUPS = 2
SSD_STATE = 128
SSD_CONV_DIM = SSD_WIDTH + 2 * SSD_GROUPS * SSD_STATE
PROJ_SIZES = (S5_WIDTH, GDN_CONV_DIM, GDN_WIDTH, GDN_HEADS, GDN_HEADS,
              SSD_WIDTH, SSD_CONV_DIM, SSD_HEADS)
D_IN_PROJ = sum(PROJ_SIZES)
MOE_GROUPS = 4
EXPERTS_PER_GROUP = 8
N_EXPERTS = MOE_GROUPS * EXPERTS_PER_GROUP
TOP_K = 2
D_EXPERT = 256
MOE_BLOCK = 128

kernel_name = "hymba_s5_gdn_ssd_hiermoe_adaln"


def rms_norm(x, w):
    xf = x.astype(jnp.float32)
    xf = xf * lax.rsqrt(jnp.mean(xf * xf, axis=-1, keepdims=True) + EPS)
    return (xf * w.astype(jnp.float32)).astype(x.dtype)


def l2norm(x):
    return x * lax.rsqrt(jnp.sum(x * x, axis=-1, keepdims=True) + EPS)


def causal_conv(x, w, b=None):
    K, S = w.shape[0], x.shape[1]
    xp = jnp.pad(x, ((0, 0), (K - 1, 0), (0, 0)))
    y = sum(xp[:, j:j + S] * w[j] for j in range(K))
    if b is not None:
        y = y + b
    return jax.nn.silu(y)


def _linear_recurrence(e1, e2):
    a1, b1 = e1
    a2, b2 = e2
    return a1 * a2, a2 * b1 + b2


def s5_mixer(u, a_re, a_im, b_re, b_im, c_re, c_im, d_skip, log_dt, w_glu, norm_w):
    Bsz, S, _ = u.shape
    f32 = jnp.float32
    uf = u.astype(f32).reshape(Bsz, S, S5_GROUPS, S5_CH)
    lam = lax.complex(a_re.astype(f32), a_im.astype(f32))
    step = jnp.exp(log_dt.astype(f32))[:, None]
    lam_bar = jnp.exp(lam * step)
    b_bar = ((lam_bar - 1.0) / lam)[..., None] * lax.complex(b_re.astype(f32), b_im.astype(f32))
    bu = jnp.einsum('gpc,bsgc->bsgp', b_bar, uf.astype(jnp.complex64))
    lam_seq = jnp.broadcast_to(lam_bar, bu.shape)
    _, states = lax.associative_scan(_linear_recurrence, (lam_seq, bu), axis=1)
    c_mat = lax.complex(c_re.astype(f32), c_im.astype(f32))
    y = jnp.einsum('gcp,bsgp->bsgc', c_mat, states).real + d_skip.astype(f32).reshape(S5_GROUPS, S5_CH) * uf
    y = jax.nn.gelu(y.reshape(Bsz, S, S5_WIDTH))
    y = y * jax.nn.sigmoid(y @ w_glu.astype(f32))
    return rms_norm(y, norm_w).astype(u.dtype)


def gdn_mixer(qkv, z, a, b, conv_w, a_log, dt_bias, norm_w):
    Bsz, S, _ = qkv.shape
    H, Dh, C = GDN_HEADS, GDN_HEAD_DIM, CHUNK
    NC = S // C
    f32 = jnp.float32
    qkv_c = causal_conv(qkv, conv_w).astype(f32)

    def chunks(t):
        return t.reshape(Bsz, NC, C, H, Dh).transpose(0, 3, 1, 2, 4)

    q, k, v = (chunks(t) for t in jnp.split(qkv_c, 3, axis=-1))
    q = l2norm(q) * Dh ** -0.5
    k = l2norm(k)
    beta = jax.nn.sigmoid(b.astype(f32)).reshape(Bsz, NC, C, H).transpose(0, 3, 1, 2)
    g = -jnp.exp(a_log.astype(f32)) * jax.nn.softplus(a.astype(f32) + dt_bias.astype(f32))
    gc = jnp.cumsum(g.reshape(Bsz, NC, C, H).transpose(0, 3, 1, 2), axis=-1)
    causal = jnp.tril(jnp.ones((C, C), bool))
    strict = jnp.tril(jnp.ones((C, C), bool), -1)
    decay = jnp.exp(jnp.where(causal, gc[..., :, None] - gc[..., None, :], -jnp.inf))
    k_beta = k * beta[..., None]
    lower = jnp.where(strict, jnp.einsum('bhncd,bhnmd->bhncm', k_beta, k) * decay, 0.0)
    rhs = jnp.concatenate([v * beta[..., None], k_beta * jnp.exp(gc)[..., None]], axis=-1)
    sol = lax.linalg.triangular_solve(lower, rhs, left_side=True, lower=True, unit_diagonal=True)
    u_val, w_dec = sol[..., :Dh], sol[..., Dh:]
    attn = jnp.einsum('bhncd,bhnmd->bhncm', q, k) * decay
    q_dec = q * jnp.exp(gc)[..., None]
    k_dec = k * jnp.exp(gc[..., -1:] - gc)[..., None]
    g_tot = jnp.exp(gc[..., -1])

    def step(state, inp):
        attn_c, q_c, w_c, u_c, k_c, gt_c = inp
        v_new = u_c - jnp.einsum('bhcd,bhde->bhce', w_c, state)
        o = jnp.einsum('bhcd,bhde->bhce', q_c, state) + jnp.einsum('bhcm,bhme->bhce', attn_c, v_new)
        state = state * gt_c[..., None, None] + jnp.einsum('bhcd,bhce->bhde', k_c, v_new)
        return state, o

    xs = tuple(jnp.moveaxis(t, 2, 0) for t in (attn, q_dec, w_dec, u_val, k_dec, g_tot))
    _, o = lax.scan(step, jnp.zeros((Bsz, H, Dh, Dh), f32), xs)
    o = o.transpose(1, 0, 3, 2, 4).reshape(Bsz, S, H, Dh)
    o = rms_norm(o, norm_w) * jax.nn.silu(z.astype(f32).reshape(Bsz, S, H, Dh))
    return o.reshape(Bsz, S, GDN_WIDTH).astype(qkv.dtype)


def ssd_mixer(z, xbc, dt_raw, conv_w, conv_b, a_log, dt_bias, d_skip, norm_w):
    Bsz, S, _ = z.shape
    H, P, G, N, C = SSD_HEADS, SSD_HEAD_DIM, SSD_GROUPS, SSD_STATE, CHUNK
    R, NC = H // G, S // C
    f32 = jnp.float32
    xbc_c = causal_conv(xbc, conv_w, conv_b).astype(f32)
    xs, bm, cm = jnp.split(xbc_c, [SSD_WIDTH, SSD_WIDTH + G * N], axis=-1)
    xs = xs.reshape(Bsz, NC, C, G, R, P)
    bm = bm.reshape(Bsz, NC, C, G, N)
    cm = cm.reshape(Bsz, NC, C, G, N)
    dt = jax.nn.softplus(dt_raw.astype(f32) + dt_bias.astype(f32)).reshape(Bsz, NC, C, G, R)
    a = -jnp.exp(a_log.astype(f32)).reshape(G, R)
    x_dt = xs * dt[..., None]
    acs = jnp.cumsum((dt * a).transpose(0, 3, 4, 1, 2), axis=-1)
    causal = jnp.tril(jnp.ones((C, C), bool))
    seg = jnp.exp(jnp.where(causal, acs[..., :, None] - acs[..., None, :], -jnp.inf))
    cb = jnp.einsum('bnlgd,bnmgd->bgnlm', cm, bm)
    y_diag = jnp.einsum('bgnlm,bgrnlm,bnmgrp->bnlgrp', cb, seg, x_dt)
    decay_in = jnp.exp(acs[..., -1:] - acs)
    chunk_states = jnp.einsum('bnmgd,bgrnm,bnmgrp->bngrpd', bm, decay_in, x_dt)
    chunk_decay = jnp.exp(acs[..., -1])
    decay_out = jnp.exp(acs)

    def step(state, inp):
        c_n, dout_n, cs_n, cd_n = inp
        y_off = jnp.einsum('blgd,bgrpd,bgrl->blgrp', c_n, state, dout_n)
        state = state * cd_n[..., None, None] + cs_n
        return state, y_off

    scan_in = (jnp.moveaxis(cm, 1, 0), jnp.moveaxis(decay_out, 3, 0),
               jnp.moveaxis(chunk_states, 1, 0), jnp.moveaxis(chunk_decay, 3, 0))
    _, y_off = lax.scan(step, jnp.zeros((Bsz, G, R, P, N), f32), scan_in)
    y = y_diag + jnp.moveaxis(y_off, 0, 1) + d_skip.astype(f32).reshape(G, R, 1) * xs
    y = y.reshape(Bsz, S, SSD_WIDTH) * jax.nn.silu(z.astype(f32))
    y = rms_norm(y.reshape(Bsz, S, G, SSD_WIDTH // G), norm_w.reshape(G, SSD_WIDTH // G))
    return y.reshape(Bsz, S, SSD_WIDTH).astype(z.dtype)


def hier_moe(h, w_grp, b_grp, w_rt, b_rt, w_gate, w_up, w_down):
    Bsz, S, D = h.shape
    N = Bsz * S
    NK = N * TOP_K
    f32 = jnp.float32
    xt = h.reshape(N, D)
    lg = (xt @ w_grp).astype(f32) + b_grp.astype(f32)
    g_idx = jnp.argmax(lg, axis=-1)
    g_w = jnp.take_along_axis(jax.nn.softmax(lg, axis=-1), g_idx[:, None], axis=1)[:, 0]
    le = ((xt @ w_rt).astype(f32) + b_rt.astype(f32)).reshape(N, MOE_GROUPS, EXPERTS_PER_GROUP)
    le_sel = jnp.take_along_axis(le, g_idx[:, None, None], axis=1)[:, 0]
    top_v, top_i = lax.top_k(le_sel, TOP_K)
    w_e = jax.nn.softmax(top_v, axis=-1)
    eid = (g_idx[:, None] * EXPERTS_PER_GROUP + top_i).reshape(NK).astype(jnp.int32)
    wt = (g_w[:, None] * w_e).reshape(NK)
    tok = jnp.repeat(jnp.arange(N, dtype=jnp.int32), TOP_K)
    order = jnp.argsort(eid)
    eid_s, tok_s, wt_s = eid[order], tok[order], wt[order]
    counts = jnp.bincount(eid, length=N_EXPERTS)
    padded = ((counts + MOE_BLOCK - 1) // MOE_BLOCK) * MOE_BLOCK
    start = jnp.cumsum(counts) - counts
    pend = jnp.cumsum(padded)
    pstart = pend - padded
    dest = pstart[eid_s] + jnp.arange(NK, dtype=jnp.int32) - start[eid_s]
    L_pad = NK + N_EXPERTS * MOE_BLOCK
    n_blk = L_pad // MOE_BLOCK
    buf_tok = jnp.full((L_pad,), N, jnp.int32).at[dest].set(tok_s)
    buf_wt = jnp.zeros((L_pad,), f32).at[dest].set(wt_s)
    blk_e = jnp.minimum(jnp.searchsorted(pend, jnp.arange(n_blk) * MOE_BLOCK, side='right'), N_EXPERTS - 1)
    x_pad = jnp.concatenate([xt, jnp.zeros((1, D), xt.dtype)], axis=0)
    xb = x_pad[buf_tok].reshape(n_blk, MOE_BLOCK, D)

    def expert_block(args):
        xblk, e = args
        hid = jax.nn.silu(xblk @ w_gate[e]) * (xblk @ w_up[e])
        return hid @ w_down[e]

    yb = lax.map(expert_block, (xb, blk_e)).reshape(L_pad, D)
    y = yb * buf_wt[:, None].astype(yb.dtype)
    out = jax.ops.segment_sum(y, buf_tok, num_segments=N + 1)[:N]
    return out.reshape(Bsz, S, D)


def setup_inputs(seed: int = 0) -> dict:
    key = jax.random.key(seed)
    k = jax.random.split(key, 36)
    f32 = jnp.float32
    L, D = DEPTH, D_MODEL

    def nrm(i, shape, scale):
        return scale * jax.random.normal(k[i], shape, f32)

    def unif(i, shape, lo, hi):
        return jax.random.uniform(k[i], shape, f32, lo, hi)

    def dt_bias(i, shape):
        dt = jnp.exp(unif(i, shape, math.log(1e-3), math.log(1e-1)))
        return dt + jnp.log(-jnp.expm1(-dt))

    s5_n = jnp.arange(S5_STATE, dtype=f32)
    return {
        "x": nrm(0, (BATCH, SEQ, D), 1.0),
        "c": nrm(1, (BATCH, D), 1.0),
        "w_ada": nrm(2, (L, D, 6 * D), 0.5 * D ** -0.5),
        "b_ada": nrm(3, (L, 6 * D), 0.02),
        "norm_mix": 1.0 + nrm(4, (L, D), 0.02),
        "norm_ffn": 1.0 + nrm(5, (L, D), 0.02),
        "w_in": nrm(6, (L, D, D_IN_PROJ), D ** -0.5),
        "w_out": nrm(7, (L, D_MIX, D), D_MIX ** -0.5),
        "s5_a_re": -0.5 + nrm(8, (L, S5_GROUPS, S5_STATE), 0.01),
        "s5_a_im": math.pi * s5_n + nrm(9, (L, S5_GROUPS, S5_STATE), 0.01),
        "s5_b_re": nrm(10, (L, S5_GROUPS, S5_STATE, S5_CH), (2 * S5_CH) ** -0.5),
        "s5_b_im": nrm(11, (L, S5_GROUPS, S5_STATE, S5_CH), (2 * S5_CH) ** -0.5),
        "s5_c_re": nrm(12, (L, S5_GROUPS, S5_CH, S5_STATE), (2 * S5_STATE) ** -0.5),
        "s5_c_im": nrm(13, (L, S5_GROUPS, S5_CH, S5_STATE), (2 * S5_STATE) ** -0.5),
        "s5_d": nrm(14, (L, S5_WIDTH), 1.0),
        "s5_log_dt": unif(15, (L, S5_GROUPS), math.log(1e-3), math.log(1e-1)),
        "s5_w_glu": nrm(16, (L, S5_WIDTH, S5_WIDTH), S5_WIDTH ** -0.5),
        "s5_norm": 1.0 + nrm(17, (L, S5_WIDTH), 0.02),
        "gdn_conv_w": nrm(18, (L, CONV_WIDTH, GDN_CONV_DIM), CONV_WIDTH ** -0.5),
        "gdn_a_log": jnp.log(unif(19, (L, GDN_HEADS), 1.0, 16.0)),
        "gdn_dt_bias": dt_bias(20, (L, GDN_HEADS)),
        "gdn_norm": 1.0 + nrm(21, (L, GDN_HEAD_DIM), 0.02),
        "ssd_conv_w": nrm(22, (L, CONV_WIDTH, SSD_CONV_DIM), CONV_WIDTH ** -0.5),
        "ssd_conv_b": nrm(23, (L, SSD_CONV_DIM), 0.02),
        "ssd_a_log": jnp.log(unif(24, (L, SSD_HEADS), 1.0, 16.0)),
        "ssd_dt_bias": dt_bias(25, (L, SSD_HEADS)),
        "ssd_d": 1.0 + nrm(26, (L, SSD_HEADS), 0.02),
        "ssd_norm": 1.0 + nrm(27, (L, SSD_WIDTH), 0.02),
        "moe_w_grp": nrm(28, (L, D, MOE_GROUPS), D ** -0.5),
        "moe_b_grp": nrm(29, (L, MOE_GROUPS), 0.01),
        "moe_w_rt": nrm(30, (L, D, N_EXPERTS), D ** -0.5),
        "moe_b_rt": nrm(31, (L, N_EXPERTS), 0.01),
        "moe_w_gate": nrm(32, (L, N_EXPERTS, D, D_EXPERT), D ** -0.5),
        "moe_w_up": nrm(33, (L, N_EXPERTS, D, D_EXPERT), D ** -0.5),
        "moe_w_down": nrm(34, (L, N_EXPERTS, D_EXPERT, D), D_EXPERT ** -0.5),
        "norm_final": 1.0 + nrm(35, (D,), 0.02),
    }


def reference(x, c, w_ada, b_ada, norm_mix, norm_ffn, w_in, w_out,
              s5_a_re, s5_a_im, s5_b_re, s5_b_im, s5_c_re, s5_c_im, s5_d, s5_log_dt, s5_w_glu, s5_norm,
              gdn_conv_w, gdn_a_log, gdn_dt_bias, gdn_norm,
              ssd_conv_w, ssd_conv_b, ssd_a_log, ssd_dt_bias, ssd_d, ssd_norm,
              moe_w_grp, moe_b_grp, moe_w_rt, moe_b_rt, moe_w_gate, moe_w_up, moe_w_down,
              norm_final):
    Bsz = x.shape[0]
    splits = np.cumsum(PROJ_SIZES)[:-1].tolist()
    cond = jax.nn.silu(c)
    for l in range(DEPTH):
        mod = (cond @ w_ada[l] + b_ada[l]).reshape(Bsz, 6, 1, D_MODEL)
        h = rms_norm(x, norm_mix[l]) * (1.0 + mod[:, 1]) + mod[:, 0]
        proj = h @ w_in[l]
        s5_u, g_qkv, g_z, g_a, g_b, s_z, s_xbc, s_dt = jnp.split(proj, splits, axis=-1)
        y_s5 = s5_mixer(s5_u, s5_a_re[l], s5_a_im[l], s5_b_re[l], s5_b_im[l], s5_c_re[l], s5_c_im[l],
                        s5_d[l], s5_log_dt[l], s5_w_glu[l], s5_norm[l])
        y_gdn = gdn_mixer(g_qkv, g_z, g_a, g_b, gdn_conv_w[l], gdn_a_log[l], gdn_dt_bias[l], gdn_norm[l])
        y_ssd = ssd_mixer(s_z, s_xbc, s_dt, ssd_conv_w[l], ssd_conv_b[l], ssd_a_log[l], ssd_dt_bias[l],
                          ssd_d[l], ssd_norm[l])
        y = jnp.concatenate([y_s5, y_gdn, y_ssd], axis=-1) @ w_out[l]
        x = x + mod[:, 2] * y
        h = rms_norm(x, norm_ffn[l]) * (1.0 + mod[:, 4]) + mod[:, 3]
        x = x + mod[:, 5] * hier_moe(h, moe_w_grp[l], moe_b_grp[l], moe_w_rt[l], moe_b_rt[l],
                                     moe_w_gate[l], moe_w_up[l], moe_w_down[l])
    return rms_norm(x, norm_final)
```

```python
import functools

import numpy as np
import jax
import jax.numpy as jnp
from jax import lax
from jax.experimental import pallas as pl
from jax.experimental.pallas import tpu as pltpu

F32 = jnp.float32
BF16 = jnp.bfloat16

D_MODEL = 1024
DEPTH = 4
EPS = 1e-6
CONV_WIDTH = 4
CHUNK = 64
S5_WIDTH = 256
S5_CH = 16
S5_GROUPS = 16
S5_STATE = 64
S5_LANES = S5_GROUPS * S5_STATE
GDN_WIDTH = 384
GDN_HEAD_DIM = 64
GDN_HEADS = 6
GDN_CONV_DIM = 3 * GDN_WIDTH
SSD_WIDTH = 384
SSD_HEAD_DIM = 64
SSD_HEADS = 6
SSD_GROUPS = 2
SSD_STATE = 128
SSD_GROUP_WIDTH = SSD_WIDTH // SSD_GROUPS
SSD_CONV_DIM = SSD_WIDTH + 2 * SSD_GROUPS * SSD_STATE
PROJ_SIZES = (S5_WIDTH, GDN_CONV_DIM, GDN_WIDTH, GDN_HEADS, GDN_HEADS, SSD_WIDTH, SSD_CONV_DIM, SSD_HEADS)
MOE_GROUPS = 4
EXPERTS_PER_GROUP = 8
N_EXPERTS = 32
TOP_K = 2
D_EXPERT = 256

LANE = 128
SUBLANE = 8
SMALL_W = LANE
A_LANE, B_LANE, DT_LANE = 0, 8, 16
GRP_LANE = N_EXPERTS
MOE_ROWS = 256
VMEM_LIMIT = 56 * 1024 * 1024
NEG_BIG = -1e30


def _cparams(sem):
    return pltpu.CompilerParams(dimension_semantics=sem, vmem_limit_bytes=VMEM_LIMIT)


def _split(a):
    hi = a.astype(BF16)
    lo = (a - hi.astype(F32)).astype(BF16)
    return hi, lo


_NN = (((1,), (0,)), ((), ()))
_NT = (((1,), (1,)), ((), ()))
_TN = (((0,), (0,)), ((), ()))


def _dot(a, b, dims=_NN):
    return lax.dot_general(a.astype(BF16), b.astype(BF16), dims, preferred_element_type=F32)


def _dot2(a, b_bf16):
    hi, lo = _split(a)
    return (lax.dot_general(hi, b_bf16, _NN, preferred_element_type=F32)
            + lax.dot_general(lo, b_bf16, _NN, preferred_element_type=F32))


def _dot3(a, b):
    ah, al = _split(a)
    bh, bl = _split(b)
    d = functools.partial(lax.dot_general, dimension_numbers=_NN, preferred_element_type=F32)
    return d(ah, bh) + (d(ah, bl) + d(al, bh))


def _silu(x):
    return x * jax.nn.sigmoid(x)


def _softplus(x):
    return jnp.maximum(x, 0.0) + jnp.log(1.0 + jnp.exp(-jnp.abs(x)))


def _norm_mod(x, w, scale, shift):
    ms = jnp.mean(x * x, axis=-1, keepdims=True)
    return (x * lax.rsqrt(ms + EPS) * w) * (1.0 + scale) + shift


def _mod_kernel(c_ref, w_ref, b_ref, o_ref):
    cond = _silu(c_ref[...])
    o_ref[0] = _dot3(cond, w_ref[0]) + b_ref[0]


def _mod_call(c, w_ada, b_ada):
    L, D, W = w_ada.shape
    B = c.shape[0]
    tn = 1536
    return pl.pallas_call(
        _mod_kernel,
        out_shape=jax.ShapeDtypeStruct((L, B, W), F32),
        grid=(L, W // tn),
        in_specs=[pl.BlockSpec((B, D), lambda l, j: (0, 0)),
                  pl.BlockSpec((1, D, tn), lambda l, j: (l, 0, j)),
                  pl.BlockSpec((1, 1, tn), lambda l, j: (l, 0, j))],
        out_specs=pl.BlockSpec((1, B, tn), lambda l, j: (l, 0, j)),
        compiler_params=_cparams(("arbitrary", "arbitrary")),
        name="adaln_mod",
    )(c, w_ada, b_ada.reshape(L, 1, W))


PROJ_OUT_W = (S5_WIDTH, GDN_CONV_DIM, GDN_WIDTH, SSD_WIDTH, SSD_CONV_DIM, SMALL_W)


def _proj_kernel(x_ref, mod_ref, nw_ref, w_ref, *o_refs):
    x = x_ref[0]
    h = _norm_mod(x, nw_ref[...], mod_ref[0, 1:2, :], mod_ref[0, 0:1, :]).astype(BF16)
    off = 0
    for o_ref in o_refs:
        n = o_ref.shape[-1]
        o_ref[0] = jnp.dot(h, w_ref[:, off:off + n], preferred_element_type=F32)
        off += n


def _arrange_w_in(w_in_l):
    splits = np.cumsum(PROJ_SIZES)[:-1].tolist()
    s5_u, g_qkv, g_z, g_a, g_b, s_z, s_xbc, s_dt = jnp.split(w_in_l, splits, axis=-1)
    D = w_in_l.shape[0]
    small = jnp.zeros((D, SMALL_W), w_in_l.dtype)
    small = small.at[:, A_LANE:A_LANE + GDN_HEADS].set(g_a)
    small = small.at[:, B_LANE:B_LANE + GDN_HEADS].set(g_b)
    small = small.at[:, DT_LANE:DT_LANE + SSD_HEADS].set(s_dt)
    return jnp.concatenate([s5_u, g_qkv, g_z, s_z, s_xbc, small], axis=-1).astype(BF16)


def _proj_call(x, mod_l, norm_w, w_arr):
    B, S, D = x.shape
    tm = min(512, S)
    W = w_arr.shape[1]
    out_shape = [jax.ShapeDtypeStruct((B, S, n), F32) for n in PROJ_OUT_W]
    out_specs = [pl.BlockSpec((1, tm, n), lambda b, i: (b, i, 0)) for n in PROJ_OUT_W]
    return pl.pallas_call(
        _proj_kernel,
        out_shape=out_shape,
        grid=(B, S // tm),
        in_specs=[pl.BlockSpec((1, tm, D), lambda b, i: (b, i, 0)),
                  pl.BlockSpec((1, 6, D), lambda b, i: (b, 0, 0)),
                  pl.BlockSpec((1, D), lambda b, i: (0, 0)),
                  pl.BlockSpec((D, W), lambda b, i: (0, 0))],
        out_specs=out_specs,
        compiler_params=_cparams(("arbitrary", "arbitrary")),
        name="norm_in_proj",
    )(x, mod_l, norm_w.reshape(1, D), w_arr)


def _s5_kernel(u_ref, bm_ref, cm_ref, lam_ref, dsk_ref, glu_ref, nw_ref, o_ref, x_s, st_s, *, T):
    B = u_ref.shape[0]
    P = S5_LANES

    @pl.when(pl.program_id(0) == 0)
    def _():
        st_s[...] = jnp.zeros_like(st_s)

    nj = P // LANE
    u = u_ref[...].reshape(B * T, S5_WIDTH)
    ub = u.astype(BF16)
    for j in range(2 * nj):
        x_s[j] = jnp.dot(ub, bm_ref[:, j * LANE:(j + 1) * LANE], preferred_element_type=F32)
    lr = jnp.broadcast_to(lam_ref[0:1, :], (B, P))
    li = jnp.broadcast_to(lam_ref[1:2, :], (B, P))

    def step(t, carry):
        sr, si = carry
        rows = pl.ds(t, B, stride=T)
        bur = jnp.concatenate([x_s[j, rows, :] for j in range(nj)], axis=1)
        bui = jnp.concatenate([x_s[nj + j, rows, :] for j in range(nj)], axis=1)
        nr = lr * sr - li * si + bur
        ni = lr * si + li * sr + bui
        for j in range(nj):
            x_s[j, rows, :] = nr[:, j * LANE:(j + 1) * LANE]
            x_s[nj + j, rows, :] = ni[:, j * LANE:(j + 1) * LANE]
        return nr, ni

    sr, si = lax.fori_loop(0, T, step, (st_s[:, 0:P], st_s[:, P:2 * P]))
    st_s[:, 0:P] = sr
    st_s[:, P:2 * P] = si

    xall = jnp.concatenate([x_s[j] for j in range(2 * nj)], axis=1).astype(BF16)
    y = jnp.dot(xall, cm_ref[...], preferred_element_type=F32) + dsk_ref[...] * u
    y = jax.nn.gelu(y)
    y = y * jax.nn.sigmoid(jnp.dot(y.astype(BF16), glu_ref[...], preferred_element_type=F32))
    ms = jnp.mean(y * y, axis=-1, keepdims=True)
    y = y * lax.rsqrt(ms + EPS) * nw_ref[...]
    o_ref[...] = y.reshape(B, T, S5_WIDTH).astype(o_ref.dtype)


def _s5_params(a_re, a_im, b_re, b_im, c_re, c_im, log_dt):
    G, P, CH = S5_GROUPS, S5_STATE, S5_CH
    lam = lax.complex(a_re.astype(F32), a_im.astype(F32))
    step = jnp.exp(log_dt.astype(F32))[:, None]
    lam_bar = jnp.exp(lam * step)
    b_bar = ((lam_bar - 1.0) / lam)[..., None] * lax.complex(b_re.astype(F32), b_im.astype(F32))
    eye = jnp.eye(G, dtype=F32)
    bre = jnp.einsum('gpc,gh->gchp', b_bar.real, eye).reshape(G * CH, G * P)
    bim = jnp.einsum('gpc,gh->gchp', b_bar.imag, eye).reshape(G * CH, G * P)
    bm = jnp.concatenate([bre, bim], axis=1).astype(BF16)
    cre = jnp.einsum('gcp,gh->gphc', c_re.astype(F32), eye).reshape(G * P, G * CH)
    cim = jnp.einsum('gcp,gh->gphc', c_im.astype(F32), eye).reshape(G * P, G * CH)
    cm = jnp.concatenate([cre, -cim], axis=0).astype(BF16)
    lam_rows = jnp.zeros((SUBLANE, G * P), F32)
    lam_rows = lam_rows.at[0].set(lam_bar.real.reshape(-1)).at[1].set(lam_bar.imag.reshape(-1))
    return bm, cm, lam_rows


def _s5_call(u, bm, cm, lam_rows, d_skip, w_glu, norm_w):
    B, S, W = u.shape
    T = min(128, S)
    P2 = 2 * S5_LANES
    const = lambda shape: pl.BlockSpec(shape, lambda i: tuple(0 for _ in shape))
    return pl.pallas_call(
        functools.partial(_s5_kernel, T=T),
        out_shape=jax.ShapeDtypeStruct((B, S, W), BF16),
        grid=(S // T,),
        in_specs=[pl.BlockSpec((B, T, W), lambda i: (0, i, 0)),
                  const((W, P2)), const((P2, W)), const((SUBLANE, S5_LANES)),
                  const((1, W)), const((W, W)), const((1, W))],
        out_specs=pl.BlockSpec((B, T, W), lambda i: (0, i, 0)),
        scratch_shapes=[pltpu.VMEM((P2 // LANE, B * T, LANE), F32), pltpu.VMEM((B, P2), F32)],
        compiler_params=_cparams(("arbitrary",)),
        name="s5_mixer",
    )(u, bm, cm, lam_rows, d_skip.reshape(1, W).astype(F32), w_glu.astype(BF16), norm_w.reshape(1, W).astype(F32))


def _causal_conv_silu(x, tail_ref, cw_ref, bias):
    n = x.shape[0]
    xf = jnp.concatenate([tail_ref[...], x], axis=0)
    acc = x * cw_ref[CONV_WIDTH - 1:CONV_WIDTH, :]
    for k in range(1, CONV_WIDTH):
        acc = acc + pltpu.roll(xf, k, axis=0)[SUBLANE:, :] * cw_ref[CONV_WIDTH - 1 - k:CONV_WIDTH - k, :]
    tail_ref[...] = x[n - SUBLANE:, :]
    if bias is not None:
        acc = acc + bias
    return _silu(acc)


def _lanes_from(sm, off):
    return pltpu.roll(sm, SMALL_W - off, axis=1) if off else sm


def _chunk_scans(g):
    n = g.shape[0]
    rin = lax.broadcasted_iota(jnp.int32, g.shape, 0) & (CHUNK - 1)
    pre = g
    suf = jnp.where(rin < CHUNK - 1, pltpu.roll(g, n - 1, axis=0), 0.0)
    s = 1
    while s < CHUNK:
        pre = pre + jnp.where(rin >= s, pltpu.roll(pre, s, axis=0), 0.0)
        suf = suf + jnp.where(rin + s <= CHUNK - 1, pltpu.roll(suf, n - s, axis=0), 0.0)
        s *= 2
    return pre, suf


def _tri_masks():
    r = lax.broadcasted_iota(jnp.int32, (CHUNK, CHUNK), 0)
    c = lax.broadcasted_iota(jnp.int32, (CHUNK, CHUNK), 1)
    return r >= c, r > c, r == c


def _head_expand():
    m = np.zeros((SMALL_W, GDN_WIDTH), np.float32)
    for h in range(GDN_HEADS):
        m[h, h * GDN_HEAD_DIM:(h + 1) * GDN_HEAD_DIM] = 1.0
    return jnp.asarray(m, BF16)


def _block_ones(width, blk):
    idx = np.arange(width) // blk
    return jnp.asarray((idx[:, None] == idx[None, :]).astype(np.float32), BF16)


def _gdn_kernel(qkv_ref, z_ref, sm_ref, cw_ref, hp_ref, nw_ref, e_ref, xp_ref, o_ref,
                tail_s, st_s, kn_s, kb_s, qn_s, qd_s, kd_s, rhs_s, eg_s, gc_s, gct_s, o_s, *, TB):
    H, Dh, C = GDN_HEADS, GDN_HEAD_DIM, CHUNK
    W = GDN_WIDTH
    ncb = TB // C

    @pl.when(pl.program_id(1) == 0)
    def _():
        tail_s[...] = jnp.zeros_like(tail_s)
        st_s[...] = jnp.zeros_like(st_s)

    xc = _causal_conv_silu(qkv_ref[0], tail_s, cw_ref, None)
    q, k, v = xc[:, 0:W], xc[:, W:2 * W], xc[:, 2 * W:3 * W]
    e = e_ref[...]
    xp = xp_ref[...]
    qn = q * lax.rsqrt(_dot2(q * q, e) + EPS) * (Dh ** -0.5)
    kn = k * lax.rsqrt(_dot2(k * k, e) + EPS)

    sm = sm_ref[0]
    lane = lax.broadcasted_iota(jnp.int32, sm.shape, 1)
    head_lane = lane < H
    g = jnp.where(head_lane, hp_ref[0:1, :] * _softplus(_lanes_from(sm, A_LANE) + hp_ref[1:2, :]), 0.0)
    beta = jnp.where(head_lane, jax.nn.sigmoid(_lanes_from(sm, B_LANE)), 0.0)
    gc, rc = _chunk_scans(g)
    bx = _dot2(beta, xp)
    egx = _dot2(jnp.exp(gc), xp)
    erx = _dot2(jnp.exp(rc), xp)
    kb = kn * bx
    kn_s[...] = kn
    kb_s[...] = kb
    qn_s[...] = qn
    qd_s[...] = qn * egx
    kd_s[...] = kn * erx
    for h in range(H):
        sl = slice(h * Dh, (h + 1) * Dh)
        rhs_s[:, 2 * h * Dh:(2 * h + 1) * Dh] = v[:, sl] * bx[:, sl]
        rhs_s[:, (2 * h + 1) * Dh:(2 * h + 2) * Dh] = kb[:, sl] * egx[:, sl]
    gc_s[...] = gc
    for c in range(ncb):
        gct_s[c] = gc[c * C:(c + 1) * C, :].T
        eg_s[c] = jnp.broadcast_to(egx[(c + 1) * C - 1:(c + 1) * C, :], (SUBLANE, W))

    causal, strict, diag = _tri_masks()
    eye = jnp.where(diag, 1.0, 0.0).astype(F32)

    def chunk(c, carry):
        r0 = pl.multiple_of(c * C, C)
        rows = pl.ds(r0, C)
        gcol = gc_s[rows, :]
        grow = gct_s[c]
        for h in range(H):
            sl = slice(h * Dh, (h + 1) * Dh)
            dec = jnp.exp(jnp.where(causal, gcol[:, h:h + 1] - grow[h:h + 1, :], NEG_BIG))
            kn_h = kn_s[rows, sl]
            low = jnp.where(strict, _dot(kb_s[rows, sl], kn_h, _NT) * dec, 0.0)
            attn = _dot(qn_s[rows, sl], kn_h, _NT) * dec
            p = -low
            inv = eye + p
            for _ in range(5):
                p = _dot3(p, p)
                inv = inv + _dot3(inv, p)
            uw = _dot3(inv, rhs_s[rows, 2 * h * Dh:(2 * h + 2) * Dh])
            u_val, w_dec = uw[:, 0:Dh], uw[:, Dh:2 * Dh]
            st = st_s[h]
            v_new = u_val - _dot(w_dec, st)
            o_s[rows, sl] = _dot(qd_s[rows, sl], st) + _dot(attn, v_new)
            g_tot = eg_s[c, 0:1, sl]
            st_s[h] = st * g_tot + _dot(kd_s[rows, sl], v_new, _TN)
        return carry

    lax.fori_loop(0, ncb, chunk, 0)

    o = o_s[...]
    ms = _dot2(o * o, e) * (1.0 / Dh)
    o = o * lax.rsqrt(ms + EPS) * nw_ref[...]
    o_ref[0] = (o * _silu(z_ref[0])).astype(o_ref.dtype)


def _gdn_call(qkv, z, sm, conv_w, a_log, dt_bias, norm_w):
    B, S, _ = qkv.shape
    TB = min(512, S)
    W, H = GDN_WIDTH, GDN_HEADS
    hp = jnp.zeros((SUBLANE, SMALL_W), F32)
    hp = hp.at[0, :H].set(-jnp.exp(a_log.astype(F32))).at[1, :H].set(dt_bias.astype(F32))
    nw = jnp.tile(norm_w.astype(F32), H).reshape(1, W)
    const = lambda shape: pl.BlockSpec(shape, lambda b, i: tuple(0 for _ in shape))
    blk = lambda n: pl.BlockSpec((1, TB, n), lambda b, i: (b, i, 0))
    f = lambda *shape: pltpu.VMEM(shape, F32)
    return pl.pallas_call(
        functools.partial(_gdn_kernel, TB=TB),
        out_shape=jax.ShapeDtypeStruct((B, S, W), BF16),
        grid=(B, S // TB),
        in_specs=[blk(GDN_CONV_DIM), blk(W), blk(SMALL_W),
                  const((CONV_WIDTH, GDN_CONV_DIM)), const((SUBLANE, SMALL_W)), const((1, W)),
                  const((W, W)), const((SMALL_W, W))],
        out_specs=blk(W),
        scratch_shapes=[f(SUBLANE, GDN_CONV_DIM), f(H, GDN_HEAD_DIM, GDN_HEAD_DIM),
                        f(TB, W), f(TB, W), f(TB, W), f(TB, W), f(TB, W), f(TB, 2 * W), f(TB // CHUNK, SUBLANE, W),
                        f(TB, SMALL_W), f(TB // CHUNK, SMALL_W, CHUNK), f(TB, W)],
        compiler_params=_cparams(("arbitrary", "arbitrary")),
        name="gdn_mixer",
    )(qkv, z, sm, conv_w.astype(F32), hp, nw, _block_ones(W, GDN_HEAD_DIM), _head_expand())


def _ssd_kernel(xbc_ref, z_ref, sm_ref, cw_ref, cb_ref, hp_ref, dsk_ref, nw_ref, e_ref, xp_ref, o_ref,
                tail_s, st_s, xs_s, xdt_s, xdd_s, bm_s, cm_s, ea_s, el_s, ac_s, act_s, y_s, *, TB):
    H, P, G, N, C = SSD_HEADS, SSD_HEAD_DIM, SSD_GROUPS, SSD_STATE, CHUNK
    R = H // G
    W, GW = SSD_WIDTH, SSD_GROUP_WIDTH
    ncb = TB // C

    @pl.when(pl.program_id(1) == 0)
    def _():
        tail_s[...] = jnp.zeros_like(tail_s)
        st_s[...] = jnp.zeros_like(st_s)

    xc = _causal_conv_silu(xbc_ref[0], tail_s, cw_ref, cb_ref[...])
    xs = xc[:, 0:W]
    xp = xp_ref[...]

    sm = sm_ref[0]
    lane = lax.broadcasted_iota(jnp.int32, sm.shape, 1)
    head_lane = lane < H
    dt = jnp.where(head_lane, _softplus(_lanes_from(sm, DT_LANE) + hp_ref[1:2, :]), 0.0)
    acs, rcs = _chunk_scans(dt * hp_ref[0:1, :])
    dtx = _dot2(dt, xp)
    eax = _dot2(jnp.exp(acs), xp)
    erx = _dot2(jnp.exp(rcs), xp)
    xdt = xs * dtx
    xs_s[...] = xs
    xdt_s[...] = xdt
    xdd_s[...] = xdt * erx
    bm_s[...] = xc[:, W:W + G * N]
    cm_s[...] = xc[:, W + G * N:W + 2 * G * N]
    ea_s[...] = eax
    ac_s[...] = acs
    for c in range(ncb):
        act_s[c] = acs[c * C:(c + 1) * C, :].T
        el_s[c] = jnp.broadcast_to(eax[(c + 1) * C - 1:(c + 1) * C, :], (SUBLANE, W))

    causal, _, _ = _tri_masks()

    def chunk(c, carry):
        r0 = pl.multiple_of(c * C, C)
        rows = pl.ds(r0, C)
        acol = ac_s[rows, :]
        arow = act_s[c]
        for gi in range(G):
            gsl = slice(gi * GW, (gi + 1) * GW)
            b_g = bm_s[rows, gi * N:(gi + 1) * N]
            c_g = cm_s[rows, gi * N:(gi + 1) * N]
            cb = _dot(c_g, b_g, _NT)
            st = st_s[gi]
            y_off = _dot(c_g, st) * ea_s[rows, gsl]
            for r in range(R):
                h = gi * R + r
                sl = slice(h * P, (h + 1) * P)
                seg = jnp.exp(jnp.where(causal, acol[:, h:h + 1] - arow[h:h + 1, :], NEG_BIG))
                y_s[rows, sl] = _dot(cb * seg, xdt_s[rows, sl]) + y_off[:, r * P:(r + 1) * P]
            st_s[gi] = st * el_s[c, 0:1, gsl] + _dot(b_g, xdd_s[rows, gsl], _TN)
        return carry

    lax.fori_loop(0, ncb, chunk, 0)

    y = y_s[...] + dsk_ref[...] * xs_s[...]
    y = y * _silu(z_ref[0])
    ms = _dot2(y * y, e_ref[...]) * (1.0 / GW)
    o_ref[0] = (y * lax.rsqrt(ms + EPS) * nw_ref[...]).astype(o_ref.dtype)


def _ssd_call(xbc, z, sm, conv_w, conv_b, a_log, dt_bias, d_skip, norm_w):
    B, S, _ = xbc.shape
    TB = min(512, S)
    W, H, G, N = SSD_WIDTH, SSD_HEADS, SSD_GROUPS, SSD_STATE
    hp = jnp.zeros((SUBLANE, SMALL_W), F32)
    hp = hp.at[0, :H].set(-jnp.exp(a_log.astype(F32))).at[1, :H].set(dt_bias.astype(F32))
    dsk = jnp.repeat(d_skip.astype(F32), SSD_HEAD_DIM).reshape(1, W)
    const = lambda shape: pl.BlockSpec(shape, lambda b, i: tuple(0 for _ in shape))
    blk = lambda n: pl.BlockSpec((1, TB, n), lambda b, i: (b, i, 0))
    f = lambda *shape: pltpu.VMEM(shape, F32)
    return pl.pallas_call(
        functools.partial(_ssd_kernel, TB=TB),
        out_shape=jax.ShapeDtypeStruct((B, S, W), BF16),
        grid=(B, S // TB),
        in_specs=[blk(SSD_CONV_DIM), blk(W), blk(SMALL_W),
                  const((CONV_WIDTH, SSD_CONV_DIM)), const((1, SSD_CONV_DIM)), const((SUBLANE, SMALL_W)),
                  const((1, W)), const((1, W)), const((W, W)), const((SMALL_W, W))],
        out_specs=blk(W),
        scratch_shapes=[f(SUBLANE, SSD_CONV_DIM), f(G, N, SSD_GROUP_WIDTH),
                        f(TB, W), f(TB, W), f(TB, W), f(TB, G * N), f(TB, G * N), f(TB, W),
                        f(TB // CHUNK, SUBLANE, W),
                        f(TB, SMALL_W), f(TB // CHUNK, SMALL_W, CHUNK), f(TB, W)],
        compiler_params=_cparams(("arbitrary", "arbitrary")),
        name="ssd_mixer",
    )(xbc, z, sm, conv_w.astype(F32), conv_b.reshape(1, -1).astype(F32), hp, dsk,
      norm_w.reshape(1, W).astype(F32), _block_ones(W, SSD_GROUP_WIDTH), _head_expand())


def _out_kernel(x_ref, y1_ref, y2_ref, y3_ref, mod_ref, nw_ref, wo_ref, wr_ref, br_ref,
                x1_ref, h_ref, rt_ref):
    y = (jnp.dot(y1_ref[0], wo_ref[0:S5_WIDTH, :], preferred_element_type=F32)
         + jnp.dot(y2_ref[0], wo_ref[S5_WIDTH:S5_WIDTH + GDN_WIDTH, :], preferred_element_type=F32)
         + jnp.dot(y3_ref[0], wo_ref[S5_WIDTH + GDN_WIDTH:, :], preferred_element_type=F32))
    x1 = x_ref[0] + mod_ref[0, 2:3, :] * y
    x1_ref[0] = x1
    h = _norm_mod(x1, nw_ref[...], mod_ref[0, 4:5, :], mod_ref[0, 3:4, :])
    h_ref[0] = h
    lg = jnp.dot(h.astype(BF16), wr_ref[...], preferred_element_type=F32) + br_ref[...]

    lane = lax.broadcasted_iota(jnp.int32, lg.shape, 1)
    lanef = lane.astype(F32)
    big = float(4 * LANE)
    grp = (lane >= GRP_LANE) & (lane < GRP_LANE + MOE_GROUPS)
    lgm = jnp.where(grp, lg, -jnp.inf)
    m = jnp.max(lgm, axis=-1, keepdims=True)
    gidx = jnp.min(jnp.where(lgm == m, lanef - GRP_LANE, big), axis=-1, keepdims=True)
    g_w = 1.0 / jnp.sum(jnp.where(grp, jnp.exp(lg - m), 0.0), axis=-1, keepdims=True)
    in_grp = (lane < N_EXPERTS) & ((lane // EXPERTS_PER_GROUP).astype(F32) == gidx)
    le = jnp.where(in_grp, lg, -jnp.inf)
    v1 = jnp.max(le, axis=-1, keepdims=True)
    i1 = jnp.min(jnp.where(le == v1, lanef, big), axis=-1, keepdims=True)
    le2 = jnp.where(lanef == i1, -jnp.inf, le)
    v2 = jnp.max(le2, axis=-1, keepdims=True)
    i2 = jnp.min(jnp.where(le2 == v2, lanef, big), axis=-1, keepdims=True)
    e2 = jnp.exp(v2 - v1)
    w1 = g_w / (1.0 + e2)
    w2 = g_w * e2 / (1.0 + e2)
    rt = jnp.where(lane == 0, i1, jnp.where(lane == 1, i2, jnp.where(lane == 2, w1, jnp.where(lane == 3, w2, 0.0))))
    rt_ref[0] = rt


def _out_call(x, y1, y2, y3, mod_l, norm_w, w_out, w_router, b_router):
    B, S, D = x.shape
    tm = min(512, S)
    blk = lambda n: pl.BlockSpec((1, tm, n), lambda b, i: (b, i, 0))
    const = lambda shape: pl.BlockSpec(shape, lambda b, i: tuple(0 for _ in shape))
    return pl.pallas_call(
        _out_kernel,
        out_shape=[jax.ShapeDtypeStruct((B, S, D), F32), jax.ShapeDtypeStruct((B, S, D), F32),
                   jax.ShapeDtypeStruct((B, S, LANE), F32)],
        grid=(B, S // tm),
        in_specs=[blk(D), blk(S5_WIDTH), blk(GDN_WIDTH), blk(SSD_WIDTH),
                  pl.BlockSpec((1, 6, D), lambda b, i: (b, 0, 0)),
                  const((1, D)), const((D, D)), const((D, LANE)), const((1, LANE))],
        out_specs=[blk(D), blk(D), blk(LANE)],
        compiler_params=_cparams(("arbitrary", "arbitrary")),
        name="out_proj_router",
    )(x, y1, y2, y3, mod_l, norm_w.reshape(1, D), w_out, w_router, b_router)


def _router_params(w_grp, b_grp, w_rt, b_rt):
    D = w_grp.shape[0]
    w = jnp.zeros((D, LANE), F32).at[:, 0:N_EXPERTS].set(w_rt).at[:, GRP_LANE:GRP_LANE + MOE_GROUPS].set(w_grp)
    b = jnp.zeros((1, LANE), F32).at[0, 0:N_EXPERTS].set(b_rt).at[0, GRP_LANE:GRP_LANE + MOE_GROUPS].set(b_grp)
    return w.astype(BF16), b


def _expert_kernel(nused_ref, blke_ref, tok_ref, dst_ref, h_hbm, wg_ref, wu_ref, wd_ref, y_hbm,
                   xbuf, ybuf, sem_in, sem_out):
    i = pl.program_id(0)

    def gather(r):
        return pltpu.make_async_copy(h_hbm.at[pl.ds(tok_ref[0, 0, r], 1)], xbuf.at[pl.ds(r, 1)], sem_in)

    def scatter(r):
        return pltpu.make_async_copy(ybuf.at[pl.ds(r, 1)], y_hbm.at[pl.ds(dst_ref[0, 0, r], 1)], sem_out)

    def each_row(fn):
        def body(r, carry):
            fn(r)
            return carry
        lax.fori_loop(0, MOE_ROWS, body, 0)

    @pl.when(i < nused_ref[0])
    def _():
        each_row(lambda r: gather(r).start())
        each_row(lambda r: gather(r).wait())
        xb = xbuf[...].astype(BF16)
        hid = _silu(jnp.dot(xb, wg_ref[0], preferred_element_type=F32)) * jnp.dot(xb, wu_ref[0], preferred_element_type=F32)
        ybuf[...] = jnp.dot(hid.astype(BF16), wd_ref[0], preferred_element_type=F32)
        each_row(lambda r: scatter(r).start())
        each_row(lambda r: scatter(r).wait())


def _expert_call(h2, n_used, blk_e, buf_tok, buf_dst, w_gate, w_up, w_down, n_rows_out):
    N, D = h2.shape
    n_blk = buf_tok.shape[0]
    grid_spec = pltpu.PrefetchScalarGridSpec(
        num_scalar_prefetch=2,
        grid=(n_blk,),
        in_specs=[pl.BlockSpec((1, 1, MOE_ROWS), lambda i, nu, be: (i, 0, 0), memory_space=pltpu.SMEM),
                  pl.BlockSpec((1, 1, MOE_ROWS), lambda i, nu, be: (i, 0, 0), memory_space=pltpu.SMEM),
                  pl.BlockSpec(memory_space=pl.ANY),
                  pl.BlockSpec((1, D, D_EXPERT), lambda i, nu, be: (be[i], 0, 0)),
                  pl.BlockSpec((1, D, D_EXPERT), lambda i, nu, be: (be[i], 0, 0)),
                  pl.BlockSpec((1, D_EXPERT, D), lambda i, nu, be: (be[i], 0, 0))],
        out_specs=pl.BlockSpec(memory_space=pl.ANY),
        scratch_shapes=[pltpu.VMEM((MOE_ROWS, D), F32), pltpu.VMEM((MOE_ROWS, D), F32),
                        pltpu.SemaphoreType.DMA, pltpu.SemaphoreType.DMA],
    )
    return pl.pallas_call(
        _expert_kernel,
        out_shape=jax.ShapeDtypeStruct((n_rows_out, D), F32),
        grid_spec=grid_spec,
        compiler_params=_cparams(("arbitrary",)),
        name="expert_mlp",
    )(n_used, blk_e, buf_tok, buf_dst, h2, w_gate, w_up, w_down)


def _dispatch(route, N):
    NK = N * TOP_K
    L_pad = NK + N_EXPERTS * MOE_ROWS
    n_blk = L_pad // MOE_ROWS
    eid = route[:, 0:TOP_K].astype(jnp.int32).reshape(NK)
    onehot = (eid[:, None] == jnp.arange(N_EXPERTS, dtype=jnp.int32)[None, :]).astype(jnp.int32)
    csum = jnp.cumsum(onehot, axis=0)
    counts = csum[-1]
    pos = jnp.sum((csum - onehot) * onehot, axis=1)
    padded = ((counts + MOE_ROWS - 1) // MOE_ROWS) * MOE_ROWS
    pend = jnp.cumsum(padded)
    pstart = pend - padded
    dest = pstart[eid] + pos
    n_used = (pend[-1] // MOE_ROWS).astype(jnp.int32).reshape(1)
    flat = jnp.arange(NK, dtype=jnp.int32)
    buf_tok = jnp.zeros((L_pad,), jnp.int32).at[dest].set(flat // TOP_K)
    buf_dst = (NK + jnp.arange(L_pad, dtype=jnp.int32) % MOE_ROWS).at[dest].set(flat)
    blk_e = jnp.minimum(jnp.searchsorted(pend, jnp.arange(n_blk, dtype=jnp.int32) * MOE_ROWS, side='right'),
                        N_EXPERTS - 1).astype(jnp.int32)
    return n_used, blk_e, buf_tok.reshape(n_blk, 1, MOE_ROWS), buf_dst.reshape(n_blk, 1, MOE_ROWS)


def _combine_kernel(x_ref, y_ref, rt_ref, mod_ref, nf_ref, o_ref, *, final):
    D = x_ref.shape[-1]
    rt = rt_ref[0]
    y = rt[:, 2:3] * y_ref[0, :, 0:D] + rt[:, 3:4] * y_ref[0, :, D:2 * D]
    x2 = x_ref[0] + mod_ref[0, 5:6, :] * y
    if final:
        ms = jnp.mean(x2 * x2, axis=-1, keepdims=True)
        x2 = x2 * lax.rsqrt(ms + EPS) * nf_ref[...]
    o_ref[0] = x2


def _combine_call(x1, y2, route, mod_l, norm_final, final):
    B, S, D = x1.shape
    tm = min(512, S)
    blk = lambda n: pl.BlockSpec((1, tm, n), lambda b, i: (b, i, 0))
    nb = S // tm
    return pl.pallas_call(
        functools.partial(_combine_kernel, final=final),
        out_shape=jax.ShapeDtypeStruct((B, S, D), F32),
        grid=(B, nb),
        in_specs=[blk(D),
                  pl.BlockSpec((1, tm, TOP_K * D), lambda b, i: (0, b * nb + i, 0)),
                  blk(LANE),
                  pl.BlockSpec((1, 6, D), lambda b, i: (b, 0, 0)),
                  pl.BlockSpec((1, D), lambda b, i: (0, 0))],
        out_specs=blk(D),
        compiler_params=_cparams(("arbitrary", "arbitrary")),
        name="moe_combine",
    )(x1, y2, route, mod_l, norm_final.reshape(1, D))


def _layer(x, mod_l, p, final, norm_final):
    B, S, D = x.shape
    N = B * S
    s5_u, g_qkv, g_z, s_z, s_xbc, small = _proj_call(x, mod_l, p["norm_mix"], _arrange_w_in(p["w_in"]))
    bm, cm, lam_rows = _s5_params(p["s5_a_re"], p["s5_a_im"], p["s5_b_re"], p["s5_b_im"],
                                  p["s5_c_re"], p["s5_c_im"], p["s5_log_dt"])
    y_s5 = _s5_call(s5_u, bm, cm, lam_rows, p["s5_d"], p["s5_w_glu"], p["s5_norm"])
    y_gdn = _gdn_call(g_qkv, g_z, small, p["gdn_conv_w"], p["gdn_a_log"], p["gdn_dt_bias"], p["gdn_norm"])
    y_ssd = _ssd_call(s_xbc, s_z, small, p["ssd_conv_w"], p["ssd_conv_b"], p["ssd_a_log"], p["ssd_dt_bias"],
                      p["ssd_d"], p["ssd_norm"])
    w_router, b_router = _router_params(p["moe_w_grp"], p["moe_b_grp"], p["moe_w_rt"], p["moe_b_rt"])
    x1, h2, route = _out_call(x, y_s5, y_gdn, y_ssd, mod_l, p["norm_ffn"], p["w_out"].astype(BF16),
                              w_router, b_router)
    route2 = route.reshape(N, LANE)
    n_used, blk_e, buf_tok, buf_dst = _dispatch(route2, N)
    n_rows_out = N * TOP_K + MOE_ROWS
    y2 = _expert_call(h2.reshape(N, D), n_used, blk_e, buf_tok, buf_dst,
                      p["moe_w_gate"].astype(BF16), p["moe_w_up"].astype(BF16), p["moe_w_down"].astype(BF16),
                      n_rows_out)
    y2 = y2.reshape(1, n_rows_out // TOP_K, TOP_K * D)
    return _combine_call(x1, y2, route, mod_l, norm_final, final)


def kernel(x, c, w_ada, b_ada, norm_mix, norm_ffn, w_in, w_out, s5_a_re, s5_a_im, s5_b_re, s5_b_im, s5_c_re, s5_c_im, s5_d, s5_log_dt, s5_w_glu, s5_norm, gdn_conv_w, gdn_a_log, gdn_dt_bias, gdn_norm, ssd_conv_w, ssd_conv_b, ssd_a_log, ssd_dt_bias, ssd_d, ssd_norm, moe_w_grp, moe_b_grp, moe_w_rt, moe_b_rt, moe_w_gate, moe_w_up, moe_w_down, norm_final):
    stacked = dict(norm_mix=norm_mix, norm_ffn=norm_ffn, w_in=w_in, w_out=w_out, s5_a_re=s5_a_re, s5_a_im=s5_a_im,
                   s5_b_re=s5_b_re, s5_b_im=s5_b_im, s5_c_re=s5_c_re, s5_c_im=s5_c_im, s5_d=s5_d,
                   s5_log_dt=s5_log_dt, s5_w_glu=s5_w_glu, s5_norm=s5_norm, gdn_conv_w=gdn_conv_w,
                   gdn_a_log=gdn_a_log, gdn_dt_bias=gdn_dt_bias, gdn_norm=gdn_norm, ssd_conv_w=ssd_conv_w,
                   ssd_conv_b=ssd_conv_b, ssd_a_log=ssd_a_log, ssd_dt_bias=ssd_dt_bias, ssd_d=ssd_d,
                   ssd_norm=ssd_norm, moe_w_grp=moe_w_grp, moe_b_grp=moe_b_grp, moe_w_rt=moe_w_rt,
                   moe_b_rt=moe_b_rt, moe_w_gate=moe_w_gate, moe_w_up=moe_w_up, moe_w_down=moe_w_down)
    L = w_in.shape[0]
    B, S, D = x.shape
    mod = _mod_call(c, w_ada, b_ada).reshape(L, B, 6, D)
    for l in range(L):
        p = {k: v[l] for k, v in stacked.items()}
        x = _layer(x, mod[l], p, l == L - 1, norm_final)
    return x
```

```python
import functools

import numpy as np
import jax
import jax.numpy as jnp
from jax import lax
from jax.experimental import pallas as pl
from jax.experimental.pallas import tpu as pltpu

F32 = jnp.float32
BF16 = jnp.bfloat16

D_MODEL = 1024
DEPTH = 4
EPS = 1e-6
CONV_WIDTH = 4
CHUNK = 64
S5_WIDTH = 256
S5_CH = 16
S5_GROUPS = 16
S5_STATE = 64
S5_LANES = S5_GROUPS * S5_STATE
GDN_WIDTH = 384
GDN_HEAD_DIM = 64
GDN_HEADS = 6
GDN_CONV_DIM = 3 * GDN_WIDTH
SSD_WIDTH = 384
SSD_HEAD_DIM = 64
SSD_HEADS = 6
SSD_GROUPS = 2
SSD_STATE = 128
SSD_GROUP_WIDTH = SSD_WIDTH // SSD_GROUPS
SSD_CONV_DIM = SSD_WIDTH + 2 * SSD_GROUPS * SSD_STATE
PROJ_SIZES = (S5_WIDTH, GDN_CONV_DIM, GDN_WIDTH, GDN_HEADS, GDN_HEADS, SSD_WIDTH, SSD_CONV_DIM, SSD_HEADS)
MOE_GROUPS = 4
EXPERTS_PER_GROUP = 8
N_EXPERTS = 32
TOP_K = 2
D_EXPERT = 256

LANE = 128
SUBLANE = 8
SMALL_W = LANE
A_LANE, B_LANE, DT_LANE = 0, 8, 16
GRP_LANE = N_EXPERTS
MOE_ROWS = 256
GDN_CHUNKS_PER_ITER = 2
VMEM_LIMIT = 56 * 1024 * 1024
NEG_BIG = -1e30


def _cparams(sem):
    return pltpu.CompilerParams(dimension_semantics=sem, vmem_limit_bytes=VMEM_LIMIT)


def _split(a):
    hi = a.astype(BF16)
    lo = (a - hi.astype(F32)).astype(BF16)
    return hi, lo


_NN = (((1,), (0,)), ((), ()))
_NT = (((1,), (1,)), ((), ()))
_TN = (((0,), (0,)), ((), ()))


def _dot(a, b, dims=_NN):
    return lax.dot_general(a.astype(BF16), b.astype(BF16), dims, preferred_element_type=F32)


def _dot2(a, b_bf16):
    hi, lo = _split(a)
    return (lax.dot_general(hi, b_bf16, _NN, preferred_element_type=F32)
            + lax.dot_general(lo, b_bf16, _NN, preferred_element_type=F32))


def _dot3(a, b):
    ah, al = _split(a)
    bh, bl = _split(b)
    d = functools.partial(lax.dot_general, dimension_numbers=_NN, preferred_element_type=F32)
    return d(ah, bh) + (d(ah, bl) + d(al, bh))


def _silu(x):
    return x * jax.nn.sigmoid(x)


def _softplus(x):
    return jnp.maximum(x, 0.0) + jnp.log(1.0 + jnp.exp(-jnp.abs(x)))


def _norm_mod(x, w, scale, shift):
    ms = jnp.mean(x * x, axis=-1, keepdims=True)
    return (x * lax.rsqrt(ms + EPS) * w) * (1.0 + scale) + shift


def _mod_kernel(c_ref, w_ref, b_ref, o_ref):
    cond = _silu(c_ref[...])
    o_ref[0] = _dot3(cond, w_ref[0]) + b_ref[0]


def _mod_call(c, w_ada, b_ada):
    L, D, W = w_ada.shape
    B = c.shape[0]
    tn = 1536
    return pl.pallas_call(
        _mod_kernel,
        out_shape=jax.ShapeDtypeStruct((L, B, W), F32),
        grid=(L, W // tn),
        in_specs=[pl.BlockSpec((B, D), lambda l, j: (0, 0)),
                  pl.BlockSpec((1, D, tn), lambda l, j: (l, 0, j)),
                  pl.BlockSpec((1, 1, tn), lambda l, j: (l, 0, j))],
        out_specs=pl.BlockSpec((1, B, tn), lambda l, j: (l, 0, j)),
        compiler_params=_cparams(("arbitrary", "arbitrary")),
        name="adaln_mod",
    )(c, w_ada, b_ada.reshape(L, 1, W))


PROJ_OUT_W = (S5_WIDTH, GDN_CONV_DIM, GDN_WIDTH, SSD_WIDTH, SSD_CONV_DIM, SMALL_W)


def _proj_kernel(x_ref, mod_ref, nw_ref, w_ref, *o_refs):
    x = x_ref[0]
    h = _norm_mod(x, nw_ref[...], mod_ref[0, 1:2, :], mod_ref[0, 0:1, :]).astype(BF16)
    off = 0
    for o_ref in o_refs:
        n = o_ref.shape[-1]
        o_ref[0] = jnp.dot(h, w_ref[:, off:off + n], preferred_element_type=F32)
        off += n


def _arrange_w_in(w_in_l):
    splits = np.cumsum(PROJ_SIZES)[:-1].tolist()
    s5_u, g_qkv, g_z, g_a, g_b, s_z, s_xbc, s_dt = jnp.split(w_in_l, splits, axis=-1)
    D = w_in_l.shape[0]
    small = jnp.zeros((D, SMALL_W), w_in_l.dtype)
    small = small.at[:, A_LANE:A_LANE + GDN_HEADS].set(g_a)
    small = small.at[:, B_LANE:B_LANE + GDN_HEADS].set(g_b)
    small = small.at[:, DT_LANE:DT_LANE + SSD_HEADS].set(s_dt)
    return jnp.concatenate([s5_u, g_qkv, g_z, s_z, s_xbc, small], axis=-1).astype(BF16)


def _proj_call(x, mod_l, norm_w, w_arr):
    B, S, D = x.shape
    tm = min(512, S)
    W = w_arr.shape[1]
    out_shape = [jax.ShapeDtypeStruct((B, S, n), F32) for n in PROJ_OUT_W]
    out_specs = [pl.BlockSpec((1, tm, n), lambda b, i: (b, i, 0)) for n in PROJ_OUT_W]
    return pl.pallas_call(
        _proj_kernel,
        out_shape=out_shape,
        grid=(B, S // tm),
        in_specs=[pl.BlockSpec((1, tm, D), lambda b, i: (b, i, 0)),
                  pl.BlockSpec((1, 6, D), lambda b, i: (b, 0, 0)),
                  pl.BlockSpec((1, D), lambda b, i: (0, 0)),
                  pl.BlockSpec((D, W), lambda b, i: (0, 0))],
        out_specs=out_specs,
        compiler_params=_cparams(("arbitrary", "arbitrary")),
        name="norm_in_proj",
    )(x, mod_l, norm_w.reshape(1, D), w_arr)


def _s5_kernel(u_ref, bm_ref, cm_ref, lam_ref, dsk_ref, glu_ref, nw_ref, o_ref, x_s, st_s, *, T):
    B = u_ref.shape[0]
    P = S5_LANES

    @pl.when(pl.program_id(0) == 0)
    def _():
        st_s[...] = jnp.zeros_like(st_s)

    nj = P // LANE
    u = u_ref[...].reshape(B * T, S5_WIDTH)
    ub = u.astype(BF16)
    for j in range(2 * nj):
        x_s[j] = jnp.dot(ub, bm_ref[:, j * LANE:(j + 1) * LANE], preferred_element_type=F32)
    lr = jnp.broadcast_to(lam_ref[0:1, :], (B, P))
    li = jnp.broadcast_to(lam_ref[1:2, :], (B, P))

    def step(t, carry):
        sr, si = carry
        rows = pl.ds(t, B, stride=T)
        bur = jnp.concatenate([x_s[j, rows, :] for j in range(nj)], axis=1)
        bui = jnp.concatenate([x_s[nj + j, rows, :] for j in range(nj)], axis=1)
        nr = lr * sr - li * si + bur
        ni = lr * si + li * sr + bui
        for j in range(nj):
            x_s[j, rows, :] = nr[:, j * LANE:(j + 1) * LANE]
            x_s[nj + j, rows, :] = ni[:, j * LANE:(j + 1) * LANE]
        return nr, ni

    sr, si = lax.fori_loop(0, T, step, (st_s[:, 0:P], st_s[:, P:2 * P]))
    st_s[:, 0:P] = sr
    st_s[:, P:2 * P] = si

    xall = jnp.concatenate([x_s[j] for j in range(2 * nj)], axis=1).astype(BF16)
    y = jnp.dot(xall, cm_ref[...], preferred_element_type=F32) + dsk_ref[...] * u
    y = jax.nn.gelu(y)
    y = y * jax.nn.sigmoid(jnp.dot(y.astype(BF16), glu_ref[...], preferred_element_type=F32))
    ms = jnp.mean(y * y, axis=-1, keepdims=True)
    y = y * lax.rsqrt(ms + EPS) * nw_ref[...]
    o_ref[...] = y.reshape(B, T, S5_WIDTH).astype(o_ref.dtype)


def _s5_params(a_re, a_im, b_re, b_im, c_re, c_im, log_dt):
    G, P, CH = S5_GROUPS, S5_STATE, S5_CH
    lam = lax.complex(a_re.astype(F32), a_im.astype(F32))
    step = jnp.exp(log_dt.astype(F32))[:, None]
    lam_bar = jnp.exp(lam * step)
    b_bar = ((lam_bar - 1.0) / lam)[..., None] * lax.complex(b_re.astype(F32), b_im.astype(F32))
    eye = jnp.eye(G, dtype=F32)
    bre = jnp.einsum('gpc,gh->gchp', b_bar.real, eye).reshape(G * CH, G * P)
    bim = jnp.einsum('gpc,gh->gchp', b_bar.imag, eye).reshape(G * CH, G * P)
    bm = jnp.concatenate([bre, bim], axis=1).astype(BF16)
    cre = jnp.einsum('gcp,gh->gphc', c_re.astype(F32), eye).reshape(G * P, G * CH)
    cim = jnp.einsum('gcp,gh->gphc', c_im.astype(F32), eye).reshape(G * P, G * CH)
    cm = jnp.concatenate([cre, -cim], axis=0).astype(BF16)
    lam_rows = jnp.zeros((SUBLANE, G * P), F32)
    lam_rows = lam_rows.at[0].set(lam_bar.real.reshape(-1)).at[1].set(lam_bar.imag.reshape(-1))
    return bm, cm, lam_rows


def _s5_call(u, bm, cm, lam_rows, d_skip, w_glu, norm_w):
    B, S, W = u.shape
    T = min(128, S)
    P2 = 2 * S5_LANES
    const = lambda shape: pl.BlockSpec(shape, lambda i: tuple(0 for _ in shape))
    return pl.pallas_call(
        functools.partial(_s5_kernel, T=T),
        out_shape=jax.ShapeDtypeStruct((B, S, W), BF16),
        grid=(S // T,),
        in_specs=[pl.BlockSpec((B, T, W), lambda i: (0, i, 0)),
                  const((W, P2)), const((P2, W)), const((SUBLANE, S5_LANES)),
                  const((1, W)), const((W, W)), const((1, W))],
        out_specs=pl.BlockSpec((B, T, W), lambda i: (0, i, 0)),
        scratch_shapes=[pltpu.VMEM((P2 // LANE, B * T, LANE), F32), pltpu.VMEM((B, P2), F32)],
        compiler_params=_cparams(("arbitrary",)),
        name="s5_mixer",
    )(u, bm, cm, lam_rows, d_skip.reshape(1, W).astype(F32), w_glu.astype(BF16), norm_w.reshape(1, W).astype(F32))


def _causal_conv_silu(x, tail_ref, cw_ref, bias):
    n = x.shape[0]
    xf = jnp.concatenate([tail_ref[...], x], axis=0)
    acc = x * cw_ref[CONV_WIDTH - 1:CONV_WIDTH, :]
    for k in range(1, CONV_WIDTH):
        acc = acc + pltpu.roll(xf, k, axis=0)[SUBLANE:, :] * cw_ref[CONV_WIDTH - 1 - k:CONV_WIDTH - k, :]
    tail_ref[...] = x[n - SUBLANE:, :]
    if bias is not None:
        acc = acc + bias
    return _silu(acc)


def _lanes_from(sm, off):
    return pltpu.roll(sm, SMALL_W - off, axis=1) if off else sm


def _chunk_scans(g):
    n = g.shape[0]
    rin = lax.broadcasted_iota(jnp.int32, g.shape, 0) & (CHUNK - 1)
    pre = g
    suf = jnp.where(rin < CHUNK - 1, pltpu.roll(g, n - 1, axis=0), 0.0)
    s = 1
    while s < CHUNK:
        pre = pre + jnp.where(rin >= s, pltpu.roll(pre, s, axis=0), 0.0)
        suf = suf + jnp.where(rin + s <= CHUNK - 1, pltpu.roll(suf, n - s, axis=0), 0.0)
        s *= 2
    return pre, suf


def _tri_masks():
    r = lax.broadcasted_iota(jnp.int32, (CHUNK, CHUNK), 0)
    c = lax.broadcasted_iota(jnp.int32, (CHUNK, CHUNK), 1)
    return r >= c, r > c, r == c


def _head_expand():
    m = np.zeros((SMALL_W, GDN_WIDTH), np.float32)
    for h in range(GDN_HEADS):
        m[h, h * GDN_HEAD_DIM:(h + 1) * GDN_HEAD_DIM] = 1.0
    return jnp.asarray(m, BF16)


def _block_ones(width, blk):
    idx = np.arange(width) // blk
    return jnp.asarray((idx[:, None] == idx[None, :]).astype(np.float32), BF16)


def _gdn_kernel(qkv_ref, z_ref, sm_ref, cw_ref, hp_ref, nw_ref, e_ref, xp_ref, o_ref,
                tail_s, st_s, kn_s, kb_s, qn_s, qd_s, kd_s, rhs_s, eg_s, gc_s, gct_s, o_s, at_s, *, TB):
    H, Dh, C = GDN_HEADS, GDN_HEAD_DIM, CHUNK
    W = GDN_WIDTH
    ncb = TB // C

    @pl.when(pl.program_id(1) == 0)
    def _():
        tail_s[...] = jnp.zeros_like(tail_s)
        st_s[...] = jnp.zeros_like(st_s)

    xc = _causal_conv_silu(qkv_ref[0], tail_s, cw_ref, None)
    q, k, v = xc[:, 0:W], xc[:, W:2 * W], xc[:, 2 * W:3 * W]
    e = e_ref[...]
    xp = xp_ref[...]
    qn = q * lax.rsqrt(_dot2(q * q, e) + EPS) * (Dh ** -0.5)
    kn = k * lax.rsqrt(_dot2(k * k, e) + EPS)

    sm = sm_ref[0]
    lane = lax.broadcasted_iota(jnp.int32, sm.shape, 1)
    head_lane = lane < H
    g = jnp.where(head_lane, hp_ref[0:1, :] * _softplus(_lanes_from(sm, A_LANE) + hp_ref[1:2, :]), 0.0)
    beta = jnp.where(head_lane, jax.nn.sigmoid(_lanes_from(sm, B_LANE)), 0.0)
    gc, rc = _chunk_scans(g)
    bx = _dot2(beta, xp)
    egx = _dot2(jnp.exp(gc), xp)
    erx = _dot2(jnp.exp(rc), xp)
    kb = kn * bx
    kn_s[...] = kn
    kb_s[...] = kb
    qn_s[...] = qn
    qd_s[...] = qn * egx
    kd_s[...] = kn * erx
    for h in range(H):
        sl = slice(h * Dh, (h + 1) * Dh)
        rhs_s[:, 2 * h * Dh:(2 * h + 1) * Dh] = v[:, sl] * bx[:, sl]
        rhs_s[:, (2 * h + 1) * Dh:(2 * h + 2) * Dh] = kb[:, sl] * egx[:, sl]
    gc_s[...] = gc
    for c in range(ncb):
        gct_s[c] = gc[c * C:(c + 1) * C, :].T
        eg_s[c] = jnp.broadcast_to(egx[(c + 1) * C - 1:(c + 1) * C, :], (SUBLANE, W))

    causal, strict, diag = _tri_masks()
    eye = jnp.where(diag, 1.0, 0.0).astype(F32)

    heads = [slice(h * Dh, (h + 1) * Dh) for h in range(H)]

    def solve(it, carry):
        probs = []
        for cc in range(GDN_CHUNKS_PER_ITER):
            c = it * GDN_CHUNKS_PER_ITER + cc
            rows = pl.ds(pl.multiple_of(c * C, C), C)
            gcol = gc_s[rows, :]
            grow = gct_s[c]
            probs += [(rows, h, sl, gcol, grow) for h, sl in enumerate(heads)]
        decs = [jnp.exp(jnp.where(causal, gcol[:, h:h + 1] - grow[h:h + 1, :], NEG_BIG))
                for rows, h, sl, gcol, grow in probs]
        kns = [kn_s[rows, sl] for rows, h, sl, _, _ in probs]
        lows = [jnp.where(strict, _dot(kb_s[rows, sl], kn, _NT) * dec, 0.0)
                for (rows, h, sl, _, _), kn, dec in zip(probs, kns, decs)]
        for (rows, h, sl, _, _), kn, dec in zip(probs, kns, decs):
            at_s[rows, sl] = _dot(qn_s[rows, sl], kn, _NT) * dec
        ps = [-low for low in lows]
        invs = [eye + p for p in ps]
        for _ in range(5):
            ps = [_dot3(p, p) for p in ps]
            invs = [inv + _dot3(inv, p) for inv, p in zip(invs, ps)]
        for (rows, h, sl, _, _), inv in zip(probs, invs):
            cols = slice(2 * h * Dh, (2 * h + 2) * Dh)
            rhs_s[rows, cols] = _dot3(inv, rhs_s[rows, cols])
        return carry

    lax.fori_loop(0, ncb // GDN_CHUNKS_PER_ITER, solve, 0)

    def recur(c, carry):
        rows = pl.ds(pl.multiple_of(c * C, C), C)
        sts = [st_s[h] for h in range(H)]
        uws = [rhs_s[rows, 2 * h * Dh:(2 * h + 2) * Dh] for h in range(H)]
        v_news = [uw[:, 0:Dh] - _dot(uw[:, Dh:2 * Dh], st) for uw, st in zip(uws, sts)]
        for sl, st, v_new in zip(heads, sts, v_news):
            o_s[rows, sl] = _dot(qd_s[rows, sl], st) + _dot(at_s[rows, sl], v_new)
        for h, (sl, st, v_new) in enumerate(zip(heads, sts, v_news)):
            st_s[h] = st * eg_s[c, 0:1, sl] + _dot(kd_s[rows, sl], v_new, _TN)
        return carry

    lax.fori_loop(0, ncb, recur, 0)

    o = o_s[...]
    ms = _dot2(o * o, e) * (1.0 / Dh)
    o = o * lax.rsqrt(ms + EPS) * nw_ref[...]
    o_ref[0] = (o * _silu(z_ref[0])).astype(o_ref.dtype)


def _gdn_call(qkv, z, sm, conv_w, a_log, dt_bias, norm_w):
    B, S, _ = qkv.shape
    TB = min(512, S)
    W, H = GDN_WIDTH, GDN_HEADS
    hp = jnp.zeros((SUBLANE, SMALL_W), F32)
    hp = hp.at[0, :H].set(-jnp.exp(a_log.astype(F32))).at[1, :H].set(dt_bias.astype(F32))
    nw = jnp.tile(norm_w.astype(F32), H).reshape(1, W)
    const = lambda shape: pl.BlockSpec(shape, lambda b, i: tuple(0 for _ in shape))
    blk = lambda n: pl.BlockSpec((1, TB, n), lambda b, i: (b, i, 0))
    f = lambda *shape: pltpu.VMEM(shape, F32)
    return pl.pallas_call(
        functools.partial(_gdn_kernel, TB=TB),
        out_shape=jax.ShapeDtypeStruct((B, S, W), BF16),
        grid=(B, S // TB),
        in_specs=[blk(GDN_CONV_DIM), blk(W), blk(SMALL_W),
                  const((CONV_WIDTH, GDN_CONV_DIM)), const((SUBLANE, SMALL_W)), const((1, W)),
                  const((W, W)), const((SMALL_W, W))],
        out_specs=blk(W),
        scratch_shapes=[f(SUBLANE, GDN_CONV_DIM), f(H, GDN_HEAD_DIM, GDN_HEAD_DIM),
                        f(TB, W), f(TB, W), f(TB, W), f(TB, W), f(TB, W), f(TB, 2 * W), f(TB // CHUNK, SUBLANE, W),
                        f(TB, SMALL_W), f(TB // CHUNK, SMALL_W, CHUNK), f(TB, W), f(TB, W)],
        compiler_params=_cparams(("arbitrary", "arbitrary")),
        name="gdn_mixer",
    )(qkv, z, sm, conv_w.astype(F32), hp, nw, _block_ones(W, GDN_HEAD_DIM), _head_expand())


def _ssd_kernel(xbc_ref, z_ref, sm_ref, cw_ref, cb_ref, hp_ref, dsk_ref, nw_ref, e_ref, xp_ref, o_ref,
                tail_s, st_s, xs_s, xdt_s, xdd_s, bm_s, cm_s, ea_s, el_s, ac_s, act_s, y_s, *, TB):
    H, P, G, N, C = SSD_HEADS, SSD_HEAD_DIM, SSD_GROUPS, SSD_STATE, CHUNK
    R = H // G
    W, GW = SSD_WIDTH, SSD_GROUP_WIDTH
    ncb = TB // C

    @pl.when(pl.program_id(1) == 0)
    def _():
        tail_s[...] = jnp.zeros_like(tail_s)
        st_s[...] = jnp.zeros_like(st_s)

    xc = _causal_conv_silu(xbc_ref[0], tail_s, cw_ref, cb_ref[...])
    xs = xc[:, 0:W]
    xp = xp_ref[...]

    sm = sm_ref[0]
    lane = lax.broadcasted_iota(jnp.int32, sm.shape, 1)
    head_lane = lane < H
    dt = jnp.where(head_lane, _softplus(_lanes_from(sm, DT_LANE) + hp_ref[1:2, :]), 0.0)
    acs, rcs = _chunk_scans(dt * hp_ref[0:1, :])
    dtx = _dot2(dt, xp)
    eax = _dot2(jnp.exp(acs), xp)
    erx = _dot2(jnp.exp(rcs), xp)
    xdt = xs * dtx
    xs_s[...] = xs
    xdt_s[...] = xdt
    xdd_s[...] = xdt * erx
    bm_s[...] = xc[:, W:W + G * N]
    cm_s[...] = xc[:, W + G * N:W + 2 * G * N]
    ea_s[...] = eax
    ac_s[...] = acs
    for c in range(ncb):
        act_s[c] = acs[c * C:(c + 1) * C, :].T
        el_s[c] = jnp.broadcast_to(eax[(c + 1) * C - 1:(c + 1) * C, :], (SUBLANE, W))

    causal, _, _ = _tri_masks()

    def chunk(c, carry):
        r0 = pl.multiple_of(c * C, C)
        rows = pl.ds(r0, C)
        acol = ac_s[rows, :]
        arow = act_s[c]
        for gi in range(G):
            gsl = slice(gi * GW, (gi + 1) * GW)
            b_g = bm_s[rows, gi * N:(gi + 1) * N]
            c_g = cm_s[rows, gi * N:(gi + 1) * N]
            cb = _dot(c_g, b_g, _NT)
            st = st_s[gi]
            y_off = _dot(c_g, st) * ea_s[rows, gsl]
            for r in range(R):
                h = gi * R + r
                sl = slice(h * P, (h + 1) * P)
                seg = jnp.exp(jnp.where(causal, acol[:, h:h + 1] - arow[h:h + 1, :], NEG_BIG))
                y_s[rows, sl] = _dot(cb * seg, xdt_s[rows, sl]) + y_off[:, r * P:(r + 1) * P]
            st_s[gi] = st * el_s[c, 0:1, gsl] + _dot(b_g, xdd_s[rows, gsl], _TN)
        return carry

    lax.fori_loop(0, ncb, chunk, 0)

    y = y_s[...] + dsk_ref[...] * xs_s[...]
    y = y * _silu(z_ref[0])
    ms = _dot2(y * y, e_ref[...]) * (1.0 / GW)
    o_ref[0] = (y * lax.rsqrt(ms + EPS) * nw_ref[...]).astype(o_ref.dtype)


def _ssd_call(xbc, z, sm, conv_w, conv_b, a_log, dt_bias, d_skip, norm_w):
    B, S, _ = xbc.shape
    TB = min(512, S)
    W, H, G, N = SSD_WIDTH, SSD_HEADS, SSD_GROUPS, SSD_STATE
    hp = jnp.zeros((SUBLANE, SMALL_W), F32)
    hp = hp.at[0, :H].set(-jnp.exp(a_log.astype(F32))).at[1, :H].set(dt_bias.astype(F32))
    dsk = jnp.repeat(d_skip.astype(F32), SSD_HEAD_DIM).reshape(1, W)
    const = lambda shape: pl.BlockSpec(shape, lambda b, i: tuple(0 for _ in shape))
    blk = lambda n: pl.BlockSpec((1, TB, n), lambda b, i: (b, i, 0))
    f = lambda *shape: pltpu.VMEM(shape, F32)
    return pl.pallas_call(
        functools.partial(_ssd_kernel, TB=TB),
        out_shape=jax.ShapeDtypeStruct((B, S, W), BF16),
        grid=(B, S // TB),
        in_specs=[blk(SSD_CONV_DIM), blk(W), blk(SMALL_W),
                  const((CONV_WIDTH, SSD_CONV_DIM)), const((1, SSD_CONV_DIM)), const((SUBLANE, SMALL_W)),
                  const((1, W)), const((1, W)), const((W, W)), const((SMALL_W, W))],
        out_specs=blk(W),
        scratch_shapes=[f(SUBLANE, SSD_CONV_DIM), f(G, N, SSD_GROUP_WIDTH),
                        f(TB, W), f(TB, W), f(TB, W), f(TB, G * N), f(TB, G * N), f(TB, W),
                        f(TB // CHUNK, SUBLANE, W),
                        f(TB, SMALL_W), f(TB // CHUNK, SMALL_W, CHUNK), f(TB, W)],
        compiler_params=_cparams(("arbitrary", "arbitrary")),
        name="ssd_mixer",
    )(xbc, z, sm, conv_w.astype(F32), conv_b.reshape(1, -1).astype(F32), hp, dsk,
      norm_w.reshape(1, W).astype(F32), _block_ones(W, SSD_GROUP_WIDTH), _head_expand())


def _out_kernel(x_ref, y1_ref, y2_ref, y3_ref, mod_ref, nw_ref, wo_ref, wr_ref, br_ref, tri_ref,
                x1_ref, h_ref, rt_ref, cnt_ref, run_s):
    y = (jnp.dot(y1_ref[0], wo_ref[0:S5_WIDTH, :], preferred_element_type=F32)
         + jnp.dot(y2_ref[0], wo_ref[S5_WIDTH:S5_WIDTH + GDN_WIDTH, :], preferred_element_type=F32)
         + jnp.dot(y3_ref[0], wo_ref[S5_WIDTH + GDN_WIDTH:, :], preferred_element_type=F32))
    x1 = x_ref[0] + mod_ref[0, 2:3, :] * y
    x1_ref[0] = x1
    h = _norm_mod(x1, nw_ref[...], mod_ref[0, 4:5, :], mod_ref[0, 3:4, :])
    h_ref[0] = h
    lg = jnp.dot(h.astype(BF16), wr_ref[...], preferred_element_type=F32) + br_ref[...]

    lane = lax.broadcasted_iota(jnp.int32, lg.shape, 1)
    lanef = lane.astype(F32)
    big = float(4 * LANE)
    grp = (lane >= GRP_LANE) & (lane < GRP_LANE + MOE_GROUPS)
    lgm = jnp.where(grp, lg, -jnp.inf)
    m = jnp.max(lgm, axis=-1, keepdims=True)
    gidx = jnp.min(jnp.where(lgm == m, lanef - GRP_LANE, big), axis=-1, keepdims=True)
    g_w = 1.0 / jnp.sum(jnp.where(grp, jnp.exp(lg - m), 0.0), axis=-1, keepdims=True)
    in_grp = (lane < N_EXPERTS) & ((lane // EXPERTS_PER_GROUP).astype(F32) == gidx)
    le = jnp.where(in_grp, lg, -jnp.inf)
    v1 = jnp.max(le, axis=-1, keepdims=True)
    i1 = jnp.min(jnp.where(le == v1, lanef, big), axis=-1, keepdims=True)
    le2 = jnp.where(lanef == i1, -jnp.inf, le)
    v2 = jnp.max(le2, axis=-1, keepdims=True)
    i2 = jnp.min(jnp.where(le2 == v2, lanef, big), axis=-1, keepdims=True)
    e2 = jnp.exp(v2 - v1)
    w1 = g_w / (1.0 + e2)
    w2 = g_w * e2 / (1.0 + e2)

    @pl.when((pl.program_id(0) == 0) & (pl.program_id(1) == 0))
    def _():
        run_s[...] = jnp.zeros_like(run_s)

    chosen = jnp.where((lanef == i1) | (lanef == i2), 1.0, 0.0)
    before = jnp.dot(tri_ref[...], chosen.astype(BF16), preferred_element_type=F32) + run_s[0:1, :]
    p1 = jnp.sum(jnp.where(lanef == i1, before, 0.0), axis=-1, keepdims=True)
    p2 = jnp.sum(jnp.where(lanef == i2, before, 0.0), axis=-1, keepdims=True)
    run_s[...] = run_s[...] + jnp.sum(chosen, axis=0, keepdims=True)
    cnt_ref[...] = run_s[...]

    rt = jnp.zeros_like(lg)
    for k, val in enumerate((i1, i2, w1, w2, p1, p2)):
        rt = jnp.where(lane == k, val, rt)
    rt_ref[0] = rt


def _out_call(x, y1, y2, y3, mod_l, norm_w, w_out, w_router, b_router):
    B, S, D = x.shape
    tm = min(512, S)
    blk = lambda n: pl.BlockSpec((1, tm, n), lambda b, i: (b, i, 0))
    const = lambda shape: pl.BlockSpec(shape, lambda b, i: tuple(0 for _ in shape))
    tri = jnp.asarray(np.tril(np.ones((tm, tm), np.float32), -1), BF16)
    return pl.pallas_call(
        _out_kernel,
        out_shape=[jax.ShapeDtypeStruct((B, S, D), F32), jax.ShapeDtypeStruct((B, S, D), F32),
                   jax.ShapeDtypeStruct((B, S, LANE), F32), jax.ShapeDtypeStruct((SUBLANE, LANE), F32)],
        grid=(B, S // tm),
        in_specs=[blk(D), blk(S5_WIDTH), blk(GDN_WIDTH), blk(SSD_WIDTH),
                  pl.BlockSpec((1, 6, D), lambda b, i: (b, 0, 0)),
                  const((1, D)), const((D, D)), const((D, LANE)), const((1, LANE)), const((tm, tm))],
        out_specs=[blk(D), blk(D), blk(LANE), const((SUBLANE, LANE))],
        scratch_shapes=[pltpu.VMEM((SUBLANE, LANE), F32)],
        compiler_params=_cparams(("arbitrary", "arbitrary")),
        name="out_proj_router",
    )(x, y1, y2, y3, mod_l, norm_w.reshape(1, D), w_out, w_router, b_router, tri)


def _router_params(w_grp, b_grp, w_rt, b_rt):
    D = w_grp.shape[0]
    w = jnp.zeros((D, LANE), F32).at[:, 0:N_EXPERTS].set(w_rt).at[:, GRP_LANE:GRP_LANE + MOE_GROUPS].set(w_grp)
    b = jnp.zeros((1, LANE), F32).at[0, 0:N_EXPERTS].set(b_rt).at[0, GRP_LANE:GRP_LANE + MOE_GROUPS].set(b_grp)
    return w.astype(BF16), b


def _expert_kernel(nused_ref, blke_ref, tok_ref, tokn_ref, dst_ref, h_hbm, wg_ref, wu_ref, wd_ref, y_hbm,
                   xbuf, ybuf, sem_in, sem_out):
    i = pl.program_id(0)
    n_used = nused_ref[0]
    slot = lax.rem(i, 2)

    def row_loop(fn):
        def body(r, carry):
            fn(r)
            return carry
        lax.fori_loop(0, MOE_ROWS, body, 0, unroll=8)

    def gather(idx_ref, s, r, src_row=None):
        src = idx_ref[0, 0, r] if src_row is None else src_row
        return pltpu.make_async_copy(h_hbm.at[pl.ds(src, 1)], xbuf.at[s, pl.ds(r, 1)], sem_in.at[s])

    def scatter(s, r, dst_row=None):
        dst = dst_ref[0, 0, r] if dst_row is None else dst_row
        return pltpu.make_async_copy(ybuf.at[s, pl.ds(r, 1)], y_hbm.at[pl.ds(dst, 1)], sem_out.at[s])

    @pl.when(i < n_used)
    def _():
        @pl.when(i == 0)
        def _():
            row_loop(lambda r: gather(tok_ref, 0, r).start())

        @pl.when(i + 1 < n_used)
        def _():
            row_loop(lambda r: gather(tokn_ref, 1 - slot, r).start())

        row_loop(lambda r: gather(tok_ref, slot, r, 0).wait())

        @pl.when(i >= 2)
        def _():
            row_loop(lambda r: scatter(slot, r, 0).wait())

        xb = xbuf[slot].astype(BF16)
        hid = _silu(jnp.dot(xb, wg_ref[0], preferred_element_type=F32)) * jnp.dot(xb, wu_ref[0], preferred_element_type=F32)
        ybuf[slot] = jnp.dot(hid.astype(BF16), wd_ref[0], preferred_element_type=F32)
        row_loop(lambda r: scatter(slot, r).start())

        @pl.when(i == n_used - 1)
        def _():
            @pl.when(i >= 1)
            def _():
                row_loop(lambda r: scatter(1 - slot, r, 0).wait())
            row_loop(lambda r: scatter(slot, r, 0).wait())


def _expert_call(h2, n_used, blk_e, buf_tok, buf_dst, w_gate, w_up, w_down, n_rows_out):
    N, D = h2.shape
    n_blk = buf_tok.shape[0]
    idx_blk = lambda fn: pl.BlockSpec((1, 1, MOE_ROWS), fn, memory_space=pltpu.SMEM)
    grid_spec = pltpu.PrefetchScalarGridSpec(
        num_scalar_prefetch=2,
        grid=(n_blk,),
        in_specs=[idx_blk(lambda i, nu, be: (i, 0, 0)),
                  idx_blk(lambda i, nu, be: (jnp.minimum(i + 1, n_blk - 1), 0, 0)),
                  idx_blk(lambda i, nu, be: (i, 0, 0)),
                  pl.BlockSpec(memory_space=pl.ANY),
                  pl.BlockSpec((1, D, D_EXPERT), lambda i, nu, be: (be[i], 0, 0)),
                  pl.BlockSpec((1, D, D_EXPERT), lambda i, nu, be: (be[i], 0, 0)),
                  pl.BlockSpec((1, D_EXPERT, D), lambda i, nu, be: (be[i], 0, 0))],
        out_specs=pl.BlockSpec(memory_space=pl.ANY),
        scratch_shapes=[pltpu.VMEM((2, MOE_ROWS, D), F32), pltpu.VMEM((2, MOE_ROWS, D), F32),
                        pltpu.SemaphoreType.DMA((2,)), pltpu.SemaphoreType.DMA((2,))],
    )
    return pl.pallas_call(
        _expert_kernel,
        out_shape=jax.ShapeDtypeStruct((n_rows_out, D), F32),
        grid_spec=grid_spec,
        compiler_params=_cparams(("arbitrary",)),
        name="expert_mlp",
    )(n_used, blk_e, buf_tok, buf_tok, buf_dst, h2, w_gate, w_up, w_down)


def _dispatch(route, counts, N):
    NK = N * TOP_K
    L_pad = NK + N_EXPERTS * MOE_ROWS
    n_blk = L_pad // MOE_ROWS
    eid = route[:, 0:TOP_K].astype(jnp.int32)
    pos = route[:, 4:4 + TOP_K].astype(jnp.int32)
    counts = counts.astype(jnp.int32)
    padded = ((counts + MOE_ROWS - 1) // MOE_ROWS) * MOE_ROWS
    pend = jnp.cumsum(padded)
    pstart = pend - padded
    dest = (pstart[eid] + pos).reshape(NK)
    n_used = (pend[-1] // MOE_ROWS).astype(jnp.int32).reshape(1)
    flat = jnp.arange(NK, dtype=jnp.int32)
    buf_flat = jnp.full((L_pad,), -1, jnp.int32).at[dest].set(flat, unique_indices=True)
    valid = buf_flat >= 0
    tok = buf_flat // TOP_K
    buf_tok = jnp.where(valid, tok, 0)
    spare = NK + jnp.arange(L_pad, dtype=jnp.int32) % MOE_ROWS
    buf_dst = jnp.where(valid, (buf_flat % TOP_K) * N + tok, spare)
    blk_e = jnp.minimum(jnp.searchsorted(pend, jnp.arange(n_blk, dtype=jnp.int32) * MOE_ROWS, side='right'),
                        N_EXPERTS - 1).astype(jnp.int32)
    return n_used, blk_e, buf_tok.reshape(n_blk, 1, MOE_ROWS), buf_dst.reshape(n_blk, 1, MOE_ROWS)


def _combine_kernel(x_ref, ya_ref, yb_ref, rt_ref, mod_ref, nf_ref, o_ref, *, final):
    rt = rt_ref[0]
    y = rt[:, 2:3] * ya_ref[...] + rt[:, 3:4] * yb_ref[...]
    x2 = x_ref[0] + mod_ref[0, 5:6, :] * y
    if final:
        ms = jnp.mean(x2 * x2, axis=-1, keepdims=True)
        x2 = x2 * lax.rsqrt(ms + EPS) * nf_ref[...]
    o_ref[0] = x2


def _combine_call(x1, y2, route, mod_l, norm_final, final):
    B, S, D = x1.shape
    tm = min(512, S)
    blk = lambda n: pl.BlockSpec((1, tm, n), lambda b, i: (b, i, 0))
    nb = S // tm
    return pl.pallas_call(
        functools.partial(_combine_kernel, final=final),
        out_shape=jax.ShapeDtypeStruct((B, S, D), F32),
        grid=(B, nb),
        in_specs=[blk(D),
                  pl.BlockSpec((tm, D), lambda b, i: (b * nb + i, 0)),
                  pl.BlockSpec((tm, D), lambda b, i: (B * nb + b * nb + i, 0)),
                  blk(LANE),
                  pl.BlockSpec((1, 6, D), lambda b, i: (b, 0, 0)),
                  pl.BlockSpec((1, D), lambda b, i: (0, 0))],
        out_specs=blk(D),
        compiler_params=_cparams(("arbitrary", "arbitrary")),
        name="moe_combine",
    )(x1, y2, y2, route, mod_l, norm_final.reshape(1, D))


def _layer(x, mod_l, p, final, norm_final):
    B, S, D = x.shape
    N = B * S
    s5_u, g_qkv, g_z, s_z, s_xbc, small = _proj_call(x, mod_l, p["norm_mix"], _arrange_w_in(p["w_in"]))
    bm, cm, lam_rows = _s5_params(p["s5_a_re"], p["s5_a_im"], p["s5_b_re"], p["s5_b_im"],
                                  p["s5_c_re"], p["s5_c_im"], p["s5_log_dt"])
    y_s5 = _s5_call(s5_u, bm, cm, lam_rows, p["s5_d"], p["s5_w_glu"], p["s5_norm"])
    y_gdn = _gdn_call(g_qkv, g_z, small, p["gdn_conv_w"], p["gdn_a_log"], p["gdn_dt_bias"], p["gdn_norm"])
    y_ssd = _ssd_call(s_xbc, s_z, small, p["ssd_conv_w"], p["ssd_conv_b"], p["ssd_a_log"], p["ssd_dt_bias"],
                      p["ssd_d"], p["ssd_norm"])
    w_router, b_router = _router_params(p["moe_w_grp"], p["moe_b_grp"], p["moe_w_rt"], p["moe_b_rt"])
    x1, h2, route, counts = _out_call(x, y_s5, y_gdn, y_ssd, mod_l, p["norm_ffn"], p["w_out"].astype(BF16),
                                      w_router, b_router)
    n_used, blk_e, buf_tok, buf_dst = _dispatch(route.reshape(N, LANE), counts[0, 0:N_EXPERTS], N)
    y2 = _expert_call(h2.reshape(N, D), n_used, blk_e, buf_tok, buf_dst,
                      p["moe_w_gate"].astype(BF16), p["moe_w_up"].astype(BF16), p["moe_w_down"].astype(BF16),
                      N * TOP_K + MOE_ROWS)
    return _combine_call(x1, y2, route, mod_l, norm_final, final)


def kernel(x, c, w_ada, b_ada, norm_mix, norm_ffn, w_in, w_out, s5_a_re, s5_a_im, s5_b_re, s5_b_im, s5_c_re, s5_c_im, s5_d, s5_log_dt, s5_w_glu, s5_norm, gdn_conv_w, gdn_a_log, gdn_dt_bias, gdn_norm, ssd_conv_w, ssd_conv_b, ssd_a_log, ssd_dt_bias, ssd_d, ssd_norm, moe_w_grp, moe_b_grp, moe_w_rt, moe_b_rt, moe_w_gate, moe_w_up, moe_w_down, norm_final):
    stacked = dict(norm_mix=norm_mix, norm_ffn=norm_ffn, w_in=w_in, w_out=w_out, s5_a_re=s5_a_re, s5_a_im=s5_a_im,
                   s5_b_re=s5_b_re, s5_b_im=s5_b_im, s5_c_re=s5_c_re, s5_c_im=s5_c_im, s5_d=s5_d,
                   s5_log_dt=s5_log_dt, s5_w_glu=s5_w_glu, s5_norm=s5_norm, gdn_conv_w=gdn_conv_w,
                   gdn_a_log=gdn_a_log, gdn_dt_bias=gdn_dt_bias, gdn_norm=gdn_norm, ssd_conv_w=ssd_conv_w,
                   ssd_conv_b=ssd_conv_b, ssd_a_log=ssd_a_log, ssd_dt_bias=ssd_dt_bias, ssd_d=ssd_d,
                   ssd_norm=ssd_norm, moe_w_grp=moe_w_grp, moe_b_grp=moe_b_grp, moe_w_rt=moe_w_rt,
                   moe_b_rt=moe_b_rt, moe_w_gate=moe_w_gate, moe_w_up=moe_w_up, moe_w_down=moe_w_down)
    L = w_in.shape[0]
    B, S, D = x.shape
    mod = _mod_call(c, w_ada, b_ada).reshape(L, B, 6, D)
    for l in range(L):
        p = {k: v[l] for k, v in stacked.items()}
        x = _layer(x, mod[l], p, l == L - 1, norm_final)
    return x
```

```python
import functools

import numpy as np
import jax
import jax.numpy as jnp
from jax import lax
from jax.experimental import pallas as pl
from jax.experimental.pallas import tpu as pltpu

F32 = jnp.float32
BF16 = jnp.bfloat16

D_MODEL = 1024
DEPTH = 4
EPS = 1e-6
CONV_WIDTH = 4
CHUNK = 64
S5_WIDTH = 256
S5_CH = 16
S5_GROUPS = 16
S5_STATE = 64
S5_LANES = S5_GROUPS * S5_STATE
GDN_WIDTH = 384
GDN_HEAD_DIM = 64
GDN_HEADS = 6
GDN_CONV_DIM = 3 * GDN_WIDTH
SSD_WIDTH = 384
SSD_HEAD_DIM = 64
SSD_HEADS = 6
SSD_GROUPS = 2
SSD_STATE = 128
SSD_GROUP_WIDTH = SSD_WIDTH // SSD_GROUPS
SSD_CONV_DIM = SSD_WIDTH + 2 * SSD_GROUPS * SSD_STATE
PROJ_SIZES = (S5_WIDTH, GDN_CONV_DIM, GDN_WIDTH, GDN_HEADS, GDN_HEADS, SSD_WIDTH, SSD_CONV_DIM, SSD_HEADS)
MOE_GROUPS = 4
EXPERTS_PER_GROUP = 8
N_EXPERTS = 32
TOP_K = 2
D_EXPERT = 256

LANE = 128
SUBLANE = 8
SMALL_W = LANE
A_LANE, B_LANE, DT_LANE = 0, 8, 16
GRP_LANE = N_EXPERTS
MOE_ROWS = 256
MOE_TOKEN_TILE = 256
GDN_CHUNKS_PER_ITER = 2
VMEM_LIMIT = 56 * 1024 * 1024
NEG_BIG = -1e30


def _cparams(sem):
    return pltpu.CompilerParams(dimension_semantics=sem, vmem_limit_bytes=VMEM_LIMIT)


def _split(a):
    hi = a.astype(BF16)
    lo = (a - hi.astype(F32)).astype(BF16)
    return hi, lo


_NN = (((1,), (0,)), ((), ()))
_NT = (((1,), (1,)), ((), ()))
_TN = (((0,), (0,)), ((), ()))


def _dot(a, b, dims=_NN):
    return lax.dot_general(a.astype(BF16), b.astype(BF16), dims, preferred_element_type=F32)


def _dot2(a, b_bf16):
    hi, lo = _split(a)
    return (lax.dot_general(hi, b_bf16, _NN, preferred_element_type=F32)
            + lax.dot_general(lo, b_bf16, _NN, preferred_element_type=F32))


def _dot3(a, b):
    ah, al = _split(a)
    bh, bl = _split(b)
    d = functools.partial(lax.dot_general, dimension_numbers=_NN, preferred_element_type=F32)
    return d(ah, bh) + (d(ah, bl) + d(al, bh))


def _silu(x):
    return x * jax.nn.sigmoid(x)


def _softplus(x):
    return jnp.maximum(x, 0.0) + jnp.log(1.0 + jnp.exp(-jnp.abs(x)))


def _norm_mod(x, w, scale, shift):
    ms = jnp.mean(x * x, axis=-1, keepdims=True)
    return (x * lax.rsqrt(ms + EPS) * w) * (1.0 + scale) + shift


def _mod_kernel(c_ref, w_ref, b_ref, o_ref):
    cond = _silu(c_ref[...])
    o_ref[0] = _dot3(cond, w_ref[0]) + b_ref[0]


def _mod_call(c, w_ada, b_ada):
    L, D, W = w_ada.shape
    B = c.shape[0]
    tn = 1536
    return pl.pallas_call(
        _mod_kernel,
        out_shape=jax.ShapeDtypeStruct((L, B, W), F32),
        grid=(L, W // tn),
        in_specs=[pl.BlockSpec((B, D), lambda l, j: (0, 0)),
                  pl.BlockSpec((1, D, tn), lambda l, j: (l, 0, j)),
                  pl.BlockSpec((1, 1, tn), lambda l, j: (l, 0, j))],
        out_specs=pl.BlockSpec((1, B, tn), lambda l, j: (l, 0, j)),
        compiler_params=_cparams(("arbitrary", "arbitrary")),
        name="adaln_mod",
    )(c, w_ada, b_ada.reshape(L, 1, W))


PROJ_OUT_W = (S5_WIDTH, GDN_CONV_DIM, GDN_WIDTH, SSD_WIDTH, SSD_CONV_DIM, SMALL_W)


def _proj_kernel(x_ref, mod_ref, nw_ref, w_ref, *o_refs):
    x = x_ref[0]
    h = _norm_mod(x, nw_ref[...], mod_ref[0, 1:2, :], mod_ref[0, 0:1, :]).astype(BF16)
    off = 0
    for o_ref in o_refs:
        n = o_ref.shape[-1]
        o_ref[0] = jnp.dot(h, w_ref[:, off:off + n], preferred_element_type=F32)
        off += n


def _arrange_w_in(w_in_l):
    splits = np.cumsum(PROJ_SIZES)[:-1].tolist()
    s5_u, g_qkv, g_z, g_a, g_b, s_z, s_xbc, s_dt = jnp.split(w_in_l, splits, axis=-1)
    D = w_in_l.shape[0]
    small = jnp.zeros((D, SMALL_W), w_in_l.dtype)
    small = small.at[:, A_LANE:A_LANE + GDN_HEADS].set(g_a)
    small = small.at[:, B_LANE:B_LANE + GDN_HEADS].set(g_b)
    small = small.at[:, DT_LANE:DT_LANE + SSD_HEADS].set(s_dt)
    return jnp.concatenate([s5_u, g_qkv, g_z, s_z, s_xbc, small], axis=-1).astype(BF16)


def _proj_call(x, mod_l, norm_w, w_arr):
    B, S, D = x.shape
    tm = min(512, S)
    W = w_arr.shape[1]
    out_shape = [jax.ShapeDtypeStruct((B, S, n), F32) for n in PROJ_OUT_W]
    out_specs = [pl.BlockSpec((1, tm, n), lambda b, i: (b, i, 0)) for n in PROJ_OUT_W]
    return pl.pallas_call(
        _proj_kernel,
        out_shape=out_shape,
        grid=(B, S // tm),
        in_specs=[pl.BlockSpec((1, tm, D), lambda b, i: (b, i, 0)),
                  pl.BlockSpec((1, 6, D), lambda b, i: (b, 0, 0)),
                  pl.BlockSpec((1, D), lambda b, i: (0, 0)),
                  pl.BlockSpec((D, W), lambda b, i: (0, 0))],
        out_specs=out_specs,
        compiler_params=_cparams(("arbitrary", "arbitrary")),
        name="norm_in_proj",
    )(x, mod_l, norm_w.reshape(1, D), w_arr)


def _s5_kernel(u_ref, bm_ref, cm_ref, lam_ref, dsk_ref, glu_ref, nw_ref, perm_ref, o_ref,
               x_s, st_s, ubt_s, u_s, *, T):
    B = u_ref.shape[0]
    P = S5_LANES

    @pl.when(pl.program_id(0) == 0)
    def _():
        st_s[...] = jnp.zeros_like(st_s)

    nw = S5_WIDTH // LANE
    ubt = u_ref[...].reshape(B * T, S5_WIDTH)
    for j in range(nw):
        ubt_s[j] = ubt[:, j * LANE:(j + 1) * LANE]

    def regroup(t, carry):
        for j in range(nw):
            u_s[pl.ds(pl.multiple_of(t * B, B), B), j * LANE:(j + 1) * LANE] = ubt_s[j, pl.ds(t, B, stride=T), :]
        return carry

    lax.fori_loop(0, T, regroup, 0, unroll=8)
    u = u_s[...]
    x_s[...] = jnp.dot(u.astype(BF16), bm_ref[...], preferred_element_type=F32)
    lr = jnp.broadcast_to(lam_ref[0:1, :], (B, P))
    li = jnp.broadcast_to(lam_ref[1:2, :], (B, P))

    def step(t, carry):
        sr, si = carry
        rows = pl.ds(pl.multiple_of(t * B, B), B)
        nr = lr * sr - li * si + x_s[rows, 0:P]
        ni = lr * si + li * sr + x_s[rows, P:2 * P]
        x_s[rows, 0:P] = nr
        x_s[rows, P:2 * P] = ni
        return nr, ni

    sr, si = lax.fori_loop(0, T, step, (st_s[:, 0:P], st_s[:, P:2 * P]), unroll=4)
    st_s[:, 0:P] = sr
    st_s[:, P:2 * P] = si

    y = jnp.dot(x_s[...].astype(BF16), cm_ref[...], preferred_element_type=F32) + dsk_ref[...] * u
    y = jax.nn.gelu(y)
    y = y * jax.nn.sigmoid(jnp.dot(y.astype(BF16), glu_ref[...], preferred_element_type=F32))
    ms = jnp.mean(y * y, axis=-1, keepdims=True)
    y = (y * lax.rsqrt(ms + EPS) * nw_ref[...]).astype(BF16)
    y = jnp.dot(perm_ref[...], y, preferred_element_type=F32)
    o_ref[...] = y.reshape(B, T, S5_WIDTH).astype(o_ref.dtype)


def _s5_params(a_re, a_im, b_re, b_im, c_re, c_im, log_dt):
    G, P, CH = S5_GROUPS, S5_STATE, S5_CH
    lam = lax.complex(a_re.astype(F32), a_im.astype(F32))
    step = jnp.exp(log_dt.astype(F32))[:, None]
    lam_bar = jnp.exp(lam * step)
    b_bar = ((lam_bar - 1.0) / lam)[..., None] * lax.complex(b_re.astype(F32), b_im.astype(F32))
    eye = jnp.eye(G, dtype=F32)
    bre = jnp.einsum('gpc,gh->gchp', b_bar.real, eye).reshape(G * CH, G * P)
    bim = jnp.einsum('gpc,gh->gchp', b_bar.imag, eye).reshape(G * CH, G * P)
    bm = jnp.concatenate([bre, bim], axis=1).astype(BF16)
    cre = jnp.einsum('gcp,gh->gphc', c_re.astype(F32), eye).reshape(G * P, G * CH)
    cim = jnp.einsum('gcp,gh->gphc', c_im.astype(F32), eye).reshape(G * P, G * CH)
    cm = jnp.concatenate([cre, -cim], axis=0).astype(BF16)
    lam_rows = jnp.zeros((SUBLANE, G * P), F32)
    lam_rows = lam_rows.at[0].set(lam_bar.real.reshape(-1)).at[1].set(lam_bar.imag.reshape(-1))
    return bm, cm, lam_rows


def _s5_call(u, bm, cm, lam_rows, d_skip, w_glu, norm_w):
    B, S, W = u.shape
    T = min(128, S)
    P2 = 2 * S5_LANES
    const = lambda shape: pl.BlockSpec(shape, lambda i: tuple(0 for _ in shape))
    r = np.arange(B * T)
    perm = np.zeros((B * T, B * T), np.float32)
    perm[r, (r % T) * B + r // T] = 1.0
    return pl.pallas_call(
        functools.partial(_s5_kernel, T=T),
        out_shape=jax.ShapeDtypeStruct((B, S, W), BF16),
        grid=(S // T,),
        in_specs=[pl.BlockSpec((B, T, W), lambda i: (0, i, 0)),
                  const((W, P2)), const((P2, W)), const((SUBLANE, S5_LANES)),
                  const((1, W)), const((W, W)), const((1, W)), const((B * T, B * T))],
        out_specs=pl.BlockSpec((B, T, W), lambda i: (0, i, 0)),
        scratch_shapes=[pltpu.VMEM((B * T, P2), F32), pltpu.VMEM((B, P2), F32),
                        pltpu.VMEM((W // LANE, B * T, LANE), F32), pltpu.VMEM((B * T, W), F32)],
        compiler_params=_cparams(("arbitrary",)),
        name="s5_mixer",
    )(u, bm, cm, lam_rows, d_skip.reshape(1, W).astype(F32), w_glu.astype(BF16), norm_w.reshape(1, W).astype(F32),
      jnp.asarray(perm, BF16))


def _causal_conv_silu(x, tail_ref, cw_ref, bias):
    n = x.shape[0]
    xf = jnp.concatenate([tail_ref[...], x], axis=0)
    acc = x * cw_ref[CONV_WIDTH - 1:CONV_WIDTH, :]
    for k in range(1, CONV_WIDTH):
        acc = acc + pltpu.roll(xf, k, axis=0)[SUBLANE:, :] * cw_ref[CONV_WIDTH - 1 - k:CONV_WIDTH - k, :]
    tail_ref[...] = x[n - SUBLANE:, :]
    if bias is not None:
        acc = acc + bias
    return _silu(acc)


def _store_row_tiles(ref, val):
    n = val.shape[0]
    for j in range(val.shape[1] // LANE):
        ref[pl.ds(j, n, stride=SUBLANE), :] = val[:, j * LANE:(j + 1) * LANE]


def _load_row_tiles(ref, n):
    return jnp.concatenate([ref[pl.ds(j, n, stride=SUBLANE), :] for j in range(SUBLANE)], axis=1)


def _lanes_from(sm, off):
    return pltpu.roll(sm, SMALL_W - off, axis=1) if off else sm


def _chunk_scans(g):
    n = g.shape[0]
    rin = lax.broadcasted_iota(jnp.int32, g.shape, 0) & (CHUNK - 1)
    pre = g
    suf = jnp.where(rin < CHUNK - 1, pltpu.roll(g, n - 1, axis=0), 0.0)
    s = 1
    while s < CHUNK:
        pre = pre + jnp.where(rin >= s, pltpu.roll(pre, s, axis=0), 0.0)
        suf = suf + jnp.where(rin + s <= CHUNK - 1, pltpu.roll(suf, n - s, axis=0), 0.0)
        s *= 2
    return pre, suf


def _tri_masks():
    r = lax.broadcasted_iota(jnp.int32, (CHUNK, CHUNK), 0)
    c = lax.broadcasted_iota(jnp.int32, (CHUNK, CHUNK), 1)
    return r >= c, r > c, r == c


def _head_expand():
    m = np.zeros((SMALL_W, GDN_WIDTH), np.float32)
    for h in range(GDN_HEADS):
        m[h, h * GDN_HEAD_DIM:(h + 1) * GDN_HEAD_DIM] = 1.0
    return jnp.asarray(m, BF16)


def _block_ones(width, blk):
    idx = np.arange(width) // blk
    return jnp.asarray((idx[:, None] == idx[None, :]).astype(np.float32), BF16)


def _gdn_kernel(qkv_ref, z_ref, sm_ref, cw_ref, hp_ref, nw_ref, e_ref, xp_ref, o_ref,
                tail_s, st_s, kn_s, kb_s, qn_s, qd_s, kd_s, rhs_s, eg_s, gc_s, gct_s, o_s, at_s, *, TB):
    H, Dh, C = GDN_HEADS, GDN_HEAD_DIM, CHUNK
    W = GDN_WIDTH
    ncb = TB // C

    @pl.when(pl.program_id(1) == 0)
    def _():
        tail_s[...] = jnp.zeros_like(tail_s)
        st_s[...] = jnp.zeros_like(st_s)

    xc = _causal_conv_silu(qkv_ref[0], tail_s, cw_ref, None)
    q, k, v = xc[:, 0:W], xc[:, W:2 * W], xc[:, 2 * W:3 * W]
    e = e_ref[...]
    xp = xp_ref[...]
    qn = q * lax.rsqrt(_dot2(q * q, e) + EPS) * (Dh ** -0.5)
    kn = k * lax.rsqrt(_dot2(k * k, e) + EPS)

    sm = sm_ref[0]
    lane = lax.broadcasted_iota(jnp.int32, sm.shape, 1)
    head_lane = lane < H
    g = jnp.where(head_lane, hp_ref[0:1, :] * _softplus(_lanes_from(sm, A_LANE) + hp_ref[1:2, :]), 0.0)
    beta = jnp.where(head_lane, jax.nn.sigmoid(_lanes_from(sm, B_LANE)), 0.0)
    gc, rc = _chunk_scans(g)
    bx = _dot2(beta, xp)
    egx = _dot2(jnp.exp(gc), xp)
    erx = _dot2(jnp.exp(rc), xp)
    kb = kn * bx
    kn_s[...] = kn
    kb_s[...] = kb
    qn_s[...] = qn
    qd_s[...] = qn * egx
    kd_s[...] = kn * erx
    for h in range(H):
        sl = slice(h * Dh, (h + 1) * Dh)
        rhs_s[:, 2 * h * Dh:(2 * h + 1) * Dh] = v[:, sl] * bx[:, sl]
        rhs_s[:, (2 * h + 1) * Dh:(2 * h + 2) * Dh] = kb[:, sl] * egx[:, sl]
    gc_s[...] = gc
    for c in range(ncb):
        gct_s[c] = gc[c * C:(c + 1) * C, :].T
        eg_s[c] = jnp.broadcast_to(egx[(c + 1) * C - 1:(c + 1) * C, :], (SUBLANE, W))

    causal, strict, diag = _tri_masks()
    eye = jnp.where(diag, 1.0, 0.0).astype(F32)

    heads = [slice(h * Dh, (h + 1) * Dh) for h in range(H)]

    def solve(it, carry):
        probs = []
        for cc in range(GDN_CHUNKS_PER_ITER):
            c = it * GDN_CHUNKS_PER_ITER + cc
            rows = pl.ds(pl.multiple_of(c * C, C), C)
            gcol = gc_s[rows, :]
            grow = gct_s[c]
            probs += [(rows, h, sl, gcol, grow) for h, sl in enumerate(heads)]
        decs = [jnp.exp(jnp.where(causal, gcol[:, h:h + 1] - grow[h:h + 1, :], NEG_BIG))
                for rows, h, sl, gcol, grow in probs]
        kns = [kn_s[rows, sl] for rows, h, sl, _, _ in probs]
        lows = [jnp.where(strict, _dot(kb_s[rows, sl], kn, _NT) * dec, 0.0)
                for (rows, h, sl, _, _), kn, dec in zip(probs, kns, decs)]
        for (rows, h, sl, _, _), kn, dec in zip(probs, kns, decs):
            at_s[rows, sl] = _dot(qn_s[rows, sl], kn, _NT) * dec
        ps = [-low for low in lows]
        invs = [eye + p for p in ps]
        for _ in range(5):
            ps = [_dot3(p, p) for p in ps]
            invs = [inv + _dot3(inv, p) for inv, p in zip(invs, ps)]
        for (rows, h, sl, _, _), inv in zip(probs, invs):
            cols = slice(2 * h * Dh, (2 * h + 2) * Dh)
            rhs_s[rows, cols] = _dot3(inv, rhs_s[rows, cols])
        return carry

    lax.fori_loop(0, ncb // GDN_CHUNKS_PER_ITER, solve, 0)

    def recur(c, carry):
        rows = pl.ds(pl.multiple_of(c * C, C), C)
        sts = [st_s[h] for h in range(H)]
        uws = [rhs_s[rows, 2 * h * Dh:(2 * h + 2) * Dh] for h in range(H)]
        v_news = [uw[:, 0:Dh] - _dot(uw[:, Dh:2 * Dh], st) for uw, st in zip(uws, sts)]
        for sl, st, v_new in zip(heads, sts, v_news):
            o_s[rows, sl] = _dot(qd_s[rows, sl], st) + _dot(at_s[rows, sl], v_new)
        for h, (sl, st, v_new) in enumerate(zip(heads, sts, v_news)):
            st_s[h] = st * eg_s[c, 0:1, sl] + _dot(kd_s[rows, sl], v_new, _TN)
        return carry

    lax.fori_loop(0, ncb, recur, 0)

    o = o_s[...]
    ms = _dot2(o * o, e) * (1.0 / Dh)
    o = o * lax.rsqrt(ms + EPS) * nw_ref[...]
    o_ref[0] = (o * _silu(z_ref[0])).astype(o_ref.dtype)


def _gdn_call(qkv, z, sm, conv_w, a_log, dt_bias, norm_w):
    B, S, _ = qkv.shape
    TB = min(512, S)
    W, H = GDN_WIDTH, GDN_HEADS
    hp = jnp.zeros((SUBLANE, SMALL_W), F32)
    hp = hp.at[0, :H].set(-jnp.exp(a_log.astype(F32))).at[1, :H].set(dt_bias.astype(F32))
    nw = jnp.tile(norm_w.astype(F32), H).reshape(1, W)
    const = lambda shape: pl.BlockSpec(shape, lambda b, i: tuple(0 for _ in shape))
    blk = lambda n: pl.BlockSpec((1, TB, n), lambda b, i: (b, i, 0))
    f = lambda *shape: pltpu.VMEM(shape, F32)
    return pl.pallas_call(
        functools.partial(_gdn_kernel, TB=TB),
        out_shape=jax.ShapeDtypeStruct((B, S, W), BF16),
        grid=(B, S // TB),
        in_specs=[blk(GDN_CONV_DIM), blk(W), blk(SMALL_W),
                  const((CONV_WIDTH, GDN_CONV_DIM)), const((SUBLANE, SMALL_W)), const((1, W)),
                  const((W, W)), const((SMALL_W, W))],
        out_specs=blk(W),
        scratch_shapes=[f(SUBLANE, GDN_CONV_DIM), f(H, GDN_HEAD_DIM, GDN_HEAD_DIM),
                        f(TB, W), f(TB, W), f(TB, W), f(TB, W), f(TB, W), f(TB, 2 * W), f(TB // CHUNK, SUBLANE, W),
                        f(TB, SMALL_W), f(TB // CHUNK, SMALL_W, CHUNK), f(TB, W), f(TB, W)],
        compiler_params=_cparams(("arbitrary", "arbitrary")),
        name="gdn_mixer",
    )(qkv, z, sm, conv_w.astype(F32), hp, nw, _block_ones(W, GDN_HEAD_DIM), _head_expand())


def _ssd_kernel(xbc_ref, z_ref, sm_ref, cw_ref, cb_ref, hp_ref, dsk_ref, nw_ref, e_ref, xp_ref, o_ref,
                tail_s, st_s, xs_s, xdt_s, xdd_s, bm_s, cm_s, ea_s, el_s, ac_s, act_s, y_s, *, TB):
    H, P, G, N, C = SSD_HEADS, SSD_HEAD_DIM, SSD_GROUPS, SSD_STATE, CHUNK
    R = H // G
    W, GW = SSD_WIDTH, SSD_GROUP_WIDTH
    ncb = TB // C

    @pl.when(pl.program_id(1) == 0)
    def _():
        tail_s[...] = jnp.zeros_like(tail_s)
        st_s[...] = jnp.zeros_like(st_s)

    xc = _causal_conv_silu(xbc_ref[0], tail_s, cw_ref, cb_ref[...])
    xs = xc[:, 0:W]
    xp = xp_ref[...]

    sm = sm_ref[0]
    lane = lax.broadcasted_iota(jnp.int32, sm.shape, 1)
    head_lane = lane < H
    dt = jnp.where(head_lane, _softplus(_lanes_from(sm, DT_LANE) + hp_ref[1:2, :]), 0.0)
    acs, rcs = _chunk_scans(dt * hp_ref[0:1, :])
    dtx = _dot2(dt, xp)
    eax = _dot2(jnp.exp(acs), xp)
    erx = _dot2(jnp.exp(rcs), xp)
    xdt = xs * dtx
    xs_s[...] = xs
    xdt_s[...] = xdt
    xdd_s[...] = xdt * erx
    bm_s[...] = xc[:, W:W + G * N]
    cm_s[...] = xc[:, W + G * N:W + 2 * G * N]
    ea_s[...] = eax
    ac_s[...] = acs
    for c in range(ncb):
        act_s[c] = acs[c * C:(c + 1) * C, :].T
        el_s[c] = jnp.broadcast_to(eax[(c + 1) * C - 1:(c + 1) * C, :], (SUBLANE, W))

    causal, _, _ = _tri_masks()

    def chunk(c, carry):
        r0 = pl.multiple_of(c * C, C)
        rows = pl.ds(r0, C)
        acol = ac_s[rows, :]
        arow = act_s[c]
        for gi in range(G):
            gsl = slice(gi * GW, (gi + 1) * GW)
            b_g = bm_s[rows, gi * N:(gi + 1) * N]
            c_g = cm_s[rows, gi * N:(gi + 1) * N]
            cb = _dot(c_g, b_g, _NT)
            st = st_s[gi]
            y_off = _dot(c_g, st) * ea_s[rows, gsl]
            for r in range(R):
                h = gi * R + r
                sl = slice(h * P, (h + 1) * P)
                seg = jnp.exp(jnp.where(causal, acol[:, h:h + 1] - arow[h:h + 1, :], NEG_BIG))
                y_s[rows, sl] = _dot(cb * seg, xdt_s[rows, sl]) + y_off[:, r * P:(r + 1) * P]
            st_s[gi] = st * el_s[c, 0:1, gsl] + _dot(b_g, xdd_s[rows, gsl], _TN)
        return carry

    lax.fori_loop(0, ncb, chunk, 0)

    y = y_s[...] + dsk_ref[...] * xs_s[...]
    y = y * _silu(z_ref[0])
    ms = _dot2(y * y, e_ref[...]) * (1.0 / GW)
    o_ref[0] = (y * lax.rsqrt(ms + EPS) * nw_ref[...]).astype(o_ref.dtype)


def _ssd_call(xbc, z, sm, conv_w, conv_b, a_log, dt_bias, d_skip, norm_w):
    B, S, _ = xbc.shape
    TB = min(512, S)
    W, H, G, N = SSD_WIDTH, SSD_HEADS, SSD_GROUPS, SSD_STATE
    hp = jnp.zeros((SUBLANE, SMALL_W), F32)
    hp = hp.at[0, :H].set(-jnp.exp(a_log.astype(F32))).at[1, :H].set(dt_bias.astype(F32))
    dsk = jnp.repeat(d_skip.astype(F32), SSD_HEAD_DIM).reshape(1, W)
    const = lambda shape: pl.BlockSpec(shape, lambda b, i: tuple(0 for _ in shape))
    blk = lambda n: pl.BlockSpec((1, TB, n), lambda b, i: (b, i, 0))
    f = lambda *shape: pltpu.VMEM(shape, F32)
    return pl.pallas_call(
        functools.partial(_ssd_kernel, TB=TB),
        out_shape=jax.ShapeDtypeStruct((B, S, W), BF16),
        grid=(B, S // TB),
        in_specs=[blk(SSD_CONV_DIM), blk(W), blk(SMALL_W),
                  const((CONV_WIDTH, SSD_CONV_DIM)), const((1, SSD_CONV_DIM)), const((SUBLANE, SMALL_W)),
                  const((1, W)), const((1, W)), const((W, W)), const((SMALL_W, W))],
        out_specs=blk(W),
        scratch_shapes=[f(SUBLANE, SSD_CONV_DIM), f(G, N, SSD_GROUP_WIDTH),
                        f(TB, W), f(TB, W), f(TB, W), f(TB, G * N), f(TB, G * N), f(TB, W),
                        f(TB // CHUNK, SUBLANE, W),
                        f(TB, SMALL_W), f(TB // CHUNK, SMALL_W, CHUNK), f(TB, W)],
        compiler_params=_cparams(("arbitrary", "arbitrary")),
        name="ssd_mixer",
    )(xbc, z, sm, conv_w.astype(F32), conv_b.reshape(1, -1).astype(F32), hp, dsk,
      norm_w.reshape(1, W).astype(F32), _block_ones(W, SSD_GROUP_WIDTH), _head_expand())


def _out_kernel(x_ref, y1_ref, y2_ref, y3_ref, mod_ref, nw_ref, wo_ref, wr_ref, br_ref, tri_ref,
                x1_ref, h_ref, rt_ref, cnt_ref, run_s):
    y = (jnp.dot(y1_ref[0], wo_ref[0:S5_WIDTH, :], preferred_element_type=F32)
         + jnp.dot(y2_ref[0], wo_ref[S5_WIDTH:S5_WIDTH + GDN_WIDTH, :], preferred_element_type=F32)
         + jnp.dot(y3_ref[0], wo_ref[S5_WIDTH + GDN_WIDTH:, :], preferred_element_type=F32))
    x1 = x_ref[0] + mod_ref[0, 2:3, :] * y
    x1_ref[0] = x1
    h = _norm_mod(x1, nw_ref[...], mod_ref[0, 4:5, :], mod_ref[0, 3:4, :])
    _store_row_tiles(h_ref, h)
    lg =jnp.dot(h.astype(BF16), wr_ref[...], preferred_element_type=F32) + br_ref[...]

    lane = lax.broadcasted_iota(jnp.int32, lg.shape, 1)
    lanef = lane.astype(F32)
    big = float(4 * LANE)
    grp = (lane >= GRP_LANE) & (lane < GRP_LANE + MOE_GROUPS)
    lgm = jnp.where(grp, lg, -jnp.inf)
    m = jnp.max(lgm, axis=-1, keepdims=True)
    gidx = jnp.min(jnp.where(lgm == m, lanef - GRP_LANE, big), axis=-1, keepdims=True)
    g_w = 1.0 / jnp.sum(jnp.where(grp, jnp.exp(lg - m), 0.0), axis=-1, keepdims=True)
    in_grp = (lane < N_EXPERTS) & ((lane // EXPERTS_PER_GROUP).astype(F32) == gidx)
    le = jnp.where(in_grp, lg, -jnp.inf)
    v1 = jnp.max(le, axis=-1, keepdims=True)
    i1 = jnp.min(jnp.where(le == v1, lanef, big), axis=-1, keepdims=True)
    le2 = jnp.where(lanef == i1, -jnp.inf, le)
    v2 = jnp.max(le2, axis=-1, keepdims=True)
    i2 = jnp.min(jnp.where(le2 == v2, lanef, big), axis=-1, keepdims=True)
    e2 = jnp.exp(v2 - v1)
    w1 = g_w / (1.0 + e2)
    w2 = g_w * e2 / (1.0 + e2)

    @pl.when((pl.program_id(0) == 0) & (pl.program_id(1) == 0))
    def _():
        run_s[...] = jnp.zeros_like(run_s)

    chosen = jnp.where((lanef == i1) | (lanef == i2), 1.0, 0.0)
    before = jnp.dot(tri_ref[...], chosen.astype(BF16), preferred_element_type=F32) + run_s[0:1, :]
    p1 = jnp.sum(jnp.where(lanef == i1, before, 0.0), axis=-1, keepdims=True)
    p2 = jnp.sum(jnp.where(lanef == i2, before, 0.0), axis=-1, keepdims=True)
    run_s[...] = run_s[...] + jnp.sum(chosen, axis=0, keepdims=True)
    cnt_ref[...] = run_s[...]

    rt = jnp.zeros_like(lg)
    for k, val in enumerate((i1, i2, w1, w2, p1, p2)):
        rt = jnp.where(lane == k, val, rt)
    rt_ref[0] = rt


def _out_call(x, y1, y2, y3, mod_l, norm_w, w_out, w_router, b_router):
    B, S, D = x.shape
    tm = min(512, S)
    blk = lambda n: pl.BlockSpec((1, tm, n), lambda b, i: (b, i, 0))
    const = lambda shape: pl.BlockSpec(shape, lambda b, i: tuple(0 for _ in shape))
    tri = jnp.asarray(np.tril(np.ones((tm, tm), np.float32), -1), BF16)
    nb = S // tm
    tiles = D // LANE
    return pl.pallas_call(
        _out_kernel,
        out_shape=[jax.ShapeDtypeStruct((B, S, D), F32), jax.ShapeDtypeStruct((B * S * tiles, LANE), F32),
                   jax.ShapeDtypeStruct((B, S, LANE), F32), jax.ShapeDtypeStruct((SUBLANE, LANE), F32)],
        grid=(B, nb),
        in_specs=[blk(D), blk(S5_WIDTH), blk(GDN_WIDTH), blk(SSD_WIDTH),
                  pl.BlockSpec((1, 6, D), lambda b, i: (b, 0, 0)),
                  const((1, D)), const((D, D)), const((D, LANE)), const((1, LANE)), const((tm, tm))],
        out_specs=[blk(D), pl.BlockSpec((tm * tiles, LANE), lambda b, i: (b * nb + i, 0)),
                   blk(LANE), const((SUBLANE, LANE))],
        scratch_shapes=[pltpu.VMEM((SUBLANE, LANE), F32)],
        compiler_params=_cparams(("arbitrary", "arbitrary")),
        name="out_proj_router",
    )(x, y1, y2, y3, mod_l, norm_w.reshape(1, D), w_out, w_router, b_router, tri)


def _router_params(w_grp, b_grp, w_rt, b_rt):
    D = w_grp.shape[0]
    w = jnp.zeros((D, LANE), F32).at[:, 0:N_EXPERTS].set(w_rt).at[:, GRP_LANE:GRP_LANE + MOE_GROUPS].set(w_grp)
    b = jnp.zeros((1, LANE), F32).at[0, 0:N_EXPERTS].set(b_rt).at[0, GRP_LANE:GRP_LANE + MOE_GROUPS].set(b_grp)
    return w.astype(BF16), b


def _tile_copy_loop(n, fn):
    def body(t, carry):
        fn(t)
        return carry
    lax.fori_loop(0, n, body, 0, unroll=8)


def _scatter_kernel(d0_ref, d1_ref, h_ref, xs_hbm, stage, sem):
    i = pl.program_id(0)
    n = pl.num_programs(0)
    slot = lax.rem(i, 2)
    tm = d0_ref.shape[-1]

    def copy(s, t, dst):
        return pltpu.make_async_copy(stage.at[s, pl.ds(t * SUBLANE, SUBLANE)],
                                     xs_hbm.at[pl.ds(dst * SUBLANE, SUBLANE)], sem.at[s])

    def wait_slot(s):
        _tile_copy_loop(TOP_K * tm, lambda t: copy(s, 0, 0).wait())

    @pl.when(i >= 2)
    def _():
        wait_slot(slot)

    stage[slot] = h_ref[...]

    def start(t):
        copy(slot, t, d0_ref[0, 0, t]).start()
        copy(slot, t, d1_ref[0, 0, t]).start()
    _tile_copy_loop(tm, start)

    @pl.when(i == n - 1)
    def _():
        @pl.when(i >= 1)
        def _():
            wait_slot(1 - slot)
        wait_slot(slot)


def _scatter_call(h2t, dest0, dest1, n_rows):
    n_tiles, _, tm = dest0.shape
    idx_blk = pl.BlockSpec((1, 1, tm), lambda i: (i, 0, 0), memory_space=pltpu.SMEM)
    return pl.pallas_call(
        _scatter_kernel,
        out_shape=jax.ShapeDtypeStruct((n_rows * SUBLANE, LANE), F32),
        grid=(n_tiles,),
        in_specs=[idx_blk, idx_blk, pl.BlockSpec((tm * SUBLANE, LANE), lambda i: (i, 0))],
        out_specs=pl.BlockSpec(memory_space=pl.ANY),
        scratch_shapes=[pltpu.VMEM((2, tm * SUBLANE, LANE), F32), pltpu.SemaphoreType.DMA((2,))],
        compiler_params=_cparams(("arbitrary",)),
        name="moe_scatter",
    )(dest0, dest1, h2t)


def _expert_kernel(nused_ref, blke_ref, nvalid_ref, xs_ref, wg_ref, wu_ref, wd_ref, ys_ref):
    i = pl.program_id(0)

    @pl.when(i < nused_ref[0])
    def _():
        x = _load_row_tiles(xs_ref, MOE_ROWS)
        row = lax.broadcasted_iota(jnp.int32, (MOE_ROWS, 1), 0)
        xb = jnp.where(row < nvalid_ref[i], x, 0.0).astype(BF16)
        hid = _silu(jnp.dot(xb, wg_ref[0], preferred_element_type=F32)) * jnp.dot(xb, wu_ref[0], preferred_element_type=F32)
        _store_row_tiles(ys_ref, jnp.dot(hid.astype(BF16), wd_ref[0], preferred_element_type=F32))

    @pl.when(i >= nused_ref[0])
    def _():
        ys_ref[...] = jnp.zeros_like(ys_ref)


def _expert_call(xs, n_used, blk_e, n_valid, w_gate, w_up, w_down):
    D = w_gate.shape[1]
    n_blk = blk_e.shape[0]
    rows_blk = pl.BlockSpec((MOE_ROWS * SUBLANE, LANE), lambda i, nu, be, nv: (i, 0))
    grid_spec = pltpu.PrefetchScalarGridSpec(
        num_scalar_prefetch=3,
        grid=(n_blk,),
        in_specs=[rows_blk,
                  pl.BlockSpec((1, D, D_EXPERT), lambda i, nu, be, nv: (be[i], 0, 0)),
                  pl.BlockSpec((1, D, D_EXPERT), lambda i, nu, be, nv: (be[i], 0, 0)),
                  pl.BlockSpec((1, D_EXPERT, D), lambda i, nu, be, nv: (be[i], 0, 0))],
        out_specs=rows_blk,
    )
    return pl.pallas_call(
        _expert_kernel,
        out_shape=jax.ShapeDtypeStruct(xs.shape, F32),
        grid_spec=grid_spec,
        compiler_params=_cparams(("arbitrary",)),
        name="expert_mlp",
    )(n_used, blk_e, n_valid, xs, w_gate, w_up, w_down)


def _dispatch(route, counts, N, tm):
    L_pad = N * TOP_K + N_EXPERTS * MOE_ROWS
    n_blk = L_pad // MOE_ROWS
    eid = route[:, 0:TOP_K].astype(jnp.int32)
    pos = route[:, 4:4 + TOP_K].astype(jnp.int32)
    counts = counts.astype(jnp.int32)
    padded = ((counts + MOE_ROWS - 1) // MOE_ROWS) * MOE_ROWS
    pend = jnp.cumsum(padded)
    pstart = pend - padded
    dest = pstart[eid] + pos
    n_used = (pend[-1] // MOE_ROWS).astype(jnp.int32).reshape(1)
    blk_row0 = jnp.arange(n_blk, dtype=jnp.int32) * MOE_ROWS
    blk_e = jnp.minimum(jnp.sum((pend[None, :] <= blk_row0[:, None]).astype(jnp.int32), axis=1), N_EXPERTS - 1)
    n_valid = jnp.clip((pstart + counts)[blk_e] - blk_row0, 0, MOE_ROWS).astype(jnp.int32)
    dest0 = dest[:, 0].reshape(N // tm, 1, tm)
    dest1 = dest[:, 1].reshape(N // tm, 1, tm)
    return n_used, blk_e.astype(jnp.int32), n_valid, dest0, dest1, L_pad


def _combine_kernel(d0_ref, d1_ref, d0n_ref, d1n_ref, x_ref, rt_ref, mod_ref, nf_ref, ys_hbm, o_ref,
                    buf, sem, *, final):
    i = pl.program_id(0)
    n = pl.num_programs(0)
    slot = lax.rem(i, 2)
    tm = x_ref.shape[0]

    def copy(s, k, t, src):
        return pltpu.make_async_copy(ys_hbm.at[pl.ds(src * SUBLANE, SUBLANE)],
                                     buf.at[s, k, pl.ds(t * SUBLANE, SUBLANE)], sem.at[s])

    def start_tile(s, a_ref, b_ref):
        def start(t):
            copy(s, 0, t, a_ref[0, 0, t]).start()
            copy(s, 1, t, b_ref[0, 0, t]).start()
        _tile_copy_loop(tm, start)

    @pl.when(i == 0)
    def _():
        start_tile(0, d0_ref, d1_ref)

    @pl.when(i + 1 < n)
    def _():
        start_tile(1 - slot, d0n_ref, d1n_ref)

    _tile_copy_loop(TOP_K * tm, lambda t: copy(slot, 0, 0, 0).wait())
    rt = rt_ref[...]
    y = rt[:, 2:3] * _load_row_tiles(buf.at[slot, 0], tm) + rt[:, 3:4] * _load_row_tiles(buf.at[slot, 1], tm)
    x2 = x_ref[...] + mod_ref[0, 5:6, :] * y
    if final:
        ms = jnp.mean(x2 * x2, axis=-1, keepdims=True)
        x2 = x2 * lax.rsqrt(ms + EPS) * nf_ref[...]
    o_ref[...] = x2


def _combine_call(x1, ys, route, dest0, dest1, mod_l, norm_final, final):
    B, S, D = x1.shape
    N = B * S
    n_tiles, _, tm = dest0.shape
    per_b = S // tm
    idx_blk = lambda fn: pl.BlockSpec((1, 1, tm), fn, memory_space=pltpu.SMEM)
    cur = lambda i: (i, 0, 0)
    nxt = lambda i: (jnp.minimum(i + 1, n_tiles - 1), 0, 0)
    out = pl.pallas_call(
        functools.partial(_combine_kernel, final=final),
        out_shape=jax.ShapeDtypeStruct((N, D), F32),
        grid=(n_tiles,),
        in_specs=[idx_blk(cur), idx_blk(cur), idx_blk(nxt), idx_blk(nxt),
                  pl.BlockSpec((tm, D), lambda i: (i, 0)),
                  pl.BlockSpec((tm, LANE), lambda i: (i, 0)),
                  pl.BlockSpec((1, 6, D), lambda i: (i // per_b, 0, 0)),
                  pl.BlockSpec((1, D), lambda i: (0, 0)),
                  pl.BlockSpec(memory_space=pl.ANY)],
        out_specs=pl.BlockSpec((tm, D), lambda i: (i, 0)),
        scratch_shapes=[pltpu.VMEM((2, TOP_K, tm * SUBLANE, LANE), F32), pltpu.SemaphoreType.DMA((2,))],
        compiler_params=_cparams(("arbitrary",)),
        name="moe_combine",
    )(dest0, dest1, dest0, dest1, x1.reshape(N, D), route.reshape(N, LANE), mod_l, norm_final.reshape(1, D), ys)
    return out.reshape(B, S, D)


def _layer(x, mod_l, p, final, norm_final):
    B, S, D = x.shape
    N = B * S
    s5_u, g_qkv, g_z, s_z, s_xbc, small = _proj_call(x, mod_l, p["norm_mix"], _arrange_w_in(p["w_in"]))
    bm, cm, lam_rows = _s5_params(p["s5_a_re"], p["s5_a_im"], p["s5_b_re"], p["s5_b_im"],
                                  p["s5_c_re"], p["s5_c_im"], p["s5_log_dt"])
    y_s5 = _s5_call(s5_u, bm, cm, lam_rows, p["s5_d"], p["s5_w_glu"], p["s5_norm"])
    y_gdn = _gdn_call(g_qkv, g_z, small, p["gdn_conv_w"], p["gdn_a_log"], p["gdn_dt_bias"], p["gdn_norm"])
    y_ssd = _ssd_call(s_xbc, s_z, small, p["ssd_conv_w"], p["ssd_conv_b"], p["ssd_a_log"], p["ssd_dt_bias"],
                      p["ssd_d"], p["ssd_norm"])
    w_router, b_router = _router_params(p["moe_w_grp"], p["moe_b_grp"], p["moe_w_rt"], p["moe_b_rt"])
    x1, h2, route, counts = _out_call(x, y_s5, y_gdn, y_ssd, mod_l, p["norm_ffn"], p["w_out"].astype(BF16),
                                      w_router, b_router)
    n_used, blk_e, n_valid, dest0, dest1, n_rows = _dispatch(route.reshape(N, LANE), counts[0, 0:N_EXPERTS], N,
                                                             min(MOE_TOKEN_TILE, S))
    xs = _scatter_call(h2, dest0, dest1, n_rows)
    ys = _expert_call(xs, n_used, blk_e, n_valid,
                      p["moe_w_gate"].astype(BF16), p["moe_w_up"].astype(BF16), p["moe_w_down"].astype(BF16))
    return _combine_call(x1, ys, route, dest0, dest1, mod_l, norm_final, final)


def kernel(x, c, w_ada, b_ada, norm_mix, norm_ffn, w_in, w_out, s5_a_re, s5_a_im, s5_b_re, s5_b_im, s5_c_re, s5_c_im, s5_d, s5_log_dt, s5_w_glu, s5_norm, gdn_conv_w, gdn_a_log, gdn_dt_bias, gdn_norm, ssd_conv_w, ssd_conv_b, ssd_a_log, ssd_dt_bias, ssd_d, ssd_norm, moe_w_grp, moe_b_grp, moe_w_rt, moe_b_rt, moe_w_gate, moe_w_up, moe_w_down, norm_final):
    stacked = dict(norm_mix=norm_mix, norm_ffn=norm_ffn, w_in=w_in, w_out=w_out, s5_a_re=s5_a_re, s5_a_im=s5_a_im,
                   s5_b_re=s5_b_re, s5_b_im=s5_b_im, s5_c_re=s5_c_re, s5_c_im=s5_c_im, s5_d=s5_d,
                   s5_log_dt=s5_log_dt, s5_w_glu=s5_w_glu, s5_norm=s5_norm, gdn_conv_w=gdn_conv_w,
                   gdn_a_log=gdn_a_log, gdn_dt_bias=gdn_dt_bias, gdn_norm=gdn_norm, ssd_conv_w=ssd_conv_w,
                   ssd_conv_b=ssd_conv_b, ssd_a_log=ssd_a_log, ssd_dt_bias=ssd_dt_bias, ssd_d=ssd_d,
                   ssd_norm=ssd_norm, moe_w_grp=moe_w_grp, moe_b_grp=moe_b_grp, moe_w_rt=moe_w_rt,
                   moe_b_rt=moe_b_rt, moe_w_gate=moe_w_gate, moe_w_up=moe_w_up, moe_w_down=moe_w_down)
    L = w_in.shape[0]
    B, S, D = x.shape
    mod = _mod_call(c, w_ada, b_ada).reshape(L, B, 6, D)
    for l in range(L):
        p = {k: v[l] for k, v in stacked.items()}
        x = _layer(x, mod[l], p, l == L - 1, norm_final)
    return x
```

```python
import functools

import numpy as np
import jax
import jax.numpy as jnp
from jax import lax
from jax.experimental import pallas as pl
from jax.experimental.pallas import tpu as pltpu

F32 = jnp.float32
BF16 = jnp.bfloat16

D_MODEL = 1024
DEPTH = 4
EPS = 1e-6
CONV_WIDTH = 4
CHUNK = 64
S5_WIDTH = 256
S5_CH = 16
S5_GROUPS = 16
S5_STATE = 64
S5_LANES = S5_GROUPS * S5_STATE
GDN_WIDTH = 384
GDN_HEAD_DIM = 64
GDN_HEADS = 6
GDN_CONV_DIM = 3 * GDN_WIDTH
SSD_WIDTH = 384
SSD_HEAD_DIM = 64
SSD_HEADS = 6
SSD_GROUPS = 2
SSD_STATE = 128
SSD_GROUP_WIDTH = SSD_WIDTH // SSD_GROUPS
SSD_CONV_DIM = SSD_WIDTH + 2 * SSD_GROUPS * SSD_STATE
PROJ_SIZES = (S5_WIDTH, GDN_CONV_DIM, GDN_WIDTH, GDN_HEADS, GDN_HEADS, SSD_WIDTH, SSD_CONV_DIM, SSD_HEADS)
MOE_GROUPS = 4
EXPERTS_PER_GROUP = 8
N_EXPERTS = 32
TOP_K = 2
D_EXPERT = 256

LANE = 128
SUBLANE = 8
SMALL_W = LANE
A_LANE, B_LANE, DT_LANE = 0, 8, 16
GRP_LANE = N_EXPERTS
MOE_ROWS = 256
MOE_TOKEN_TILE = 256
GDN_CHUNKS_PER_ITER = 2
SSD_CHUNKS_PER_ITER = 2
VMEM_LIMIT = 56 * 1024 * 1024
NEG_BIG = -1e30


def _cparams(sem):
    return pltpu.CompilerParams(dimension_semantics=sem, vmem_limit_bytes=VMEM_LIMIT)


def _split(a):
    hi = a.astype(BF16)
    lo = (a - hi.astype(F32)).astype(BF16)
    return hi, lo


_NN = (((1,), (0,)), ((), ()))
_NT = (((1,), (1,)), ((), ()))
_TN = (((0,), (0,)), ((), ()))


def _dot(a, b, dims=_NN):
    return lax.dot_general(a.astype(BF16), b.astype(BF16), dims, preferred_element_type=F32)


def _dot2(a, b_bf16):
    hi, lo = _split(a)
    return (lax.dot_general(hi, b_bf16, _NN, preferred_element_type=F32)
            + lax.dot_general(lo, b_bf16, _NN, preferred_element_type=F32))


def _dot3(a, b):
    ah, al = _split(a)
    bh, bl = _split(b)
    d = functools.partial(lax.dot_general, dimension_numbers=_NN, preferred_element_type=F32)
    return d(ah, bh) + (d(ah, bl) + d(al, bh))


def _silu(x):
    return x * jax.nn.sigmoid(x)


def _softplus(x):
    return jnp.maximum(x, 0.0) + jnp.log(1.0 + jnp.exp(-jnp.abs(x)))


def _norm_mod(x, w, scale, shift):
    ms = jnp.mean(x * x, axis=-1, keepdims=True)
    return (x * lax.rsqrt(ms + EPS) * w) * (1.0 + scale) + shift


def _mod_kernel(c_ref, w_ref, b_ref, o_ref):
    cond = _silu(c_ref[...])
    o_ref[0] = _dot3(cond, w_ref[0]) + b_ref[0]


def _mod_call(c, w_ada, b_ada):
    L, D, W = w_ada.shape
    B = c.shape[0]
    tn = 1536
    return pl.pallas_call(
        _mod_kernel,
        out_shape=jax.ShapeDtypeStruct((L, B, W), F32),
        grid=(L, W // tn),
        in_specs=[pl.BlockSpec((B, D), lambda l, j: (0, 0)),
                  pl.BlockSpec((1, D, tn), lambda l, j: (l, 0, j)),
                  pl.BlockSpec((1, 1, tn), lambda l, j: (l, 0, j))],
        out_specs=pl.BlockSpec((1, B, tn), lambda l, j: (l, 0, j)),
        compiler_params=_cparams(("arbitrary", "arbitrary")),
        name="adaln_mod",
    )(c, w_ada, b_ada.reshape(L, 1, W))


PROJ_OUT_W = (S5_WIDTH, GDN_CONV_DIM, GDN_WIDTH, SSD_WIDTH, SSD_CONV_DIM, SMALL_W)


def _proj_kernel(x_ref, mod_ref, nw_ref, w_ref, *o_refs):
    x = x_ref[0]
    h = _norm_mod(x, nw_ref[...], mod_ref[0, 1:2, :], mod_ref[0, 0:1, :]).astype(BF16)
    off = 0
    for o_ref in o_refs:
        n = o_ref.shape[-1]
        o_ref[0] = jnp.dot(h, w_ref[:, off:off + n], preferred_element_type=F32)
        off += n


def _arrange_w_in(w_in_l):
    splits = np.cumsum(PROJ_SIZES)[:-1].tolist()
    s5_u, g_qkv, g_z, g_a, g_b, s_z, s_xbc, s_dt = jnp.split(w_in_l, splits, axis=-1)
    D = w_in_l.shape[0]
    small = jnp.zeros((D, SMALL_W), w_in_l.dtype)
    small = small.at[:, A_LANE:A_LANE + GDN_HEADS].set(g_a)
    small = small.at[:, B_LANE:B_LANE + GDN_HEADS].set(g_b)
    small = small.at[:, DT_LANE:DT_LANE + SSD_HEADS].set(s_dt)
    return jnp.concatenate([s5_u, g_qkv, g_z, s_z, s_xbc, small], axis=-1).astype(BF16)


def _proj_call(x, mod_l, norm_w, w_arr):
    B, S, D = x.shape
    tm = min(512, S)
    W = w_arr.shape[1]
    out_shape = [jax.ShapeDtypeStruct((B, S, n), F32) for n in PROJ_OUT_W]
    out_specs = [pl.BlockSpec((1, tm, n), lambda b, i: (b, i, 0)) for n in PROJ_OUT_W]
    return pl.pallas_call(
        _proj_kernel,
        out_shape=out_shape,
        grid=(B, S // tm),
        in_specs=[pl.BlockSpec((1, tm, D), lambda b, i: (b, i, 0)),
                  pl.BlockSpec((1, 6, D), lambda b, i: (b, 0, 0)),
                  pl.BlockSpec((1, D), lambda b, i: (0, 0)),
                  pl.BlockSpec((D, W), lambda b, i: (0, 0))],
        out_specs=out_specs,
        compiler_params=_cparams(("arbitrary", "arbitrary")),
        name="norm_in_proj",
    )(x, mod_l, norm_w.reshape(1, D), w_arr)


def _s5_kernel(u_ref, bm_ref, cm_ref, lam_ref, dsk_ref, glu_ref, nw_ref, perm_ref, o_ref,
               x_s, st_s, ubt_s, u_s, *, T):
    B = u_ref.shape[0]
    P = S5_LANES

    @pl.when(pl.program_id(0) == 0)
    def _():
        st_s[...] = jnp.zeros_like(st_s)

    nw = S5_WIDTH // LANE
    ubt = u_ref[...].reshape(B * T, S5_WIDTH)
    for j in range(nw):
        ubt_s[j] = ubt[:, j * LANE:(j + 1) * LANE]

    def regroup(t, carry):
        for j in range(nw):
            u_s[pl.ds(pl.multiple_of(t * B, B), B), j * LANE:(j + 1) * LANE] = ubt_s[j, pl.ds(t, B, stride=T), :]
        return carry

    lax.fori_loop(0, T, regroup, 0, unroll=8)
    u = u_s[...]
    x_s[...] = jnp.dot(u.astype(BF16), bm_ref[...], preferred_element_type=F32)
    lr = jnp.broadcast_to(lam_ref[0:1, :], (B, P))
    li = jnp.broadcast_to(lam_ref[1:2, :], (B, P))

    def step(t, carry):
        sr, si = carry
        rows = pl.ds(pl.multiple_of(t * B, B), B)
        nr = lr * sr - li * si + x_s[rows, 0:P]
        ni = lr * si + li * sr + x_s[rows, P:2 * P]
        x_s[rows, 0:P] = nr
        x_s[rows, P:2 * P] = ni
        return nr, ni

    sr, si = lax.fori_loop(0, T, step, (st_s[:, 0:P], st_s[:, P:2 * P]), unroll=4)
    st_s[:, 0:P] = sr
    st_s[:, P:2 * P] = si

    y = jnp.dot(x_s[...].astype(BF16), cm_ref[...], preferred_element_type=F32) + dsk_ref[...] * u
    y = jax.nn.gelu(y)
    y = y * jax.nn.sigmoid(jnp.dot(y.astype(BF16), glu_ref[...], preferred_element_type=F32))
    ms = jnp.mean(y * y, axis=-1, keepdims=True)
    y = (y * lax.rsqrt(ms + EPS) * nw_ref[...]).astype(BF16)
    y = jnp.dot(perm_ref[...], y, preferred_element_type=F32)
    o_ref[...] = y.reshape(B, T, S5_WIDTH).astype(o_ref.dtype)


def _s5_params(a_re, a_im, b_re, b_im, c_re, c_im, log_dt):
    G, P, CH = S5_GROUPS, S5_STATE, S5_CH
    lam = lax.complex(a_re.astype(F32), a_im.astype(F32))
    step = jnp.exp(log_dt.astype(F32))[:, None]
    lam_bar = jnp.exp(lam * step)
    b_bar = ((lam_bar - 1.0) / lam)[..., None] * lax.complex(b_re.astype(F32), b_im.astype(F32))
    eye = jnp.eye(G, dtype=F32)
    bre = jnp.einsum('gpc,gh->gchp', b_bar.real, eye).reshape(G * CH, G * P)
    bim = jnp.einsum('gpc,gh->gchp', b_bar.imag, eye).reshape(G * CH, G * P)
    bm = jnp.concatenate([bre, bim], axis=1).astype(BF16)
    cre = jnp.einsum('gcp,gh->gphc', c_re.astype(F32), eye).reshape(G * P, G * CH)
    cim = jnp.einsum('gcp,gh->gphc', c_im.astype(F32), eye).reshape(G * P, G * CH)
    cm = jnp.concatenate([cre, -cim], axis=0).astype(BF16)
    lam_rows = jnp.zeros((SUBLANE, G * P), F32)
    lam_rows = lam_rows.at[0].set(lam_bar.real.reshape(-1)).at[1].set(lam_bar.imag.reshape(-1))
    return bm, cm, lam_rows


def _s5_call(u, bm, cm, lam_rows, d_skip, w_glu, norm_w):
    B, S, W = u.shape
    T = min(128, S)
    P2 = 2 * S5_LANES
    const = lambda shape: pl.BlockSpec(shape, lambda i: tuple(0 for _ in shape))
    r = np.arange(B * T)
    perm = np.zeros((B * T, B * T), np.float32)
    perm[r, (r % T) * B + r // T] = 1.0
    return pl.pallas_call(
        functools.partial(_s5_kernel, T=T),
        out_shape=jax.ShapeDtypeStruct((B, S, W), BF16),
        grid=(S // T,),
        in_specs=[pl.BlockSpec((B, T, W), lambda i: (0, i, 0)),
                  const((W, P2)), const((P2, W)), const((SUBLANE, S5_LANES)),
                  const((1, W)), const((W, W)), const((1, W)), const((B * T, B * T))],
        out_specs=pl.BlockSpec((B, T, W), lambda i: (0, i, 0)),
        scratch_shapes=[pltpu.VMEM((B * T, P2), F32), pltpu.VMEM((B, P2), F32),
                        pltpu.VMEM((W // LANE, B * T, LANE), F32), pltpu.VMEM((B * T, W), F32)],
        compiler_params=_cparams(("arbitrary",)),
        name="s5_mixer",
    )(u, bm, cm, lam_rows, d_skip.reshape(1, W).astype(F32), w_glu.astype(BF16), norm_w.reshape(1, W).astype(F32),
      jnp.asarray(perm, BF16))


def _causal_conv_silu(x, xf_ref, cw_ref, bias):
    n = x.shape[0]
    xf_ref[SUBLANE:, :] = x
    acc = x * cw_ref[CONV_WIDTH - 1:CONV_WIDTH, :]
    for k in range(1, CONV_WIDTH):
        acc = acc + xf_ref[SUBLANE - k:SUBLANE - k + n, :] * cw_ref[CONV_WIDTH - 1 - k:CONV_WIDTH - k, :]
    xf_ref[0:SUBLANE, :] = x[n - SUBLANE:, :]
    if bias is not None:
        acc = acc + bias
    return _silu(acc)


def _store_row_tiles(ref, val):
    n = val.shape[0]
    for j in range(val.shape[1] // LANE):
        ref[pl.ds(j, n, stride=SUBLANE), :] = val[:, j * LANE:(j + 1) * LANE]


def _load_row_tiles(ref, n):
    return jnp.concatenate([ref[pl.ds(j, n, stride=SUBLANE), :] for j in range(SUBLANE)], axis=1)


def _lanes_from(sm, off):
    return pltpu.roll(sm, SMALL_W - off, axis=1) if off else sm


def _chunk_scans(g):
    n = g.shape[0]
    rin = lax.broadcasted_iota(jnp.int32, g.shape, 0) & (CHUNK - 1)
    pre = g
    suf = jnp.where(rin < CHUNK - 1, pltpu.roll(g, n - 1, axis=0), 0.0)
    s = 1
    while s < CHUNK:
        pre = pre + jnp.where(rin >= s, pltpu.roll(pre, s, axis=0), 0.0)
        suf = suf + jnp.where(rin + s <= CHUNK - 1, pltpu.roll(suf, n - s, axis=0), 0.0)
        s *= 2
    return pre, suf


def _tri_masks():
    r = lax.broadcasted_iota(jnp.int32, (CHUNK, CHUNK), 0)
    c = lax.broadcasted_iota(jnp.int32, (CHUNK, CHUNK), 1)
    return r >= c, r > c, r == c


def _head_expand():
    m = np.zeros((SMALL_W, GDN_WIDTH), np.float32)
    for h in range(GDN_HEADS):
        m[h, h * GDN_HEAD_DIM:(h + 1) * GDN_HEAD_DIM] = 1.0
    return jnp.asarray(m, BF16)


def _block_ones(width, blk):
    idx = np.arange(width) // blk
    return jnp.asarray((idx[:, None] == idx[None, :]).astype(np.float32), BF16)


def _gdn_kernel(qkv_ref, z_ref, sm_ref, cw_ref, hp_ref, nw_ref, e_ref, xp_ref, o_ref,
                tail_s, st_s, kn_s, kb_s, qn_s, qd_s, kd_s, rhs_s, eg_s, gc_s, gct_s, o_s, at_s, *, TB):
    H, Dh, C = GDN_HEADS, GDN_HEAD_DIM, CHUNK
    W = GDN_WIDTH
    ncb = TB // C

    @pl.when(pl.program_id(1) == 0)
    def _():
        tail_s[0:SUBLANE, :] = jnp.zeros((SUBLANE, tail_s.shape[1]), F32)
        st_s[...] = jnp.zeros_like(st_s)

    xc = _causal_conv_silu(qkv_ref[0], tail_s, cw_ref, None)
    q, k, v = xc[:, 0:W], xc[:, W:2 * W], xc[:, 2 * W:3 * W]
    e = e_ref[...]
    xp = xp_ref[...]
    qn = q * lax.rsqrt(_dot2(q * q, e) + EPS) * (Dh ** -0.5)
    kn = k * lax.rsqrt(_dot2(k * k, e) + EPS)

    sm = sm_ref[0]
    lane = lax.broadcasted_iota(jnp.int32, sm.shape, 1)
    head_lane = lane < H
    g = jnp.where(head_lane, hp_ref[0:1, :] * _softplus(_lanes_from(sm, A_LANE) + hp_ref[1:2, :]), 0.0)
    beta = jnp.where(head_lane, jax.nn.sigmoid(_lanes_from(sm, B_LANE)), 0.0)
    gc, rc = _chunk_scans(g)
    bx = _dot2(beta, xp)
    egx = _dot2(jnp.exp(gc), xp)
    erx = _dot2(jnp.exp(rc), xp)
    kb = kn * bx
    kn_s[...] = kn
    kb_s[...] = kb
    qn_s[...] = qn
    qd_s[...] = qn * egx
    kd_s[...] = kn * erx
    for h in range(H):
        sl = slice(h * Dh, (h + 1) * Dh)
        rhs_s[:, 2 * h * Dh:(2 * h + 1) * Dh] = v[:, sl] * bx[:, sl]
        rhs_s[:, (2 * h + 1) * Dh:(2 * h + 2) * Dh] = kb[:, sl] * egx[:, sl]
    gc_s[...] = gc
    for c in range(ncb):
        gct_s[c] = gc[c * C:(c + 1) * C, :].T
        eg_s[c] = jnp.broadcast_to(egx[(c + 1) * C - 1:(c + 1) * C, :], (SUBLANE, W))

    causal, strict, diag = _tri_masks()
    eye = jnp.where(diag, 1.0, 0.0).astype(F32)

    heads = [slice(h * Dh, (h + 1) * Dh) for h in range(H)]

    def solve(it, carry):
        probs = []
        for cc in range(GDN_CHUNKS_PER_ITER):
            c = it * GDN_CHUNKS_PER_ITER + cc
            rows = pl.ds(pl.multiple_of(c * C, C), C)
            gcol = gc_s[rows, :]
            grow = gct_s[c]
            probs += [(rows, h, sl, gcol, grow) for h, sl in enumerate(heads)]
        decs = [jnp.exp(jnp.where(causal, gcol[:, h:h + 1] - grow[h:h + 1, :], NEG_BIG))
                for rows, h, sl, gcol, grow in probs]
        kns = [kn_s[rows, sl] for rows, h, sl, _, _ in probs]
        lows = [jnp.where(strict, _dot(kb_s[rows, sl], kn, _NT) * dec, 0.0)
                for (rows, h, sl, _, _), kn, dec in zip(probs, kns, decs)]
        for (rows, h, sl, _, _), kn, dec in zip(probs, kns, decs):
            at_s[rows, sl] = _dot(qn_s[rows, sl], kn, _NT) * dec
        ps = [-low for low in lows]
        invs = [eye + p for p in ps]
        for _ in range(5):
            ps = [_dot3(p, p) for p in ps]
            invs = [inv + _dot3(inv, p) for inv, p in zip(invs, ps)]
        for (rows, h, sl, _, _), inv in zip(probs, invs):
            cols = slice(2 * h * Dh, (2 * h + 2) * Dh)
            rhs_s[rows, cols] = _dot3(inv, rhs_s[rows, cols])
        return carry

    lax.fori_loop(0, ncb // GDN_CHUNKS_PER_ITER, solve, 0)

    def recur(c, carry):
        rows = pl.ds(pl.multiple_of(c * C, C), C)
        sts = [st_s[h] for h in range(H)]
        uws = [rhs_s[rows, 2 * h * Dh:(2 * h + 2) * Dh] for h in range(H)]
        v_news = [uw[:, 0:Dh] - _dot(uw[:, Dh:2 * Dh], st) for uw, st in zip(uws, sts)]
        for sl, st, v_new in zip(heads, sts, v_news):
            o_s[rows, sl] = _dot(qd_s[rows, sl], st) + _dot(at_s[rows, sl], v_new)
        for h, (sl, st, v_new) in enumerate(zip(heads, sts, v_news)):
            st_s[h] = st * eg_s[c, 0:1, sl] + _dot(kd_s[rows, sl], v_new, _TN)
        return carry

    lax.fori_loop(0, ncb, recur, 0)

    o = o_s[...]
    ms = _dot2(o * o, e) * (1.0 / Dh)
    o = o * lax.rsqrt(ms + EPS) * nw_ref[...]
    o_ref[0] = (o * _silu(z_ref[0])).astype(o_ref.dtype)


def _gdn_call(qkv, z, sm, conv_w, a_log, dt_bias, norm_w):
    B, S, _ = qkv.shape
    TB = min(512, S)
    W, H = GDN_WIDTH, GDN_HEADS
    hp = jnp.zeros((SUBLANE, SMALL_W), F32)
    hp = hp.at[0, :H].set(-jnp.exp(a_log.astype(F32))).at[1, :H].set(dt_bias.astype(F32))
    nw = jnp.tile(norm_w.astype(F32), H).reshape(1, W)
    const = lambda shape: pl.BlockSpec(shape, lambda b, i: tuple(0 for _ in shape))
    blk = lambda n: pl.BlockSpec((1, TB, n), lambda b, i: (b, i, 0))
    f = lambda *shape: pltpu.VMEM(shape, F32)
    return pl.pallas_call(
        functools.partial(_gdn_kernel, TB=TB),
        out_shape=jax.ShapeDtypeStruct((B, S, W), BF16),
        grid=(B, S // TB),
        in_specs=[blk(GDN_CONV_DIM), blk(W), blk(SMALL_W),
                  const((CONV_WIDTH, GDN_CONV_DIM)), const((SUBLANE, SMALL_W)), const((1, W)),
                  const((W, W)), const((SMALL_W, W))],
        out_specs=blk(W),
        scratch_shapes=[f(SUBLANE + TB, GDN_CONV_DIM), f(H, GDN_HEAD_DIM, GDN_HEAD_DIM),
                        f(TB, W), f(TB, W), f(TB, W), f(TB, W), f(TB, W), f(TB, 2 * W), f(TB // CHUNK, SUBLANE, W),
                        f(TB, SMALL_W), f(TB // CHUNK, SMALL_W, CHUNK), f(TB, W), f(TB, W)],
        compiler_params=_cparams(("arbitrary", "arbitrary")),
        name="gdn_mixer",
    )(qkv, z, sm, conv_w.astype(F32), hp, nw, _block_ones(W, GDN_HEAD_DIM), _head_expand())


def _ssd_kernel(xbc_ref, z_ref, sm_ref, cw_ref, cb_ref, hp_ref, dsk_ref, nw_ref, e_ref, xp_ref, o_ref,
                tail_s, st_s, xs_s, xdt_s, xdd_s, bm_s, cm_s, ea_s, el_s, ac_s, act_s, y_s, inc_s, *, TB):
    H, P, G, N, C = SSD_HEADS, SSD_HEAD_DIM, SSD_GROUPS, SSD_STATE, CHUNK
    R = H // G
    W, GW = SSD_WIDTH, SSD_GROUP_WIDTH
    ncb = TB // C

    @pl.when(pl.program_id(1) == 0)
    def _():
        tail_s[0:SUBLANE, :] = jnp.zeros((SUBLANE, tail_s.shape[1]), F32)
        st_s[...] = jnp.zeros_like(st_s)

    xc = _causal_conv_silu(xbc_ref[0], tail_s, cw_ref, cb_ref[...])
    xs = xc[:, 0:W]
    xp = xp_ref[...]

    sm = sm_ref[0]
    lane = lax.broadcasted_iota(jnp.int32, sm.shape, 1)
    head_lane = lane < H
    dt = jnp.where(head_lane, _softplus(_lanes_from(sm, DT_LANE) + hp_ref[1:2, :]), 0.0)
    acs, rcs = _chunk_scans(dt * hp_ref[0:1, :])
    dtx = _dot2(dt, xp)
    eax = _dot2(jnp.exp(acs), xp)
    erx = _dot2(jnp.exp(rcs), xp)
    xdt = xs * dtx
    xs_s[...] = xs
    xdt_s[...] = xdt
    xdd_s[...] = xdt * erx
    bm_s[...] = xc[:, W:W + G * N]
    cm_s[...] = xc[:, W + G * N:W + 2 * G * N]
    ea_s[...] = eax
    ac_s[...] = acs
    for c in range(ncb):
        act_s[c] = acs[c * C:(c + 1) * C, :].T
        el_s[c] = jnp.broadcast_to(eax[(c + 1) * C - 1:(c + 1) * C, :], (SUBLANE, W))

    causal, _, _ = _tri_masks()

    groups = [(gi, slice(gi * GW, (gi + 1) * GW), slice(gi * N, (gi + 1) * N)) for gi in range(G)]

    def local(it, carry):
        work = []
        for cc in range(SSD_CHUNKS_PER_ITER):
            c = it * SSD_CHUNKS_PER_ITER + cc
            rows = pl.ds(pl.multiple_of(c * C, C), C)
            work += [(c, rows, ac_s[rows, :], act_s[c], g) for g in groups]
        cbs = [_dot(cm_s[rows, nsl], bm_s[rows, nsl], _NT) for c, rows, _, _, (gi, gsl, nsl) in work]
        for (c, rows, acol, arow, (gi, gsl, nsl)), cb in zip(work, cbs):
            for r in range(R):
                h = gi * R + r
                sl = slice(h * P, (h + 1) * P)
                seg = jnp.exp(jnp.where(causal, acol[:, h:h + 1] - arow[h:h + 1, :], NEG_BIG))
                y_s[rows, sl] = _dot(cb * seg, xdt_s[rows, sl])
        for c, rows, _, _, (gi, gsl, nsl) in work:
            inc_s[c, gi] = _dot(bm_s[rows, nsl], xdd_s[rows, gsl], _TN)
        return carry

    lax.fori_loop(0, ncb // SSD_CHUNKS_PER_ITER, local, 0)

    def recur(c, carry):
        rows = pl.ds(pl.multiple_of(c * C, C), C)
        sts = [st_s[gi] for gi, _, _ in groups]
        for (gi, gsl, nsl), st in zip(groups, sts):
            y_s[rows, gsl] = y_s[rows, gsl] + _dot(cm_s[rows, nsl], st) * ea_s[rows, gsl]
        for (gi, gsl, nsl), st in zip(groups, sts):
            st_s[gi] = st * el_s[c, 0:1, gsl] + inc_s[c, gi]
        return carry

    lax.fori_loop(0, ncb, recur, 0)

    y = y_s[...] + dsk_ref[...] * xs_s[...]
    y = y * _silu(z_ref[0])
    ms = _dot2(y * y, e_ref[...]) * (1.0 / GW)
    o_ref[0] = (y * lax.rsqrt(ms + EPS) * nw_ref[...]).astype(o_ref.dtype)


def _ssd_call(xbc, z, sm, conv_w, conv_b, a_log, dt_bias, d_skip, norm_w):
    B, S, _ = xbc.shape
    TB = min(512, S)
    W, H, G, N = SSD_WIDTH, SSD_HEADS, SSD_GROUPS, SSD_STATE
    hp = jnp.zeros((SUBLANE, SMALL_W), F32)
    hp = hp.at[0, :H].set(-jnp.exp(a_log.astype(F32))).at[1, :H].set(dt_bias.astype(F32))
    dsk = jnp.repeat(d_skip.astype(F32), SSD_HEAD_DIM).reshape(1, W)
    const = lambda shape: pl.BlockSpec(shape, lambda b, i: tuple(0 for _ in shape))
    blk = lambda n: pl.BlockSpec((1, TB, n), lambda b, i: (b, i, 0))
    f = lambda *shape: pltpu.VMEM(shape, F32)
    return pl.pallas_call(
        functools.partial(_ssd_kernel, TB=TB),
        out_shape=jax.ShapeDtypeStruct((B, S, W), BF16),
        grid=(B, S // TB),
        in_specs=[blk(SSD_CONV_DIM), blk(W), blk(SMALL_W),
                  const((CONV_WIDTH, SSD_CONV_DIM)), const((1, SSD_CONV_DIM)), const((SUBLANE, SMALL_W)),
                  const((1, W)), const((1, W)), const((W, W)), const((SMALL_W, W))],
        out_specs=blk(W),
        scratch_shapes=[f(SUBLANE + TB, SSD_CONV_DIM), f(G, N, SSD_GROUP_WIDTH),
                        f(TB, W), f(TB, W), f(TB, W), f(TB, G * N), f(TB, G * N), f(TB, W),
                        f(TB // CHUNK, SUBLANE, W),
                        f(TB, SMALL_W), f(TB // CHUNK, SMALL_W, CHUNK), f(TB, W),
                        f(TB // CHUNK, G, N, SSD_GROUP_WIDTH)],
        compiler_params=_cparams(("arbitrary", "arbitrary")),
        name="ssd_mixer",
    )(xbc, z, sm, conv_w.astype(F32), conv_b.reshape(1, -1).astype(F32), hp, dsk,
      norm_w.reshape(1, W).astype(F32), _block_ones(W, SSD_GROUP_WIDTH), _head_expand())


def _out_kernel(x_ref, y1_ref, y2_ref, y3_ref, mod_ref, nw_ref, wo_ref, wr_ref, br_ref, tri_ref,
                x1_ref, h_ref, rt_ref, cnt_ref, run_s):
    y = (jnp.dot(y1_ref[0], wo_ref[0:S5_WIDTH, :], preferred_element_type=F32)
         + jnp.dot(y2_ref[0], wo_ref[S5_WIDTH:S5_WIDTH + GDN_WIDTH, :], preferred_element_type=F32)
         + jnp.dot(y3_ref[0], wo_ref[S5_WIDTH + GDN_WIDTH:, :], preferred_element_type=F32))
    x1 = x_ref[0] + mod_ref[0, 2:3, :] * y
    x1_ref[0] = x1
    h = _norm_mod(x1, nw_ref[...], mod_ref[0, 4:5, :], mod_ref[0, 3:4, :])
    _store_row_tiles(h_ref, h)
    lg =jnp.dot(h.astype(BF16), wr_ref[...], preferred_element_type=F32) + br_ref[...]

    lane = lax.broadcasted_iota(jnp.int32, lg.shape, 1)
    lanef = lane.astype(F32)
    big = float(4 * LANE)
    grp = (lane >= GRP_LANE) & (lane < GRP_LANE + MOE_GROUPS)
    lgm = jnp.where(grp, lg, -jnp.inf)
    m = jnp.max(lgm, axis=-1, keepdims=True)
    gidx = jnp.min(jnp.where(lgm == m, lanef - GRP_LANE, big), axis=-1, keepdims=True)
    g_w = 1.0 / jnp.sum(jnp.where(grp, jnp.exp(lg - m), 0.0), axis=-1, keepdims=True)
    in_grp = (lane < N_EXPERTS) & ((lane // EXPERTS_PER_GROUP).astype(F32) == gidx)
    le = jnp.where(in_grp, lg, -jnp.inf)
    v1 = jnp.max(le, axis=-1, keepdims=True)
    i1 = jnp.min(jnp.where(le == v1, lanef, big), axis=-1, keepdims=True)
    le2 = jnp.where(lanef == i1, -jnp.inf, le)
    v2 = jnp.max(le2, axis=-1, keepdims=True)
    i2 = jnp.min(jnp.where(le2 == v2, lanef, big), axis=-1, keepdims=True)
    e2 = jnp.exp(v2 - v1)
    w1 = g_w / (1.0 + e2)
    w2 = g_w * e2 / (1.0 + e2)

    @pl.when((pl.program_id(0) == 0) & (pl.program_id(1) == 0))
    def _():
        run_s[...] = jnp.zeros_like(run_s)

    chosen = jnp.where((lanef == i1) | (lanef == i2), 1.0, 0.0)
    before = jnp.dot(tri_ref[...], chosen.astype(BF16), preferred_element_type=F32) + run_s[0:1, :]
    p1 = jnp.sum(jnp.where(lanef == i1, before, 0.0), axis=-1, keepdims=True)
    p2 = jnp.sum(jnp.where(lanef == i2, before, 0.0), axis=-1, keepdims=True)
    run_s[...] = run_s[...] + jnp.sum(chosen, axis=0, keepdims=True)
    cnt_ref[...] = run_s[...]

    rt = jnp.zeros_like(lg)
    for k, val in enumerate((i1, i2, w1, w2, p1, p2)):
        rt = jnp.where(lane == k, val, rt)
    rt_ref[0] = rt


def _out_call(x, y1, y2, y3, mod_l, norm_w, w_out, w_router, b_router):
    B, S, D = x.shape
    tm = min(512, S)
    blk = lambda n: pl.BlockSpec((1, tm, n), lambda b, i: (b, i, 0))
    const = lambda shape: pl.BlockSpec(shape, lambda b, i: tuple(0 for _ in shape))
    tri = jnp.asarray(np.tril(np.ones((tm, tm), np.float32), -1), BF16)
    nb = S // tm
    tiles = D // LANE
    return pl.pallas_call(
        _out_kernel,
        out_shape=[jax.ShapeDtypeStruct((B, S, D), F32), jax.ShapeDtypeStruct((B * S * tiles, LANE), F32),
                   jax.ShapeDtypeStruct((B, S, LANE), F32), jax.ShapeDtypeStruct((SUBLANE, LANE), F32)],
        grid=(B, nb),
        in_specs=[blk(D), blk(S5_WIDTH), blk(GDN_WIDTH), blk(SSD_WIDTH),
                  pl.BlockSpec((1, 6, D), lambda b, i: (b, 0, 0)),
                  const((1, D)), const((D, D)), const((D, LANE)), const((1, LANE)), const((tm, tm))],
        out_specs=[blk(D), pl.BlockSpec((tm * tiles, LANE), lambda b, i: (b * nb + i, 0)),
                   blk(LANE), const((SUBLANE, LANE))],
        scratch_shapes=[pltpu.VMEM((SUBLANE, LANE), F32)],
        compiler_params=_cparams(("arbitrary", "arbitrary")),
        name="out_proj_router",
    )(x, y1, y2, y3, mod_l, norm_w.reshape(1, D), w_out, w_router, b_router, tri)


def _router_params(w_grp, b_grp, w_rt, b_rt):
    D = w_grp.shape[0]
    w = jnp.zeros((D, LANE), F32).at[:, 0:N_EXPERTS].set(w_rt).at[:, GRP_LANE:GRP_LANE + MOE_GROUPS].set(w_grp)
    b = jnp.zeros((1, LANE), F32).at[0, 0:N_EXPERTS].set(b_rt).at[0, GRP_LANE:GRP_LANE + MOE_GROUPS].set(b_grp)
    return w.astype(BF16), b


def _tile_copy_loop(n, fn):
    def body(t, carry):
        fn(t)
        return carry
    lax.fori_loop(0, n, body, 0, unroll=8)


def _scatter_kernel(d0_ref, d1_ref, h_ref, xs_hbm, stage, sem):
    i = pl.program_id(0)
    n = pl.num_programs(0)
    slot = lax.rem(i, 2)
    tm = d0_ref.shape[-1]

    def copy(s, t, dst):
        return pltpu.make_async_copy(stage.at[s, pl.ds(t * SUBLANE, SUBLANE)],
                                     xs_hbm.at[pl.ds(dst * SUBLANE, SUBLANE)], sem.at[s])

    def wait_slot(s):
        _tile_copy_loop(TOP_K * tm, lambda t: copy(s, 0, 0).wait())

    @pl.when(i >= 2)
    def _():
        wait_slot(slot)

    stage[slot] = h_ref[...]

    def start(t):
        copy(slot, t, d0_ref[0, 0, t]).start(priority=0)
        copy(slot, t, d1_ref[0, 0, t]).start(priority=1)
    _tile_copy_loop(tm, start)

    @pl.when(i == n - 1)
    def _():
        @pl.when(i >= 1)
        def _():
            wait_slot(1 - slot)
        wait_slot(slot)


def _scatter_call(h2t, dest0, dest1, n_rows):
    n_tiles, _, tm = dest0.shape
    idx_blk = pl.BlockSpec((1, 1, tm), lambda i: (i, 0, 0), memory_space=pltpu.SMEM)
    return pl.pallas_call(
        _scatter_kernel,
        out_shape=jax.ShapeDtypeStruct((n_rows * SUBLANE, LANE), F32),
        grid=(n_tiles,),
        in_specs=[idx_blk, idx_blk, pl.BlockSpec((tm * SUBLANE, LANE), lambda i: (i, 0))],
        out_specs=pl.BlockSpec(memory_space=pl.ANY),
        scratch_shapes=[pltpu.VMEM((2, tm * SUBLANE, LANE), F32), pltpu.SemaphoreType.DMA((2,))],
        compiler_params=_cparams(("arbitrary",)),
        name="moe_scatter",
    )(dest0, dest1, h2t)


def _expert_kernel(nused_ref, blke_ref, nvalid_ref, xs_ref, wg_ref, wu_ref, wd_ref, ys_ref):
    i = pl.program_id(0)

    @pl.when(i < nused_ref[0])
    def _():
        x = _load_row_tiles(xs_ref, MOE_ROWS)
        row = lax.broadcasted_iota(jnp.int32, (MOE_ROWS, 1), 0)
        xb = jnp.where(row < nvalid_ref[i], x, 0.0).astype(BF16)
        hid = _silu(jnp.dot(xb, wg_ref[0], preferred_element_type=F32)) * jnp.dot(xb, wu_ref[0], preferred_element_type=F32)
        _store_row_tiles(ys_ref, jnp.dot(hid.astype(BF16), wd_ref[0], preferred_element_type=F32))

    @pl.when(i >= nused_ref[0])
    def _():
        ys_ref[...] = jnp.zeros_like(ys_ref)


def _expert_call(xs, n_used, blk_e, n_valid, w_gate, w_up, w_down):
    D = w_gate.shape[1]
    n_blk = blk_e.shape[0]
    rows_blk = pl.BlockSpec((MOE_ROWS * SUBLANE, LANE), lambda i, nu, be, nv: (i, 0))
    grid_spec = pltpu.PrefetchScalarGridSpec(
        num_scalar_prefetch=3,
        grid=(n_blk,),
        in_specs=[rows_blk,
                  pl.BlockSpec((1, D, D_EXPERT), lambda i, nu, be, nv: (be[i], 0, 0)),
                  pl.BlockSpec((1, D, D_EXPERT), lambda i, nu, be, nv: (be[i], 0, 0)),
                  pl.BlockSpec((1, D_EXPERT, D), lambda i, nu, be, nv: (be[i], 0, 0))],
        out_specs=rows_blk,
    )
    return pl.pallas_call(
        _expert_kernel,
        out_shape=jax.ShapeDtypeStruct(xs.shape, F32),
        grid_spec=grid_spec,
        compiler_params=_cparams(("arbitrary",)),
        name="expert_mlp",
    )(n_used, blk_e, n_valid, xs, w_gate, w_up, w_down)


def _dispatch(route, counts, N, tm):
    L_pad = N * TOP_K + N_EXPERTS * MOE_ROWS
    n_blk = L_pad // MOE_ROWS
    eid = route[:, 0:TOP_K].astype(jnp.int32)
    pos = route[:, 4:4 + TOP_K].astype(jnp.int32)
    counts = counts.astype(jnp.int32)
    padded = ((counts + MOE_ROWS - 1) // MOE_ROWS) * MOE_ROWS
    pend = jnp.cumsum(padded)
    pstart = pend - padded
    dest = pstart[eid] + pos
    n_used = (pend[-1] // MOE_ROWS).astype(jnp.int32).reshape(1)
    blk_row0 = jnp.arange(n_blk, dtype=jnp.int32) * MOE_ROWS
    blk_e = jnp.minimum(jnp.sum((pend[None, :] <= blk_row0[:, None]).astype(jnp.int32), axis=1), N_EXPERTS - 1)
    n_valid = jnp.clip((pstart + counts)[blk_e] - blk_row0, 0, MOE_ROWS).astype(jnp.int32)
    dest0 = dest[:, 0].reshape(N // tm, 1, tm)
    dest1 = dest[:, 1].reshape(N // tm, 1, tm)
    return n_used, blk_e.astype(jnp.int32), n_valid, dest0, dest1, L_pad


def _combine_kernel(d0_ref, d1_ref, d0n_ref, d1n_ref, x_ref, rt_ref, mod_ref, nf_ref, ys_hbm, o_ref,
                    buf, sem, *, final):
    i = pl.program_id(0)
    n = pl.num_programs(0)
    slot = lax.rem(i, 2)
    tm = x_ref.shape[0]

    def copy(s, k, t, src):
        return pltpu.make_async_copy(ys_hbm.at[pl.ds(src * SUBLANE, SUBLANE)],
                                     buf.at[s, k, pl.ds(t * SUBLANE, SUBLANE)], sem.at[s])

    def start_tile(s, a_ref, b_ref):
        def start(t):
            copy(s, 0, t, a_ref[0, 0, t]).start(priority=0)
            copy(s, 1, t, b_ref[0, 0, t]).start(priority=1)
        _tile_copy_loop(tm, start)

    @pl.when(i == 0)
    def _():
        start_tile(0, d0_ref, d1_ref)

    @pl.when(i + 1 < n)
    def _():
        start_tile(1 - slot, d0n_ref, d1n_ref)

    _tile_copy_loop(TOP_K * tm, lambda t: copy(slot, 0, 0, 0).wait())
    rt = rt_ref[...]
    y = rt[:, 2:3] * _load_row_tiles(buf.at[slot, 0], tm) + rt[:, 3:4] * _load_row_tiles(buf.at[slot, 1], tm)
    x2 = x_ref[...] + mod_ref[0, 5:6, :] * y
    if final:
        ms = jnp.mean(x2 * x2, axis=-1, keepdims=True)
        x2 = x2 * lax.rsqrt(ms + EPS) * nf_ref[...]
    o_ref[...] = x2


def _combine_call(x1, ys, route, dest0, dest1, mod_l, norm_final, final):
    B, S, D = x1.shape
    N = B * S
    n_tiles, _, tm = dest0.shape
    per_b = S // tm
    idx_blk = lambda fn: pl.BlockSpec((1, 1, tm), fn, memory_space=pltpu.SMEM)
    cur = lambda i: (i, 0, 0)
    nxt = lambda i: (jnp.minimum(i + 1, n_tiles - 1), 0, 0)
    out = pl.pallas_call(
        functools.partial(_combine_kernel, final=final),
        out_shape=jax.ShapeDtypeStruct((N, D), F32),
        grid=(n_tiles,),
        in_specs=[idx_blk(cur), idx_blk(cur), idx_blk(nxt), idx_blk(nxt),
                  pl.BlockSpec((tm, D), lambda i: (i, 0)),
                  pl.BlockSpec((tm, LANE), lambda i: (i, 0)),
                  pl.BlockSpec((1, 6, D), lambda i: (i // per_b, 0, 0)),
                  pl.BlockSpec((1, D), lambda i: (0, 0)),
                  pl.BlockSpec(memory_space=pl.ANY)],
        out_specs=pl.BlockSpec((tm, D), lambda i: (i, 0)),
        scratch_shapes=[pltpu.VMEM((2, TOP_K, tm * SUBLANE, LANE), F32), pltpu.SemaphoreType.DMA((2,))],
        compiler_params=_cparams(("arbitrary",)),
        name="moe_combine",
    )(dest0, dest1, dest0, dest1, x1.reshape(N, D), route.reshape(N, LANE), mod_l, norm_final.reshape(1, D), ys)
    return out.reshape(B, S, D)


def _layer(x, mod_l, p, final, norm_final):
    B, S, D = x.shape
    N = B * S
    s5_u, g_qkv, g_z, s_z, s_xbc, small = _proj_call(x, mod_l, p["norm_mix"], _arrange_w_in(p["w_in"]))
    bm, cm, lam_rows = _s5_params(p["s5_a_re"], p["s5_a_im"], p["s5_b_re"], p["s5_b_im"],
                                  p["s5_c_re"], p["s5_c_im"], p["s5_log_dt"])
    y_s5 = _s5_call(s5_u, bm, cm, lam_rows, p["s5_d"], p["s5_w_glu"], p["s5_norm"])
    y_gdn = _gdn_call(g_qkv, g_z, small, p["gdn_conv_w"], p["gdn_a_log"], p["gdn_dt_bias"], p["gdn_norm"])
    y_ssd = _ssd_call(s_xbc, s_z, small, p["ssd_conv_w"], p["ssd_conv_b"], p["ssd_a_log"], p["ssd_dt_bias"],
                      p["ssd_d"], p["ssd_norm"])
    w_router, b_router = _router_params(p["moe_w_grp"], p["moe_b_grp"], p["moe_w_rt"], p["moe_b_rt"])
    x1, h2, route, counts = _out_call(x, y_s5, y_gdn, y_ssd, mod_l, p["norm_ffn"], p["w_out"].astype(BF16),
                                      w_router, b_router)
    n_used, blk_e, n_valid, dest0, dest1, n_rows = _dispatch(route.reshape(N, LANE), counts[0, 0:N_EXPERTS], N,
                                                             min(MOE_TOKEN_TILE, S))
    xs = _scatter_call(h2, dest0, dest1, n_rows)
    ys = _expert_call(xs, n_used, blk_e, n_valid,
                      p["moe_w_gate"].astype(BF16), p["moe_w_up"].astype(BF16), p["moe_w_down"].astype(BF16))
    return _combine_call(x1, ys, route, dest0, dest1, mod_l, norm_final, final)


def kernel(x, c, w_ada, b_ada, norm_mix, norm_ffn, w_in, w_out, s5_a_re, s5_a_im, s5_b_re, s5_b_im, s5_c_re, s5_c_im, s5_d, s5_log_dt, s5_w_glu, s5_norm, gdn_conv_w, gdn_a_log, gdn_dt_bias, gdn_norm, ssd_conv_w, ssd_conv_b, ssd_a_log, ssd_dt_bias, ssd_d, ssd_norm, moe_w_grp, moe_b_grp, moe_w_rt, moe_b_rt, moe_w_gate, moe_w_up, moe_w_down, norm_final):
    stacked = dict(norm_mix=norm_mix, norm_ffn=norm_ffn, w_in=w_in, w_out=w_out, s5_a_re=s5_a_re, s5_a_im=s5_a_im,
                   s5_b_re=s5_b_re, s5_b_im=s5_b_im, s5_c_re=s5_c_re, s5_c_im=s5_c_im, s5_d=s5_d,
                   s5_log_dt=s5_log_dt, s5_w_glu=s5_w_glu, s5_norm=s5_norm, gdn_conv_w=gdn_conv_w,
                   gdn_a_log=gdn_a_log, gdn_dt_bias=gdn_dt_bias, gdn_norm=gdn_norm, ssd_conv_w=ssd_conv_w,
                   ssd_conv_b=ssd_conv_b, ssd_a_log=ssd_a_log, ssd_dt_bias=ssd_dt_bias, ssd_d=ssd_d,
                   ssd_norm=ssd_norm, moe_w_grp=moe_w_grp, moe_b_grp=moe_b_grp, moe_w_rt=moe_w_rt,
                   moe_b_rt=moe_b_rt, moe_w_gate=moe_w_gate, moe_w_up=moe_w_up, moe_w_down=moe_w_down)
    L = w_in.shape[0]
    B, S, D = x.shape
    mod = _mod_call(c, w_ada, b_ada).reshape(L, B, 6, D)
    for l in range(L):
        p = {k: v[l] for k, v in stacked.items()}
        x = _layer(x, mod[l], p, l == L - 1, norm_final)
    return x
```

```python
import functools

import numpy as np
import jax
import jax.numpy as jnp
from jax import lax
from jax.experimental import pallas as pl
from jax.experimental.pallas import tpu as pltpu

F32 = jnp.float32
BF16 = jnp.bfloat16

D_MODEL = 1024
DEPTH = 4
EPS = 1e-6
CONV_WIDTH = 4
CHUNK = 64
S5_WIDTH = 256
S5_CH = 16
S5_GROUPS = 16
S5_STATE = 64
S5_LANES = S5_GROUPS * S5_STATE
GDN_WIDTH = 384
GDN_HEAD_DIM = 64
GDN_HEADS = 6
GDN_CONV_DIM = 3 * GDN_WIDTH
SSD_WIDTH = 384
SSD_HEAD_DIM = 64
SSD_HEADS = 6
SSD_GROUPS = 2
SSD_STATE = 128
SSD_GROUP_WIDTH = SSD_WIDTH // SSD_GROUPS
SSD_CONV_DIM = SSD_WIDTH + 2 * SSD_GROUPS * SSD_STATE
PROJ_SIZES = (S5_WIDTH, GDN_CONV_DIM, GDN_WIDTH, GDN_HEADS, GDN_HEADS, SSD_WIDTH, SSD_CONV_DIM, SSD_HEADS)
MOE_GROUPS = 4
EXPERTS_PER_GROUP = 8
N_EXPERTS = 32
TOP_K = 2
D_EXPERT = 256

LANE = 128
SUBLANE = 8
SMALL_W = LANE
A_LANE, B_LANE, DT_LANE = 0, 8, 16
GRP_LANE = N_EXPERTS
MOE_ROWS = 512
MOE_TOKEN_TILE = 256
GDN_PAIRS_PER_ITER = 4
SSD_CHUNKS_PER_ITER = 2
VMEM_LIMIT = 56 * 1024 * 1024
NEG_BIG = -1e30


def _cparams(sem):
    return pltpu.CompilerParams(dimension_semantics=sem, vmem_limit_bytes=VMEM_LIMIT)


def _split(a):
    hi = a.astype(BF16)
    lo = (a - hi.astype(F32)).astype(BF16)
    return hi, lo


_NN = (((1,), (0,)), ((), ()))
_NT = (((1,), (1,)), ((), ()))
_TN = (((0,), (0,)), ((), ()))


def _dot(a, b, dims=_NN):
    return lax.dot_general(a.astype(BF16), b.astype(BF16), dims, preferred_element_type=F32)


def _dot2(a, b_bf16):
    hi, lo = _split(a)
    return (lax.dot_general(hi, b_bf16, _NN, preferred_element_type=F32)
            + lax.dot_general(lo, b_bf16, _NN, preferred_element_type=F32))


def _dot_sel3(a, b_bf16):
    h1 = a.astype(BF16)
    r1 = a - h1.astype(F32)
    h2 = r1.astype(BF16)
    h3 = (r1 - h2.astype(F32)).astype(BF16)
    d = functools.partial(lax.dot_general, dimension_numbers=_NN, preferred_element_type=F32)
    return d(h1, b_bf16) + (d(h2, b_bf16) + d(h3, b_bf16))


def _dot3(a, b):
    ah, al = _split(a)
    bh, bl = _split(b)
    d = functools.partial(lax.dot_general, dimension_numbers=_NN, preferred_element_type=F32)
    return d(ah, bh) + (d(ah, bl) + d(al, bh))


def _silu(x):
    return x * jax.nn.sigmoid(x)


def _softplus(x):
    return jnp.maximum(x, 0.0) + jnp.log(1.0 + jnp.exp(-jnp.abs(x)))


def _norm_mod(x, w, scale, shift):
    ms = jnp.mean(x * x, axis=-1, keepdims=True)
    return (x * lax.rsqrt(ms + EPS) * w) * (1.0 + scale) + shift


def _mod_kernel(c_ref, w_ref, b_ref, o_ref):
    cond = _silu(c_ref[...])
    o_ref[0] = _dot3(cond, w_ref[0]) + b_ref[0]


def _mod_call(c, w_ada, b_ada):
    L, D, W = w_ada.shape
    B = c.shape[0]
    tn = 1536
    return pl.pallas_call(
        _mod_kernel,
        out_shape=jax.ShapeDtypeStruct((L, B, W), F32),
        grid=(L, W // tn),
        in_specs=[pl.BlockSpec((B, D), lambda l, j: (0, 0)),
                  pl.BlockSpec((1, D, tn), lambda l, j: (l, 0, j)),
                  pl.BlockSpec((1, 1, tn), lambda l, j: (l, 0, j))],
        out_specs=pl.BlockSpec((1, B, tn), lambda l, j: (l, 0, j)),
        compiler_params=_cparams(("arbitrary", "arbitrary")),
        name="adaln_mod",
    )(c, w_ada, b_ada.reshape(L, 1, W))


PROJ_OUT_W = (S5_WIDTH, GDN_CONV_DIM, GDN_WIDTH, SSD_WIDTH, SSD_CONV_DIM, SMALL_W)


def _proj_kernel(x_ref, mod_ref, nw_ref, w_ref, *o_refs):
    x = x_ref[0]
    h = _norm_mod(x, nw_ref[...], mod_ref[0, 1:2, :], mod_ref[0, 0:1, :]).astype(BF16)
    off = 0
    for o_ref in o_refs:
        n = o_ref.shape[-1]
        o_ref[0] = jnp.dot(h, w_ref[:, off:off + n], preferred_element_type=F32)
        off += n


def _arrange_w_in(w_in_l):
    splits = np.cumsum(PROJ_SIZES)[:-1].tolist()
    s5_u, g_qkv, g_z, g_a, g_b, s_z, s_xbc, s_dt = jnp.split(w_in_l, splits, axis=-1)
    D = w_in_l.shape[0]
    small = jnp.zeros((D, SMALL_W), w_in_l.dtype)
    small = small.at[:, A_LANE:A_LANE + GDN_HEADS].set(g_a)
    small = small.at[:, B_LANE:B_LANE + GDN_HEADS].set(g_b)
    small = small.at[:, DT_LANE:DT_LANE + SSD_HEADS].set(s_dt)
    return jnp.concatenate([s5_u, g_qkv, g_z, s_z, s_xbc, small], axis=-1).astype(BF16)


def _proj_call(x, mod_l, norm_w, w_arr):
    B, S, D = x.shape
    tm = min(512, S)
    W = w_arr.shape[1]
    out_shape = [jax.ShapeDtypeStruct((B, S, n), F32) for n in PROJ_OUT_W]
    out_specs = [pl.BlockSpec((1, tm, n), lambda b, i: (b, i, 0)) for n in PROJ_OUT_W]
    return pl.pallas_call(
        _proj_kernel,
        out_shape=out_shape,
        grid=(B, S // tm),
        in_specs=[pl.BlockSpec((1, tm, D), lambda b, i: (b, i, 0)),
                  pl.BlockSpec((1, 6, D), lambda b, i: (b, 0, 0)),
                  pl.BlockSpec((1, D), lambda b, i: (0, 0)),
                  pl.BlockSpec((D, W), lambda b, i: (0, 0))],
        out_specs=out_specs,
        compiler_params=_cparams(("arbitrary", "arbitrary")),
        name="norm_in_proj",
    )(x, mod_l, norm_w.reshape(1, D), w_arr)


def _s5_kernel(u_ref, bm_ref, cm_ref, lam_ref, dsk_ref, glu_ref, nw_ref, perm_ref, o_ref,
               x_s, st_s, ubt_s, u_s, *, T):
    B = u_ref.shape[0]
    P = S5_LANES

    @pl.when(pl.program_id(0) == 0)
    def _():
        st_s[...] = jnp.zeros_like(st_s)

    nw = S5_WIDTH // LANE
    ubt = u_ref[...].reshape(B * T, S5_WIDTH)
    for j in range(nw):
        ubt_s[j] = ubt[:, j * LANE:(j + 1) * LANE]

    def regroup(t, carry):
        for j in range(nw):
            u_s[pl.ds(pl.multiple_of(t * B, B), B), j * LANE:(j + 1) * LANE] = ubt_s[j, pl.ds(t, B, stride=T), :]
        return carry

    lax.fori_loop(0, T, regroup, 0, unroll=8)
    u = u_s[...]
    x_s[...] = jnp.dot(u.astype(BF16), bm_ref[...], preferred_element_type=F32)
    lr = jnp.broadcast_to(lam_ref[0:1, :], (B, P))
    li = jnp.broadcast_to(lam_ref[1:2, :], (B, P))

    def step(t, carry):
        sr, si = carry
        rows = pl.ds(pl.multiple_of(t * B, B), B)
        nr = lr * sr - li * si + x_s[rows, 0:P]
        ni = lr * si + li * sr + x_s[rows, P:2 * P]
        x_s[rows, 0:P] = nr
        x_s[rows, P:2 * P] = ni
        return nr, ni

    sr, si = lax.fori_loop(0, T, step, (st_s[:, 0:P], st_s[:, P:2 * P]), unroll=4)
    st_s[:, 0:P] = sr
    st_s[:, P:2 * P] = si

    y = jnp.dot(x_s[...].astype(BF16), cm_ref[...], preferred_element_type=F32) + dsk_ref[...] * u
    y = jax.nn.gelu(y)
    y = y * jax.nn.sigmoid(jnp.dot(y.astype(BF16), glu_ref[...], preferred_element_type=F32))
    ms = jnp.mean(y * y, axis=-1, keepdims=True)
    y = (y * lax.rsqrt(ms + EPS) * nw_ref[...]).astype(BF16)
    y = jnp.dot(perm_ref[...], y, preferred_element_type=F32)
    o_ref[...] = y.reshape(B, T, S5_WIDTH).astype(o_ref.dtype)


def _s5_params(a_re, a_im, b_re, b_im, c_re, c_im, log_dt):
    G, P, CH = S5_GROUPS, S5_STATE, S5_CH
    lam = lax.complex(a_re.astype(F32), a_im.astype(F32))
    step = jnp.exp(log_dt.astype(F32))[:, None]
    lam_bar = jnp.exp(lam * step)
    b_bar = ((lam_bar - 1.0) / lam)[..., None] * lax.complex(b_re.astype(F32), b_im.astype(F32))
    eye = jnp.eye(G, dtype=F32)
    bre = jnp.einsum('gpc,gh->gchp', b_bar.real, eye).reshape(G * CH, G * P)
    bim = jnp.einsum('gpc,gh->gchp', b_bar.imag, eye).reshape(G * CH, G * P)
    bm = jnp.concatenate([bre, bim], axis=1).astype(BF16)
    cre = jnp.einsum('gcp,gh->gphc', c_re.astype(F32), eye).reshape(G * P, G * CH)
    cim = jnp.einsum('gcp,gh->gphc', c_im.astype(F32), eye).reshape(G * P, G * CH)
    cm = jnp.concatenate([cre, -cim], axis=0).astype(BF16)
    lam_rows = jnp.zeros((SUBLANE, G * P), F32)
    lam_rows = lam_rows.at[0].set(lam_bar.real.reshape(-1)).at[1].set(lam_bar.imag.reshape(-1))
    return bm, cm, lam_rows


def _s5_call(u, bm, cm, lam_rows, d_skip, w_glu, norm_w):
    B, S, W = u.shape
    T = min(128, S)
    P2 = 2 * S5_LANES
    const = lambda shape: pl.BlockSpec(shape, lambda i: tuple(0 for _ in shape))
    r = np.arange(B * T)
    perm = np.zeros((B * T, B * T), np.float32)
    perm[r, (r % T) * B + r // T] = 1.0
    return pl.pallas_call(
        functools.partial(_s5_kernel, T=T),
        out_shape=jax.ShapeDtypeStruct((B, S, W), BF16),
        grid=(S // T,),
        in_specs=[pl.BlockSpec((B, T, W), lambda i: (0, i, 0)),
                  const((W, P2)), const((P2, W)), const((SUBLANE, S5_LANES)),
                  const((1, W)), const((W, W)), const((1, W)), const((B * T, B * T))],
        out_specs=pl.BlockSpec((B, T, W), lambda i: (0, i, 0)),
        scratch_shapes=[pltpu.VMEM((B * T, P2), F32), pltpu.VMEM((B, P2), F32),
                        pltpu.VMEM((W // LANE, B * T, LANE), F32), pltpu.VMEM((B * T, W), F32)],
        compiler_params=_cparams(("arbitrary",)),
        name="s5_mixer",
    )(u, bm, cm, lam_rows, d_skip.reshape(1, W).astype(F32), w_glu.astype(BF16), norm_w.reshape(1, W).astype(F32),
      jnp.asarray(perm, BF16))


def _causal_conv_silu(x, xf_ref, cw_ref, bias):
    n = x.shape[0]
    xf_ref[SUBLANE:, :] = x
    acc = x * cw_ref[CONV_WIDTH - 1:CONV_WIDTH, :]
    for k in range(1, CONV_WIDTH):
        acc = acc + xf_ref[SUBLANE - k:SUBLANE - k + n, :] * cw_ref[CONV_WIDTH - 1 - k:CONV_WIDTH - k, :]
    xf_ref[0:SUBLANE, :] = x[n - SUBLANE:, :]
    if bias is not None:
        acc = acc + bias
    return _silu(acc)


def _store_row_tiles(ref, val):
    n = val.shape[0]
    for j in range(val.shape[1] // LANE):
        ref[pl.ds(j, n, stride=SUBLANE), :] = val[:, j * LANE:(j + 1) * LANE]


def _load_row_tiles(ref, n):
    return jnp.concatenate([ref[pl.ds(j, n, stride=SUBLANE), :] for j in range(SUBLANE)], axis=1)


def _lanes_from(sm, off):
    return pltpu.roll(sm, SMALL_W - off, axis=1) if off else sm


def _chunk_scans(g):
    n = g.shape[0]
    rin = lax.broadcasted_iota(jnp.int32, g.shape, 0) & (CHUNK - 1)
    pre = g
    suf = jnp.where(rin < CHUNK - 1, pltpu.roll(g, n - 1, axis=0), 0.0)
    s = 1
    while s < CHUNK:
        pre = pre + jnp.where(rin >= s, pltpu.roll(pre, s, axis=0), 0.0)
        suf = suf + jnp.where(rin + s <= CHUNK - 1, pltpu.roll(suf, n - s, axis=0), 0.0)
        s *= 2
    return pre, suf


def _tri_masks():
    r = lax.broadcasted_iota(jnp.int32, (CHUNK, CHUNK), 0)
    c = lax.broadcasted_iota(jnp.int32, (CHUNK, CHUNK), 1)
    return r >= c, r > c, r == c


def _head_expand():
    m = np.zeros((SMALL_W, GDN_WIDTH), np.float32)
    for h in range(GDN_HEADS):
        m[h, h * GDN_HEAD_DIM:(h + 1) * GDN_HEAD_DIM] = 1.0
    return jnp.asarray(m, BF16)


def _block_ones(width, blk):
    idx = np.arange(width) // blk
    return jnp.asarray((idx[:, None] == idx[None, :]).astype(np.float32), BF16)


def _gdn_kernel(qkv_ref, z_ref, sm_ref, cw_ref, hp_ref, nw_ref, e_ref, xp_ref, bdm_ref, bdr_ref, o_ref,
                tail_s, st_s, kn_s, kb_s, qn_s, qd_s, kd_s, rhs_s, eg_s, gcx_s, gw_s, o_s, at_s, *, TB):
    H, Dh, C = GDN_HEADS, GDN_HEAD_DIM, CHUNK
    W = GDN_WIDTH
    ncb = TB // C

    @pl.when(pl.program_id(1) == 0)
    def _():
        tail_s[0:SUBLANE, :] = jnp.zeros((SUBLANE, tail_s.shape[1]), F32)
        st_s[...] = jnp.zeros_like(st_s)

    xc = _causal_conv_silu(qkv_ref[0], tail_s, cw_ref, None)
    q, k, v = xc[:, 0:W], xc[:, W:2 * W], xc[:, 2 * W:3 * W]
    e = e_ref[...]
    xp = xp_ref[...]
    qn = q * lax.rsqrt(_dot2(q * q, e) + EPS) * (Dh ** -0.5)
    kn = k * lax.rsqrt(_dot2(k * k, e) + EPS)

    sm = sm_ref[0]
    lane = lax.broadcasted_iota(jnp.int32, sm.shape, 1)
    head_lane = lane < H
    g = jnp.where(head_lane, hp_ref[0:1, :] * _softplus(_lanes_from(sm, A_LANE) + hp_ref[1:2, :]), 0.0)
    beta = jnp.where(head_lane, jax.nn.sigmoid(_lanes_from(sm, B_LANE)), 0.0)
    gc, rc = _chunk_scans(g)
    bx = _dot2(beta, xp)
    egx = _dot2(jnp.exp(gc), xp)
    erx = _dot2(jnp.exp(rc), xp)
    kb = kn * bx
    kn_s[...] = kn
    kb_s[...] = kb
    qn_s[...] = qn
    qd_s[...] = qn * egx
    kd_s[...] = kn * erx
    for h in range(H):
        sl = slice(h * Dh, (h + 1) * Dh)
        rhs_s[:, 2 * h * Dh:(2 * h + 1) * Dh] = v[:, sl] * bx[:, sl]
        rhs_s[:, (2 * h + 1) * Dh:(2 * h + 2) * Dh] = kb[:, sl] * egx[:, sl]
    gcx = _dot_sel3(gc, xp)
    gcx_s[...] = gcx
    r384 = lax.broadcasted_iota(jnp.int32, (C, W), 0)
    l384 = lax.broadcasted_iota(jnp.int32, (C, W), 1) & (Dh - 1)
    for c in range(ncb):
        diag_c = jnp.where(r384 == l384, gcx[c * C:(c + 1) * C, :], 0.0)
        gw_s[c] = jnp.broadcast_to(jnp.sum(diag_c, axis=0, keepdims=True), (SUBLANE, W))
        eg_s[c] = jnp.broadcast_to(egx[(c + 1) * C - 1:(c + 1) * C, :], (SUBLANE, W))

    heads = [slice(h * Dh, (h + 1) * Dh) for h in range(H)]

    GL = 4 * Dh
    row_w = lax.broadcasted_iota(jnp.int32, (C, GL), 0)
    col_w = lax.broadcasted_iota(jnp.int32, (C, GL), 1) & (Dh - 1)
    causal_w, strict_w = row_w >= col_w, row_w > col_w
    eye_w = jnp.where(row_w == col_w, 1.0, 0.0).astype(F32)
    bdm = bdm_ref[...]
    bdr = bdr_ref[...]
    mm = functools.partial(lax.dot_general, dimension_numbers=_NN, preferred_element_type=F32)

    def block_diag(m, mask):
        return jnp.concatenate([m] * 4, axis=0) * mask

    npair = min(GDN_PAIRS_PER_ITER, ncb // 2)

    def solve(it, carry):
        chunks = [2 * npair * it + k for k in range(2 * npair)]
        rows = [pl.ds(pl.multiple_of(c * C, C), C) for c in chunks]

        def groups(get, n):
            out = []
            for j in range(npair):
                a0, a1 = get(2 * j), get(2 * j + 1)
                out += [a0[:, 0:4 * n], a1[:, 0:4 * n],
                        jnp.concatenate([a0[:, 4 * n:6 * n], a1[:, 4 * n:6 * n]], axis=1)]
            return out

        def ungroup(ref, vals, n):
            for j in range(npair):
                ref[rows[2 * j], 0:4 * n] = vals[3 * j]
                ref[rows[2 * j + 1], 0:4 * n] = vals[3 * j + 1]
                ref[rows[2 * j], 4 * n:6 * n] = vals[3 * j + 2][:, 0:2 * n]
                ref[rows[2 * j + 1], 4 * n:6 * n] = vals[3 * j + 2][:, 2 * n:4 * n]

        kn_g = groups(lambda k: kn_s[rows[k], :], Dh)
        kb_g = groups(lambda k: kb_s[rows[k], :], Dh)
        qn_g = groups(lambda k: qn_s[rows[k], :], Dh)
        gx_g = groups(lambda k: gcx_s[rows[k], :], Dh)
        gw_g = groups(lambda k: gw_s[chunks[k], 0:1, :], Dh)
        decs = [jnp.exp(jnp.where(causal_w, gx - gw, NEG_BIG)) for gx, gw in zip(gx_g, gw_g)]
        prods = [lax.dot_general(jnp.concatenate([kb, qn], axis=0).astype(BF16), block_diag(kn.astype(BF16), bdm),
                                 _NT, preferred_element_type=F32)
                 for kn, kb, qn in zip(kn_g, kb_g, qn_g)]
        ungroup(at_s, [pr[C:2 * C] * dec for pr, dec in zip(prods, decs)], Dh)
        def times(lhs, p=None):
            lh, ll = _split(lhs)
            ph, pl_ = (lh[0:C], ll[0:C]) if p is None else _split(p)
            rh, rl = block_diag(ph, bdm), block_diag(pl_, bdm)
            return mm(lh, rh) + (mm(lh, rl) + mm(ll, rh))

        ps = [-jnp.where(strict_w, pr[0:C] * dec, 0.0) for pr, dec in zip(prods, decs)]
        invs = [eye_w + p for p in ps]
        ps = [times(p) for p in ps]
        for _ in range(4):
            outs = [times(jnp.concatenate([p, inv], axis=0)) for p, inv in zip(ps, invs)]
            invs = [inv + o[C:2 * C] for inv, o in zip(invs, outs)]
            ps = [o[0:C] for o in outs]
        invs = [inv + times(inv, p) for p, inv in zip(ps, invs)]
        uws = []
        for inv, r in zip(invs, groups(lambda k: rhs_s[rows[k], :], 2 * Dh)):
            ih, il = _split(inv)
            rh, rl = _split(r)
            rh, rl = block_diag(rh, bdr), block_diag(rl, bdr)
            uws.append(mm(ih, rh) + (mm(ih, rl) + mm(il, rh)))
        ungroup(rhs_s, uws, 2 * Dh)
        return carry

    lax.fori_loop(0, ncb // (2 * npair), solve, 0)

    def recur(c, carry):
        rows = pl.ds(pl.multiple_of(c * C, C), C)
        sts = [st_s[h] for h in range(H)]
        uws = [rhs_s[rows, 2 * h * Dh:(2 * h + 2) * Dh] for h in range(H)]
        v_news = [uw[:, 0:Dh] - _dot(uw[:, Dh:2 * Dh], st) for uw, st in zip(uws, sts)]
        for sl, st, v_new in zip(heads, sts, v_news):
            o_s[rows, sl] = _dot(qd_s[rows, sl], st) + _dot(at_s[rows, sl], v_new)
        for h, (sl, st, v_new) in enumerate(zip(heads, sts, v_news)):
            st_s[h] = st * eg_s[c, 0:1, sl] + _dot(kd_s[rows, sl], v_new, _TN)
        return carry

    lax.fori_loop(0, ncb, recur, 0)

    o = o_s[...]
    ms = _dot2(o * o, e) * (1.0 / Dh)
    o = o * lax.rsqrt(ms + EPS) * nw_ref[...]
    o_ref[0] = (o * _silu(z_ref[0])).astype(o_ref.dtype)


def _gdn_call(qkv, z, sm, conv_w, a_log, dt_bias, norm_w):
    B, S, _ = qkv.shape
    TB = min(512, S)
    W, H, Dh = GDN_WIDTH, GDN_HEADS, GDN_HEAD_DIM
    hp = jnp.zeros((SUBLANE, SMALL_W), F32)
    hp = hp.at[0, :H].set(-jnp.exp(a_log.astype(F32))).at[1, :H].set(dt_bias.astype(F32))
    nw = jnp.tile(norm_w.astype(F32), H).reshape(1, W)
    const = lambda shape: pl.BlockSpec(shape, lambda b, i: tuple(0 for _ in shape))
    blk = lambda n: pl.BlockSpec((1, TB, n), lambda b, i: (b, i, 0))
    f = lambda *shape: pltpu.VMEM(shape, F32)
    return pl.pallas_call(
        functools.partial(_gdn_kernel, TB=TB),
        out_shape=jax.ShapeDtypeStruct((B, S, W), BF16),
        grid=(B, S // TB),
        in_specs=[blk(GDN_CONV_DIM), blk(W), blk(SMALL_W),
                  const((CONV_WIDTH, GDN_CONV_DIM)), const((SUBLANE, SMALL_W)), const((1, W)),
                  const((W, W)), const((SMALL_W, W)), const((4 * Dh, 4 * Dh)), const((4 * Dh, 8 * Dh))],
        out_specs=blk(W),
        scratch_shapes=[f(SUBLANE + TB, GDN_CONV_DIM), f(H, GDN_HEAD_DIM, GDN_HEAD_DIM),
                        f(TB, W), f(TB, W), f(TB, W), f(TB, W), f(TB, W), f(TB, 2 * W), f(TB // CHUNK, SUBLANE, W),
                        f(TB, W), f(TB // CHUNK, SUBLANE, W), f(TB, W), f(TB, W)],
        compiler_params=_cparams(("arbitrary", "arbitrary")),
        name="gdn_mixer",
    )(qkv, z, sm, conv_w.astype(F32), hp, nw, _block_ones(W, Dh), _head_expand(),
      _block_ones(4 * Dh, Dh), jnp.repeat(_block_ones(4 * Dh, Dh), 2, axis=1))


def _ssd_kernel(xbc_ref, z_ref, sm_ref, cw_ref, cb_ref, hp_ref, dsk_ref, nw_ref, e_ref, xp_ref, o_ref,
                tail_s, st_s, xs_s, xdt_s, xdd_s, bm_s, cm_s, ea_s, el_s, ac_s, act_s, y_s, inc_s, *, TB):
    H, P, G, N, C = SSD_HEADS, SSD_HEAD_DIM, SSD_GROUPS, SSD_STATE, CHUNK
    R = H // G
    W, GW = SSD_WIDTH, SSD_GROUP_WIDTH
    ncb = TB // C

    @pl.when(pl.program_id(1) == 0)
    def _():
        tail_s[0:SUBLANE, :] = jnp.zeros((SUBLANE, tail_s.shape[1]), F32)
        st_s[...] = jnp.zeros_like(st_s)

    xc = _causal_conv_silu(xbc_ref[0], tail_s, cw_ref, cb_ref[...])
    xs = xc[:, 0:W]
    xp = xp_ref[...]

    sm = sm_ref[0]
    lane = lax.broadcasted_iota(jnp.int32, sm.shape, 1)
    head_lane = lane < H
    dt = jnp.where(head_lane, _softplus(_lanes_from(sm, DT_LANE) + hp_ref[1:2, :]), 0.0)
    acs, rcs = _chunk_scans(dt * hp_ref[0:1, :])
    dtx = _dot2(dt, xp)
    eax = _dot2(jnp.exp(acs), xp)
    erx = _dot2(jnp.exp(rcs), xp)
    xdt = xs * dtx
    xs_s[...] = xs
    xdt_s[...] = xdt
    xdd_s[...] = xdt * erx
    bm_s[...] = xc[:, W:W + G * N]
    cm_s[...] = xc[:, W + G * N:W + 2 * G * N]
    ea_s[...] = eax
    ac_s[...] = acs
    for c in range(ncb):
        act_s[c] = acs[c * C:(c + 1) * C, :].T
        el_s[c] = jnp.broadcast_to(eax[(c + 1) * C - 1:(c + 1) * C, :], (SUBLANE, W))

    causal, _, _ = _tri_masks()

    groups = [(gi, slice(gi * GW, (gi + 1) * GW), slice(gi * N, (gi + 1) * N)) for gi in range(G)]

    def local(it, carry):
        work = []
        for cc in range(SSD_CHUNKS_PER_ITER):
            c = it * SSD_CHUNKS_PER_ITER + cc
            rows = pl.ds(pl.multiple_of(c * C, C), C)
            work += [(c, rows, ac_s[rows, :], act_s[c], g) for g in groups]
        cbs = [_dot(cm_s[rows, nsl], bm_s[rows, nsl], _NT) for c, rows, _, _, (gi, gsl, nsl) in work]
        for (c, rows, acol, arow, (gi, gsl, nsl)), cb in zip(work, cbs):
            for r in range(R):
                h = gi * R + r
                sl = slice(h * P, (h + 1) * P)
                seg = jnp.exp(jnp.where(causal, acol[:, h:h + 1] - arow[h:h + 1, :], NEG_BIG))
                y_s[rows, sl] = _dot(cb * seg, xdt_s[rows, sl])
        for c, rows, _, _, (gi, gsl, nsl) in work:
            inc_s[c, gi] = _dot(bm_s[rows, nsl], xdd_s[rows, gsl], _TN)
        return carry

    lax.fori_loop(0, ncb // SSD_CHUNKS_PER_ITER, local, 0)

    def recur(c, carry):
        rows = pl.ds(pl.multiple_of(c * C, C), C)
        sts = [st_s[gi] for gi, _, _ in groups]
        for (gi, gsl, nsl), st in zip(groups, sts):
            y_s[rows, gsl] = y_s[rows, gsl] + _dot(cm_s[rows, nsl], st) * ea_s[rows, gsl]
        for (gi, gsl, nsl), st in zip(groups, sts):
            st_s[gi] = st * el_s[c, 0:1, gsl] + inc_s[c, gi]
        return carry

    lax.fori_loop(0, ncb, recur, 0)

    y = y_s[...] + dsk_ref[...] * xs_s[...]
    y = y * _silu(z_ref[0])
    ms = _dot2(y * y, e_ref[...]) * (1.0 / GW)
    o_ref[0] = (y * lax.rsqrt(ms + EPS) * nw_ref[...]).astype(o_ref.dtype)


def _ssd_call(xbc, z, sm, conv_w, conv_b, a_log, dt_bias, d_skip, norm_w):
    B, S, _ = xbc.shape
    TB = min(512, S)
    W, H, G, N = SSD_WIDTH, SSD_HEADS, SSD_GROUPS, SSD_STATE
    hp = jnp.zeros((SUBLANE, SMALL_W), F32)
    hp = hp.at[0, :H].set(-jnp.exp(a_log.astype(F32))).at[1, :H].set(dt_bias.astype(F32))
    dsk = jnp.repeat(d_skip.astype(F32), SSD_HEAD_DIM).reshape(1, W)
    const = lambda shape: pl.BlockSpec(shape, lambda b, i: tuple(0 for _ in shape))
    blk = lambda n: pl.BlockSpec((1, TB, n), lambda b, i: (b, i, 0))
    f = lambda *shape: pltpu.VMEM(shape, F32)
    return pl.pallas_call(
        functools.partial(_ssd_kernel, TB=TB),
        out_shape=jax.ShapeDtypeStruct((B, S, W), BF16),
        grid=(B, S // TB),
        in_specs=[blk(SSD_CONV_DIM), blk(W), blk(SMALL_W),
                  const((CONV_WIDTH, SSD_CONV_DIM)), const((1, SSD_CONV_DIM)), const((SUBLANE, SMALL_W)),
                  const((1, W)), const((1, W)), const((W, W)), const((SMALL_W, W))],
        out_specs=blk(W),
        scratch_shapes=[f(SUBLANE + TB, SSD_CONV_DIM), f(G, N, SSD_GROUP_WIDTH),
                        f(TB, W), f(TB, W), f(TB, W), f(TB, G * N), f(TB, G * N), f(TB, W),
                        f(TB // CHUNK, SUBLANE, W),
                        f(TB, SMALL_W), f(TB // CHUNK, SMALL_W, CHUNK), f(TB, W),
                        f(TB // CHUNK, G, N, SSD_GROUP_WIDTH)],
        compiler_params=_cparams(("arbitrary", "arbitrary")),
        name="ssd_mixer",
    )(xbc, z, sm, conv_w.astype(F32), conv_b.reshape(1, -1).astype(F32), hp, dsk,
      norm_w.reshape(1, W).astype(F32), _block_ones(W, SSD_GROUP_WIDTH), _head_expand())


def _out_kernel(x_ref, y1_ref, y2_ref, y3_ref, mod_ref, nw_ref, wo_ref, wr_ref, br_ref, tri_ref,
                x1_ref, h_ref, rt_ref, cnt_ref, run_s):
    y = (jnp.dot(y1_ref[0], wo_ref[0:S5_WIDTH, :], preferred_element_type=F32)
         + jnp.dot(y2_ref[0], wo_ref[S5_WIDTH:S5_WIDTH + GDN_WIDTH, :], preferred_element_type=F32)
         + jnp.dot(y3_ref[0], wo_ref[S5_WIDTH + GDN_WIDTH:, :], preferred_element_type=F32))
    x1 = x_ref[0] + mod_ref[0, 2:3, :] * y
    x1_ref[0] = x1
    h = _norm_mod(x1, nw_ref[...], mod_ref[0, 4:5, :], mod_ref[0, 3:4, :])
    _store_row_tiles(h_ref, h)
    lg =jnp.dot(h.astype(BF16), wr_ref[...], preferred_element_type=F32) + br_ref[...]

    lane = lax.broadcasted_iota(jnp.int32, lg.shape, 1)
    lanef = lane.astype(F32)
    big = float(4 * LANE)
    grp = (lane >= GRP_LANE) & (lane < GRP_LANE + MOE_GROUPS)
    lgm = jnp.where(grp, lg, -jnp.inf)
    m = jnp.max(lgm, axis=-1, keepdims=True)
    gidx = jnp.min(jnp.where(lgm == m, lanef - GRP_LANE, big), axis=-1, keepdims=True)
    g_w = 1.0 / jnp.sum(jnp.where(grp, jnp.exp(lg - m), 0.0), axis=-1, keepdims=True)
    in_grp = (lane < N_EXPERTS) & ((lane // EXPERTS_PER_GROUP).astype(F32) == gidx)
    le = jnp.where(in_grp, lg, -jnp.inf)
    v1 = jnp.max(le, axis=-1, keepdims=True)
    i1 = jnp.min(jnp.where(le == v1, lanef, big), axis=-1, keepdims=True)
    le2 = jnp.where(lanef == i1, -jnp.inf, le)
    v2 = jnp.max(le2, axis=-1, keepdims=True)
    i2 = jnp.min(jnp.where(le2 == v2, lanef, big), axis=-1, keepdims=True)
    e2 = jnp.exp(v2 - v1)
    w1 = g_w / (1.0 + e2)
    w2 = g_w * e2 / (1.0 + e2)

    @pl.when((pl.program_id(0) == 0) & (pl.program_id(1) == 0))
    def _():
        run_s[...] = jnp.zeros_like(run_s)

    chosen = jnp.where((lanef == i1) | (lanef == i2), 1.0, 0.0)
    before = jnp.dot(tri_ref[...], chosen.astype(BF16), preferred_element_type=F32) + run_s[0:1, :]
    p1 = jnp.sum(jnp.where(lanef == i1, before, 0.0), axis=-1, keepdims=True)
    p2 = jnp.sum(jnp.where(lanef == i2, before, 0.0), axis=-1, keepdims=True)
    run_s[...] = run_s[...] + jnp.sum(chosen, axis=0, keepdims=True)
    cnt_ref[...] = run_s[...]

    rt = jnp.zeros_like(lg)
    for k, val in enumerate((i1, i2, w1, w2, p1, p2)):
        rt = jnp.where(lane == k, val, rt)
    rt_ref[0] = rt


def _out_call(x, y1, y2, y3, mod_l, norm_w, w_out, w_router, b_router):
    B, S, D = x.shape
    tm = min(512, S)
    blk = lambda n: pl.BlockSpec((1, tm, n), lambda b, i: (b, i, 0))
    const = lambda shape: pl.BlockSpec(shape, lambda b, i: tuple(0 for _ in shape))
    tri = jnp.asarray(np.tril(np.ones((tm, tm), np.float32), -1), BF16)
    nb = S // tm
    tiles = D // LANE
    return pl.pallas_call(
        _out_kernel,
        out_shape=[jax.ShapeDtypeStruct((B, S, D), F32), jax.ShapeDtypeStruct((B * S * tiles, LANE), F32),
                   jax.ShapeDtypeStruct((B, S, LANE), F32), jax.ShapeDtypeStruct((SUBLANE, LANE), F32)],
        grid=(B, nb),
        in_specs=[blk(D), blk(S5_WIDTH), blk(GDN_WIDTH), blk(SSD_WIDTH),
                  pl.BlockSpec((1, 6, D), lambda b, i: (b, 0, 0)),
                  const((1, D)), const((D, D)), const((D, LANE)), const((1, LANE)), const((tm, tm))],
        out_specs=[blk(D), pl.BlockSpec((tm * tiles, LANE), lambda b, i: (b * nb + i, 0)),
                   blk(LANE), const((SUBLANE, LANE))],
        scratch_shapes=[pltpu.VMEM((SUBLANE, LANE), F32)],
        compiler_params=_cparams(("arbitrary", "arbitrary")),
        name="out_proj_router",
    )(x, y1, y2, y3, mod_l, norm_w.reshape(1, D), w_out, w_router, b_router, tri)


def _router_params(w_grp, b_grp, w_rt, b_rt):
    D = w_grp.shape[0]
    w = jnp.zeros((D, LANE), F32).at[:, 0:N_EXPERTS].set(w_rt).at[:, GRP_LANE:GRP_LANE + MOE_GROUPS].set(w_grp)
    b = jnp.zeros((1, LANE), F32).at[0, 0:N_EXPERTS].set(b_rt).at[0, GRP_LANE:GRP_LANE + MOE_GROUPS].set(b_grp)
    return w.astype(BF16), b


def _tile_copy_loop(n, fn):
    def body(t, carry):
        fn(t)
        return carry
    lax.fori_loop(0, n, body, 0, unroll=8)


def _scatter_kernel(d0_ref, d1_ref, h_ref, xs_hbm, stage, sem):
    i = pl.program_id(0)
    n = pl.num_programs(0)
    slot = lax.rem(i, 2)
    tm = d0_ref.shape[-1]

    def copy(s, t, dst):
        return pltpu.make_async_copy(stage.at[s, pl.ds(t * SUBLANE, SUBLANE)],
                                     xs_hbm.at[pl.ds(dst * SUBLANE, SUBLANE)], sem.at[s])

    def wait_slot(s):
        _tile_copy_loop(TOP_K * tm, lambda t: copy(s, 0, 0).wait())

    @pl.when(i >= 2)
    def _():
        wait_slot(slot)

    stage[slot] = h_ref[...]

    def start(t):
        copy(slot, t, d0_ref[0, 0, t]).start(priority=0)
        copy(slot, t, d1_ref[0, 0, t]).start(priority=1)
    _tile_copy_loop(tm, start)

    @pl.when(i == n - 1)
    def _():
        @pl.when(i >= 1)
        def _():
            wait_slot(1 - slot)
        wait_slot(slot)


def _scatter_call(h2t, dest0, dest1, n_rows):
    n_tiles, _, tm = dest0.shape
    idx_blk = pl.BlockSpec((1, 1, tm), lambda i: (i, 0, 0), memory_space=pltpu.SMEM)
    return pl.pallas_call(
        _scatter_kernel,
        out_shape=jax.ShapeDtypeStruct((n_rows * SUBLANE, LANE), F32),
        grid=(n_tiles,),
        in_specs=[idx_blk, idx_blk, pl.BlockSpec((tm * SUBLANE, LANE), lambda i: (i, 0))],
        out_specs=pl.BlockSpec(memory_space=pl.ANY),
        scratch_shapes=[pltpu.VMEM((2, tm * SUBLANE, LANE), F32), pltpu.SemaphoreType.DMA((2,))],
        compiler_params=_cparams(("arbitrary",)),
        name="moe_scatter",
    )(dest0, dest1, h2t)


def _expert_kernel(nused_ref, blke_ref, nvalid_ref, xs_ref, wg_ref, wu_ref, wd_ref, ys_ref):
    i = pl.program_id(0)

    @pl.when(i < nused_ref[0])
    def _():
        x = _load_row_tiles(xs_ref, MOE_ROWS)
        row = lax.broadcasted_iota(jnp.int32, (MOE_ROWS, 1), 0)
        xb = jnp.where(row < nvalid_ref[i], x, 0.0).astype(BF16)
        hid = _silu(jnp.dot(xb, wg_ref[0], preferred_element_type=F32)) * jnp.dot(xb, wu_ref[0], preferred_element_type=F32)
        _store_row_tiles(ys_ref, jnp.dot(hid.astype(BF16), wd_ref[0], preferred_element_type=F32))

    @pl.when(i >= nused_ref[0])
    def _():
        ys_ref[...] = jnp.zeros_like(ys_ref)


def _expert_call(xs, n_used, blk_e, n_valid, w_gate, w_up, w_down):
    D = w_gate.shape[1]
    n_blk = blk_e.shape[0]
    rows_blk = pl.BlockSpec((MOE_ROWS * SUBLANE, LANE), lambda i, nu, be, nv: (i, 0))
    grid_spec = pltpu.PrefetchScalarGridSpec(
        num_scalar_prefetch=3,
        grid=(n_blk,),
        in_specs=[rows_blk,
                  pl.BlockSpec((1, D, D_EXPERT), lambda i, nu, be, nv: (be[i], 0, 0)),
                  pl.BlockSpec((1, D, D_EXPERT), lambda i, nu, be, nv: (be[i], 0, 0)),
                  pl.BlockSpec((1, D_EXPERT, D), lambda i, nu, be, nv: (be[i], 0, 0))],
        out_specs=rows_blk,
    )
    return pl.pallas_call(
        _expert_kernel,
        out_shape=jax.ShapeDtypeStruct(xs.shape, F32),
        grid_spec=grid_spec,
        compiler_params=_cparams(("arbitrary",)),
        name="expert_mlp",
    )(n_used, blk_e, n_valid, xs, w_gate, w_up, w_down)


def _dispatch(route, counts, N, tm):
    L_pad = N * TOP_K + N_EXPERTS * MOE_ROWS
    n_blk = L_pad // MOE_ROWS
    eid = route[:, 0:TOP_K].astype(jnp.int32)
    pos = route[:, 4:4 + TOP_K].astype(jnp.int32)
    counts = counts.astype(jnp.int32)
    padded = ((counts + MOE_ROWS - 1) // MOE_ROWS) * MOE_ROWS
    pend = jnp.cumsum(padded)
    pstart = pend - padded
    dest = pstart[eid] + pos
    n_used = (pend[-1] // MOE_ROWS).astype(jnp.int32).reshape(1)
    blk_row0 = jnp.arange(n_blk, dtype=jnp.int32) * MOE_ROWS
    blk_e = jnp.minimum(jnp.sum((pend[None, :] <= blk_row0[:, None]).astype(jnp.int32), axis=1), N_EXPERTS - 1)
    n_valid = jnp.clip((pstart + counts)[blk_e] - blk_row0, 0, MOE_ROWS).astype(jnp.int32)
    dest0 = dest[:, 0].reshape(N // tm, 1, tm)
    dest1 = dest[:, 1].reshape(N // tm, 1, tm)
    return n_used, blk_e.astype(jnp.int32), n_valid, dest0, dest1, L_pad


def _combine_kernel(d0_ref, d1_ref, d0n_ref, d1n_ref, x_ref, rt_ref, mod_ref, nf_ref, ys_hbm, o_ref,
                    buf, sem, *, final):
    i = pl.program_id(0)
    n = pl.num_programs(0)
    slot = lax.rem(i, 2)
    tm = x_ref.shape[0]

    def copy(s, k, t, src):
        return pltpu.make_async_copy(ys_hbm.at[pl.ds(src * SUBLANE, SUBLANE)],
                                     buf.at[s, k, pl.ds(t * SUBLANE, SUBLANE)], sem.at[s])

    def start_tile(s, a_ref, b_ref):
        def start(t):
            copy(s, 0, t, a_ref[0, 0, t]).start(priority=0)
            copy(s, 1, t, b_ref[0, 0, t]).start(priority=1)
        _tile_copy_loop(tm, start)

    @pl.when(i == 0)
    def _():
        start_tile(0, d0_ref, d1_ref)

    @pl.when(i + 1 < n)
    def _():
        start_tile(1 - slot, d0n_ref, d1n_ref)

    _tile_copy_loop(TOP_K * tm, lambda t: copy(slot, 0, 0, 0).wait())
    rt = rt_ref[...]
    y = rt[:, 2:3] * _load_row_tiles(buf.at[slot, 0], tm) + rt[:, 3:4] * _load_row_tiles(buf.at[slot, 1], tm)
    x2 = x_ref[...] + mod_ref[0, 5:6, :] * y
    if final:
        ms = jnp.mean(x2 * x2, axis=-1, keepdims=True)
        x2 = x2 * lax.rsqrt(ms + EPS) * nf_ref[...]
    o_ref[...] = x2


def _combine_call(x1, ys, route, dest0, dest1, mod_l, norm_final, final):
    B, S, D = x1.shape
    N = B * S
    n_tiles, _, tm = dest0.shape
    per_b = S // tm
    idx_blk = lambda fn: pl.BlockSpec((1, 1, tm), fn, memory_space=pltpu.SMEM)
    cur = lambda i: (i, 0, 0)
    nxt = lambda i: (jnp.minimum(i + 1, n_tiles - 1), 0, 0)
    out = pl.pallas_call(
        functools.partial(_combine_kernel, final=final),
        out_shape=jax.ShapeDtypeStruct((N, D), F32),
        grid=(n_tiles,),
        in_specs=[idx_blk(cur), idx_blk(cur), idx_blk(nxt), idx_blk(nxt),
                  pl.BlockSpec((tm, D), lambda i: (i, 0)),
                  pl.BlockSpec((tm, LANE), lambda i: (i, 0)),
                  pl.BlockSpec((1, 6, D), lambda i: (i // per_b, 0, 0)),
                  pl.BlockSpec((1, D), lambda i: (0, 0)),
                  pl.BlockSpec(memory_space=pl.ANY)],
        out_specs=pl.BlockSpec((tm, D), lambda i: (i, 0)),
        scratch_shapes=[pltpu.VMEM((2, TOP_K, tm * SUBLANE, LANE), F32), pltpu.SemaphoreType.DMA((2,))],
        compiler_params=_cparams(("arbitrary",)),
        name="moe_combine",
    )(dest0, dest1, dest0, dest1, x1.reshape(N, D), route.reshape(N, LANE), mod_l, norm_final.reshape(1, D), ys)
    return out.reshape(B, S, D)


def _layer(x, mod_l, p, final, norm_final):
    B, S, D = x.shape
    N = B * S
    s5_u, g_qkv, g_z, s_z, s_xbc, small = _proj_call(x, mod_l, p["norm_mix"], _arrange_w_in(p["w_in"]))
    bm, cm, lam_rows = _s5_params(p["s5_a_re"], p["s5_a_im"], p["s5_b_re"], p["s5_b_im"],
                                  p["s5_c_re"], p["s5_c_im"], p["s5_log_dt"])
    y_s5 = _s5_call(s5_u, bm, cm, lam_rows, p["s5_d"], p["s5_w_glu"], p["s5_norm"])
    y_gdn = _gdn_call(g_qkv, g_z, small, p["gdn_conv_w"], p["gdn_a_log"], p["gdn_dt_bias"], p["gdn_norm"])
    y_ssd = _ssd_call(s_xbc, s_z, small, p["ssd_conv_w"], p["ssd_conv_b"], p["ssd_a_log"], p["ssd_dt_bias"],
                      p["ssd_d"], p["ssd_norm"])
    w_router, b_router = _router_params(p["moe_w_grp"], p["moe_b_grp"], p["moe_w_rt"], p["moe_b_rt"])
    x1, h2, route, counts = _out_call(x, y_s5, y_gdn, y_ssd, mod_l, p["norm_ffn"], p["w_out"].astype(BF16),
                                      w_router, b_router)
    n_used, blk_e, n_valid, dest0, dest1, n_rows = _dispatch(route.reshape(N, LANE), counts[0, 0:N_EXPERTS], N,
                                                             min(MOE_TOKEN_TILE, S))
    xs = _scatter_call(h2, dest0, dest1, n_rows)
    ys = _expert_call(xs, n_used, blk_e, n_valid,
                      p["moe_w_gate"].astype(BF16), p["moe_w_up"].astype(BF16), p["moe_w_down"].astype(BF16))
    return _combine_call(x1, ys, route, dest0, dest1, mod_l, norm_final, final)


def kernel(x, c, w_ada, b_ada, norm_mix, norm_ffn, w_in, w_out, s5_a_re, s5_a_im, s5_b_re, s5_b_im, s5_c_re, s5_c_im, s5_d, s5_log_dt, s5_w_glu, s5_norm, gdn_conv_w, gdn_a_log, gdn_dt_bias, gdn_norm, ssd_conv_w, ssd_conv_b, ssd_a_log, ssd_dt_bias, ssd_d, ssd_norm, moe_w_grp, moe_b_grp, moe_w_rt, moe_b_rt, moe_w_gate, moe_w_up, moe_w_down, norm_final):
    stacked = dict(norm_mix=norm_mix, norm_ffn=norm_ffn, w_in=w_in, w_out=w_out, s5_a_re=s5_a_re, s5_a_im=s5_a_im,
                   s5_b_re=s5_b_re, s5_b_im=s5_b_im, s5_c_re=s5_c_re, s5_c_im=s5_c_im, s5_d=s5_d,
                   s5_log_dt=s5_log_dt, s5_w_glu=s5_w_glu, s5_norm=s5_norm, gdn_conv_w=gdn_conv_w,
                   gdn_a_log=gdn_a_log, gdn_dt_bias=gdn_dt_bias, gdn_norm=gdn_norm, ssd_conv_w=ssd_conv_w,
                   ssd_conv_b=ssd_conv_b, ssd_a_log=ssd_a_log, ssd_dt_bias=ssd_dt_bias, ssd_d=ssd_d,
                   ssd_norm=ssd_norm, moe_w_grp=moe_w_grp, moe_b_grp=moe_b_grp, moe_w_rt=moe_w_rt,
                   moe_b_rt=moe_b_rt, moe_w_gate=moe_w_gate, moe_w_up=moe_w_up, moe_w_down=moe_w_down)
    L = w_in.shape[0]
    B, S, D = x.shape
    mod = _mod_call(c, w_ada, b_ada).reshape(L, B, 6, D)
    for l in range(L):
        p = {k: v[l] for k, v in stacked.items()}
        x = _layer(x, mod[l], p, l == L - 1, norm_final)
    return x
```

```python
import functools

import numpy as np
import jax
import jax.numpy as jnp
from jax import lax
from jax.experimental import pallas as pl
from jax.experimental.pallas import tpu as pltpu

F32 = jnp.float32
BF16 = jnp.bfloat16

D_MODEL = 1024
DEPTH = 4
EPS = 1e-6
CONV_WIDTH = 4
CHUNK = 64
S5_WIDTH = 256
S5_CH = 16
S5_GROUPS = 16
S5_STATE = 64
S5_LANES = S5_GROUPS * S5_STATE
GDN_WIDTH = 384
GDN_HEAD_DIM = 64
GDN_HEADS = 6
GDN_CONV_DIM = 3 * GDN_WIDTH
SSD_WIDTH = 384
SSD_HEAD_DIM = 64
SSD_HEADS = 6
SSD_GROUPS = 2
SSD_STATE = 128
SSD_GROUP_WIDTH = SSD_WIDTH // SSD_GROUPS
SSD_CONV_DIM = SSD_WIDTH + 2 * SSD_GROUPS * SSD_STATE
PROJ_SIZES = (S5_WIDTH, GDN_CONV_DIM, GDN_WIDTH, GDN_HEADS, GDN_HEADS, SSD_WIDTH, SSD_CONV_DIM, SSD_HEADS)
MOE_GROUPS = 4
EXPERTS_PER_GROUP = 8
N_EXPERTS = 32
TOP_K = 2
D_EXPERT = 256

LANE = 128
SUBLANE = 8
SMALL_W = LANE
A_LANE, B_LANE, DT_LANE = 0, GDN_HEADS, 2 * GDN_HEADS
GRP_LANE = N_EXPERTS
MOE_ROWS = 512
MOE_TOKEN_TILE = 256
GDN_PAIRS_PER_ITER = 4
SSD_CHUNKS_PER_ITER = 2
VMEM_LIMIT = 56 * 1024 * 1024
NEG_BIG = -1e30


def _cparams(sem):
    return pltpu.CompilerParams(dimension_semantics=sem, vmem_limit_bytes=VMEM_LIMIT)


def _split(a):
    hi = a.astype(BF16)
    lo = (a - hi.astype(F32)).astype(BF16)
    return hi, lo


_NN = (((1,), (0,)), ((), ()))
_NT = (((1,), (1,)), ((), ()))
_TN = (((0,), (0,)), ((), ()))


def _dot(a, b, dims=_NN):
    return lax.dot_general(a.astype(BF16), b.astype(BF16), dims, preferred_element_type=F32)


def _dot2(a, b_bf16):
    hi, lo = _split(a)
    return (lax.dot_general(hi, b_bf16, _NN, preferred_element_type=F32)
            + lax.dot_general(lo, b_bf16, _NN, preferred_element_type=F32))


def _dot_sel3(a, b_bf16):
    h1 = a.astype(BF16)
    r1 = a - h1.astype(F32)
    h2 = r1.astype(BF16)
    h3 = (r1 - h2.astype(F32)).astype(BF16)
    d = functools.partial(lax.dot_general, dimension_numbers=_NN, preferred_element_type=F32)
    return d(h1, b_bf16) + (d(h2, b_bf16) + d(h3, b_bf16))


def _dot3(a, b):
    ah, al = _split(a)
    bh, bl = _split(b)
    d = functools.partial(lax.dot_general, dimension_numbers=_NN, preferred_element_type=F32)
    return d(ah, bh) + (d(ah, bl) + d(al, bh))


def _silu(x):
    return x * jax.nn.sigmoid(x)


def _softplus(x):
    return jnp.maximum(x, 0.0) + jnp.log(1.0 + jnp.exp(-jnp.abs(x)))


def _norm_mod(x, w, scale, shift):
    ms = jnp.mean(x * x, axis=-1, keepdims=True)
    return (x * lax.rsqrt(ms + EPS) * w) * (1.0 + scale) + shift


def _mod_kernel(c_ref, w_ref, b_ref, o_ref):
    cond = _silu(c_ref[...])
    o_ref[0] = _dot3(cond, w_ref[0]) + b_ref[0]


def _mod_call(c, w_ada, b_ada):
    L, D, W = w_ada.shape
    B = c.shape[0]
    tn = 1536
    return pl.pallas_call(
        _mod_kernel,
        out_shape=jax.ShapeDtypeStruct((L, B, W), F32),
        grid=(L, W // tn),
        in_specs=[pl.BlockSpec((B, D), lambda l, j: (0, 0)),
                  pl.BlockSpec((1, D, tn), lambda l, j: (l, 0, j)),
                  pl.BlockSpec((1, 1, tn), lambda l, j: (l, 0, j))],
        out_specs=pl.BlockSpec((1, B, tn), lambda l, j: (l, 0, j)),
        compiler_params=_cparams(("arbitrary", "arbitrary")),
        name="adaln_mod",
    )(c, w_ada, b_ada.reshape(L, 1, W))


PROJ_OUT_W = (S5_WIDTH, GDN_CONV_DIM, GDN_WIDTH, SSD_WIDTH, SSD_CONV_DIM, SMALL_W)


def _proj_kernel(x_ref, mod_ref, nw_ref, w_ref, *o_refs):
    x = x_ref[0]
    h = _norm_mod(x, nw_ref[...], mod_ref[0, 1:2, :], mod_ref[0, 0:1, :]).astype(BF16)
    off = 0
    for o_ref in o_refs:
        n = o_ref.shape[-1]
        o_ref[0] = jnp.dot(h, w_ref[0, :, off:off + n], preferred_element_type=F32)
        off += n


_W_HEAD = S5_WIDTH + GDN_CONV_DIM + GDN_WIDTH
_W_AB = _W_HEAD + 2 * GDN_HEADS
_W_SSD = _W_AB + SSD_WIDTH + SSD_CONV_DIM
_W_END = _W_SSD + SSD_HEADS
PROJ_W = sum(PROJ_OUT_W)
W_PREP_ROWS = 256


def _w_in_prep_kernel(w_ref, o_ref):
    n_ssd = _W_SSD - _W_AB
    for r in range(0, w_ref.shape[1], W_PREP_ROWS):
        w = w_ref[0, r:r + W_PREP_ROWS, :]
        rows = slice(r, r + W_PREP_ROWS)
        o_ref[0, rows, 0:_W_HEAD] = w[:, 0:_W_HEAD].astype(BF16)
        o_ref[0, rows, _W_HEAD:_W_HEAD + n_ssd] = w[:, _W_AB:_W_SSD].astype(BF16)
        small = jnp.concatenate([w[:, _W_HEAD:_W_AB], w[:, _W_SSD:_W_END],
                                 jnp.zeros((W_PREP_ROWS, SMALL_W - (_W_AB - _W_HEAD) - (_W_END - _W_SSD)), F32)], axis=1)
        o_ref[0, rows, _W_HEAD + n_ssd:PROJ_W] = small.astype(BF16)


def _w_in_prep_call(w_in):
    L, D, W = w_in.shape
    return pl.pallas_call(
        _w_in_prep_kernel,
        out_shape=jax.ShapeDtypeStruct((L, D, PROJ_W), BF16),
        grid=(L,),
        in_specs=[pl.BlockSpec((1, D, W), lambda l: (l, 0, 0))],
        out_specs=pl.BlockSpec((1, D, PROJ_W), lambda l: (l, 0, 0)),
        compiler_params=_cparams(("arbitrary",)),
        name="w_in_prep",
    )(w_in)


def _proj_call(x, mod_l, norm_w, w_arr, l):
    B, S, D = x.shape
    tm = min(512, S)
    out_shape = [jax.ShapeDtypeStruct((B, S, n), F32) for n in PROJ_OUT_W]
    out_specs = [pl.BlockSpec((1, tm, n), lambda b, i: (b, i, 0)) for n in PROJ_OUT_W]
    return pl.pallas_call(
        _proj_kernel,
        out_shape=out_shape,
        grid=(B, S // tm),
        in_specs=[pl.BlockSpec((1, tm, D), lambda b, i: (b, i, 0)),
                  pl.BlockSpec((1, 6, D), lambda b, i: (b, 0, 0)),
                  pl.BlockSpec((1, D), lambda b, i: (0, 0)),
                  pl.BlockSpec((1, D, PROJ_W), lambda b, i: (l, 0, 0))],
        out_specs=out_specs,
        compiler_params=_cparams(("arbitrary", "arbitrary")),
        name="norm_in_proj",
    )(x, mod_l, norm_w.reshape(1, D), w_arr)


def _s5_kernel(u_ref, bm_ref, cm_ref, lam_ref, dsk_ref, glu_ref, nw_ref, perm_ref, o_ref,
               x_s, st_s, ubt_s, u_s, *, T):
    B = u_ref.shape[0]
    P = S5_LANES

    @pl.when(pl.program_id(0) == 0)
    def _():
        st_s[...] = jnp.zeros_like(st_s)

    nw = S5_WIDTH // LANE
    ubt = u_ref[...].reshape(B * T, S5_WIDTH)
    for j in range(nw):
        ubt_s[j] = ubt[:, j * LANE:(j + 1) * LANE]

    def regroup(t, carry):
        for j in range(nw):
            u_s[pl.ds(pl.multiple_of(t * B, B), B), j * LANE:(j + 1) * LANE] = ubt_s[j, pl.ds(t, B, stride=T), :]
        return carry

    lax.fori_loop(0, T, regroup, 0, unroll=8)
    u = u_s[...]
    x_s[...] = jnp.dot(u.astype(BF16), bm_ref[...], preferred_element_type=F32)
    lr = jnp.broadcast_to(lam_ref[0:1, :], (B, P))
    li = jnp.broadcast_to(lam_ref[1:2, :], (B, P))

    def step(t, carry):
        sr, si = carry
        rows = pl.ds(pl.multiple_of(t * B, B), B)
        nr = lr * sr - li * si + x_s[rows, 0:P]
        ni = lr * si + li * sr + x_s[rows, P:2 * P]
        x_s[rows, 0:P] = nr
        x_s[rows, P:2 * P] = ni
        return nr, ni

    sr, si = lax.fori_loop(0, T, step, (st_s[:, 0:P], st_s[:, P:2 * P]), unroll=4)
    st_s[:, 0:P] = sr
    st_s[:, P:2 * P] = si

    y = jnp.dot(x_s[...].astype(BF16), cm_ref[...], preferred_element_type=F32) + dsk_ref[...] * u
    y = jax.nn.gelu(y)
    y = y * jax.nn.sigmoid(jnp.dot(y.astype(BF16), glu_ref[...], preferred_element_type=F32))
    ms = jnp.mean(y * y, axis=-1, keepdims=True)
    y = (y * lax.rsqrt(ms + EPS) * nw_ref[...]).astype(BF16)
    y = jnp.dot(perm_ref[...], y, preferred_element_type=F32)
    o_ref[...] = y.reshape(B, T, S5_WIDTH).astype(o_ref.dtype)


def _s5_params(a_re, a_im, b_re, b_im, c_re, c_im, log_dt):
    G, P, CH = S5_GROUPS, S5_STATE, S5_CH
    lam = lax.complex(a_re.astype(F32), a_im.astype(F32))
    step = jnp.exp(log_dt.astype(F32))[:, None]
    lam_bar = jnp.exp(lam * step)
    b_bar = ((lam_bar - 1.0) / lam)[..., None] * lax.complex(b_re.astype(F32), b_im.astype(F32))
    eye = jnp.eye(G, dtype=F32)
    bre = jnp.einsum('gpc,gh->gchp', b_bar.real, eye).reshape(G * CH, G * P)
    bim = jnp.einsum('gpc,gh->gchp', b_bar.imag, eye).reshape(G * CH, G * P)
    bm = jnp.concatenate([bre, bim], axis=1).astype(BF16)
    cre = jnp.einsum('gcp,gh->gphc', c_re.astype(F32), eye).reshape(G * P, G * CH)
    cim = jnp.einsum('gcp,gh->gphc', c_im.astype(F32), eye).reshape(G * P, G * CH)
    cm = jnp.concatenate([cre, -cim], axis=0).astype(BF16)
    lam_rows = jnp.zeros((SUBLANE, G * P), F32)
    lam_rows = lam_rows.at[0].set(lam_bar.real.reshape(-1)).at[1].set(lam_bar.imag.reshape(-1))
    return bm, cm, lam_rows


def _s5_call(u, bm, cm, lam_rows, d_skip, w_glu, norm_w):
    B, S, W = u.shape
    T = min(128, S)
    P2 = 2 * S5_LANES
    const = lambda shape: pl.BlockSpec(shape, lambda i: tuple(0 for _ in shape))
    r = np.arange(B * T)
    perm = np.zeros((B * T, B * T), np.float32)
    perm[r, (r % T) * B + r // T] = 1.0
    return pl.pallas_call(
        functools.partial(_s5_kernel, T=T),
        out_shape=jax.ShapeDtypeStruct((B, S, W), BF16),
        grid=(S // T,),
        in_specs=[pl.BlockSpec((B, T, W), lambda i: (0, i, 0)),
                  const((W, P2)), const((P2, W)), const((SUBLANE, S5_LANES)),
                  const((1, W)), const((W, W)), const((1, W)), const((B * T, B * T))],
        out_specs=pl.BlockSpec((B, T, W), lambda i: (0, i, 0)),
        scratch_shapes=[pltpu.VMEM((B * T, P2), F32), pltpu.VMEM((B, P2), F32),
                        pltpu.VMEM((W // LANE, B * T, LANE), F32), pltpu.VMEM((B * T, W), F32)],
        compiler_params=_cparams(("arbitrary",)),
        name="s5_mixer",
    )(u, bm, cm, lam_rows, d_skip.reshape(1, W).astype(F32), w_glu.astype(BF16), norm_w.reshape(1, W).astype(F32),
      jnp.asarray(perm, BF16))


def _causal_conv_silu(x, xf_ref, cw_ref, bias):
    n = x.shape[0]
    xf_ref[SUBLANE:, :] = x
    acc = x * cw_ref[CONV_WIDTH - 1:CONV_WIDTH, :]
    for k in range(1, CONV_WIDTH):
        acc = acc + xf_ref[SUBLANE - k:SUBLANE - k + n, :] * cw_ref[CONV_WIDTH - 1 - k:CONV_WIDTH - k, :]
    xf_ref[0:SUBLANE, :] = x[n - SUBLANE:, :]
    if bias is not None:
        acc = acc + bias
    return _silu(acc)


def _store_row_tiles(ref, val):
    n = val.shape[0]
    for j in range(val.shape[1] // LANE):
        ref[pl.ds(j, n, stride=SUBLANE), :] = val[:, j * LANE:(j + 1) * LANE]


def _load_row_tiles(ref, n):
    return jnp.concatenate([ref[pl.ds(j, n, stride=SUBLANE), :] for j in range(SUBLANE)], axis=1)


def _lanes_from(sm, off):
    return pltpu.roll(sm, SMALL_W - off, axis=1) if off else sm


def _chunk_scans(g):
    n = g.shape[0]
    rin = lax.broadcasted_iota(jnp.int32, g.shape, 0) & (CHUNK - 1)
    pre = g
    suf = jnp.where(rin < CHUNK - 1, pltpu.roll(g, n - 1, axis=0), 0.0)
    s = 1
    while s < CHUNK:
        pre = pre + jnp.where(rin >= s, pltpu.roll(pre, s, axis=0), 0.0)
        suf = suf + jnp.where(rin + s <= CHUNK - 1, pltpu.roll(suf, n - s, axis=0), 0.0)
        s *= 2
    return pre, suf


def _tri_masks():
    r = lax.broadcasted_iota(jnp.int32, (CHUNK, CHUNK), 0)
    c = lax.broadcasted_iota(jnp.int32, (CHUNK, CHUNK), 1)
    return r >= c, r > c, r == c


def _head_expand():
    m = np.zeros((SMALL_W, GDN_WIDTH), np.float32)
    for h in range(GDN_HEADS):
        m[h, h * GDN_HEAD_DIM:(h + 1) * GDN_HEAD_DIM] = 1.0
    return jnp.asarray(m, BF16)


def _block_ones(width, blk):
    idx = np.arange(width) // blk
    return jnp.asarray((idx[:, None] == idx[None, :]).astype(np.float32), BF16)


def _gdn_kernel(qkv_ref, z_ref, sm_ref, cw_ref, hp_ref, nw_ref, e_ref, xp_ref, bdm_ref, bdr_ref, o_ref,
                tail_s, st_s, kn_s, kb_s, qn_s, qd_s, kd_s, rhs_s, eg_s, gcx_s, gw_s, o_s, at_s, *, TB):
    H, Dh, C = GDN_HEADS, GDN_HEAD_DIM, CHUNK
    W = GDN_WIDTH
    ncb = TB // C

    @pl.when(pl.program_id(1) == 0)
    def _():
        tail_s[0:SUBLANE, :] = jnp.zeros((SUBLANE, tail_s.shape[1]), F32)
        st_s[...] = jnp.zeros_like(st_s)

    xc = _causal_conv_silu(qkv_ref[0], tail_s, cw_ref, None)
    q, k, v = xc[:, 0:W], xc[:, W:2 * W], xc[:, 2 * W:3 * W]
    e = e_ref[...]
    xp = xp_ref[...]
    qn = q * lax.rsqrt(_dot2(q * q, e) + EPS) * (Dh ** -0.5)
    kn = k * lax.rsqrt(_dot2(k * k, e) + EPS)

    sm = sm_ref[0]
    lane = lax.broadcasted_iota(jnp.int32, sm.shape, 1)
    head_lane = lane < H
    g = jnp.where(head_lane, hp_ref[0:1, :] * _softplus(_lanes_from(sm, A_LANE) + hp_ref[1:2, :]), 0.0)
    beta = jnp.where(head_lane, jax.nn.sigmoid(_lanes_from(sm, B_LANE)), 0.0)
    gc, rc = _chunk_scans(g)
    bx = _dot2(beta, xp)
    egx = _dot2(jnp.exp(gc), xp)
    erx = _dot2(jnp.exp(rc), xp)
    kb = kn * bx
    kn_s[...] = kn
    kb_s[...] = kb
    qn_s[...] = qn
    qd_s[...] = qn * egx
    kd_s[...] = kn * erx
    for h in range(H):
        sl = slice(h * Dh, (h + 1) * Dh)
        rhs_s[:, 2 * h * Dh:(2 * h + 1) * Dh] = v[:, sl] * bx[:, sl]
        rhs_s[:, (2 * h + 1) * Dh:(2 * h + 2) * Dh] = kb[:, sl] * egx[:, sl]
    gcx = _dot_sel3(gc, xp)
    gcx_s[...] = gcx
    r384 = lax.broadcasted_iota(jnp.int32, (C, W), 0)
    l384 = lax.broadcasted_iota(jnp.int32, (C, W), 1) & (Dh - 1)
    for c in range(ncb):
        diag_c = jnp.where(r384 == l384, gcx[c * C:(c + 1) * C, :], 0.0)
        gw_s[c] = jnp.broadcast_to(jnp.sum(diag_c, axis=0, keepdims=True), (SUBLANE, W))
        eg_s[c] = jnp.broadcast_to(egx[(c + 1) * C - 1:(c + 1) * C, :], (SUBLANE, W))

    heads = [slice(h * Dh, (h + 1) * Dh) for h in range(H)]

    GL = 4 * Dh
    row_w = lax.broadcasted_iota(jnp.int32, (C, GL), 0)
    col_w = lax.broadcasted_iota(jnp.int32, (C, GL), 1) & (Dh - 1)
    causal_w, strict_w = row_w >= col_w, row_w > col_w
    eye_w = jnp.where(row_w == col_w, 1.0, 0.0).astype(F32)
    bdm = bdm_ref[...]
    bdr = bdr_ref[...]
    mm = functools.partial(lax.dot_general, dimension_numbers=_NN, preferred_element_type=F32)

    def block_diag(m, mask):
        return jnp.concatenate([m] * 4, axis=0) * mask

    npair = min(GDN_PAIRS_PER_ITER, ncb // 2)

    def solve(it, carry):
        chunks = [2 * npair * it + k for k in range(2 * npair)]
        rows = [pl.ds(pl.multiple_of(c * C, C), C) for c in chunks]

        def groups(get, n):
            out = []
            for j in range(npair):
                a0, a1 = get(2 * j), get(2 * j + 1)
                out += [a0[:, 0:4 * n], a1[:, 0:4 * n],
                        jnp.concatenate([a0[:, 4 * n:6 * n], a1[:, 4 * n:6 * n]], axis=1)]
            return out

        def ungroup(ref, vals, n):
            for j in range(npair):
                ref[rows[2 * j], 0:4 * n] = vals[3 * j]
                ref[rows[2 * j + 1], 0:4 * n] = vals[3 * j + 1]
                ref[rows[2 * j], 4 * n:6 * n] = vals[3 * j + 2][:, 0:2 * n]
                ref[rows[2 * j + 1], 4 * n:6 * n] = vals[3 * j + 2][:, 2 * n:4 * n]

        kn_g = groups(lambda k: kn_s[rows[k], :], Dh)
        kb_g = groups(lambda k: kb_s[rows[k], :], Dh)
        qn_g = groups(lambda k: qn_s[rows[k], :], Dh)
        gx_g = groups(lambda k: gcx_s[rows[k], :], Dh)
        gw_g = groups(lambda k: gw_s[chunks[k], 0:1, :], Dh)
        decs = [jnp.exp(jnp.where(causal_w, gx - gw, NEG_BIG)) for gx, gw in zip(gx_g, gw_g)]
        prods = [lax.dot_general(jnp.concatenate([kb, qn], axis=0).astype(BF16), block_diag(kn.astype(BF16), bdm),
                                 _NT, preferred_element_type=F32)
                 for kn, kb, qn in zip(kn_g, kb_g, qn_g)]
        ungroup(at_s, [pr[C:2 * C] * dec for pr, dec in zip(prods, decs)], Dh)
        def times(lhs, p=None):
            lh, ll = _split(lhs)
            ph, pl_ = (lh[0:C], ll[0:C]) if p is None else _split(p)
            rh, rl = block_diag(ph, bdm), block_diag(pl_, bdm)
            return mm(lh, rh) + (mm(lh, rl) + mm(ll, rh))

        ps = [-jnp.where(strict_w, pr[0:C] * dec, 0.0) for pr, dec in zip(prods, decs)]
        invs = [eye_w + p for p in ps]
        ps = [times(p) for p in ps]
        for _ in range(4):
            outs = [times(jnp.concatenate([p, inv], axis=0)) for p, inv in zip(ps, invs)]
            invs = [inv + o[C:2 * C] for inv, o in zip(invs, outs)]
            ps = [o[0:C] for o in outs]
        invs = [inv + times(inv, p) for p, inv in zip(ps, invs)]
        uws = []
        for inv, r in zip(invs, groups(lambda k: rhs_s[rows[k], :], 2 * Dh)):
            ih, il = _split(inv)
            rh, rl = _split(r)
            rh, rl = block_diag(rh, bdr), block_diag(rl, bdr)
            uws.append(mm(ih, rh) + (mm(ih, rl) + mm(il, rh)))
        ungroup(rhs_s, uws, 2 * Dh)
        return carry

    lax.fori_loop(0, ncb // (2 * npair), solve, 0)

    def recur(c, carry):
        rows = pl.ds(pl.multiple_of(c * C, C), C)
        sts = [st_s[h] for h in range(H)]
        uws = [rhs_s[rows, 2 * h * Dh:(2 * h + 2) * Dh] for h in range(H)]
        v_news = [uw[:, 0:Dh] - _dot(uw[:, Dh:2 * Dh], st) for uw, st in zip(uws, sts)]
        for sl, st, v_new in zip(heads, sts, v_news):
            o_s[rows, sl] = _dot(qd_s[rows, sl], st) + _dot(at_s[rows, sl], v_new)
        for h, (sl, st, v_new) in enumerate(zip(heads, sts, v_news)):
            st_s[h] = st * eg_s[c, 0:1, sl] + _dot(kd_s[rows, sl], v_new, _TN)
        return carry

    lax.fori_loop(0, ncb, recur, 0)

    o = o_s[...]
    ms = _dot2(o * o, e) * (1.0 / Dh)
    o = o * lax.rsqrt(ms + EPS) * nw_ref[...]
    o_ref[0] = (o * _silu(z_ref[0])).astype(o_ref.dtype)


def _gdn_call(qkv, z, sm, conv_w, a_log, dt_bias, norm_w):
    B, S, _ = qkv.shape
    TB = min(512, S)
    W, H, Dh = GDN_WIDTH, GDN_HEADS, GDN_HEAD_DIM
    hp = jnp.zeros((SUBLANE, SMALL_W), F32)
    hp = hp.at[0, :H].set(-jnp.exp(a_log.astype(F32))).at[1, :H].set(dt_bias.astype(F32))
    nw = jnp.tile(norm_w.astype(F32), H).reshape(1, W)
    const = lambda shape: pl.BlockSpec(shape, lambda b, i: tuple(0 for _ in shape))
    blk = lambda n: pl.BlockSpec((1, TB, n), lambda b, i: (b, i, 0))
    f = lambda *shape: pltpu.VMEM(shape, F32)
    return pl.pallas_call(
        functools.partial(_gdn_kernel, TB=TB),
        out_shape=jax.ShapeDtypeStruct((B, S, W), BF16),
        grid=(B, S // TB),
        in_specs=[blk(GDN_CONV_DIM), blk(W), blk(SMALL_W),
                  const((CONV_WIDTH, GDN_CONV_DIM)), const((SUBLANE, SMALL_W)), const((1, W)),
                  const((W, W)), const((SMALL_W, W)), const((4 * Dh, 4 * Dh)), const((4 * Dh, 8 * Dh))],
        out_specs=blk(W),
        scratch_shapes=[f(SUBLANE + TB, GDN_CONV_DIM), f(H, GDN_HEAD_DIM, GDN_HEAD_DIM),
                        f(TB, W), f(TB, W), f(TB, W), f(TB, W), f(TB, W), f(TB, 2 * W), f(TB // CHUNK, SUBLANE, W),
                        f(TB, W), f(TB // CHUNK, SUBLANE, W), f(TB, W), f(TB, W)],
        compiler_params=_cparams(("arbitrary", "arbitrary")),
        name="gdn_mixer",
    )(qkv, z, sm, conv_w.astype(F32), hp, nw, _block_ones(W, Dh), _head_expand(),
      _block_ones(4 * Dh, Dh), jnp.repeat(_block_ones(4 * Dh, Dh), 2, axis=1))


def _ssd_kernel(xbc_ref, z_ref, sm_ref, cw_ref, cb_ref, hp_ref, dsk_ref, nw_ref, e_ref, xp_ref, o_ref,
                tail_s, st_s, xs_s, xdt_s, xdd_s, bm_s, cm_s, ea_s, el_s, ac_s, act_s, y_s, inc_s, *, TB):
    H, P, G, N, C = SSD_HEADS, SSD_HEAD_DIM, SSD_GROUPS, SSD_STATE, CHUNK
    R = H // G
    W, GW = SSD_WIDTH, SSD_GROUP_WIDTH
    ncb = TB // C

    @pl.when(pl.program_id(1) == 0)
    def _():
        tail_s[0:SUBLANE, :] = jnp.zeros((SUBLANE, tail_s.shape[1]), F32)
        st_s[...] = jnp.zeros_like(st_s)

    xc = _causal_conv_silu(xbc_ref[0], tail_s, cw_ref, cb_ref[...])
    xs = xc[:, 0:W]
    xp = xp_ref[...]

    sm = sm_ref[0]
    lane = lax.broadcasted_iota(jnp.int32, sm.shape, 1)
    head_lane = lane < H
    dt = jnp.where(head_lane, _softplus(_lanes_from(sm, DT_LANE) + hp_ref[1:2, :]), 0.0)
    acs, rcs = _chunk_scans(dt * hp_ref[0:1, :])
    dtx = _dot2(dt, xp)
    eax = _dot2(jnp.exp(acs), xp)
    erx = _dot2(jnp.exp(rcs), xp)
    xdt = xs * dtx
    xs_s[...] = xs
    xdt_s[...] = xdt
    xdd_s[...] = xdt * erx
    bm_s[...] = xc[:, W:W + G * N]
    cm_s[...] = xc[:, W + G * N:W + 2 * G * N]
    ea_s[...] = eax
    ac_s[...] = acs
    for c in range(ncb):
        act_s[c] = acs[c * C:(c + 1) * C, :].T
        el_s[c] = jnp.broadcast_to(eax[(c + 1) * C - 1:(c + 1) * C, :], (SUBLANE, W))

    causal, _, _ = _tri_masks()

    groups = [(gi, slice(gi * GW, (gi + 1) * GW), slice(gi * N, (gi + 1) * N)) for gi in range(G)]

    def local(it, carry):
        work = []
        for cc in range(SSD_CHUNKS_PER_ITER):
            c = it * SSD_CHUNKS_PER_ITER + cc
            rows = pl.ds(pl.multiple_of(c * C, C), C)
            work += [(c, rows, ac_s[rows, :], act_s[c], g) for g in groups]
        cbs = [_dot(cm_s[rows, nsl], bm_s[rows, nsl], _NT) for c, rows, _, _, (gi, gsl, nsl) in work]
        for (c, rows, acol, arow, (gi, gsl, nsl)), cb in zip(work, cbs):
            for r in range(R):
                h = gi * R + r
                sl = slice(h * P, (h + 1) * P)
                seg = jnp.exp(jnp.where(causal, acol[:, h:h + 1] - arow[h:h + 1, :], NEG_BIG))
                y_s[rows, sl] = _dot(cb * seg, xdt_s[rows, sl])
        for c, rows, _, _, (gi, gsl, nsl) in work:
            inc_s[c, gi] = _dot(bm_s[rows, nsl], xdd_s[rows, gsl], _TN)
        return carry

    lax.fori_loop(0, ncb // SSD_CHUNKS_PER_ITER, local, 0)

    def recur(c, carry):
        rows = pl.ds(pl.multiple_of(c * C, C), C)
        sts = [st_s[gi] for gi, _, _ in groups]
        for (gi, gsl, nsl), st in zip(groups, sts):
            y_s[rows, gsl] = y_s[rows, gsl] + _dot(cm_s[rows, nsl], st) * ea_s[rows, gsl]
        for (gi, gsl, nsl), st in zip(groups, sts):
            st_s[gi] = st * el_s[c, 0:1, gsl] + inc_s[c, gi]
        return carry

    lax.fori_loop(0, ncb, recur, 0)

    y = y_s[...] + dsk_ref[...] * xs_s[...]
    y = y * _silu(z_ref[0])
    ms = _dot2(y * y, e_ref[...]) * (1.0 / GW)
    o_ref[0] = (y * lax.rsqrt(ms + EPS) * nw_ref[...]).astype(o_ref.dtype)


def _ssd_call(xbc, z, sm, conv_w, conv_b, a_log, dt_bias, d_skip, norm_w):
    B, S, _ = xbc.shape
    TB = min(512, S)
    W, H, G, N = SSD_WIDTH, SSD_HEADS, SSD_GROUPS, SSD_STATE
    hp = jnp.zeros((SUBLANE, SMALL_W), F32)
    hp = hp.at[0, :H].set(-jnp.exp(a_log.astype(F32))).at[1, :H].set(dt_bias.astype(F32))
    dsk = jnp.repeat(d_skip.astype(F32), SSD_HEAD_DIM).reshape(1, W)
    const = lambda shape: pl.BlockSpec(shape, lambda b, i: tuple(0 for _ in shape))
    blk = lambda n: pl.BlockSpec((1, TB, n), lambda b, i: (b, i, 0))
    f = lambda *shape: pltpu.VMEM(shape, F32)
    return pl.pallas_call(
        functools.partial(_ssd_kernel, TB=TB),
        out_shape=jax.ShapeDtypeStruct((B, S, W), BF16),
        grid=(B, S // TB),
        in_specs=[blk(SSD_CONV_DIM), blk(W), blk(SMALL_W),
                  const((CONV_WIDTH, SSD_CONV_DIM)), const((1, SSD_CONV_DIM)), const((SUBLANE, SMALL_W)),
                  const((1, W)), const((1, W)), const((W, W)), const((SMALL_W, W))],
        out_specs=blk(W),
        scratch_shapes=[f(SUBLANE + TB, SSD_CONV_DIM), f(G, N, SSD_GROUP_WIDTH),
                        f(TB, W), f(TB, W), f(TB, W), f(TB, G * N), f(TB, G * N), f(TB, W),
                        f(TB // CHUNK, SUBLANE, W),
                        f(TB, SMALL_W), f(TB // CHUNK, SMALL_W, CHUNK), f(TB, W),
                        f(TB // CHUNK, G, N, SSD_GROUP_WIDTH)],
        compiler_params=_cparams(("arbitrary", "arbitrary")),
        name="ssd_mixer",
    )(xbc, z, sm, conv_w.astype(F32), conv_b.reshape(1, -1).astype(F32), hp, dsk,
      norm_w.reshape(1, W).astype(F32), _block_ones(W, SSD_GROUP_WIDTH), _head_expand())


def _out_kernel(x_ref, y1_ref, y2_ref, y3_ref, mod_ref, nw_ref, wo_ref, wr_ref, br_ref, tri_ref,
                x1_ref, h_ref, rt_ref, cnt_ref, run_s, wo_s):
    first = (pl.program_id(0) == 0) & (pl.program_id(1) == 0)

    @pl.when(first)
    def _():
        wo_s[...] = wo_ref[0].astype(BF16)

    y = (jnp.dot(y1_ref[0], wo_s[0:S5_WIDTH, :], preferred_element_type=F32)
         + jnp.dot(y2_ref[0], wo_s[S5_WIDTH:S5_WIDTH + GDN_WIDTH, :], preferred_element_type=F32)
         + jnp.dot(y3_ref[0], wo_s[S5_WIDTH + GDN_WIDTH:, :], preferred_element_type=F32))
    x1 = x_ref[0] + mod_ref[0, 2:3, :] * y
    x1_ref[0] = x1
    h = _norm_mod(x1, nw_ref[...], mod_ref[0, 4:5, :], mod_ref[0, 3:4, :])
    _store_row_tiles(h_ref, h)
    lg =jnp.dot(h.astype(BF16), wr_ref[...], preferred_element_type=F32) + br_ref[...]

    lane = lax.broadcasted_iota(jnp.int32, lg.shape, 1)
    lanef = lane.astype(F32)
    big = float(4 * LANE)
    grp = (lane >= GRP_LANE) & (lane < GRP_LANE + MOE_GROUPS)
    lgm = jnp.where(grp, lg, -jnp.inf)
    m = jnp.max(lgm, axis=-1, keepdims=True)
    gidx = jnp.min(jnp.where(lgm == m, lanef - GRP_LANE, big), axis=-1, keepdims=True)
    g_w = 1.0 / jnp.sum(jnp.where(grp, jnp.exp(lg - m), 0.0), axis=-1, keepdims=True)
    in_grp = (lane < N_EXPERTS) & ((lane // EXPERTS_PER_GROUP).astype(F32) == gidx)
    le = jnp.where(in_grp, lg, -jnp.inf)
    v1 = jnp.max(le, axis=-1, keepdims=True)
    i1 = jnp.min(jnp.where(le == v1, lanef, big), axis=-1, keepdims=True)
    le2 = jnp.where(lanef == i1, -jnp.inf, le)
    v2 = jnp.max(le2, axis=-1, keepdims=True)
    i2 = jnp.min(jnp.where(le2 == v2, lanef, big), axis=-1, keepdims=True)
    e2 = jnp.exp(v2 - v1)
    w1 = g_w / (1.0 + e2)
    w2 = g_w * e2 / (1.0 + e2)

    @pl.when(first)
    def _():
        run_s[...] = jnp.zeros_like(run_s)

    chosen = jnp.where((lanef == i1) | (lanef == i2), 1.0, 0.0)
    before = jnp.dot(tri_ref[...], chosen.astype(BF16), preferred_element_type=F32) + run_s[0:1, :]
    p1 = jnp.sum(jnp.where(lanef == i1, before, 0.0), axis=-1, keepdims=True)
    p2 = jnp.sum(jnp.where(lanef == i2, before, 0.0), axis=-1, keepdims=True)
    run_s[...] = run_s[...] + jnp.sum(chosen, axis=0, keepdims=True)
    cnt_ref[...] = run_s[...]

    rt = jnp.zeros_like(lg)
    for k, val in enumerate((i1, i2, w1, w2, p1, p2)):
        rt = jnp.where(lane == k, val, rt)
    rt_ref[0] = rt


def _out_call(x, y1, y2, y3, mod_l, norm_w, w_out, w_router, b_router, l):
    B, S, D = x.shape
    tm = min(512, S)
    blk = lambda n: pl.BlockSpec((1, tm, n), lambda b, i: (b, i, 0))
    const = lambda shape: pl.BlockSpec(shape, lambda b, i: tuple(0 for _ in shape))
    tri = jnp.asarray(np.tril(np.ones((tm, tm), np.float32), -1), BF16)
    nb = S // tm
    tiles = D // LANE
    return pl.pallas_call(
        _out_kernel,
        out_shape=[jax.ShapeDtypeStruct((B, S, D), F32), jax.ShapeDtypeStruct((B * S * tiles, LANE), F32),
                   jax.ShapeDtypeStruct((B, S, LANE), F32), jax.ShapeDtypeStruct((SUBLANE, LANE), F32)],
        grid=(B, nb),
        in_specs=[blk(D), blk(S5_WIDTH), blk(GDN_WIDTH), blk(SSD_WIDTH),
                  pl.BlockSpec((1, 6, D), lambda b, i: (b, 0, 0)),
                  const((1, D)), pl.BlockSpec((1, D, D), lambda b, i: (l, 0, 0)),
                  const((D, LANE)), const((1, LANE)), const((tm, tm))],
        out_specs=[blk(D), pl.BlockSpec((tm * tiles, LANE), lambda b, i: (b * nb + i, 0)),
                   blk(LANE), const((SUBLANE, LANE))],
        scratch_shapes=[pltpu.VMEM((SUBLANE, LANE), F32), pltpu.VMEM((D, D), BF16)],
        compiler_params=_cparams(("arbitrary", "arbitrary")),
        name="out_proj_router",
    )(x, y1, y2, y3, mod_l, norm_w.reshape(1, D), w_out, w_router, b_router, tri)


def _router_params(w_grp, b_grp, w_rt, b_rt):
    D = w_grp.shape[0]
    w = jnp.zeros((D, LANE), F32).at[:, 0:N_EXPERTS].set(w_rt).at[:, GRP_LANE:GRP_LANE + MOE_GROUPS].set(w_grp)
    b = jnp.zeros((1, LANE), F32).at[0, 0:N_EXPERTS].set(b_rt).at[0, GRP_LANE:GRP_LANE + MOE_GROUPS].set(b_grp)
    return w.astype(BF16), b


def _tile_copy_loop(n, fn):
    def body(t, carry):
        fn(t)
        return carry
    lax.fori_loop(0, n, body, 0, unroll=8)


def _scatter_kernel(d0_ref, d1_ref, h_ref, xs_hbm, stage, sem):
    i = pl.program_id(0)
    n = pl.num_programs(0)
    slot = lax.rem(i, 2)
    tm = d0_ref.shape[-1]

    def copy(s, t, dst):
        return pltpu.make_async_copy(stage.at[s, pl.ds(t * SUBLANE, SUBLANE)],
                                     xs_hbm.at[pl.ds(dst * SUBLANE, SUBLANE)], sem.at[s])

    def wait_slot(s):
        _tile_copy_loop(TOP_K * tm, lambda t: copy(s, 0, 0).wait())

    @pl.when(i >= 2)
    def _():
        wait_slot(slot)

    stage[slot] = h_ref[...]

    def start(t):
        copy(slot, t, d0_ref[0, 0, t]).start(priority=0)
        copy(slot, t, d1_ref[0, 0, t]).start(priority=1)
    _tile_copy_loop(tm, start)

    @pl.when(i == n - 1)
    def _():
        @pl.when(i >= 1)
        def _():
            wait_slot(1 - slot)
        wait_slot(slot)


def _scatter_call(h2t, dest0, dest1, n_rows):
    n_tiles, _, tm = dest0.shape
    idx_blk = pl.BlockSpec((1, 1, tm), lambda i: (i, 0, 0), memory_space=pltpu.SMEM)
    return pl.pallas_call(
        _scatter_kernel,
        out_shape=jax.ShapeDtypeStruct((n_rows * SUBLANE, LANE), F32),
        grid=(n_tiles,),
        in_specs=[idx_blk, idx_blk, pl.BlockSpec((tm * SUBLANE, LANE), lambda i: (i, 0))],
        out_specs=pl.BlockSpec(memory_space=pl.ANY),
        scratch_shapes=[pltpu.VMEM((2, tm * SUBLANE, LANE), F32), pltpu.SemaphoreType.DMA((2,))],
        compiler_params=_cparams(("arbitrary",)),
        name="moe_scatter",
    )(dest0, dest1, h2t)


def _expert_kernel(nused_ref, blke_ref, nvalid_ref, xs_ref, wg_ref, wu_ref, wd_ref, ys_ref, wg_s, wu_s, wd_s):
    i = pl.program_id(0)

    @pl.when(i < nused_ref[0])
    def _():
        @pl.when((i == 0) | (blke_ref[i] != blke_ref[jnp.maximum(i - 1, 0)]))
        def _():
            wg_s[...] = wg_ref[0, 0].astype(BF16)
            wu_s[...] = wu_ref[0, 0].astype(BF16)
            wd_s[...] = wd_ref[0, 0].astype(BF16)

        x = _load_row_tiles(xs_ref, MOE_ROWS)
        row = lax.broadcasted_iota(jnp.int32, (MOE_ROWS, 1), 0)
        xb = jnp.where(row < nvalid_ref[i], x, 0.0).astype(BF16)
        hid = _silu(jnp.dot(xb, wg_s[...], preferred_element_type=F32)) * jnp.dot(xb, wu_s[...], preferred_element_type=F32)
        _store_row_tiles(ys_ref, jnp.dot(hid.astype(BF16), wd_s[...], preferred_element_type=F32))

    @pl.when(i >= nused_ref[0])
    def _():
        ys_ref[...] = jnp.zeros_like(ys_ref)


def _expert_call(xs, n_used, blk_e, n_valid, w_gate, w_up, w_down, l):
    D = w_gate.shape[2]
    n_blk = blk_e.shape[0]
    rows_blk = pl.BlockSpec((MOE_ROWS * SUBLANE, LANE), lambda i, nu, be, nv: (i, 0))
    grid_spec = pltpu.PrefetchScalarGridSpec(
        num_scalar_prefetch=3,
        grid=(n_blk,),
        in_specs=[rows_blk,
                  pl.BlockSpec((1, 1, D, D_EXPERT), lambda i, nu, be, nv: (l, be[i], 0, 0)),
                  pl.BlockSpec((1, 1, D, D_EXPERT), lambda i, nu, be, nv: (l, be[i], 0, 0)),
                  pl.BlockSpec((1, 1, D_EXPERT, D), lambda i, nu, be, nv: (l, be[i], 0, 0))],
        out_specs=rows_blk,
        scratch_shapes=[pltpu.VMEM((D, D_EXPERT), BF16), pltpu.VMEM((D, D_EXPERT), BF16),
                        pltpu.VMEM((D_EXPERT, D), BF16)],
    )
    return pl.pallas_call(
        _expert_kernel,
        out_shape=jax.ShapeDtypeStruct(xs.shape, F32),
        grid_spec=grid_spec,
        compiler_params=_cparams(("arbitrary",)),
        name="expert_mlp",
    )(n_used, blk_e, n_valid, xs, w_gate, w_up, w_down)


def _dispatch(route, counts, N, tm):
    L_pad = N * TOP_K + N_EXPERTS * MOE_ROWS
    n_blk = L_pad // MOE_ROWS
    eid = route[:, 0:TOP_K].astype(jnp.int32)
    pos = route[:, 4:4 + TOP_K].astype(jnp.int32)
    counts = counts.astype(jnp.int32)
    padded = ((counts + MOE_ROWS - 1) // MOE_ROWS) * MOE_ROWS
    pend = jnp.cumsum(padded)
    pstart = pend - padded
    dest = pstart[eid] + pos
    n_used = (pend[-1] // MOE_ROWS).astype(jnp.int32).reshape(1)
    blk_row0 = jnp.arange(n_blk, dtype=jnp.int32) * MOE_ROWS
    blk_e = jnp.minimum(jnp.sum((pend[None, :] <= blk_row0[:, None]).astype(jnp.int32), axis=1), N_EXPERTS - 1)
    n_valid = jnp.clip((pstart + counts)[blk_e] - blk_row0, 0, MOE_ROWS).astype(jnp.int32)
    dest0 = dest[:, 0].reshape(N // tm, 1, tm)
    dest1 = dest[:, 1].reshape(N // tm, 1, tm)
    return n_used, blk_e.astype(jnp.int32), n_valid, dest0, dest1, L_pad


def _combine_kernel(d0_ref, d1_ref, d0n_ref, d1n_ref, x_ref, rt_ref, mod_ref, nf_ref, ys_hbm, o_ref,
                    buf, sem, *, final):
    i = pl.program_id(0)
    n = pl.num_programs(0)
    slot = lax.rem(i, 2)
    tm = x_ref.shape[0]

    def copy(s, k, t, src):
        return pltpu.make_async_copy(ys_hbm.at[pl.ds(src * SUBLANE, SUBLANE)],
                                     buf.at[s, k, pl.ds(t * SUBLANE, SUBLANE)], sem.at[s])

    def start_tile(s, a_ref, b_ref):
        def start(t):
            copy(s, 0, t, a_ref[0, 0, t]).start(priority=0)
            copy(s, 1, t, b_ref[0, 0, t]).start(priority=1)
        _tile_copy_loop(tm, start)

    @pl.when(i == 0)
    def _():
        start_tile(0, d0_ref, d1_ref)

    @pl.when(i + 1 < n)
    def _():
        start_tile(1 - slot, d0n_ref, d1n_ref)

    _tile_copy_loop(TOP_K * tm, lambda t: copy(slot, 0, 0, 0).wait())
    rt = rt_ref[...]
    y = rt[:, 2:3] * _load_row_tiles(buf.at[slot, 0], tm) + rt[:, 3:4] * _load_row_tiles(buf.at[slot, 1], tm)
    x2 = x_ref[...] + mod_ref[0, 5:6, :] * y
    if final:
        ms = jnp.mean(x2 * x2, axis=-1, keepdims=True)
        x2 = x2 * lax.rsqrt(ms + EPS) * nf_ref[...]
    o_ref[...] = x2


def _combine_call(x1, ys, route, dest0, dest1, mod_l, norm_final, final):
    B, S, D = x1.shape
    N = B * S
    n_tiles, _, tm = dest0.shape
    per_b = S // tm
    idx_blk = lambda fn: pl.BlockSpec((1, 1, tm), fn, memory_space=pltpu.SMEM)
    cur = lambda i: (i, 0, 0)
    nxt = lambda i: (jnp.minimum(i + 1, n_tiles - 1), 0, 0)
    out = pl.pallas_call(
        functools.partial(_combine_kernel, final=final),
        out_shape=jax.ShapeDtypeStruct((N, D), F32),
        grid=(n_tiles,),
        in_specs=[idx_blk(cur), idx_blk(cur), idx_blk(nxt), idx_blk(nxt),
                  pl.BlockSpec((tm, D), lambda i: (i, 0)),
                  pl.BlockSpec((tm, LANE), lambda i: (i, 0)),
                  pl.BlockSpec((1, 6, D), lambda i: (i // per_b, 0, 0)),
                  pl.BlockSpec((1, D), lambda i: (0, 0)),
                  pl.BlockSpec(memory_space=pl.ANY)],
        out_specs=pl.BlockSpec((tm, D), lambda i: (i, 0)),
        scratch_shapes=[pltpu.VMEM((2, TOP_K, tm * SUBLANE, LANE), F32), pltpu.SemaphoreType.DMA((2,))],
        compiler_params=_cparams(("arbitrary",)),
        name="moe_combine",
    )(dest0, dest1, dest0, dest1, x1.reshape(N, D), route.reshape(N, LANE), mod_l, norm_final.reshape(1, D), ys)
    return out.reshape(B, S, D)


def _layer(x, mod_l, p, big, l, final, norm_final):
    B, S, D = x.shape
    N = B * S
    s5_u, g_qkv, g_z, s_z, s_xbc, small = _proj_call(x, mod_l, p["norm_mix"], big["w_in"], l)
    bm, cm, lam_rows = _s5_params(p["s5_a_re"], p["s5_a_im"], p["s5_b_re"], p["s5_b_im"],
                                  p["s5_c_re"], p["s5_c_im"], p["s5_log_dt"])
    y_s5 = _s5_call(s5_u, bm, cm, lam_rows, p["s5_d"], p["s5_w_glu"], p["s5_norm"])
    y_gdn = _gdn_call(g_qkv, g_z, small, p["gdn_conv_w"], p["gdn_a_log"], p["gdn_dt_bias"], p["gdn_norm"])
    y_ssd = _ssd_call(s_xbc, s_z, small, p["ssd_conv_w"], p["ssd_conv_b"], p["ssd_a_log"], p["ssd_dt_bias"],
                      p["ssd_d"], p["ssd_norm"])
    w_router, b_router = _router_params(p["moe_w_grp"], p["moe_b_grp"], p["moe_w_rt"], p["moe_b_rt"])
    x1, h2, route, counts = _out_call(x, y_s5, y_gdn, y_ssd, mod_l, p["norm_ffn"], big["w_out"],
                                      w_router, b_router, l)
    n_used, blk_e, n_valid, dest0, dest1, n_rows = _dispatch(route.reshape(N, LANE), counts[0, 0:N_EXPERTS], N,
                                                             min(MOE_TOKEN_TILE, S))
    xs = _scatter_call(h2, dest0, dest1, n_rows)
    ys = _expert_call(xs, n_used, blk_e, n_valid, big["moe_w_gate"], big["moe_w_up"], big["moe_w_down"], l)
    return _combine_call(x1, ys, route, dest0, dest1, mod_l, norm_final, final)


def kernel(x, c, w_ada, b_ada, norm_mix, norm_ffn, w_in, w_out, s5_a_re, s5_a_im, s5_b_re, s5_b_im, s5_c_re, s5_c_im, s5_d, s5_log_dt, s5_w_glu, s5_norm, gdn_conv_w, gdn_a_log, gdn_dt_bias, gdn_norm, ssd_conv_w, ssd_conv_b, ssd_a_log, ssd_dt_bias, ssd_d, ssd_norm, moe_w_grp, moe_b_grp, moe_w_rt, moe_b_rt, moe_w_gate, moe_w_up, moe_w_down, norm_final):
    stacked = dict(norm_mix=norm_mix, norm_ffn=norm_ffn, s5_a_re=s5_a_re, s5_a_im=s5_a_im,
                   s5_b_re=s5_b_re, s5_b_im=s5_b_im, s5_c_re=s5_c_re, s5_c_im=s5_c_im, s5_d=s5_d,
                   s5_log_dt=s5_log_dt, s5_w_glu=s5_w_glu, s5_norm=s5_norm, gdn_conv_w=gdn_conv_w,
                   gdn_a_log=gdn_a_log, gdn_dt_bias=gdn_dt_bias, gdn_norm=gdn_norm, ssd_conv_w=ssd_conv_w,
                   ssd_conv_b=ssd_conv_b, ssd_a_log=ssd_a_log, ssd_dt_bias=ssd_dt_bias, ssd_d=ssd_d,
                   ssd_norm=ssd_norm, moe_w_grp=moe_w_grp, moe_b_grp=moe_b_grp, moe_w_rt=moe_w_rt,
                   moe_b_rt=moe_b_rt)
    big = dict(w_in=_w_in_prep_call(w_in), w_out=w_out, moe_w_gate=moe_w_gate, moe_w_up=moe_w_up,
               moe_w_down=moe_w_down)
    L = w_in.shape[0]
    B, S, D = x.shape
    mod = _mod_call(c, w_ada, b_ada).reshape(L, B, 6, D)
    for l in range(L):
        p = {k: v[l] for k, v in stacked.items()}
        x = _layer(x, mod[l], p, big, l, l == L - 1, norm_final)
    return x
```

```python
import functools

import numpy as np
import jax
import jax.numpy as jnp
from jax import lax
from jax.experimental import pallas as pl
from jax.experimental.pallas import tpu as pltpu

F32 = jnp.float32
BF16 = jnp.bfloat16

D_MODEL = 1024
DEPTH = 4
EPS = 1e-6
CONV_WIDTH = 4
CHUNK = 64
S5_WIDTH = 256
S5_CH = 16
S5_GROUPS = 16
S5_STATE = 64
S5_LANES = S5_GROUPS * S5_STATE
GDN_WIDTH = 384
GDN_HEAD_DIM = 64
GDN_HEADS = 6
GDN_CONV_DIM = 3 * GDN_WIDTH
SSD_WIDTH = 384
SSD_HEAD_DIM = 64
SSD_HEADS = 6
SSD_GROUPS = 2
SSD_STATE = 128
SSD_GROUP_WIDTH = SSD_WIDTH // SSD_GROUPS
SSD_CONV_DIM = SSD_WIDTH + 2 * SSD_GROUPS * SSD_STATE
PROJ_SIZES = (S5_WIDTH, GDN_CONV_DIM, GDN_WIDTH, GDN_HEADS, GDN_HEADS, SSD_WIDTH, SSD_CONV_DIM, SSD_HEADS)
MOE_GROUPS = 4
EXPERTS_PER_GROUP = 8
N_EXPERTS = 32
TOP_K = 2
D_EXPERT = 256

LANE = 128
SUBLANE = 8
SMALL_W = LANE
A_LANE, B_LANE, DT_LANE = 0, GDN_HEADS, 2 * GDN_HEADS
GRP_LANE = N_EXPERTS
MOE_ROWS = 512
MOE_TOKEN_TILE = 256
GDN_PAIRS_PER_ITER = 4
SSD_CHUNKS_PER_ITER = 2
VMEM_LIMIT = 56 * 1024 * 1024
NEG_BIG = -1e30


def _cparams(sem):
    return pltpu.CompilerParams(dimension_semantics=sem, vmem_limit_bytes=VMEM_LIMIT)


def _split(a):
    hi = a.astype(BF16)
    lo = (a - hi.astype(F32)).astype(BF16)
    return hi, lo


_NN = (((1,), (0,)), ((), ()))
_NT = (((1,), (1,)), ((), ()))
_TN = (((0,), (0,)), ((), ()))


def _dot(a, b, dims=_NN):
    return lax.dot_general(a.astype(BF16), b.astype(BF16), dims, preferred_element_type=F32)


def _dot2(a, b_bf16):
    hi, lo = _split(a)
    return (lax.dot_general(hi, b_bf16, _NN, preferred_element_type=F32)
            + lax.dot_general(lo, b_bf16, _NN, preferred_element_type=F32))


def _dot_sel3(a, b_bf16):
    h1 = a.astype(BF16)
    r1 = a - h1.astype(F32)
    h2 = r1.astype(BF16)
    h3 = (r1 - h2.astype(F32)).astype(BF16)
    d = functools.partial(lax.dot_general, dimension_numbers=_NN, preferred_element_type=F32)
    return d(h1, b_bf16) + (d(h2, b_bf16) + d(h3, b_bf16))


def _dot3(a, b):
    ah, al = _split(a)
    bh, bl = _split(b)
    d = functools.partial(lax.dot_general, dimension_numbers=_NN, preferred_element_type=F32)
    return d(ah, bh) + (d(ah, bl) + d(al, bh))


def _silu(x):
    return x * jax.nn.sigmoid(x)


def _softplus(x):
    return jnp.maximum(x, 0.0) + jnp.log(1.0 + jnp.exp(-jnp.abs(x)))


def _norm_mod(x, w, scale, shift):
    ms = jnp.mean(x * x, axis=-1, keepdims=True)
    return (x * lax.rsqrt(ms + EPS) * w) * (1.0 + scale) + shift


def _mod_kernel(c_ref, w_ref, b_ref, o_ref):
    cond = _silu(c_ref[...])
    o_ref[0] = _dot3(cond, w_ref[0]) + b_ref[0]


def _mod_call(c, w_ada, b_ada):
    L, D, W = w_ada.shape
    B = c.shape[0]
    tn = 1536
    return pl.pallas_call(
        _mod_kernel,
        out_shape=jax.ShapeDtypeStruct((L, B, W), F32),
        grid=(L, W // tn),
        in_specs=[pl.BlockSpec((B, D), lambda l, j: (0, 0)),
                  pl.BlockSpec((1, D, tn), lambda l, j: (l, 0, j)),
                  pl.BlockSpec((1, 1, tn), lambda l, j: (l, 0, j))],
        out_specs=pl.BlockSpec((1, B, tn), lambda l, j: (l, 0, j)),
        compiler_params=_cparams(("arbitrary", "arbitrary")),
        name="adaln_mod",
    )(c, w_ada, b_ada.reshape(L, 1, W))


PROJ_OUT_W = (S5_WIDTH, GDN_CONV_DIM, GDN_WIDTH, SSD_WIDTH, SSD_CONV_DIM, SMALL_W)


def _proj_kernel(x_ref, mod_ref, nw_ref, w_ref, *o_refs):
    x = x_ref[0]
    h = _norm_mod(x, nw_ref[...], mod_ref[0, 1:2, :], mod_ref[0, 0:1, :]).astype(BF16)
    proj = jnp.dot(h, w_ref[0], preferred_element_type=F32)
    off = 0
    for o_ref in o_refs:
        n = o_ref.shape[-1]
        o_ref[0] = proj[:, off:off + n]
        off += n


_W_HEAD = S5_WIDTH + GDN_CONV_DIM + GDN_WIDTH
_W_AB = _W_HEAD + 2 * GDN_HEADS
_W_SSD = _W_AB + SSD_WIDTH + SSD_CONV_DIM
_W_END = _W_SSD + SSD_HEADS
PROJ_W = sum(PROJ_OUT_W)
W_PREP_ROWS = 256


def _w_in_prep_kernel(w_ref, o_ref):
    n_ssd = _W_SSD - _W_AB
    for r in range(0, w_ref.shape[1], W_PREP_ROWS):
        w = w_ref[0, r:r + W_PREP_ROWS, :]
        rows = slice(r, r + W_PREP_ROWS)
        o_ref[0, rows, 0:_W_HEAD] = w[:, 0:_W_HEAD].astype(BF16)
        o_ref[0, rows, _W_HEAD:_W_HEAD + n_ssd] = w[:, _W_AB:_W_SSD].astype(BF16)
        small = jnp.concatenate([w[:, _W_HEAD:_W_AB], w[:, _W_SSD:_W_END],
                                 jnp.zeros((W_PREP_ROWS, SMALL_W - (_W_AB - _W_HEAD) - (_W_END - _W_SSD)), F32)], axis=1)
        o_ref[0, rows, _W_HEAD + n_ssd:PROJ_W] = small.astype(BF16)


def _w_in_prep_call(w_in):
    L, D, W = w_in.shape
    return pl.pallas_call(
        _w_in_prep_kernel,
        out_shape=jax.ShapeDtypeStruct((L, D, PROJ_W), BF16),
        grid=(L,),
        in_specs=[pl.BlockSpec((1, D, W), lambda l: (l, 0, 0))],
        out_specs=pl.BlockSpec((1, D, PROJ_W), lambda l: (l, 0, 0)),
        compiler_params=_cparams(("arbitrary",)),
        name="w_in_prep",
    )(w_in)


def _proj_call(x, mod_l, norm_w, w_arr, l):
    B, S, D = x.shape
    tm = min(512, S)
    out_shape = [jax.ShapeDtypeStruct((B, S, n), F32) for n in PROJ_OUT_W]
    out_specs = [pl.BlockSpec((1, tm, n), lambda b, i: (b, i, 0)) for n in PROJ_OUT_W]
    return pl.pallas_call(
        _proj_kernel,
        out_shape=out_shape,
        grid=(B, S // tm),
        in_specs=[pl.BlockSpec((1, tm, D), lambda b, i: (b, i, 0)),
                  pl.BlockSpec((1, 6, D), lambda b, i: (b, 0, 0)),
                  pl.BlockSpec((1, D), lambda b, i: (0, 0)),
                  pl.BlockSpec((1, D, PROJ_W), lambda b, i: (l, 0, 0))],
        out_specs=out_specs,
        compiler_params=_cparams(("arbitrary", "arbitrary")),
        name="norm_in_proj",
    )(x, mod_l, norm_w.reshape(1, D), w_arr)


def _s5_kernel(u_ref, bm_ref, cm_ref, lam_ref, dsk_ref, glu_ref, nw_ref, perm_ref, o_ref,
               x_s, st_s, ubt_s, u_s, *, T):
    B = u_ref.shape[0]
    P = S5_LANES

    @pl.when(pl.program_id(0) == 0)
    def _():
        st_s[...] = jnp.zeros_like(st_s)

    nw = S5_WIDTH // LANE
    ubt = u_ref[...].reshape(B * T, S5_WIDTH)
    for j in range(nw):
        ubt_s[j] = ubt[:, j * LANE:(j + 1) * LANE]

    def regroup(t, carry):
        for j in range(nw):
            u_s[pl.ds(pl.multiple_of(t * B, B), B), j * LANE:(j + 1) * LANE] = ubt_s[j, pl.ds(t, B, stride=T), :]
        return carry

    lax.fori_loop(0, T, regroup, 0, unroll=8)
    u = u_s[...]
    x_s[...] = jnp.dot(u.astype(BF16), bm_ref[...], preferred_element_type=F32)
    lr = jnp.broadcast_to(lam_ref[0:1, :], (B, P))
    li = jnp.broadcast_to(lam_ref[1:2, :], (B, P))

    def step(t, carry):
        sr, si = carry
        rows = pl.ds(pl.multiple_of(t * B, B), B)
        nr = lr * sr - li * si + x_s[rows, 0:P]
        ni = lr * si + li * sr + x_s[rows, P:2 * P]
        x_s[rows, 0:P] = nr
        x_s[rows, P:2 * P] = ni
        return nr, ni

    sr, si = lax.fori_loop(0, T, step, (st_s[:, 0:P], st_s[:, P:2 * P]), unroll=4)
    st_s[:, 0:P] = sr
    st_s[:, P:2 * P] = si

    y = jnp.dot(x_s[...].astype(BF16), cm_ref[...], preferred_element_type=F32) + dsk_ref[...] * u
    y = jax.nn.gelu(y)
    y = y * jax.nn.sigmoid(jnp.dot(y.astype(BF16), glu_ref[...], preferred_element_type=F32))
    ms = jnp.mean(y * y, axis=-1, keepdims=True)
    y = (y * lax.rsqrt(ms + EPS) * nw_ref[...]).astype(BF16)
    y = jnp.dot(perm_ref[...], y, preferred_element_type=F32)
    o_ref[...] = y.reshape(B, T, S5_WIDTH).astype(o_ref.dtype)


def _s5_params(a_re, a_im, b_re, b_im, c_re, c_im, log_dt):
    G, P, CH = S5_GROUPS, S5_STATE, S5_CH
    lam = lax.complex(a_re.astype(F32), a_im.astype(F32))
    step = jnp.exp(log_dt.astype(F32))[:, None]
    lam_bar = jnp.exp(lam * step)
    b_bar = ((lam_bar - 1.0) / lam)[..., None] * lax.complex(b_re.astype(F32), b_im.astype(F32))
    eye = jnp.eye(G, dtype=F32)
    bre = jnp.einsum('gpc,gh->gchp', b_bar.real, eye).reshape(G * CH, G * P)
    bim = jnp.einsum('gpc,gh->gchp', b_bar.imag, eye).reshape(G * CH, G * P)
    bm = jnp.concatenate([bre, bim], axis=1).astype(BF16)
    cre = jnp.einsum('gcp,gh->gphc', c_re.astype(F32), eye).reshape(G * P, G * CH)
    cim = jnp.einsum('gcp,gh->gphc', c_im.astype(F32), eye).reshape(G * P, G * CH)
    cm = jnp.concatenate([cre, -cim], axis=0).astype(BF16)
    lam_rows = jnp.zeros((SUBLANE, G * P), F32)
    lam_rows = lam_rows.at[0].set(lam_bar.real.reshape(-1)).at[1].set(lam_bar.imag.reshape(-1))
    return bm, cm, lam_rows


def _s5_call(u, bm, cm, lam_rows, d_skip, w_glu, norm_w):
    B, S, W = u.shape
    T = min(128, S)
    P2 = 2 * S5_LANES
    const = lambda shape: pl.BlockSpec(shape, lambda i: tuple(0 for _ in shape))
    r = np.arange(B * T)
    perm = np.zeros((B * T, B * T), np.float32)
    perm[r, (r % T) * B + r // T] = 1.0
    return pl.pallas_call(
        functools.partial(_s5_kernel, T=T),
        out_shape=jax.ShapeDtypeStruct((B, S, W), BF16),
        grid=(S // T,),
        in_specs=[pl.BlockSpec((B, T, W), lambda i: (0, i, 0)),
                  const((W, P2)), const((P2, W)), const((SUBLANE, S5_LANES)),
                  const((1, W)), const((W, W)), const((1, W)), const((B * T, B * T))],
        out_specs=pl.BlockSpec((B, T, W), lambda i: (0, i, 0)),
        scratch_shapes=[pltpu.VMEM((B * T, P2), F32), pltpu.VMEM((B, P2), F32),
                        pltpu.VMEM((W // LANE, B * T, LANE), F32), pltpu.VMEM((B * T, W), F32)],
        compiler_params=_cparams(("arbitrary",)),
        name="s5_mixer",
    )(u, bm, cm, lam_rows, d_skip.reshape(1, W).astype(F32), w_glu.astype(BF16), norm_w.reshape(1, W).astype(F32),
      jnp.asarray(perm, BF16))


def _causal_conv_silu(x, xf_ref, cw_ref, bias):
    n = x.shape[0]
    xf_ref[SUBLANE:, :] = x
    acc = x * cw_ref[CONV_WIDTH - 1:CONV_WIDTH, :]
    for k in range(1, CONV_WIDTH):
        acc = acc + xf_ref[SUBLANE - k:SUBLANE - k + n, :] * cw_ref[CONV_WIDTH - 1 - k:CONV_WIDTH - k, :]
    xf_ref[0:SUBLANE, :] = x[n - SUBLANE:, :]
    if bias is not None:
        acc = acc + bias
    return _silu(acc)


def _store_row_tiles(ref, val):
    n = val.shape[0]
    for j in range(val.shape[1] // LANE):
        ref[pl.ds(j, n, stride=SUBLANE), :] = val[:, j * LANE:(j + 1) * LANE]


def _load_row_tiles(ref, n):
    return jnp.concatenate([ref[pl.ds(j, n, stride=SUBLANE), :] for j in range(SUBLANE)], axis=1)


def _lanes_from(sm, off):
    return pltpu.roll(sm, SMALL_W - off, axis=1) if off else sm


def _chunk_scans(g):
    n = g.shape[0]
    rin = lax.broadcasted_iota(jnp.int32, g.shape, 0) & (CHUNK - 1)
    pre = g
    suf = jnp.where(rin < CHUNK - 1, pltpu.roll(g, n - 1, axis=0), 0.0)
    s = 1
    while s < CHUNK:
        pre = pre + jnp.where(rin >= s, pltpu.roll(pre, s, axis=0), 0.0)
        suf = suf + jnp.where(rin + s <= CHUNK - 1, pltpu.roll(suf, n - s, axis=0), 0.0)
        s *= 2
    return pre, suf


def _tri_masks():
    r = lax.broadcasted_iota(jnp.int32, (CHUNK, CHUNK), 0)
    c = lax.broadcasted_iota(jnp.int32, (CHUNK, CHUNK), 1)
    return r >= c, r > c, r == c


def _head_expand():
    m = np.zeros((SMALL_W, GDN_WIDTH), np.float32)
    for h in range(GDN_HEADS):
        m[h, h * GDN_HEAD_DIM:(h + 1) * GDN_HEAD_DIM] = 1.0
    return jnp.asarray(m, BF16)


def _block_ones(width, blk):
    idx = np.arange(width) // blk
    return jnp.asarray((idx[:, None] == idx[None, :]).astype(np.float32), BF16)


def _gdn_kernel(qkv_ref, z_ref, sm_ref, cw_ref, hp_ref, nw_ref, e_ref, xp_ref, bdm_ref, o_ref,
                tail_s, st_s, kn_s, kb_s, qn_s, qd_s, kd_s, u_s, w_s, eg_s, gcx_s, gw_s,
                o_s, au_s, qe_s, ku_s, kw_s, *, TB):
    H, Dh, C = GDN_HEADS, GDN_HEAD_DIM, CHUNK
    W = GDN_WIDTH
    ncb = TB // C

    @pl.when(pl.program_id(1) == 0)
    def _():
        tail_s[0:SUBLANE, :] = jnp.zeros((SUBLANE, tail_s.shape[1]), F32)
        st_s[...] = jnp.zeros_like(st_s)

    xc = _causal_conv_silu(qkv_ref[0], tail_s, cw_ref, None)
    q, k, v = xc[:, 0:W], xc[:, W:2 * W], xc[:, 2 * W:3 * W]
    e = e_ref[...]
    xp = xp_ref[...]
    qn = q * lax.rsqrt(_dot2(q * q, e) + EPS) * (Dh ** -0.5)
    kn = k * lax.rsqrt(_dot2(k * k, e) + EPS)

    sm = sm_ref[0]
    lane = lax.broadcasted_iota(jnp.int32, sm.shape, 1)
    head_lane = lane < H
    g = jnp.where(head_lane, hp_ref[0:1, :] * _softplus(_lanes_from(sm, A_LANE) + hp_ref[1:2, :]), 0.0)
    beta = jnp.where(head_lane, jax.nn.sigmoid(_lanes_from(sm, B_LANE)), 0.0)
    gc, rc = _chunk_scans(g)
    bx = _dot2(beta, xp)
    egx = _dot2(jnp.exp(gc), xp)
    erx = _dot2(jnp.exp(rc), xp)
    kb = kn * bx
    kn_s[...] = kn
    kb_s[...] = kb
    qn_s[...] = qn
    qd_s[...] = qn * egx
    kd_s[...] = kn * erx
    u_s[...] = v * bx
    w_s[...] = kb * egx
    gcx = _dot_sel3(gc, xp)
    gcx_s[...] = gcx
    r384 = lax.broadcasted_iota(jnp.int32, (C, W), 0)
    l384 = lax.broadcasted_iota(jnp.int32, (C, W), 1) & (Dh - 1)
    for c in range(ncb):
        diag_c = jnp.where(r384 == l384, gcx[c * C:(c + 1) * C, :], 0.0)
        gw_s[c] = jnp.broadcast_to(jnp.sum(diag_c, axis=0, keepdims=True), (SUBLANE, W))
        eg_s[c] = jnp.broadcast_to(egx[(c + 1) * C - 1:(c + 1) * C, :], (SUBLANE, W))

    heads = [slice(h * Dh, (h + 1) * Dh) for h in range(H)]

    GL = 4 * Dh
    row_w = lax.broadcasted_iota(jnp.int32, (C, GL), 0)
    col_w = lax.broadcasted_iota(jnp.int32, (C, GL), 1) & (Dh - 1)
    causal_w, strict_w = row_w >= col_w, row_w > col_w
    eye_w = jnp.where(row_w == col_w, 1.0, 0.0).astype(F32)
    bdm = bdm_ref[...]
    mm = functools.partial(lax.dot_general, dimension_numbers=_NN, preferred_element_type=F32)

    def block_diag(m, mask):
        return jnp.concatenate([m] * 4, axis=0) * mask

    npair = min(GDN_PAIRS_PER_ITER, ncb // 2)

    def solve(it, carry):
        chunks = [2 * npair * it + k for k in range(2 * npair)]
        rows = [pl.ds(pl.multiple_of(c * C, C), C) for c in chunks]

        def groups(get, n):
            out = []
            for j in range(npair):
                a0, a1 = get(2 * j), get(2 * j + 1)
                out += [a0[:, 0:4 * n], a1[:, 0:4 * n],
                        jnp.concatenate([a0[:, 4 * n:6 * n], a1[:, 4 * n:6 * n]], axis=1)]
            return out

        def ungroup(ref, vals, n):
            for j in range(npair):
                ref[rows[2 * j], 0:4 * n] = vals[3 * j]
                ref[rows[2 * j + 1], 0:4 * n] = vals[3 * j + 1]
                ref[rows[2 * j], 4 * n:6 * n] = vals[3 * j + 2][:, 0:2 * n]
                ref[rows[2 * j + 1], 4 * n:6 * n] = vals[3 * j + 2][:, 2 * n:4 * n]

        kn_g = groups(lambda k: kn_s[rows[k], :], Dh)
        kb_g = groups(lambda k: kb_s[rows[k], :], Dh)
        qn_g = groups(lambda k: qn_s[rows[k], :], Dh)
        gx_g = groups(lambda k: gcx_s[rows[k], :], Dh)
        gw_g = groups(lambda k: gw_s[chunks[k], 0:1, :], Dh)
        decs = [jnp.exp(jnp.where(causal_w, gx - gw, NEG_BIG)) for gx, gw in zip(gx_g, gw_g)]
        prods = [lax.dot_general(jnp.concatenate([kb, qn], axis=0).astype(BF16), block_diag(kn.astype(BF16), bdm),
                                 _NT, preferred_element_type=F32)
                 for kn, kb, qn in zip(kn_g, kb_g, qn_g)]
        attns = [pr[C:2 * C] * dec for pr, dec in zip(prods, decs)]
        def times(lhs, p=None):
            lh, ll = _split(lhs)
            ph, pl_ = (lh[0:C], ll[0:C]) if p is None else _split(p)
            rh, rl = block_diag(ph, bdm), block_diag(pl_, bdm)
            return mm(lh, rh) + (mm(lh, rl) + mm(ll, rh))

        ps = [-jnp.where(strict_w, pr[0:C] * dec, 0.0) for pr, dec in zip(prods, decs)]
        invs = [eye_w + p for p in ps]
        ps = [times(p) for p in ps]
        for _ in range(4):
            outs = [times(jnp.concatenate([p, inv], axis=0)) for p, inv in zip(ps, invs)]
            invs = [inv + o[C:2 * C] for inv, o in zip(invs, outs)]
            ps = [o[0:C] for o in outs]
        invs = [inv + times(inv, p) for p, inv in zip(ps, invs)]
        def solved(ref):
            outs = []
            for inv, r in zip(invs, groups(lambda k: ref[rows[k], :], Dh)):
                ih, il = _split(inv)
                rh, rl = _split(r)
                rh, rl = block_diag(rh, bdm), block_diag(rl, bdm)
                outs.append(mm(ih, rh) + (mm(ih, rl) + mm(il, rh)))
            return outs

        u_g, w_g = solved(u_s), solved(w_s)
        ungroup(u_s, u_g, Dh)
        ungroup(w_s, w_g, Dh)
        at_g = [a.astype(BF16) for a in attns]
        au_g = [mm(a, block_diag(u.astype(BF16), bdm)) for a, u in zip(at_g, u_g)]
        aw_g = [mm(a, block_diag(w.astype(BF16), bdm)) for a, w in zip(at_g, w_g)]
        qd_g = groups(lambda k: qd_s[rows[k], :], Dh)
        ungroup(au_s, au_g, Dh)
        ungroup(qe_s, [qd - aw for qd, aw in zip(qd_g, aw_g)], Dh)
        for k in range(2 * npair):
            for sl in heads:
                kd_h = kd_s[rows[k], sl]
                ku_s[rows[k], sl] = _dot(kd_h, u_s[rows[k], sl], _TN)
                kw_s[rows[k], sl] = _dot(kd_h, w_s[rows[k], sl], _TN)
        return carry

    lax.fori_loop(0, ncb // (2 * npair), solve, 0)

    def recur(c, carry):
        rows = pl.ds(pl.multiple_of(c * C, C), C)
        st = st_s[...]
        sb = st.astype(BF16)
        lhs = jnp.concatenate([qe_s[rows, :], kw_s[rows, :]], axis=0).astype(BF16)
        out = jnp.concatenate(
            [mm(lhs[:, 0:GL], block_diag(sb[:, 0:GL], bdm)),
             mm(lhs[:, GL:W], jnp.concatenate([sb[:, GL:W]] * 2, axis=0) * bdm[0:W - GL, 0:W - GL])], axis=1)
        o_s[rows, :] = out[0:C] + au_s[rows, :]
        st_s[...] = st * eg_s[c, 0:1, :] + (ku_s[rows, :] - out[C:2 * C])
        return carry

    lax.fori_loop(0, ncb, recur, 0)

    o = o_s[...]
    ms = _dot2(o * o, e) * (1.0 / Dh)
    o = o * lax.rsqrt(ms + EPS) * nw_ref[...]
    o_ref[0] = (o * _silu(z_ref[0])).astype(o_ref.dtype)


def _gdn_call(qkv, z, sm, conv_w, a_log, dt_bias, norm_w):
    B, S, _ = qkv.shape
    TB = min(512, S)
    W, H, Dh = GDN_WIDTH, GDN_HEADS, GDN_HEAD_DIM
    hp = jnp.zeros((SUBLANE, SMALL_W), F32)
    hp = hp.at[0, :H].set(-jnp.exp(a_log.astype(F32))).at[1, :H].set(dt_bias.astype(F32))
    nw = jnp.tile(norm_w.astype(F32), H).reshape(1, W)
    const = lambda shape: pl.BlockSpec(shape, lambda b, i: tuple(0 for _ in shape))
    blk = lambda n: pl.BlockSpec((1, TB, n), lambda b, i: (b, i, 0))
    f = lambda *shape: pltpu.VMEM(shape, F32)
    return pl.pallas_call(
        functools.partial(_gdn_kernel, TB=TB),
        out_shape=jax.ShapeDtypeStruct((B, S, W), BF16),
        grid=(B, S // TB),
        in_specs=[blk(GDN_CONV_DIM), blk(W), blk(SMALL_W),
                  const((CONV_WIDTH, GDN_CONV_DIM)), const((SUBLANE, SMALL_W)), const((1, W)),
                  const((W, W)), const((SMALL_W, W)), const((4 * Dh, 4 * Dh))],
        out_specs=blk(W),
        scratch_shapes=[f(SUBLANE + TB, GDN_CONV_DIM), f(Dh, W)]
        + [f(TB, W)] * 7 + [f(TB // CHUNK, SUBLANE, W), f(TB, W), f(TB // CHUNK, SUBLANE, W)] + [f(TB, W)] * 5,
        compiler_params=_cparams(("arbitrary", "arbitrary")),
        name="gdn_mixer",
    )(qkv, z, sm, conv_w.astype(F32), hp, nw, _block_ones(W, Dh), _head_expand(), _block_ones(4 * Dh, Dh))


def _ssd_kernel(xbc_ref, z_ref, sm_ref, cw_ref, cb_ref, hp_ref, dsk_ref, nw_ref, e_ref, xp_ref, o_ref,
                tail_s, st_s, xs_s, xdt_s, xdd_s, bm_s, cm_s, ea_s, el_s, ac_s, act_s, y_s, inc_s, *, TB):
    H, P, G, N, C = SSD_HEADS, SSD_HEAD_DIM, SSD_GROUPS, SSD_STATE, CHUNK
    R = H // G
    W, GW = SSD_WIDTH, SSD_GROUP_WIDTH
    ncb = TB // C

    @pl.when(pl.program_id(1) == 0)
    def _():
        tail_s[0:SUBLANE, :] = jnp.zeros((SUBLANE, tail_s.shape[1]), F32)
        st_s[...] = jnp.zeros_like(st_s)

    xc = _causal_conv_silu(xbc_ref[0], tail_s, cw_ref, cb_ref[...])
    xs = xc[:, 0:W]
    xp = xp_ref[...]

    sm = sm_ref[0]
    lane = lax.broadcasted_iota(jnp.int32, sm.shape, 1)
    head_lane = lane < H
    dt = jnp.where(head_lane, _softplus(_lanes_from(sm, DT_LANE) + hp_ref[1:2, :]), 0.0)
    acs, rcs = _chunk_scans(dt * hp_ref[0:1, :])
    dtx = _dot2(dt, xp)
    eax = _dot2(jnp.exp(acs), xp)
    erx = _dot2(jnp.exp(rcs), xp)
    xdt = xs * dtx
    xs_s[...] = xs
    xdt_s[...] = xdt
    xdd_s[...] = xdt * erx
    bm_s[...] = xc[:, W:W + G * N]
    cm_s[...] = xc[:, W + G * N:W + 2 * G * N]
    ea_s[...] = eax
    ac_s[...] = acs
    for c in range(ncb):
        act_s[c] = acs[c * C:(c + 1) * C, :].T
        el_s[c] = jnp.broadcast_to(eax[(c + 1) * C - 1:(c + 1) * C, :], (SUBLANE, W))

    causal, _, _ = _tri_masks()

    groups = [(gi, slice(gi * GW, (gi + 1) * GW), slice(gi * N, (gi + 1) * N)) for gi in range(G)]

    def local(it, carry):
        work = []
        for cc in range(SSD_CHUNKS_PER_ITER):
            c = it * SSD_CHUNKS_PER_ITER + cc
            rows = pl.ds(pl.multiple_of(c * C, C), C)
            work += [(c, rows, ac_s[rows, :], act_s[c], g) for g in groups]
        cbs = [_dot(cm_s[rows, nsl], bm_s[rows, nsl], _NT) for c, rows, _, _, (gi, gsl, nsl) in work]
        for (c, rows, acol, arow, (gi, gsl, nsl)), cb in zip(work, cbs):
            for r in range(R):
                h = gi * R + r
                sl = slice(h * P, (h + 1) * P)
                seg = jnp.exp(jnp.where(causal, acol[:, h:h + 1] - arow[h:h + 1, :], NEG_BIG))
                y_s[rows, sl] = _dot(cb * seg, xdt_s[rows, sl])
        for c, rows, _, _, (gi, gsl, nsl) in work:
            inc_s[c, gi] = _dot(bm_s[rows, nsl], xdd_s[rows, gsl], _TN)
        return carry

    lax.fori_loop(0, ncb // SSD_CHUNKS_PER_ITER, local, 0)

    def recur(c, carry):
        rows = pl.ds(pl.multiple_of(c * C, C), C)
        sts = [st_s[gi] for gi, _, _ in groups]
        for (gi, gsl, nsl), st in zip(groups, sts):
            y_s[rows, gsl] = y_s[rows, gsl] + _dot(cm_s[rows, nsl], st) * ea_s[rows, gsl]
        for (gi, gsl, nsl), st in zip(groups, sts):
            st_s[gi] = st * el_s[c, 0:1, gsl] + inc_s[c, gi]
        return carry

    lax.fori_loop(0, ncb, recur, 0)

    y = y_s[...] + dsk_ref[...] * xs_s[...]
    y = y * _silu(z_ref[0])
    ms = _dot2(y * y, e_ref[...]) * (1.0 / GW)
    o_ref[0] = (y * lax.rsqrt(ms + EPS) * nw_ref[...]).astype(o_ref.dtype)


def _ssd_call(xbc, z, sm, conv_w, conv_b, a_log, dt_bias, d_skip, norm_w):
    B, S, _ = xbc.shape
    TB = min(512, S)
    W, H, G, N = SSD_WIDTH, SSD_HEADS, SSD_GROUPS, SSD_STATE
    hp = jnp.zeros((SUBLANE, SMALL_W), F32)
    hp = hp.at[0, :H].set(-jnp.exp(a_log.astype(F32))).at[1, :H].set(dt_bias.astype(F32))
    dsk = jnp.repeat(d_skip.astype(F32), SSD_HEAD_DIM).reshape(1, W)
    const = lambda shape: pl.BlockSpec(shape, lambda b, i: tuple(0 for _ in shape))
    blk = lambda n: pl.BlockSpec((1, TB, n), lambda b, i: (b, i, 0))
    f = lambda *shape: pltpu.VMEM(shape, F32)
    return pl.pallas_call(
        functools.partial(_ssd_kernel, TB=TB),
        out_shape=jax.ShapeDtypeStruct((B, S, W), BF16),
        grid=(B, S // TB),
        in_specs=[blk(SSD_CONV_DIM), blk(W), blk(SMALL_W),
                  const((CONV_WIDTH, SSD_CONV_DIM)), const((1, SSD_CONV_DIM)), const((SUBLANE, SMALL_W)),
                  const((1, W)), const((1, W)), const((W, W)), const((SMALL_W, W))],
        out_specs=blk(W),
        scratch_shapes=[f(SUBLANE + TB, SSD_CONV_DIM), f(G, N, SSD_GROUP_WIDTH),
                        f(TB, W), f(TB, W), f(TB, W), f(TB, G * N), f(TB, G * N), f(TB, W),
                        f(TB // CHUNK, SUBLANE, W),
                        f(TB, SMALL_W), f(TB // CHUNK, SMALL_W, CHUNK), f(TB, W),
                        f(TB // CHUNK, G, N, SSD_GROUP_WIDTH)],
        compiler_params=_cparams(("arbitrary", "arbitrary")),
        name="ssd_mixer",
    )(xbc, z, sm, conv_w.astype(F32), conv_b.reshape(1, -1).astype(F32), hp, dsk,
      norm_w.reshape(1, W).astype(F32), _block_ones(W, SSD_GROUP_WIDTH), _head_expand())


def _out_kernel(x_ref, y1_ref, y2_ref, y3_ref, mod_ref, nw_ref, wo_ref, wr_ref, br_ref, tri_ref,
                x1_ref, h_ref, rt_ref, cnt_ref, run_s, wo_s):
    first = (pl.program_id(0) == 0) & (pl.program_id(1) == 0)

    @pl.when(first)
    def _():
        wo_s[...] = wo_ref[0].astype(BF16)

    y = (jnp.dot(y1_ref[0], wo_s[0:S5_WIDTH, :], preferred_element_type=F32)
         + jnp.dot(y2_ref[0], wo_s[S5_WIDTH:S5_WIDTH + GDN_WIDTH, :], preferred_element_type=F32)
         + jnp.dot(y3_ref[0], wo_s[S5_WIDTH + GDN_WIDTH:, :], preferred_element_type=F32))
    x1 = x_ref[0] + mod_ref[0, 2:3, :] * y
    x1_ref[0] = x1
    h = _norm_mod(x1, nw_ref[...], mod_ref[0, 4:5, :], mod_ref[0, 3:4, :])
    _store_row_tiles(h_ref, h)
    lg =jnp.dot(h.astype(BF16), wr_ref[...], preferred_element_type=F32) + br_ref[...]

    lane = lax.broadcasted_iota(jnp.int32, lg.shape, 1)
    lanef = lane.astype(F32)
    big = float(4 * LANE)
    grp = (lane >= GRP_LANE) & (lane < GRP_LANE + MOE_GROUPS)
    lgm = jnp.where(grp, lg, -jnp.inf)
    m = jnp.max(lgm, axis=-1, keepdims=True)
    gidx = jnp.min(jnp.where(lgm == m, lanef - GRP_LANE, big), axis=-1, keepdims=True)
    g_w = 1.0 / jnp.sum(jnp.where(grp, jnp.exp(lg - m), 0.0), axis=-1, keepdims=True)
    in_grp = (lane < N_EXPERTS) & ((lane // EXPERTS_PER_GROUP).astype(F32) == gidx)
    le = jnp.where(in_grp, lg, -jnp.inf)
    v1 = jnp.max(le, axis=-1, keepdims=True)
    i1 = jnp.min(jnp.where(le == v1, lanef, big), axis=-1, keepdims=True)
    le2 = jnp.where(lanef == i1, -jnp.inf, le)
    v2 = jnp.max(le2, axis=-1, keepdims=True)
    i2 = jnp.min(jnp.where(le2 == v2, lanef, big), axis=-1, keepdims=True)
    e2 = jnp.exp(v2 - v1)
    w1 = g_w / (1.0 + e2)
    w2 = g_w * e2 / (1.0 + e2)

    @pl.when(first)
    def _():
        run_s[...] = jnp.zeros_like(run_s)

    chosen = jnp.where((lanef == i1) | (lanef == i2), 1.0, 0.0)
    before = jnp.dot(tri_ref[...], chosen.astype(BF16), preferred_element_type=F32) + run_s[0:1, :]
    p1 = jnp.sum(jnp.where(lanef == i1, before, 0.0), axis=-1, keepdims=True)
    p2 = jnp.sum(jnp.where(lanef == i2, before, 0.0), axis=-1, keepdims=True)
    run_s[...] = run_s[...] + jnp.sum(chosen, axis=0, keepdims=True)
    cnt_ref[...] = run_s[...]

    rt = jnp.zeros_like(lg)
    for k, val in enumerate((i1, i2, w1, w2, p1, p2)):
        rt = jnp.where(lane == k, val, rt)
    rt_ref[0] = rt


def _out_call(x, y1, y2, y3, mod_l, norm_w, w_out, w_router, b_router, l):
    B, S, D = x.shape
    tm = min(512, S)
    blk = lambda n: pl.BlockSpec((1, tm, n), lambda b, i: (b, i, 0))
    const = lambda shape: pl.BlockSpec(shape, lambda b, i: tuple(0 for _ in shape))
    tri = jnp.asarray(np.tril(np.ones((tm, tm), np.float32), -1), BF16)
    nb = S // tm
    tiles = D // LANE
    return pl.pallas_call(
        _out_kernel,
        out_shape=[jax.ShapeDtypeStruct((B, S, D), F32), jax.ShapeDtypeStruct((B * S * tiles, LANE), F32),
                   jax.ShapeDtypeStruct((B, S, LANE), F32), jax.ShapeDtypeStruct((SUBLANE, LANE), F32)],
        grid=(B, nb),
        in_specs=[blk(D), blk(S5_WIDTH), blk(GDN_WIDTH), blk(SSD_WIDTH),
                  pl.BlockSpec((1, 6, D), lambda b, i: (b, 0, 0)),
                  const((1, D)), pl.BlockSpec((1, D, D), lambda b, i: (l, 0, 0)),
                  const((D, LANE)), const((1, LANE)), const((tm, tm))],
        out_specs=[blk(D), pl.BlockSpec((tm * tiles, LANE), lambda b, i: (b * nb + i, 0)),
                   blk(LANE), const((SUBLANE, LANE))],
        scratch_shapes=[pltpu.VMEM((SUBLANE, LANE), F32), pltpu.VMEM((D, D), BF16)],
        compiler_params=_cparams(("arbitrary", "arbitrary")),
        name="out_proj_router",
    )(x, y1, y2, y3, mod_l, norm_w.reshape(1, D), w_out, w_router, b_router, tri)


def _router_params(w_grp, b_grp, w_rt, b_rt):
    D = w_grp.shape[0]
    w = jnp.zeros((D, LANE), F32).at[:, 0:N_EXPERTS].set(w_rt).at[:, GRP_LANE:GRP_LANE + MOE_GROUPS].set(w_grp)
    b = jnp.zeros((1, LANE), F32).at[0, 0:N_EXPERTS].set(b_rt).at[0, GRP_LANE:GRP_LANE + MOE_GROUPS].set(b_grp)
    return w.astype(BF16), b


def _tile_copy_loop(n, fn):
    def body(t, carry):
        fn(t)
        return carry
    lax.fori_loop(0, n, body, 0, unroll=32)


def _scatter_kernel(d0_ref, d1_ref, h_ref, xs_hbm, stage, sem):
    i = pl.program_id(0)
    n = pl.num_programs(0)
    slot = lax.rem(i, 2)
    tm = d0_ref.shape[-1]

    def copy(s, t, dst):
        return pltpu.make_async_copy(stage.at[s, pl.ds(t * SUBLANE, SUBLANE)],
                                     xs_hbm.at[pl.ds(dst * SUBLANE, SUBLANE)], sem.at[s])

    def wait_slot(s):
        _tile_copy_loop(TOP_K * tm, lambda t: copy(s, 0, 0).wait())

    @pl.when(i >= 2)
    def _():
        wait_slot(slot)

    stage[slot] = h_ref[...]

    def start(t):
        copy(slot, t, d0_ref[0, 0, t]).start(priority=0)
        copy(slot, t, d1_ref[0, 0, t]).start(priority=1)
    _tile_copy_loop(tm, start)

    @pl.when(i == n - 1)
    def _():
        @pl.when(i >= 1)
        def _():
            wait_slot(1 - slot)
        wait_slot(slot)


def _scatter_call(h2t, dest0, dest1, n_rows):
    n_tiles, _, tm = dest0.shape
    idx_blk = pl.BlockSpec((1, 1, tm), lambda i: (i, 0, 0), memory_space=pltpu.SMEM)
    return pl.pallas_call(
        _scatter_kernel,
        out_shape=jax.ShapeDtypeStruct((n_rows * SUBLANE, LANE), F32),
        grid=(n_tiles,),
        in_specs=[idx_blk, idx_blk, pl.BlockSpec((tm * SUBLANE, LANE), lambda i: (i, 0))],
        out_specs=pl.BlockSpec(memory_space=pl.ANY),
        scratch_shapes=[pltpu.VMEM((2, tm * SUBLANE, LANE), F32), pltpu.SemaphoreType.DMA((2,))],
        compiler_params=_cparams(("arbitrary",)),
        name="moe_scatter",
    )(dest0, dest1, h2t)


def _expert_kernel(nused_ref, blke_ref, nvalid_ref, xs_ref, wg_ref, wu_ref, wd_ref, ys_ref, wg_s, wu_s, wd_s):
    i = pl.program_id(0)

    @pl.when(i < nused_ref[0])
    def _():
        @pl.when((i == 0) | (blke_ref[i] != blke_ref[jnp.maximum(i - 1, 0)]))
        def _():
            wg_s[...] = wg_ref[0, 0].astype(BF16)
            wu_s[...] = wu_ref[0, 0].astype(BF16)
            wd_s[...] = wd_ref[0, 0].astype(BF16)

        x = _load_row_tiles(xs_ref, MOE_ROWS)
        row = lax.broadcasted_iota(jnp.int32, (MOE_ROWS, 1), 0)
        xb = jnp.where(row < nvalid_ref[i], x, 0.0).astype(BF16)
        hid = _silu(jnp.dot(xb, wg_s[...], preferred_element_type=F32)) * jnp.dot(xb, wu_s[...], preferred_element_type=F32)
        _store_row_tiles(ys_ref, jnp.dot(hid.astype(BF16), wd_s[...], preferred_element_type=F32))

    @pl.when(i >= nused_ref[0])
    def _():
        ys_ref[...] = jnp.zeros_like(ys_ref)


def _expert_call(xs, n_used, blk_e, n_valid, w_gate, w_up, w_down, l):
    D = w_gate.shape[2]
    n_blk = blk_e.shape[0]
    rows_blk = pl.BlockSpec((MOE_ROWS * SUBLANE, LANE), lambda i, nu, be, nv: (i, 0))
    grid_spec = pltpu.PrefetchScalarGridSpec(
        num_scalar_prefetch=3,
        grid=(n_blk,),
        in_specs=[rows_blk,
                  pl.BlockSpec((1, 1, D, D_EXPERT), lambda i, nu, be, nv: (l, be[i], 0, 0)),
                  pl.BlockSpec((1, 1, D, D_EXPERT), lambda i, nu, be, nv: (l, be[i], 0, 0)),
                  pl.BlockSpec((1, 1, D_EXPERT, D), lambda i, nu, be, nv: (l, be[i], 0, 0))],
        out_specs=rows_blk,
        scratch_shapes=[pltpu.VMEM((D, D_EXPERT), BF16), pltpu.VMEM((D, D_EXPERT), BF16),
                        pltpu.VMEM((D_EXPERT, D), BF16)],
    )
    return pl.pallas_call(
        _expert_kernel,
        out_shape=jax.ShapeDtypeStruct(xs.shape, F32),
        grid_spec=grid_spec,
        compiler_params=_cparams(("arbitrary",)),
        name="expert_mlp",
    )(n_used, blk_e, n_valid, xs, w_gate, w_up, w_down)


def _dispatch(route, counts, N, tm):
    L_pad = N * TOP_K + N_EXPERTS * MOE_ROWS
    n_blk = L_pad // MOE_ROWS
    eid = route[:, 0:TOP_K].astype(jnp.int32)
    pos = route[:, 4:4 + TOP_K].astype(jnp.int32)
    counts = counts.astype(jnp.int32)
    padded = ((counts + MOE_ROWS - 1) // MOE_ROWS) * MOE_ROWS
    pend = jnp.cumsum(padded)
    pstart = pend - padded
    dest = pstart[eid] + pos
    n_used = (pend[-1] // MOE_ROWS).astype(jnp.int32).reshape(1)
    blk_row0 = jnp.arange(n_blk, dtype=jnp.int32) * MOE_ROWS
    blk_e = jnp.minimum(jnp.sum((pend[None, :] <= blk_row0[:, None]).astype(jnp.int32), axis=1), N_EXPERTS - 1)
    n_valid = jnp.clip((pstart + counts)[blk_e] - blk_row0, 0, MOE_ROWS).astype(jnp.int32)
    dest0 = dest[:, 0].reshape(N // tm, 1, tm)
    dest1 = dest[:, 1].reshape(N // tm, 1, tm)
    return n_used, blk_e.astype(jnp.int32), n_valid, dest0, dest1, L_pad


def _combine_kernel(d0_ref, d1_ref, d0n_ref, d1n_ref, x_ref, rt_ref, mod_ref, nf_ref, ys_hbm, o_ref,
                    buf, sem, *, final):
    i = pl.program_id(0)
    n = pl.num_programs(0)
    slot = lax.rem(i, 2)
    tm = x_ref.shape[0]

    def copy(s, k, t, src):
        return pltpu.make_async_copy(ys_hbm.at[pl.ds(src * SUBLANE, SUBLANE)],
                                     buf.at[s, k, pl.ds(t * SUBLANE, SUBLANE)], sem.at[s])

    def start_tile(s, a_ref, b_ref):
        def start(t):
            copy(s, 0, t, a_ref[0, 0, t]).start(priority=0)
            copy(s, 1, t, b_ref[0, 0, t]).start(priority=1)
        _tile_copy_loop(tm, start)

    @pl.when(i == 0)
    def _():
        start_tile(0, d0_ref, d1_ref)

    @pl.when(i + 1 < n)
    def _():
        start_tile(1 - slot, d0n_ref, d1n_ref)

    _tile_copy_loop(TOP_K * tm, lambda t: copy(slot, 0, 0, 0).wait())
    rt = rt_ref[...]
    y = rt[:, 2:3] * _load_row_tiles(buf.at[slot, 0], tm) + rt[:, 3:4] * _load_row_tiles(buf.at[slot, 1], tm)
    x2 = x_ref[...] + mod_ref[0, 5:6, :] * y
    if final:
        ms = jnp.mean(x2 * x2, axis=-1, keepdims=True)
        x2 = x2 * lax.rsqrt(ms + EPS) * nf_ref[...]
    o_ref[...] = x2


def _combine_call(x1, ys, route, dest0, dest1, mod_l, norm_final, final):
    B, S, D = x1.shape
    N = B * S
    n_tiles, _, tm = dest0.shape
    per_b = S // tm
    idx_blk = lambda fn: pl.BlockSpec((1, 1, tm), fn, memory_space=pltpu.SMEM)
    cur = lambda i: (i, 0, 0)
    nxt = lambda i: (jnp.minimum(i + 1, n_tiles - 1), 0, 0)
    out = pl.pallas_call(
        functools.partial(_combine_kernel, final=final),
        out_shape=jax.ShapeDtypeStruct((N, D), F32),
        grid=(n_tiles,),
        in_specs=[idx_blk(cur), idx_blk(cur), idx_blk(nxt), idx_blk(nxt),
                  pl.BlockSpec((tm, D), lambda i: (i, 0)),
                  pl.BlockSpec((tm, LANE), lambda i: (i, 0)),
                  pl.BlockSpec((1, 6, D), lambda i: (i // per_b, 0, 0)),
                  pl.BlockSpec((1, D), lambda i: (0, 0)),
                  pl.BlockSpec(memory_space=pl.ANY)],
        out_specs=pl.BlockSpec((tm, D), lambda i: (i, 0)),
        scratch_shapes=[pltpu.VMEM((2, TOP_K, tm * SUBLANE, LANE), F32), pltpu.SemaphoreType.DMA((2,))],
        compiler_params=_cparams(("arbitrary",)),
        name="moe_combine",
    )(dest0, dest1, dest0, dest1, x1.reshape(N, D), route.reshape(N, LANE), mod_l, norm_final.reshape(1, D), ys)
    return out.reshape(B, S, D)


def _layer(x, mod_l, p, big, l, final, norm_final):
    B, S, D = x.shape
    N = B * S
    s5_u, g_qkv, g_z, s_z, s_xbc, small = _proj_call(x, mod_l, p["norm_mix"], big["w_in"], l)
    bm, cm, lam_rows = _s5_params(p["s5_a_re"], p["s5_a_im"], p["s5_b_re"], p["s5_b_im"],
                                  p["s5_c_re"], p["s5_c_im"], p["s5_log_dt"])
    y_s5 = _s5_call(s5_u, bm, cm, lam_rows, p["s5_d"], p["s5_w_glu"], p["s5_norm"])
    y_gdn = _gdn_call(g_qkv, g_z, small, p["gdn_conv_w"], p["gdn_a_log"], p["gdn_dt_bias"], p["gdn_norm"])
    y_ssd = _ssd_call(s_xbc, s_z, small, p["ssd_conv_w"], p["ssd_conv_b"], p["ssd_a_log"], p["ssd_dt_bias"],
                      p["ssd_d"], p["ssd_norm"])
    w_router, b_router = _router_params(p["moe_w_grp"], p["moe_b_grp"], p["moe_w_rt"], p["moe_b_rt"])
    x1, h2, route, counts = _out_call(x, y_s5, y_gdn, y_ssd, mod_l, p["norm_ffn"], big["w_out"],
                                      w_router, b_router, l)
    n_used, blk_e, n_valid, dest0, dest1, n_rows = _dispatch(route.reshape(N, LANE), counts[0, 0:N_EXPERTS], N,
                                                             min(MOE_TOKEN_TILE, S))
    xs = _scatter_call(h2, dest0, dest1, n_rows)
    ys = _expert_call(xs, n_used, blk_e, n_valid, big["moe_w_gate"], big["moe_w_up"], big["moe_w_down"], l)
    return _combine_call(x1, ys, route, dest0, dest1, mod_l, norm_final, final)


def kernel(x, c, w_ada, b_ada, norm_mix, norm_ffn, w_in, w_out, s5_a_re, s5_a_im, s5_b_re, s5_b_im, s5_c_re, s5_c_im, s5_d, s5_log_dt, s5_w_glu, s5_norm, gdn_conv_w, gdn_a_log, gdn_dt_bias, gdn_norm, ssd_conv_w, ssd_conv_b, ssd_a_log, ssd_dt_bias, ssd_d, ssd_norm, moe_w_grp, moe_b_grp, moe_w_rt, moe_b_rt, moe_w_gate, moe_w_up, moe_w_down, norm_final):
    stacked = dict(norm_mix=norm_mix, norm_ffn=norm_ffn, s5_a_re=s5_a_re, s5_a_im=s5_a_im,
                   s5_b_re=s5_b_re, s5_b_im=s5_b_im, s5_c_re=s5_c_re, s5_c_im=s5_c_im, s5_d=s5_d,
                   s5_log_dt=s5_log_dt, s5_w_glu=s5_w_glu, s5_norm=s5_norm, gdn_conv_w=gdn_conv_w,
                   gdn_a_log=gdn_a_log, gdn_dt_bias=gdn_dt_bias, gdn_norm=gdn_norm, ssd_conv_w=ssd_conv_w,
                   ssd_conv_b=ssd_conv_b, ssd_a_log=ssd_a_log, ssd_dt_bias=ssd_dt_bias, ssd_d=ssd_d,
                   ssd_norm=ssd_norm, moe_w_grp=moe_w_grp, moe_b_grp=moe_b_grp, moe_w_rt=moe_w_rt,
                   moe_b_rt=moe_b_rt)
    big = dict(w_in=_w_in_prep_call(w_in), w_out=w_out, moe_w_gate=moe_w_gate, moe_w_up=moe_w_up,
               moe_w_down=moe_w_down)
    L = w_in.shape[0]
    B, S, D = x.shape
    mod = _mod_call(c, w_ada, b_ada).reshape(L, B, 6, D)
    for l in range(L):
        p = {k: v[l] for k, v in stacked.items()}
        x = _layer(x, mod[l], p, big, l, l == L - 1, norm_final)
    return x
```

```python
import functools

import numpy as np
import jax
import jax.numpy as jnp
from jax import lax
from jax.experimental import pallas as pl
from jax.experimental.pallas import tpu as pltpu

F32 = jnp.float32
BF16 = jnp.bfloat16

D_MODEL = 1024
DEPTH = 4
EPS = 1e-6
CONV_WIDTH = 4
CHUNK = 64
S5_WIDTH = 256
S5_CH = 16
S5_GROUPS = 16
S5_STATE = 64
S5_LANES = S5_GROUPS * S5_STATE
GDN_WIDTH = 384
GDN_HEAD_DIM = 64
GDN_HEADS = 6
GDN_CONV_DIM = 3 * GDN_WIDTH
SSD_WIDTH = 384
SSD_HEAD_DIM = 64
SSD_HEADS = 6
SSD_GROUPS = 2
SSD_STATE = 128
SSD_GROUP_WIDTH = SSD_WIDTH // SSD_GROUPS
SSD_CONV_DIM = SSD_WIDTH + 2 * SSD_GROUPS * SSD_STATE
PROJ_SIZES = (S5_WIDTH, GDN_CONV_DIM, GDN_WIDTH, GDN_HEADS, GDN_HEADS, SSD_WIDTH, SSD_CONV_DIM, SSD_HEADS)
MOE_GROUPS = 4
EXPERTS_PER_GROUP = 8
N_EXPERTS = 32
TOP_K = 2
D_EXPERT = 256

LANE = 128
SUBLANE = 8
SMALL_W = LANE
A_LANE, B_LANE, DT_LANE = 0, GDN_HEADS, 2 * GDN_HEADS
GRP_LANE = N_EXPERTS
MOE_ROWS = 512
MOE_TOKEN_TILE = 256
GDN_PAIRS_PER_ITER = 4
SSD_CHUNKS_PER_ITER = 2
VMEM_LIMIT = 56 * 1024 * 1024
NEG_BIG = -1e30


def _cparams(sem):
    return pltpu.CompilerParams(dimension_semantics=sem, vmem_limit_bytes=VMEM_LIMIT)


def _split(a):
    hi = a.astype(BF16)
    lo = (a - hi.astype(F32)).astype(BF16)
    return hi, lo


_NN = (((1,), (0,)), ((), ()))
_NT = (((1,), (1,)), ((), ()))
_TN = (((0,), (0,)), ((), ()))


def _dot(a, b, dims=_NN):
    return lax.dot_general(a.astype(BF16), b.astype(BF16), dims, preferred_element_type=F32)


def _dot2(a, b_bf16):
    hi, lo = _split(a)
    return (lax.dot_general(hi, b_bf16, _NN, preferred_element_type=F32)
            + lax.dot_general(lo, b_bf16, _NN, preferred_element_type=F32))


def _dot_sel3(a, b_bf16):
    h1 = a.astype(BF16)
    r1 = a - h1.astype(F32)
    h2 = r1.astype(BF16)
    h3 = (r1 - h2.astype(F32)).astype(BF16)
    d = functools.partial(lax.dot_general, dimension_numbers=_NN, preferred_element_type=F32)
    return d(h1, b_bf16) + (d(h2, b_bf16) + d(h3, b_bf16))


def _dot3(a, b):
    ah, al = _split(a)
    bh, bl = _split(b)
    d = functools.partial(lax.dot_general, dimension_numbers=_NN, preferred_element_type=F32)
    return d(ah, bh) + (d(ah, bl) + d(al, bh))


def _silu(x):
    return x * jax.nn.sigmoid(x)


def _softplus(x):
    return jnp.maximum(x, 0.0) + jnp.log(1.0 + jnp.exp(-jnp.abs(x)))


def _norm_mod(x, w, scale, shift):
    ms = jnp.mean(x * x, axis=-1, keepdims=True)
    return (x * lax.rsqrt(ms + EPS) * w) * (1.0 + scale) + shift


def _mod_kernel(c_ref, w_ref, b_ref, o_ref):
    cond = _silu(c_ref[...])
    o_ref[0] = _dot3(cond, w_ref[0]) + b_ref[0]


def _mod_call(c, w_ada, b_ada):
    L, D, W = w_ada.shape
    B = c.shape[0]
    tn = 1536
    return pl.pallas_call(
        _mod_kernel,
        out_shape=jax.ShapeDtypeStruct((L, B, W), F32),
        grid=(L, W // tn),
        in_specs=[pl.BlockSpec((B, D), lambda l, j: (0, 0)),
                  pl.BlockSpec((1, D, tn), lambda l, j: (l, 0, j)),
                  pl.BlockSpec((1, 1, tn), lambda l, j: (l, 0, j))],
        out_specs=pl.BlockSpec((1, B, tn), lambda l, j: (l, 0, j)),
        compiler_params=_cparams(("arbitrary", "arbitrary")),
        name="adaln_mod",
    )(c, w_ada, b_ada.reshape(L, 1, W))


PROJ_OUT_W = (S5_WIDTH, GDN_CONV_DIM, GDN_WIDTH, SSD_WIDTH, SSD_CONV_DIM, SMALL_W)


def _proj_kernel(x_ref, mod_ref, nw_ref, w_ref, *o_refs):
    x = x_ref[0]
    h = _norm_mod(x, nw_ref[...], mod_ref[0, 1:2, :], mod_ref[0, 0:1, :]).astype(BF16)
    proj = jnp.dot(h, w_ref[0], preferred_element_type=F32)
    off = 0
    for o_ref in o_refs:
        n = o_ref.shape[-1]
        o_ref[0] = proj[:, off:off + n]
        off += n


_W_HEAD = S5_WIDTH + GDN_CONV_DIM + GDN_WIDTH
_W_AB = _W_HEAD + 2 * GDN_HEADS
_W_SSD = _W_AB + SSD_WIDTH + SSD_CONV_DIM
_W_END = _W_SSD + SSD_HEADS
PROJ_W = sum(PROJ_OUT_W)
W_PREP_ROWS = 256


def _w_in_prep_kernel(w_ref, o_ref):
    n_ssd = _W_SSD - _W_AB
    for r in range(0, w_ref.shape[1], W_PREP_ROWS):
        w = w_ref[0, r:r + W_PREP_ROWS, :]
        rows = slice(r, r + W_PREP_ROWS)
        o_ref[0, rows, 0:_W_HEAD] = w[:, 0:_W_HEAD].astype(BF16)
        o_ref[0, rows, _W_HEAD:_W_HEAD + n_ssd] = w[:, _W_AB:_W_SSD].astype(BF16)
        small = jnp.concatenate([w[:, _W_HEAD:_W_AB], w[:, _W_SSD:_W_END],
                                 jnp.zeros((W_PREP_ROWS, SMALL_W - (_W_AB - _W_HEAD) - (_W_END - _W_SSD)), F32)], axis=1)
        o_ref[0, rows, _W_HEAD + n_ssd:PROJ_W] = small.astype(BF16)


def _w_in_prep_call(w_in):
    L, D, W = w_in.shape
    return pl.pallas_call(
        _w_in_prep_kernel,
        out_shape=jax.ShapeDtypeStruct((L, D, PROJ_W), BF16),
        grid=(L,),
        in_specs=[pl.BlockSpec((1, D, W), lambda l: (l, 0, 0))],
        out_specs=pl.BlockSpec((1, D, PROJ_W), lambda l: (l, 0, 0)),
        compiler_params=_cparams(("arbitrary",)),
        name="w_in_prep",
    )(w_in)


def _proj_call(x, mod_l, norm_w, w_arr, l):
    B, S, D = x.shape
    tm = min(512, S)
    out_shape = [jax.ShapeDtypeStruct((B, S, n), F32) for n in PROJ_OUT_W]
    out_specs = [pl.BlockSpec((1, tm, n), lambda b, i: (b, i, 0)) for n in PROJ_OUT_W]
    return pl.pallas_call(
        _proj_kernel,
        out_shape=out_shape,
        grid=(B, S // tm),
        in_specs=[pl.BlockSpec((1, tm, D), lambda b, i: (b, i, 0)),
                  pl.BlockSpec((1, 6, D), lambda b, i: (b, 0, 0)),
                  pl.BlockSpec((1, D), lambda b, i: (0, 0)),
                  pl.BlockSpec((1, D, PROJ_W), lambda b, i: (l, 0, 0))],
        out_specs=out_specs,
        compiler_params=_cparams(("arbitrary", "arbitrary")),
        name="norm_in_proj",
    )(x, mod_l, norm_w.reshape(1, D), w_arr)


def _s5_kernel(u_ref, bm_ref, cm_ref, lam_ref, dsk_ref, glu_ref, nw_ref, perm_ref, o_ref,
               x_s, st_s, ubt_s, u_s, *, T):
    B = u_ref.shape[0]
    P = S5_LANES

    @pl.when(pl.program_id(0) == 0)
    def _():
        st_s[...] = jnp.zeros_like(st_s)

    nw = S5_WIDTH // LANE
    ubt = u_ref[...].reshape(B * T, S5_WIDTH)
    for j in range(nw):
        ubt_s[j] = ubt[:, j * LANE:(j + 1) * LANE]

    def regroup(t, carry):
        for j in range(nw):
            u_s[pl.ds(pl.multiple_of(t * B, B), B), j * LANE:(j + 1) * LANE] = ubt_s[j, pl.ds(t, B, stride=T), :]
        return carry

    lax.fori_loop(0, T, regroup, 0, unroll=8)
    u = u_s[...]
    x_s[...] = jnp.dot(u.astype(BF16), bm_ref[...], preferred_element_type=F32)
    lr = jnp.broadcast_to(lam_ref[0:1, :], (B, P))
    li = jnp.broadcast_to(lam_ref[1:2, :], (B, P))

    def step(t, carry):
        sr, si = carry
        rows = pl.ds(pl.multiple_of(t * B, B), B)
        nr = lr * sr - li * si + x_s[rows, 0:P]
        ni = lr * si + li * sr + x_s[rows, P:2 * P]
        x_s[rows, 0:P] = nr
        x_s[rows, P:2 * P] = ni
        return nr, ni

    sr, si = lax.fori_loop(0, T, step, (st_s[:, 0:P], st_s[:, P:2 * P]), unroll=4)
    st_s[:, 0:P] = sr
    st_s[:, P:2 * P] = si

    y = jnp.dot(x_s[...].astype(BF16), cm_ref[...], preferred_element_type=F32) + dsk_ref[...] * u
    y = jax.nn.gelu(y)
    y = y * jax.nn.sigmoid(jnp.dot(y.astype(BF16), glu_ref[...], preferred_element_type=F32))
    ms = jnp.mean(y * y, axis=-1, keepdims=True)
    y = (y * lax.rsqrt(ms + EPS) * nw_ref[...]).astype(BF16)
    y = jnp.dot(perm_ref[...], y, preferred_element_type=F32)
    o_ref[...] = y.reshape(B, T, S5_WIDTH).astype(o_ref.dtype)


def _s5_params(a_re, a_im, b_re, b_im, c_re, c_im, log_dt):
    G, P, CH = S5_GROUPS, S5_STATE, S5_CH
    lam = lax.complex(a_re.astype(F32), a_im.astype(F32))
    step = jnp.exp(log_dt.astype(F32))[:, None]
    lam_bar = jnp.exp(lam * step)
    b_bar = ((lam_bar - 1.0) / lam)[..., None] * lax.complex(b_re.astype(F32), b_im.astype(F32))
    eye = jnp.eye(G, dtype=F32)
    bre = jnp.einsum('gpc,gh->gchp', b_bar.real, eye).reshape(G * CH, G * P)
    bim = jnp.einsum('gpc,gh->gchp', b_bar.imag, eye).reshape(G * CH, G * P)
    bm = jnp.concatenate([bre, bim], axis=1).astype(BF16)
    cre = jnp.einsum('gcp,gh->gphc', c_re.astype(F32), eye).reshape(G * P, G * CH)
    cim = jnp.einsum('gcp,gh->gphc', c_im.astype(F32), eye).reshape(G * P, G * CH)
    cm = jnp.concatenate([cre, -cim], axis=0).astype(BF16)
    lam_rows = jnp.zeros((SUBLANE, G * P), F32)
    lam_rows = lam_rows.at[0].set(lam_bar.real.reshape(-1)).at[1].set(lam_bar.imag.reshape(-1))
    return bm, cm, lam_rows


def _s5_call(u, bm, cm, lam_rows, d_skip, w_glu, norm_w):
    B, S, W = u.shape
    T = min(128, S)
    P2 = 2 * S5_LANES
    const = lambda shape: pl.BlockSpec(shape, lambda i: tuple(0 for _ in shape))
    r = np.arange(B * T)
    perm = np.zeros((B * T, B * T), np.float32)
    perm[r, (r % T) * B + r // T] = 1.0
    return pl.pallas_call(
        functools.partial(_s5_kernel, T=T),
        out_shape=jax.ShapeDtypeStruct((B, S, W), BF16),
        grid=(S // T,),
        in_specs=[pl.BlockSpec((B, T, W), lambda i: (0, i, 0)),
                  const((W, P2)), const((P2, W)), const((SUBLANE, S5_LANES)),
                  const((1, W)), const((W, W)), const((1, W)), const((B * T, B * T))],
        out_specs=pl.BlockSpec((B, T, W), lambda i: (0, i, 0)),
        scratch_shapes=[pltpu.VMEM((B * T, P2), F32), pltpu.VMEM((B, P2), F32),
                        pltpu.VMEM((W // LANE, B * T, LANE), F32), pltpu.VMEM((B * T, W), F32)],
        compiler_params=_cparams(("arbitrary",)),
        name="s5_mixer",
    )(u, bm, cm, lam_rows, d_skip.reshape(1, W).astype(F32), w_glu.astype(BF16), norm_w.reshape(1, W).astype(F32),
      jnp.asarray(perm, BF16))


def _causal_conv_silu(x, xf_ref, cw_ref, bias):
    n = x.shape[0]
    xf_ref[SUBLANE:, :] = x
    acc = x * cw_ref[CONV_WIDTH - 1:CONV_WIDTH, :]
    for k in range(1, CONV_WIDTH):
        acc = acc + xf_ref[SUBLANE - k:SUBLANE - k + n, :] * cw_ref[CONV_WIDTH - 1 - k:CONV_WIDTH - k, :]
    xf_ref[0:SUBLANE, :] = x[n - SUBLANE:, :]
    if bias is not None:
        acc = acc + bias
    return _silu(acc)


U32 = jnp.uint32
PACK_ROWS = D_MODEL // (2 * LANE)


def _pack_rows(x):
    h = x.shape[1] // 2
    xb = x.astype(BF16).astype(F32)
    lo = lax.bitcast_convert_type(xb[:, 0:h], U32) >> 16
    hi = lax.bitcast_convert_type(xb[:, h:2 * h], U32) & U32(0xFFFF0000)
    return lo | hi


def _unpack_rows(w):
    lo = lax.bitcast_convert_type(w << 16, F32)
    hi = lax.bitcast_convert_type(w & U32(0xFFFF0000), F32)
    return jnp.concatenate([lo, hi], axis=1)


def _store_row_tiles(ref, val):
    n = val.shape[0]
    for j in range(PACK_ROWS):
        ref[pl.ds(j, n, stride=PACK_ROWS), :] = val[:, j * LANE:(j + 1) * LANE]


def _load_row_tiles(ref, n):
    return jnp.concatenate([ref[pl.ds(j, n, stride=PACK_ROWS), :] for j in range(PACK_ROWS)], axis=1)


def _lanes_from(sm, off):
    return pltpu.roll(sm, SMALL_W - off, axis=1) if off else sm


def _chunk_scans(g):
    n = g.shape[0]
    rin = lax.broadcasted_iota(jnp.int32, g.shape, 0) & (CHUNK - 1)
    pre = g
    suf = jnp.where(rin < CHUNK - 1, pltpu.roll(g, n - 1, axis=0), 0.0)
    s = 1
    while s < CHUNK:
        pre = pre + jnp.where(rin >= s, pltpu.roll(pre, s, axis=0), 0.0)
        suf = suf + jnp.where(rin + s <= CHUNK - 1, pltpu.roll(suf, n - s, axis=0), 0.0)
        s *= 2
    return pre, suf


def _tri_masks():
    r = lax.broadcasted_iota(jnp.int32, (CHUNK, CHUNK), 0)
    c = lax.broadcasted_iota(jnp.int32, (CHUNK, CHUNK), 1)
    return r >= c, r > c, r == c


def _head_expand():
    m = np.zeros((SMALL_W, GDN_WIDTH), np.float32)
    for h in range(GDN_HEADS):
        m[h, h * GDN_HEAD_DIM:(h + 1) * GDN_HEAD_DIM] = 1.0
    return jnp.asarray(m, BF16)


def _block_ones(width, blk):
    idx = np.arange(width) // blk
    return jnp.asarray((idx[:, None] == idx[None, :]).astype(np.float32), BF16)


def _gdn_kernel(qkv_ref, z_ref, sm_ref, cw_ref, hp_ref, nw_ref, e_ref, xp_ref, bdm_ref, o_ref,
                tail_s, st_s, kn_s, kb_s, qn_s, qd_s, kd_s, u_s, w_s, eg_s, gcx_s, gw_s,
                o_s, au_s, qe_s, ku_s, kw_s, *, TB):
    H, Dh, C = GDN_HEADS, GDN_HEAD_DIM, CHUNK
    W = GDN_WIDTH
    ncb = TB // C

    @pl.when(pl.program_id(1) == 0)
    def _():
        tail_s[0:SUBLANE, :] = jnp.zeros((SUBLANE, tail_s.shape[1]), F32)
        st_s[...] = jnp.zeros_like(st_s)

    xc = _causal_conv_silu(qkv_ref[0], tail_s, cw_ref, None)
    q, k, v = xc[:, 0:W], xc[:, W:2 * W], xc[:, 2 * W:3 * W]
    e = e_ref[...]
    xp = xp_ref[...]
    qn = q * lax.rsqrt(_dot2(q * q, e) + EPS) * (Dh ** -0.5)
    kn = k * lax.rsqrt(_dot2(k * k, e) + EPS)

    sm = sm_ref[0]
    lane = lax.broadcasted_iota(jnp.int32, sm.shape, 1)
    head_lane = lane < H
    g = jnp.where(head_lane, hp_ref[0:1, :] * _softplus(_lanes_from(sm, A_LANE) + hp_ref[1:2, :]), 0.0)
    beta = jnp.where(head_lane, jax.nn.sigmoid(_lanes_from(sm, B_LANE)), 0.0)
    gc, rc = _chunk_scans(g)
    bx = _dot2(beta, xp)
    egx = _dot2(jnp.exp(gc), xp)
    erx = _dot2(jnp.exp(rc), xp)
    kb = kn * bx
    kn_s[...] = kn
    kb_s[...] = kb
    qn_s[...] = qn
    qd_s[...] = qn * egx
    kd_s[...] = kn * erx
    u_s[...] = v * bx
    w_s[...] = kb * egx
    gcx = _dot_sel3(gc, xp)
    gcx_s[...] = gcx
    r384 = lax.broadcasted_iota(jnp.int32, (C, W), 0)
    l384 = lax.broadcasted_iota(jnp.int32, (C, W), 1) & (Dh - 1)
    for c in range(ncb):
        diag_c = jnp.where(r384 == l384, gcx[c * C:(c + 1) * C, :], 0.0)
        gw_s[c] = jnp.broadcast_to(jnp.sum(diag_c, axis=0, keepdims=True), (SUBLANE, W))
        eg_s[c] = jnp.broadcast_to(egx[(c + 1) * C - 1:(c + 1) * C, :], (SUBLANE, W))

    heads = [slice(h * Dh, (h + 1) * Dh) for h in range(H)]

    GL = 4 * Dh
    row_w = lax.broadcasted_iota(jnp.int32, (C, GL), 0)
    col_w = lax.broadcasted_iota(jnp.int32, (C, GL), 1) & (Dh - 1)
    causal_w, strict_w = row_w >= col_w, row_w > col_w
    eye_w = jnp.where(row_w == col_w, 1.0, 0.0).astype(F32)
    bdm = bdm_ref[...]
    mm = functools.partial(lax.dot_general, dimension_numbers=_NN, preferred_element_type=F32)

    def block_diag(m, mask):
        return jnp.concatenate([m] * 4, axis=0) * mask

    npair = min(GDN_PAIRS_PER_ITER, ncb // 2)

    def solve(it, carry):
        chunks = [2 * npair * it + k for k in range(2 * npair)]
        rows = [pl.ds(pl.multiple_of(c * C, C), C) for c in chunks]

        def groups(get, n):
            out = []
            for j in range(npair):
                a0, a1 = get(2 * j), get(2 * j + 1)
                out += [a0[:, 0:4 * n], a1[:, 0:4 * n],
                        jnp.concatenate([a0[:, 4 * n:6 * n], a1[:, 4 * n:6 * n]], axis=1)]
            return out

        def ungroup(ref, vals, n):
            for j in range(npair):
                ref[rows[2 * j], 0:4 * n] = vals[3 * j]
                ref[rows[2 * j + 1], 0:4 * n] = vals[3 * j + 1]
                ref[rows[2 * j], 4 * n:6 * n] = vals[3 * j + 2][:, 0:2 * n]
                ref[rows[2 * j + 1], 4 * n:6 * n] = vals[3 * j + 2][:, 2 * n:4 * n]

        kn_g = groups(lambda k: kn_s[rows[k], :], Dh)
        kb_g = groups(lambda k: kb_s[rows[k], :], Dh)
        qn_g = groups(lambda k: qn_s[rows[k], :], Dh)
        gx_g = groups(lambda k: gcx_s[rows[k], :], Dh)
        gw_g = groups(lambda k: gw_s[chunks[k], 0:1, :], Dh)
        decs = [jnp.exp(jnp.where(causal_w, gx - gw, NEG_BIG)) for gx, gw in zip(gx_g, gw_g)]
        prods = [lax.dot_general(jnp.concatenate([kb, qn], axis=0).astype(BF16), block_diag(kn.astype(BF16), bdm),
                                 _NT, preferred_element_type=F32)
                 for kn, kb, qn in zip(kn_g, kb_g, qn_g)]
        attns = [pr[C:2 * C] * dec for pr, dec in zip(prods, decs)]
        def times(lhs, p=None):
            lh, ll = _split(lhs)
            ph, pl_ = (lh[0:C], ll[0:C]) if p is None else _split(p)
            rh, rl = block_diag(ph, bdm), block_diag(pl_, bdm)
            return mm(lh, rh) + (mm(lh, rl) + mm(ll, rh))

        ps = [-jnp.where(strict_w, pr[0:C] * dec, 0.0) for pr, dec in zip(prods, decs)]
        invs = [eye_w + p for p in ps]
        ps = [times(p) for p in ps]
        for _ in range(4):
            outs = [times(jnp.concatenate([p, inv], axis=0)) for p, inv in zip(ps, invs)]
            invs = [inv + o[C:2 * C] for inv, o in zip(invs, outs)]
            ps = [o[0:C] for o in outs]
        invs = [inv + times(inv, p) for p, inv in zip(ps, invs)]
        def solved(ref):
            outs = []
            for inv, r in zip(invs, groups(lambda k: ref[rows[k], :], Dh)):
                ih, il = _split(inv)
                rh, rl = _split(r)
                rh, rl = block_diag(rh, bdm), block_diag(rl, bdm)
                outs.append(mm(ih, rh) + (mm(ih, rl) + mm(il, rh)))
            return outs

        u_g, w_g = solved(u_s), solved(w_s)
        ungroup(u_s, u_g, Dh)
        ungroup(w_s, w_g, Dh)
        at_g = [a.astype(BF16) for a in attns]
        au_g = [mm(a, block_diag(u.astype(BF16), bdm)) for a, u in zip(at_g, u_g)]
        aw_g = [mm(a, block_diag(w.astype(BF16), bdm)) for a, w in zip(at_g, w_g)]
        qd_g = groups(lambda k: qd_s[rows[k], :], Dh)
        ungroup(au_s, au_g, Dh)
        ungroup(qe_s, [qd - aw for qd, aw in zip(qd_g, aw_g)], Dh)
        for k in range(2 * npair):
            for sl in heads:
                kd_h = kd_s[rows[k], sl]
                ku_s[rows[k], sl] = _dot(kd_h, u_s[rows[k], sl], _TN)
                kw_s[rows[k], sl] = _dot(kd_h, w_s[rows[k], sl], _TN)
        return carry

    lax.fori_loop(0, ncb // (2 * npair), solve, 0)

    def recur(c, carry):
        rows = pl.ds(pl.multiple_of(c * C, C), C)
        st = st_s[...]
        sb = st.astype(BF16)
        lhs = jnp.concatenate([qe_s[rows, :], kw_s[rows, :]], axis=0).astype(BF16)
        out = jnp.concatenate(
            [mm(lhs[:, 0:GL], block_diag(sb[:, 0:GL], bdm)),
             mm(lhs[:, GL:W], jnp.concatenate([sb[:, GL:W]] * 2, axis=0) * bdm[0:W - GL, 0:W - GL])], axis=1)
        o_s[rows, :] = out[0:C] + au_s[rows, :]
        st_s[...] = st * eg_s[c, 0:1, :] + (ku_s[rows, :] - out[C:2 * C])
        return carry

    lax.fori_loop(0, ncb, recur, 0)

    o = o_s[...]
    ms = _dot2(o * o, e) * (1.0 / Dh)
    o = o * lax.rsqrt(ms + EPS) * nw_ref[...]
    o_ref[0] = (o * _silu(z_ref[0])).astype(o_ref.dtype)


def _gdn_call(qkv, z, sm, conv_w, a_log, dt_bias, norm_w):
    B, S, _ = qkv.shape
    TB = min(512, S)
    W, H, Dh = GDN_WIDTH, GDN_HEADS, GDN_HEAD_DIM
    hp = jnp.zeros((SUBLANE, SMALL_W), F32)
    hp = hp.at[0, :H].set(-jnp.exp(a_log.astype(F32))).at[1, :H].set(dt_bias.astype(F32))
    nw = jnp.tile(norm_w.astype(F32), H).reshape(1, W)
    const = lambda shape: pl.BlockSpec(shape, lambda b, i: tuple(0 for _ in shape))
    blk = lambda n: pl.BlockSpec((1, TB, n), lambda b, i: (b, i, 0))
    f = lambda *shape: pltpu.VMEM(shape, F32)
    return pl.pallas_call(
        functools.partial(_gdn_kernel, TB=TB),
        out_shape=jax.ShapeDtypeStruct((B, S, W), BF16),
        grid=(B, S // TB),
        in_specs=[blk(GDN_CONV_DIM), blk(W), blk(SMALL_W),
                  const((CONV_WIDTH, GDN_CONV_DIM)), const((SUBLANE, SMALL_W)), const((1, W)),
                  const((W, W)), const((SMALL_W, W)), const((4 * Dh, 4 * Dh))],
        out_specs=blk(W),
        scratch_shapes=[f(SUBLANE + TB, GDN_CONV_DIM), f(Dh, W)]
        + [f(TB, W)] * 7 + [f(TB // CHUNK, SUBLANE, W), f(TB, W), f(TB // CHUNK, SUBLANE, W)] + [f(TB, W)] * 5,
        compiler_params=_cparams(("arbitrary", "arbitrary")),
        name="gdn_mixer",
    )(qkv, z, sm, conv_w.astype(F32), hp, nw, _block_ones(W, Dh), _head_expand(), _block_ones(4 * Dh, Dh))


def _ssd_kernel(xbc_ref, z_ref, sm_ref, cw_ref, cb_ref, hp_ref, dsk_ref, nw_ref, e_ref, xp_ref, o_ref,
                tail_s, st_s, xs_s, xdt_s, xdd_s, bm_s, cm_s, ea_s, el_s, ac_s, act_s, y_s, inc_s, *, TB):
    H, P, G, N, C = SSD_HEADS, SSD_HEAD_DIM, SSD_GROUPS, SSD_STATE, CHUNK
    R = H // G
    W, GW = SSD_WIDTH, SSD_GROUP_WIDTH
    ncb = TB // C

    @pl.when(pl.program_id(1) == 0)
    def _():
        tail_s[0:SUBLANE, :] = jnp.zeros((SUBLANE, tail_s.shape[1]), F32)
        st_s[...] = jnp.zeros_like(st_s)

    xc = _causal_conv_silu(xbc_ref[0], tail_s, cw_ref, cb_ref[...])
    xs = xc[:, 0:W]
    xp = xp_ref[...]

    sm = sm_ref[0]
    lane = lax.broadcasted_iota(jnp.int32, sm.shape, 1)
    head_lane = lane < H
    dt = jnp.where(head_lane, _softplus(_lanes_from(sm, DT_LANE) + hp_ref[1:2, :]), 0.0)
    acs, rcs = _chunk_scans(dt * hp_ref[0:1, :])
    dtx = _dot2(dt, xp)
    eax = _dot2(jnp.exp(acs), xp)
    erx = _dot2(jnp.exp(rcs), xp)
    xdt = xs * dtx
    xs_s[...] = xs
    xdt_s[...] = xdt
    xdd_s[...] = xdt * erx
    bm_s[...] = xc[:, W:W + G * N]
    cm_s[...] = xc[:, W + G * N:W + 2 * G * N]
    ea_s[...] = eax
    ac_s[...] = acs
    for c in range(ncb):
        act_s[c] = acs[c * C:(c + 1) * C, :].T
        el_s[c] = jnp.broadcast_to(eax[(c + 1) * C - 1:(c + 1) * C, :], (SUBLANE, W))

    causal, _, _ = _tri_masks()

    groups = [(gi, slice(gi * GW, (gi + 1) * GW), slice(gi * N, (gi + 1) * N)) for gi in range(G)]

    def local(it, carry):
        work = []
        for cc in range(SSD_CHUNKS_PER_ITER):
            c = it * SSD_CHUNKS_PER_ITER + cc
            rows = pl.ds(pl.multiple_of(c * C, C), C)
            work += [(c, rows, ac_s[rows, :], act_s[c], g) for g in groups]
        cbs = [_dot(cm_s[rows, nsl], bm_s[rows, nsl], _NT) for c, rows, _, _, (gi, gsl, nsl) in work]
        for (c, rows, acol, arow, (gi, gsl, nsl)), cb in zip(work, cbs):
            for r in range(R):
                h = gi * R + r
                sl = slice(h * P, (h + 1) * P)
                seg = jnp.exp(jnp.where(causal, acol[:, h:h + 1] - arow[h:h + 1, :], NEG_BIG))
                y_s[rows, sl] = _dot(cb * seg, xdt_s[rows, sl])
        for c, rows, _, _, (gi, gsl, nsl) in work:
            inc_s[c, gi] = _dot(bm_s[rows, nsl], xdd_s[rows, gsl], _TN)
        return carry

    lax.fori_loop(0, ncb // SSD_CHUNKS_PER_ITER, local, 0)

    def recur(c, carry):
        rows = pl.ds(pl.multiple_of(c * C, C), C)
        sts = [st_s[gi] for gi, _, _ in groups]
        for (gi, gsl, nsl), st in zip(groups, sts):
            y_s[rows, gsl] = y_s[rows, gsl] + _dot(cm_s[rows, nsl], st) * ea_s[rows, gsl]
        for (gi, gsl, nsl), st in zip(groups, sts):
            st_s[gi] = st * el_s[c, 0:1, gsl] + inc_s[c, gi]
        return carry

    lax.fori_loop(0, ncb, recur, 0)

    y = y_s[...] + dsk_ref[...] * xs_s[...]
    y = y * _silu(z_ref[0])
    ms = _dot2(y * y, e_ref[...]) * (1.0 / GW)
    o_ref[0] = (y * lax.rsqrt(ms + EPS) * nw_ref[...]).astype(o_ref.dtype)


def _ssd_call(xbc, z, sm, conv_w, conv_b, a_log, dt_bias, d_skip, norm_w):
    B, S, _ = xbc.shape
    TB = min(512, S)
    W, H, G, N = SSD_WIDTH, SSD_HEADS, SSD_GROUPS, SSD_STATE
    hp = jnp.zeros((SUBLANE, SMALL_W), F32)
    hp = hp.at[0, :H].set(-jnp.exp(a_log.astype(F32))).at[1, :H].set(dt_bias.astype(F32))
    dsk = jnp.repeat(d_skip.astype(F32), SSD_HEAD_DIM).reshape(1, W)
    const = lambda shape: pl.BlockSpec(shape, lambda b, i: tuple(0 for _ in shape))
    blk = lambda n: pl.BlockSpec((1, TB, n), lambda b, i: (b, i, 0))
    f = lambda *shape: pltpu.VMEM(shape, F32)
    return pl.pallas_call(
        functools.partial(_ssd_kernel, TB=TB),
        out_shape=jax.ShapeDtypeStruct((B, S, W), BF16),
        grid=(B, S // TB),
        in_specs=[blk(SSD_CONV_DIM), blk(W), blk(SMALL_W),
                  const((CONV_WIDTH, SSD_CONV_DIM)), const((1, SSD_CONV_DIM)), const((SUBLANE, SMALL_W)),
                  const((1, W)), const((1, W)), const((W, W)), const((SMALL_W, W))],
        out_specs=blk(W),
        scratch_shapes=[f(SUBLANE + TB, SSD_CONV_DIM), f(G, N, SSD_GROUP_WIDTH),
                        f(TB, W), f(TB, W), f(TB, W), f(TB, G * N), f(TB, G * N), f(TB, W),
                        f(TB // CHUNK, SUBLANE, W),
                        f(TB, SMALL_W), f(TB // CHUNK, SMALL_W, CHUNK), f(TB, W),
                        f(TB // CHUNK, G, N, SSD_GROUP_WIDTH)],
        compiler_params=_cparams(("arbitrary", "arbitrary")),
        name="ssd_mixer",
    )(xbc, z, sm, conv_w.astype(F32), conv_b.reshape(1, -1).astype(F32), hp, dsk,
      norm_w.reshape(1, W).astype(F32), _block_ones(W, SSD_GROUP_WIDTH), _head_expand())


def _out_kernel(x_ref, y1_ref, y2_ref, y3_ref, mod_ref, nw_ref, wo_ref, wr_ref, br_ref, tri_ref,
                x1_ref, h_ref, rt_ref, cnt_ref, run_s, wo_s):
    first = (pl.program_id(0) == 0) & (pl.program_id(1) == 0)

    @pl.when(first)
    def _():
        wo_s[...] = wo_ref[0].astype(BF16)

    y = (jnp.dot(y1_ref[0], wo_s[0:S5_WIDTH, :], preferred_element_type=F32)
         + jnp.dot(y2_ref[0], wo_s[S5_WIDTH:S5_WIDTH + GDN_WIDTH, :], preferred_element_type=F32)
         + jnp.dot(y3_ref[0], wo_s[S5_WIDTH + GDN_WIDTH:, :], preferred_element_type=F32))
    x1 = x_ref[0] + mod_ref[0, 2:3, :] * y
    x1_ref[0] = x1
    h = _norm_mod(x1, nw_ref[...], mod_ref[0, 4:5, :], mod_ref[0, 3:4, :])
    _store_row_tiles(h_ref, _pack_rows(h))
    lg =jnp.dot(h.astype(BF16), wr_ref[...], preferred_element_type=F32) + br_ref[...]

    lane = lax.broadcasted_iota(jnp.int32, lg.shape, 1)
    lanef = lane.astype(F32)
    big = float(4 * LANE)
    grp = (lane >= GRP_LANE) & (lane < GRP_LANE + MOE_GROUPS)
    lgm = jnp.where(grp, lg, -jnp.inf)
    m = jnp.max(lgm, axis=-1, keepdims=True)
    gidx = jnp.min(jnp.where(lgm == m, lanef - GRP_LANE, big), axis=-1, keepdims=True)
    g_w = 1.0 / jnp.sum(jnp.where(grp, jnp.exp(lg - m), 0.0), axis=-1, keepdims=True)
    in_grp = (lane < N_EXPERTS) & ((lane // EXPERTS_PER_GROUP).astype(F32) == gidx)
    le = jnp.where(in_grp, lg, -jnp.inf)
    v1 = jnp.max(le, axis=-1, keepdims=True)
    i1 = jnp.min(jnp.where(le == v1, lanef, big), axis=-1, keepdims=True)
    le2 = jnp.where(lanef == i1, -jnp.inf, le)
    v2 = jnp.max(le2, axis=-1, keepdims=True)
    i2 = jnp.min(jnp.where(le2 == v2, lanef, big), axis=-1, keepdims=True)
    e2 = jnp.exp(v2 - v1)
    w1 = g_w / (1.0 + e2)
    w2 = g_w * e2 / (1.0 + e2)

    @pl.when(first)
    def _():
        run_s[...] = jnp.zeros_like(run_s)

    chosen = jnp.where((lanef == i1) | (lanef == i2), 1.0, 0.0)
    before = jnp.dot(tri_ref[...], chosen.astype(BF16), preferred_element_type=F32) + run_s[0:1, :]
    p1 = jnp.sum(jnp.where(lanef == i1, before, 0.0), axis=-1, keepdims=True)
    p2 = jnp.sum(jnp.where(lanef == i2, before, 0.0), axis=-1, keepdims=True)
    run_s[...] = run_s[...] + jnp.sum(chosen, axis=0, keepdims=True)
    cnt_ref[...] = run_s[...]

    rt = jnp.zeros_like(lg)
    for k, val in enumerate((i1, i2, w1, w2, p1, p2)):
        rt = jnp.where(lane == k, val, rt)
    rt_ref[0] = rt


def _out_call(x, y1, y2, y3, mod_l, norm_w, w_out, w_router, b_router, l):
    B, S, D = x.shape
    tm = min(512, S)
    blk = lambda n: pl.BlockSpec((1, tm, n), lambda b, i: (b, i, 0))
    const = lambda shape: pl.BlockSpec(shape, lambda b, i: tuple(0 for _ in shape))
    tri = jnp.asarray(np.tril(np.ones((tm, tm), np.float32), -1), BF16)
    nb = S // tm
    tiles = PACK_ROWS
    return pl.pallas_call(
        _out_kernel,
        out_shape=[jax.ShapeDtypeStruct((B, S, D), F32), jax.ShapeDtypeStruct((B * S * tiles, LANE), U32),
                   jax.ShapeDtypeStruct((B, S, LANE), F32), jax.ShapeDtypeStruct((SUBLANE, LANE), F32)],
        grid=(B, nb),
        in_specs=[blk(D), blk(S5_WIDTH), blk(GDN_WIDTH), blk(SSD_WIDTH),
                  pl.BlockSpec((1, 6, D), lambda b, i: (b, 0, 0)),
                  const((1, D)), pl.BlockSpec((1, D, D), lambda b, i: (l, 0, 0)),
                  const((D, LANE)), const((1, LANE)), const((tm, tm))],
        out_specs=[blk(D), pl.BlockSpec((tm * tiles, LANE), lambda b, i: (b * nb + i, 0)),
                   blk(LANE), const((SUBLANE, LANE))],
        scratch_shapes=[pltpu.VMEM((SUBLANE, LANE), F32), pltpu.VMEM((D, D), BF16)],
        compiler_params=_cparams(("arbitrary", "arbitrary")),
        name="out_proj_router",
    )(x, y1, y2, y3, mod_l, norm_w.reshape(1, D), w_out, w_router, b_router, tri)


def _router_params(w_grp, b_grp, w_rt, b_rt):
    D = w_grp.shape[0]
    w = jnp.zeros((D, LANE), F32).at[:, 0:N_EXPERTS].set(w_rt).at[:, GRP_LANE:GRP_LANE + MOE_GROUPS].set(w_grp)
    b = jnp.zeros((1, LANE), F32).at[0, 0:N_EXPERTS].set(b_rt).at[0, GRP_LANE:GRP_LANE + MOE_GROUPS].set(b_grp)
    return w.astype(BF16), b


def _tile_copy_loop(n, fn):
    def body(t, carry):
        fn(t)
        return carry
    lax.fori_loop(0, n, body, 0, unroll=32)


def _scatter_kernel(d0_ref, d1_ref, h_ref, xs_hbm, stage, sem):
    i = pl.program_id(0)
    n = pl.num_programs(0)
    slot = lax.rem(i, 2)
    tm = d0_ref.shape[-1]

    def copy(s, t, dst):
        return pltpu.make_async_copy(stage.at[s, pl.ds(t * PACK_ROWS, PACK_ROWS)],
                                     xs_hbm.at[pl.ds(dst * PACK_ROWS, PACK_ROWS)], sem.at[s])

    def wait_slot(s):
        _tile_copy_loop(TOP_K * tm, lambda t: copy(s, 0, 0).wait())

    @pl.when(i >= 2)
    def _():
        wait_slot(slot)

    stage[slot] = h_ref[...]

    def start(t):
        copy(slot, t, d0_ref[0, 0, t]).start(priority=0)
        copy(slot, t, d1_ref[0, 0, t]).start(priority=1)
    _tile_copy_loop(tm, start)

    @pl.when(i == n - 1)
    def _():
        @pl.when(i >= 1)
        def _():
            wait_slot(1 - slot)
        wait_slot(slot)


def _scatter_call(h2t, dest0, dest1, n_rows):
    n_tiles, _, tm = dest0.shape
    idx_blk = pl.BlockSpec((1, 1, tm), lambda i: (i, 0, 0), memory_space=pltpu.SMEM)
    return pl.pallas_call(
        _scatter_kernel,
        out_shape=jax.ShapeDtypeStruct((n_rows * PACK_ROWS, LANE), U32),
        grid=(n_tiles,),
        in_specs=[idx_blk, idx_blk, pl.BlockSpec((tm * PACK_ROWS, LANE), lambda i: (i, 0))],
        out_specs=pl.BlockSpec(memory_space=pl.ANY),
        scratch_shapes=[pltpu.VMEM((2, tm * PACK_ROWS, LANE), U32), pltpu.SemaphoreType.DMA((2,))],
        compiler_params=_cparams(("arbitrary",)),
        name="moe_scatter",
    )(dest0, dest1, h2t)


def _expert_kernel(nused_ref, blke_ref, nvalid_ref, xs_ref, wg_ref, wu_ref, wd_ref, ys_ref, wg_s, wu_s, wd_s):
    i = pl.program_id(0)

    @pl.when(i < nused_ref[0])
    def _():
        @pl.when((i == 0) | (blke_ref[i] != blke_ref[jnp.maximum(i - 1, 0)]))
        def _():
            wg_s[...] = wg_ref[0, 0].astype(BF16)
            wu_s[...] = wu_ref[0, 0].astype(BF16)
            wd_s[...] = wd_ref[0, 0].astype(BF16)

        row = lax.broadcasted_iota(jnp.int32, (MOE_ROWS, 1), 0)
        words = jnp.where(row < nvalid_ref[i], _load_row_tiles(xs_ref, MOE_ROWS), U32(0))
        xb = _unpack_rows(words).astype(BF16)
        hid = _silu(jnp.dot(xb, wg_s[...], preferred_element_type=F32)) * jnp.dot(xb, wu_s[...], preferred_element_type=F32)
        _store_row_tiles(ys_ref, _pack_rows(jnp.dot(hid.astype(BF16), wd_s[...], preferred_element_type=F32)))

    @pl.when(i >= nused_ref[0])
    def _():
        ys_ref[...] = jnp.zeros_like(ys_ref)


def _expert_call(xs, n_used, blk_e, n_valid, w_gate, w_up, w_down, l):
    D = w_gate.shape[2]
    n_blk = blk_e.shape[0]
    rows_blk = pl.BlockSpec((MOE_ROWS * PACK_ROWS, LANE), lambda i, nu, be, nv: (i, 0))
    grid_spec = pltpu.PrefetchScalarGridSpec(
        num_scalar_prefetch=3,
        grid=(n_blk,),
        in_specs=[rows_blk,
                  pl.BlockSpec((1, 1, D, D_EXPERT), lambda i, nu, be, nv: (l, be[i], 0, 0)),
                  pl.BlockSpec((1, 1, D, D_EXPERT), lambda i, nu, be, nv: (l, be[i], 0, 0)),
                  pl.BlockSpec((1, 1, D_EXPERT, D), lambda i, nu, be, nv: (l, be[i], 0, 0))],
        out_specs=rows_blk,
        scratch_shapes=[pltpu.VMEM((D, D_EXPERT), BF16), pltpu.VMEM((D, D_EXPERT), BF16),
                        pltpu.VMEM((D_EXPERT, D), BF16)],
    )
    return pl.pallas_call(
        _expert_kernel,
        out_shape=jax.ShapeDtypeStruct(xs.shape, U32),
        grid_spec=grid_spec,
        compiler_params=_cparams(("arbitrary",)),
        name="expert_mlp",
    )(n_used, blk_e, n_valid, xs, w_gate, w_up, w_down)


def _dest_kernel(rt_ref, ps_ref, o_ref):
    rt = rt_ref[...]
    lanef = lax.broadcasted_iota(jnp.int32, rt.shape, 1).astype(F32)
    d = jnp.zeros_like(rt)
    for k in range(TOP_K):
        start = jnp.sum(jnp.where(lanef == rt[:, k:k + 1], ps_ref[...], 0.0), axis=-1, keepdims=True)
        d = jnp.where(lanef == k, start + rt[:, 4 + k:5 + k], d)
    o_ref[0] = d.T[0:SUBLANE, :].astype(jnp.int32)


def _dest_call(route, pstart_row, tm):
    N = route.shape[0]
    return pl.pallas_call(
        _dest_kernel,
        out_shape=jax.ShapeDtypeStruct((N // tm, SUBLANE, tm), jnp.int32),
        grid=(N // tm,),
        in_specs=[pl.BlockSpec((tm, LANE), lambda i: (i, 0)), pl.BlockSpec((1, LANE), lambda i: (0, 0))],
        out_specs=pl.BlockSpec((1, SUBLANE, tm), lambda i: (i, 0, 0)),
        compiler_params=_cparams(("arbitrary",)),
        name="moe_dest",
    )(route, pstart_row)


def _dispatch(route, counts, N, tm):
    L_pad = N * TOP_K + N_EXPERTS * MOE_ROWS
    n_blk = L_pad // MOE_ROWS
    counts = counts.astype(jnp.int32)
    padded = ((counts + MOE_ROWS - 1) // MOE_ROWS) * MOE_ROWS
    pend = jnp.cumsum(padded)
    pstart = pend - padded
    n_used = (pend[-1] // MOE_ROWS).astype(jnp.int32).reshape(1)
    blk_row0 = jnp.arange(n_blk, dtype=jnp.int32) * MOE_ROWS
    blk_e = jnp.minimum(jnp.sum((pend[None, :] <= blk_row0[:, None]).astype(jnp.int32), axis=1), N_EXPERTS - 1)
    n_valid = jnp.clip((pstart + counts)[blk_e] - blk_row0, 0, MOE_ROWS).astype(jnp.int32)
    pstart_row = jnp.zeros((1, LANE), F32).at[0, 0:N_EXPERTS].set(pstart.astype(F32))
    dest = _dest_call(route, pstart_row, tm)
    return n_used, blk_e.astype(jnp.int32), n_valid, dest[:, 0:1, :], dest[:, 1:2, :], L_pad


def _combine_kernel(d0_ref, d1_ref, d0n_ref, d1n_ref, x_ref, rt_ref, mod_ref, nf_ref, ys_hbm, o_ref,
                    buf, sem, *, final):
    i = pl.program_id(0)
    n = pl.num_programs(0)
    slot = lax.rem(i, 2)
    tm = x_ref.shape[0]

    def copy(s, k, t, src):
        return pltpu.make_async_copy(ys_hbm.at[pl.ds(src * PACK_ROWS, PACK_ROWS)],
                                     buf.at[s, k, pl.ds(t * PACK_ROWS, PACK_ROWS)], sem.at[s])

    def start_tile(s, a_ref, b_ref):
        def start(t):
            copy(s, 0, t, a_ref[0, 0, t]).start(priority=0)
            copy(s, 1, t, b_ref[0, 0, t]).start(priority=1)
        _tile_copy_loop(tm, start)

    @pl.when(i == 0)
    def _():
        start_tile(0, d0_ref, d1_ref)

    @pl.when(i + 1 < n)
    def _():
        start_tile(1 - slot, d0n_ref, d1n_ref)

    _tile_copy_loop(TOP_K * tm, lambda t: copy(slot, 0, 0, 0).wait())
    rt = rt_ref[...]
    y = (rt[:, 2:3] * _unpack_rows(_load_row_tiles(buf.at[slot, 0], tm))
         + rt[:, 3:4] * _unpack_rows(_load_row_tiles(buf.at[slot, 1], tm)))
    x2 = x_ref[...] + mod_ref[0, 5:6, :] * y
    if final:
        ms = jnp.mean(x2 * x2, axis=-1, keepdims=True)
        x2 = x2 * lax.rsqrt(ms + EPS) * nf_ref[...]
    o_ref[...] = x2


def _combine_call(x1, ys, route, dest0, dest1, mod_l, norm_final, final):
    B, S, D = x1.shape
    N = B * S
    n_tiles, _, tm = dest0.shape
    per_b = S // tm
    idx_blk = lambda fn: pl.BlockSpec((1, 1, tm), fn, memory_space=pltpu.SMEM)
    cur = lambda i: (i, 0, 0)
    nxt = lambda i: (jnp.minimum(i + 1, n_tiles - 1), 0, 0)
    out = pl.pallas_call(
        functools.partial(_combine_kernel, final=final),
        out_shape=jax.ShapeDtypeStruct((N, D), F32),
        grid=(n_tiles,),
        in_specs=[idx_blk(cur), idx_blk(cur), idx_blk(nxt), idx_blk(nxt),
                  pl.BlockSpec((tm, D), lambda i: (i, 0)),
                  pl.BlockSpec((tm, LANE), lambda i: (i, 0)),
                  pl.BlockSpec((1, 6, D), lambda i: (i // per_b, 0, 0)),
                  pl.BlockSpec((1, D), lambda i: (0, 0)),
                  pl.BlockSpec(memory_space=pl.ANY)],
        out_specs=pl.BlockSpec((tm, D), lambda i: (i, 0)),
        scratch_shapes=[pltpu.VMEM((2, TOP_K, tm * PACK_ROWS, LANE), U32), pltpu.SemaphoreType.DMA((2,))],
        compiler_params=_cparams(("arbitrary",)),
        name="moe_combine",
    )(dest0, dest1, dest0, dest1, x1.reshape(N, D), route.reshape(N, LANE), mod_l, norm_final.reshape(1, D), ys)
    return out.reshape(B, S, D)


def _layer(x, mod_l, p, big, l, final, norm_final):
    B, S, D = x.shape
    N = B * S
    s5_u, g_qkv, g_z, s_z, s_xbc, small = _proj_call(x, mod_l, p["norm_mix"], big["w_in"], l)
    bm, cm, lam_rows = _s5_params(p["s5_a_re"], p["s5_a_im"], p["s5_b_re"], p["s5_b_im"],
                                  p["s5_c_re"], p["s5_c_im"], p["s5_log_dt"])
    y_s5 = _s5_call(s5_u, bm, cm, lam_rows, p["s5_d"], p["s5_w_glu"], p["s5_norm"])
    y_gdn = _gdn_call(g_qkv, g_z, small, p["gdn_conv_w"], p["gdn_a_log"], p["gdn_dt_bias"], p["gdn_norm"])
    y_ssd = _ssd_call(s_xbc, s_z, small, p["ssd_conv_w"], p["ssd_conv_b"], p["ssd_a_log"], p["ssd_dt_bias"],
                      p["ssd_d"], p["ssd_norm"])
    w_router, b_router = _router_params(p["moe_w_grp"], p["moe_b_grp"], p["moe_w_rt"], p["moe_b_rt"])
    x1, h2, route, counts = _out_call(x, y_s5, y_gdn, y_ssd, mod_l, p["norm_ffn"], big["w_out"],
                                      w_router, b_router, l)
    n_used, blk_e, n_valid, dest0, dest1, n_rows = _dispatch(route.reshape(N, LANE), counts[0, 0:N_EXPERTS], N,
                                                             min(MOE_TOKEN_TILE, S))
    xs = _scatter_call(h2, dest0, dest1, n_rows)
    ys = _expert_call(xs, n_used, blk_e, n_valid, big["moe_w_gate"], big["moe_w_up"], big["moe_w_down"], l)
    return _combine_call(x1, ys, route, dest0, dest1, mod_l, norm_final, final)


def kernel(x, c, w_ada, b_ada, norm_mix, norm_ffn, w_in, w_out, s5_a_re, s5_a_im, s5_b_re, s5_b_im, s5_c_re, s5_c_im, s5_d, s5_log_dt, s5_w_glu, s5_norm, gdn_conv_w, gdn_a_log, gdn_dt_bias, gdn_norm, ssd_conv_w, ssd_conv_b, ssd_a_log, ssd_dt_bias, ssd_d, ssd_norm, moe_w_grp, moe_b_grp, moe_w_rt, moe_b_rt, moe_w_gate, moe_w_up, moe_w_down, norm_final):
    stacked = dict(norm_mix=norm_mix, norm_ffn=norm_ffn, s5_a_re=s5_a_re, s5_a_im=s5_a_im,
                   s5_b_re=s5_b_re, s5_b_im=s5_b_im, s5_c_re=s5_c_re, s5_c_im=s5_c_im, s5_d=s5_d,
                   s5_log_dt=s5_log_dt, s5_w_glu=s5_w_glu, s5_norm=s5_norm, gdn_conv_w=gdn_conv_w,
                   gdn_a_log=gdn_a_log, gdn_dt_bias=gdn_dt_bias, gdn_norm=gdn_norm, ssd_conv_w=ssd_conv_w,
                   ssd_conv_b=ssd_conv_b, ssd_a_log=ssd_a_log, ssd_dt_bias=ssd_dt_bias, ssd_d=ssd_d,
                   ssd_norm=ssd_norm, moe_w_grp=moe_w_grp, moe_b_grp=moe_b_grp, moe_w_rt=moe_w_rt,
                   moe_b_rt=moe_b_rt)
    big = dict(w_in=_w_in_prep_call(w_in), w_out=w_out, moe_w_gate=moe_w_gate, moe_w_up=moe_w_up,
               moe_w_down=moe_w_down)
    L = w_in.shape[0]
    B, S, D = x.shape
    mod = _mod_call(c, w_ada, b_ada).reshape(L, B, 6, D)
    for l in range(L):
        p = {k: v[l] for k, v in stacked.items()}
        x = _layer(x, mod[l], p, big, l, l == L - 1, norm_final)
    return x
```

```python
import functools

import numpy as np
import jax
import jax.numpy as jnp
from jax import lax
from jax.experimental import pallas as pl
from jax.experimental.pallas import tpu as pltpu

F32 = jnp.float32
BF16 = jnp.bfloat16

D_MODEL = 1024
DEPTH = 4
EPS = 1e-6
CONV_WIDTH = 4
CHUNK = 64
S5_WIDTH = 256
S5_CH = 16
S5_GROUPS = 16
S5_STATE = 64
S5_LANES = S5_GROUPS * S5_STATE
GDN_WIDTH = 384
GDN_HEAD_DIM = 64
GDN_HEADS = 6
GDN_CONV_DIM = 3 * GDN_WIDTH
SSD_WIDTH = 384
SSD_HEAD_DIM = 64
SSD_HEADS = 6
SSD_GROUPS = 2
SSD_STATE = 128
SSD_GROUP_WIDTH = SSD_WIDTH // SSD_GROUPS
SSD_CONV_DIM = SSD_WIDTH + 2 * SSD_GROUPS * SSD_STATE
PROJ_SIZES = (S5_WIDTH, GDN_CONV_DIM, GDN_WIDTH, GDN_HEADS, GDN_HEADS, SSD_WIDTH, SSD_CONV_DIM, SSD_HEADS)
MOE_GROUPS = 4
EXPERTS_PER_GROUP = 8
N_EXPERTS = 32
TOP_K = 2
D_EXPERT = 256

LANE = 128
SUBLANE = 8
SMALL_W = LANE
A_LANE, B_LANE, DT_LANE = 0, GDN_HEADS, 2 * GDN_HEADS
GRP_LANE = N_EXPERTS
MOE_ROWS = 512
MOE_TOKEN_TILE = 256
DEST_TILES_PER_STEP = 8
SSD_CHUNKS_PER_ITER = 4
GDN_PAIRS_PER_ITER = 4
VMEM_LIMIT = 56 * 1024 * 1024
NEG_BIG = -1e30


def _cparams(sem):
    return pltpu.CompilerParams(dimension_semantics=sem, vmem_limit_bytes=VMEM_LIMIT)


def _split(a):
    hi = a.astype(BF16)
    lo = (a - hi.astype(F32)).astype(BF16)
    return hi, lo


_NN = (((1,), (0,)), ((), ()))
_NT = (((1,), (1,)), ((), ()))
_TN = (((0,), (0,)), ((), ()))


def _dot(a, b, dims=_NN):
    return lax.dot_general(a.astype(BF16), b.astype(BF16), dims, preferred_element_type=F32)


def _dot2(a, b_bf16):
    hi, lo = _split(a)
    return (lax.dot_general(hi, b_bf16, _NN, preferred_element_type=F32)
            + lax.dot_general(lo, b_bf16, _NN, preferred_element_type=F32))


def _dot_sel3(a, b_bf16):
    h1 = a.astype(BF16)
    r1 = a - h1.astype(F32)
    h2 = r1.astype(BF16)
    h3 = (r1 - h2.astype(F32)).astype(BF16)
    d = functools.partial(lax.dot_general, dimension_numbers=_NN, preferred_element_type=F32)
    return d(h1, b_bf16) + (d(h2, b_bf16) + d(h3, b_bf16))


def _dot3(a, b):
    ah, al = _split(a)
    bh, bl = _split(b)
    d = functools.partial(lax.dot_general, dimension_numbers=_NN, preferred_element_type=F32)
    return d(ah, bh) + (d(ah, bl) + d(al, bh))


def _silu(x):
    return x * jax.nn.sigmoid(x)


def _softplus(x):
    return jnp.maximum(x, 0.0) + jnp.log(1.0 + jnp.exp(-jnp.abs(x)))


def _norm_mod(x, w, scale, shift):
    ms = jnp.mean(x * x, axis=-1, keepdims=True)
    return (x * lax.rsqrt(ms + EPS) * w) * (1.0 + scale) + shift


def _mod_kernel(c_ref, w_ref, b_ref, o_ref):
    cond = _silu(c_ref[...])
    o_ref[0] = _dot3(cond, w_ref[0]) + b_ref[0]


def _mod_call(c, w_ada, b_ada):
    L, D, W = w_ada.shape
    B = c.shape[0]
    tn = 1536
    return pl.pallas_call(
        _mod_kernel,
        out_shape=jax.ShapeDtypeStruct((L, B, W), F32),
        grid=(L, W // tn),
        in_specs=[pl.BlockSpec((B, D), lambda l, j: (0, 0)),
                  pl.BlockSpec((1, D, tn), lambda l, j: (l, 0, j)),
                  pl.BlockSpec((1, 1, tn), lambda l, j: (l, 0, j))],
        out_specs=pl.BlockSpec((1, B, tn), lambda l, j: (l, 0, j)),
        compiler_params=_cparams(("arbitrary", "arbitrary")),
        name="adaln_mod",
    )(c, w_ada, b_ada.reshape(L, 1, W))


PROJ_OUT_W = (S5_WIDTH, GDN_CONV_DIM, GDN_WIDTH, SSD_WIDTH, SSD_CONV_DIM, SMALL_W)


def _proj_kernel(x_ref, mod_ref, nw_ref, w_ref, *o_refs):
    x = x_ref[0]
    h = _norm_mod(x, nw_ref[...], mod_ref[0, 1:2, :], mod_ref[0, 0:1, :]).astype(BF16)
    proj = jnp.dot(h, w_ref[0], preferred_element_type=F32)
    off = 0
    for o_ref in o_refs:
        n = o_ref.shape[-1]
        o_ref[0] = proj[:, off:off + n]
        off += n


_W_HEAD = S5_WIDTH + GDN_CONV_DIM + GDN_WIDTH
_W_AB = _W_HEAD + 2 * GDN_HEADS
_W_SSD = _W_AB + SSD_WIDTH + SSD_CONV_DIM
_W_END = _W_SSD + SSD_HEADS
PROJ_W = sum(PROJ_OUT_W)
W_PREP_ROWS = 256


def _w_in_prep_kernel(w_ref, o_ref):
    n_ssd = _W_SSD - _W_AB
    for r in range(0, w_ref.shape[1], W_PREP_ROWS):
        w = w_ref[0, r:r + W_PREP_ROWS, :]
        rows = slice(r, r + W_PREP_ROWS)
        o_ref[0, rows, 0:_W_HEAD] = w[:, 0:_W_HEAD].astype(BF16)
        o_ref[0, rows, _W_HEAD:_W_HEAD + n_ssd] = w[:, _W_AB:_W_SSD].astype(BF16)
        small = jnp.concatenate([w[:, _W_HEAD:_W_AB], w[:, _W_SSD:_W_END],
                                 jnp.zeros((W_PREP_ROWS, SMALL_W - (_W_AB - _W_HEAD) - (_W_END - _W_SSD)), F32)], axis=1)
        o_ref[0, rows, _W_HEAD + n_ssd:PROJ_W] = small.astype(BF16)


def _w_in_prep_call(w_in):
    L, D, W = w_in.shape
    return pl.pallas_call(
        _w_in_prep_kernel,
        out_shape=jax.ShapeDtypeStruct((L, D, PROJ_W), BF16),
        grid=(L,),
        in_specs=[pl.BlockSpec((1, D, W), lambda l: (l, 0, 0))],
        out_specs=pl.BlockSpec((1, D, PROJ_W), lambda l: (l, 0, 0)),
        compiler_params=_cparams(("arbitrary",)),
        name="w_in_prep",
    )(w_in)


def _proj_call(x, mod_l, norm_w, w_arr, l):
    B, S, D = x.shape
    tm = min(512, S)
    out_shape = [jax.ShapeDtypeStruct((B, S, n), F32) for n in PROJ_OUT_W]
    out_specs = [pl.BlockSpec((1, tm, n), lambda b, i: (b, i, 0)) for n in PROJ_OUT_W]
    return pl.pallas_call(
        _proj_kernel,
        out_shape=out_shape,
        grid=(B, S // tm),
        in_specs=[pl.BlockSpec((1, tm, D), lambda b, i: (b, i, 0)),
                  pl.BlockSpec((1, 6, D), lambda b, i: (b, 0, 0)),
                  pl.BlockSpec((1, D), lambda b, i: (0, 0)),
                  pl.BlockSpec((1, D, PROJ_W), lambda b, i: (l, 0, 0))],
        out_specs=out_specs,
        compiler_params=_cparams(("arbitrary", "arbitrary")),
        name="norm_in_proj",
    )(x, mod_l, norm_w.reshape(1, D), w_arr)


def _s5_kernel(u_ref, bm_ref, cm_ref, lam_ref, dsk_ref, glu_ref, nw_ref, perm_ref, o_ref,
               x_s, st_s, ubt_s, u_s, *, T):
    B = u_ref.shape[0]
    P = S5_LANES

    @pl.when(pl.program_id(0) == 0)
    def _():
        st_s[...] = jnp.zeros_like(st_s)

    nw = S5_WIDTH // LANE
    ubt = u_ref[...].reshape(B * T, S5_WIDTH)
    for j in range(nw):
        ubt_s[j] = ubt[:, j * LANE:(j + 1) * LANE]

    def regroup(t, carry):
        for j in range(nw):
            u_s[pl.ds(pl.multiple_of(t * B, B), B), j * LANE:(j + 1) * LANE] = ubt_s[j, pl.ds(t, B, stride=T), :]
        return carry

    lax.fori_loop(0, T, regroup, 0, unroll=8)
    u = u_s[...]
    x_s[...] = jnp.dot(u.astype(BF16), bm_ref[...], preferred_element_type=F32)
    lr = jnp.broadcast_to(lam_ref[0:1, :], (B, P))
    li = jnp.broadcast_to(lam_ref[1:2, :], (B, P))

    def step(t, carry):
        sr, si = carry
        rows = pl.ds(pl.multiple_of(t * B, B), B)
        nr = lr * sr - li * si + x_s[rows, 0:P]
        ni = lr * si + li * sr + x_s[rows, P:2 * P]
        x_s[rows, 0:P] = nr
        x_s[rows, P:2 * P] = ni
        return nr, ni

    sr, si = lax.fori_loop(0, T, step, (st_s[:, 0:P], st_s[:, P:2 * P]), unroll=4)
    st_s[:, 0:P] = sr
    st_s[:, P:2 * P] = si

    y = jnp.dot(x_s[...].astype(BF16), cm_ref[...], preferred_element_type=F32) + dsk_ref[...] * u
    y = jax.nn.gelu(y)
    y = y * jax.nn.sigmoid(jnp.dot(y.astype(BF16), glu_ref[...], preferred_element_type=F32))
    ms = jnp.mean(y * y, axis=-1, keepdims=True)
    y = (y * lax.rsqrt(ms + EPS) * nw_ref[...]).astype(BF16)
    y = jnp.dot(perm_ref[...], y, preferred_element_type=F32)
    o_ref[...] = y.reshape(B, T, S5_WIDTH).astype(o_ref.dtype)


def _s5_params(a_re, a_im, b_re, b_im, c_re, c_im, log_dt):
    G, P, CH = S5_GROUPS, S5_STATE, S5_CH
    lam = lax.complex(a_re.astype(F32), a_im.astype(F32))
    step = jnp.exp(log_dt.astype(F32))[:, None]
    lam_bar = jnp.exp(lam * step)
    b_bar = ((lam_bar - 1.0) / lam)[..., None] * lax.complex(b_re.astype(F32), b_im.astype(F32))
    eye = jnp.eye(G, dtype=F32)
    bre = jnp.einsum('gpc,gh->gchp', b_bar.real, eye).reshape(G * CH, G * P)
    bim = jnp.einsum('gpc,gh->gchp', b_bar.imag, eye).reshape(G * CH, G * P)
    bm = jnp.concatenate([bre, bim], axis=1).astype(BF16)
    cre = jnp.einsum('gcp,gh->gphc', c_re.astype(F32), eye).reshape(G * P, G * CH)
    cim = jnp.einsum('gcp,gh->gphc', c_im.astype(F32), eye).reshape(G * P, G * CH)
    cm = jnp.concatenate([cre, -cim], axis=0).astype(BF16)
    lam_rows = jnp.zeros((SUBLANE, G * P), F32)
    lam_rows = lam_rows.at[0].set(lam_bar.real.reshape(-1)).at[1].set(lam_bar.imag.reshape(-1))
    return bm, cm, lam_rows


def _s5_call(u, bm, cm, lam_rows, d_skip, w_glu, norm_w):
    B, S, W = u.shape
    T = min(128, S)
    P2 = 2 * S5_LANES
    const = lambda shape: pl.BlockSpec(shape, lambda i: tuple(0 for _ in shape))
    r = np.arange(B * T)
    perm = np.zeros((B * T, B * T), np.float32)
    perm[r, (r % T) * B + r // T] = 1.0
    return pl.pallas_call(
        functools.partial(_s5_kernel, T=T),
        out_shape=jax.ShapeDtypeStruct((B, S, W), BF16),
        grid=(S // T,),
        in_specs=[pl.BlockSpec((B, T, W), lambda i: (0, i, 0)),
                  const((W, P2)), const((P2, W)), const((SUBLANE, S5_LANES)),
                  const((1, W)), const((W, W)), const((1, W)), const((B * T, B * T))],
        out_specs=pl.BlockSpec((B, T, W), lambda i: (0, i, 0)),
        scratch_shapes=[pltpu.VMEM((B * T, P2), F32), pltpu.VMEM((B, P2), F32),
                        pltpu.VMEM((W // LANE, B * T, LANE), F32), pltpu.VMEM((B * T, W), F32)],
        compiler_params=_cparams(("arbitrary",)),
        name="s5_mixer",
    )(u, bm, cm, lam_rows, d_skip.reshape(1, W).astype(F32), w_glu.astype(BF16), norm_w.reshape(1, W).astype(F32),
      jnp.asarray(perm, BF16))


def _causal_conv_silu(x, xf_ref, cw_ref, bias):
    n = x.shape[0]
    xf_ref[SUBLANE:, :] = x
    acc = x * cw_ref[CONV_WIDTH - 1:CONV_WIDTH, :]
    for k in range(1, CONV_WIDTH):
        acc = acc + xf_ref[SUBLANE - k:SUBLANE - k + n, :] * cw_ref[CONV_WIDTH - 1 - k:CONV_WIDTH - k, :]
    xf_ref[0:SUBLANE, :] = x[n - SUBLANE:, :]
    if bias is not None:
        acc = acc + bias
    return _silu(acc)


U32 = jnp.uint32
PACK_ROWS = D_MODEL // (2 * LANE)


def _pack_rows(x):
    h = x.shape[1] // 2
    xb = x.astype(BF16).astype(F32)
    lo = lax.bitcast_convert_type(xb[:, 0:h], U32) >> 16
    hi = lax.bitcast_convert_type(xb[:, h:2 * h], U32) & U32(0xFFFF0000)
    return lo | hi


def _unpack_rows(w):
    lo = lax.bitcast_convert_type(w << 16, F32)
    hi = lax.bitcast_convert_type(w & U32(0xFFFF0000), F32)
    return jnp.concatenate([lo, hi], axis=1)


def _store_row_tiles(ref, val):
    n = val.shape[0]
    for j in range(PACK_ROWS):
        ref[pl.ds(j, n, stride=PACK_ROWS), :] = val[:, j * LANE:(j + 1) * LANE]


def _load_row_tiles(ref, n):
    return jnp.concatenate([ref[pl.ds(j, n, stride=PACK_ROWS), :] for j in range(PACK_ROWS)], axis=1)


def _lanes_from(sm, off):
    return pltpu.roll(sm, SMALL_W - off, axis=1) if off else sm


def _chunk_scans(g):
    n = g.shape[0]
    rin = lax.broadcasted_iota(jnp.int32, g.shape, 0) & (CHUNK - 1)
    pre = g
    suf = jnp.where(rin < CHUNK - 1, pltpu.roll(g, n - 1, axis=0), 0.0)
    s = 1
    while s < CHUNK:
        pre = pre + jnp.where(rin >= s, pltpu.roll(pre, s, axis=0), 0.0)
        suf = suf + jnp.where(rin + s <= CHUNK - 1, pltpu.roll(suf, n - s, axis=0), 0.0)
        s *= 2
    return pre, suf


def _head_expand():
    m = np.zeros((SMALL_W, GDN_WIDTH), np.float32)
    for h in range(GDN_HEADS):
        m[h, h * GDN_HEAD_DIM:(h + 1) * GDN_HEAD_DIM] = 1.0
    return jnp.asarray(m, BF16)


def _block_ones(width, blk):
    idx = np.arange(width) // blk
    return jnp.asarray((idx[:, None] == idx[None, :]).astype(np.float32), BF16)


def _gdn_kernel(qkv_ref, z_ref, sm_ref, cw_ref, hp_ref, nw_ref, e_ref, xp_ref, bdm_ref, o_ref,
                tail_s, st_s, kn_s, kb_s, qn_s, qd_s, kd_s, u_s, w_s, eg_s, gcx_s, gw_s,
                o_s, au_s, qe_s, ku_s, kw_s, *, TB):
    H, Dh, C = GDN_HEADS, GDN_HEAD_DIM, CHUNK
    W = GDN_WIDTH
    ncb = TB // C

    @pl.when(pl.program_id(1) == 0)
    def _():
        tail_s[0:SUBLANE, :] = jnp.zeros((SUBLANE, tail_s.shape[1]), F32)
        st_s[...] = jnp.zeros_like(st_s)

    xc = _causal_conv_silu(qkv_ref[0], tail_s, cw_ref, None)
    q, k, v = xc[:, 0:W], xc[:, W:2 * W], xc[:, 2 * W:3 * W]
    e = e_ref[...]
    xp = xp_ref[...]
    qn = q * lax.rsqrt(_dot2(q * q, e) + EPS) * (Dh ** -0.5)
    kn = k * lax.rsqrt(_dot2(k * k, e) + EPS)

    sm = sm_ref[0]
    lane = lax.broadcasted_iota(jnp.int32, sm.shape, 1)
    head_lane = lane < H
    g = jnp.where(head_lane, hp_ref[0:1, :] * _softplus(_lanes_from(sm, A_LANE) + hp_ref[1:2, :]), 0.0)
    beta = jnp.where(head_lane, jax.nn.sigmoid(_lanes_from(sm, B_LANE)), 0.0)
    gc, rc = _chunk_scans(g)
    bx = _dot2(beta, xp)
    egx = _dot2(jnp.exp(gc), xp)
    erx = _dot2(jnp.exp(rc), xp)
    kb = kn * bx
    kn_s[...] = kn
    kb_s[...] = kb
    qn_s[...] = qn
    qd_s[...] = qn * egx
    kd_s[...] = kn * erx
    u_s[...] = v * bx
    w_s[...] = kb * egx
    gcx = _dot_sel3(gc, xp)
    gcx_s[...] = gcx
    r384 = lax.broadcasted_iota(jnp.int32, (C, W), 0)
    l384 = lax.broadcasted_iota(jnp.int32, (C, W), 1) & (Dh - 1)
    for c in range(ncb):
        diag_c = jnp.where(r384 == l384, gcx[c * C:(c + 1) * C, :], 0.0)
        gw_s[c] = jnp.broadcast_to(jnp.sum(diag_c, axis=0, keepdims=True), (SUBLANE, W))
        eg_s[c] = jnp.broadcast_to(egx[(c + 1) * C - 1:(c + 1) * C, :], (SUBLANE, W))

    heads = [slice(h * Dh, (h + 1) * Dh) for h in range(H)]

    GL = 4 * Dh
    row_w = lax.broadcasted_iota(jnp.int32, (C, GL), 0)
    col_w = lax.broadcasted_iota(jnp.int32, (C, GL), 1) & (Dh - 1)
    causal_w, strict_w = row_w >= col_w, row_w > col_w
    eye_w = jnp.where(row_w == col_w, 1.0, 0.0).astype(F32)
    bdm = bdm_ref[...]
    mm = functools.partial(lax.dot_general, dimension_numbers=_NN, preferred_element_type=F32)

    def block_diag(m, mask):
        return jnp.concatenate([m] * 4, axis=0) * mask

    npair = min(GDN_PAIRS_PER_ITER, ncb // 2)

    def solve(it, carry):
        chunks = [2 * npair * it + k for k in range(2 * npair)]
        rows = [pl.ds(pl.multiple_of(c * C, C), C) for c in chunks]

        def groups(get, n):
            out = []
            for j in range(npair):
                a0, a1 = get(2 * j), get(2 * j + 1)
                out += [a0[:, 0:4 * n], a1[:, 0:4 * n],
                        jnp.concatenate([a0[:, 4 * n:6 * n], a1[:, 4 * n:6 * n]], axis=1)]
            return out

        def ungroup(ref, vals, n):
            for j in range(npair):
                ref[rows[2 * j], 0:4 * n] = vals[3 * j]
                ref[rows[2 * j + 1], 0:4 * n] = vals[3 * j + 1]
                ref[rows[2 * j], 4 * n:6 * n] = vals[3 * j + 2][:, 0:2 * n]
                ref[rows[2 * j + 1], 4 * n:6 * n] = vals[3 * j + 2][:, 2 * n:4 * n]

        kn_g = groups(lambda k: kn_s[rows[k], :], Dh)
        kb_g = groups(lambda k: kb_s[rows[k], :], Dh)
        qn_g = groups(lambda k: qn_s[rows[k], :], Dh)
        gx_g = groups(lambda k: gcx_s[rows[k], :], Dh)
        gw_g = groups(lambda k: gw_s[chunks[k], 0:1, :], Dh)
        decs = [jnp.exp(jnp.where(causal_w, gx - gw, NEG_BIG)) for gx, gw in zip(gx_g, gw_g)]
        prods = [lax.dot_general(jnp.concatenate([kb, qn], axis=0).astype(BF16), block_diag(kn.astype(BF16), bdm),
                                 _NT, preferred_element_type=F32)
                 for kn, kb, qn in zip(kn_g, kb_g, qn_g)]
        attns = [pr[C:2 * C] * dec for pr, dec in zip(prods, decs)]
        def times(lhs, p=None):
            lh, ll = _split(lhs)
            ph, pl_ = (lh[0:C], ll[0:C]) if p is None else _split(p)
            rh, rl = block_diag(ph, bdm), block_diag(pl_, bdm)
            return mm(lh, rh) + (mm(lh, rl) + mm(ll, rh))

        ps = [-jnp.where(strict_w, pr[0:C] * dec, 0.0) for pr, dec in zip(prods, decs)]
        invs = [eye_w + p for p in ps]
        ps = [times(p) for p in ps]
        for _ in range(4):
            outs = [times(jnp.concatenate([p, inv], axis=0)) for p, inv in zip(ps, invs)]
            invs = [inv + o[C:2 * C] for inv, o in zip(invs, outs)]
            ps = [o[0:C] for o in outs]
        invs = [inv + times(inv, p) for p, inv in zip(ps, invs)]
        def solved(ref):
            outs = []
            for inv, r in zip(invs, groups(lambda k: ref[rows[k], :], Dh)):
                ih, il = _split(inv)
                rh, rl = _split(r)
                rh, rl = block_diag(rh, bdm), block_diag(rl, bdm)
                outs.append(mm(ih, rh) + (mm(ih, rl) + mm(il, rh)))
            return outs

        u_g, w_g = solved(u_s), solved(w_s)
        ungroup(u_s, u_g, Dh)
        ungroup(w_s, w_g, Dh)
        at_g = [a.astype(BF16) for a in attns]
        au_g = [mm(a, block_diag(u.astype(BF16), bdm)) for a, u in zip(at_g, u_g)]
        aw_g = [mm(a, block_diag(w.astype(BF16), bdm)) for a, w in zip(at_g, w_g)]
        qd_g = groups(lambda k: qd_s[rows[k], :], Dh)
        ungroup(au_s, au_g, Dh)
        ungroup(qe_s, [qd - aw for qd, aw in zip(qd_g, aw_g)], Dh)
        for k in range(2 * npair):
            for sl in heads:
                kd_h = kd_s[rows[k], sl]
                ku_s[rows[k], sl] = _dot(kd_h, u_s[rows[k], sl], _TN)
                kw_s[rows[k], sl] = _dot(kd_h, w_s[rows[k], sl], _TN)
        return carry

    lax.fori_loop(0, ncb // (2 * npair), solve, 0)

    def recur(c, carry):
        rows = pl.ds(pl.multiple_of(c * C, C), C)
        st = st_s[...]
        sb = st.astype(BF16)
        lhs = jnp.concatenate([qe_s[rows, :], kw_s[rows, :]], axis=0).astype(BF16)
        out = jnp.concatenate(
            [mm(lhs[:, 0:GL], block_diag(sb[:, 0:GL], bdm)),
             mm(lhs[:, GL:W], jnp.concatenate([sb[:, GL:W]] * 2, axis=0) * bdm[0:W - GL, 0:W - GL])], axis=1)
        o_s[rows, :] = out[0:C] + au_s[rows, :]
        st_s[...] = st * eg_s[c, 0:1, :] + (ku_s[rows, :] - out[C:2 * C])
        return carry

    lax.fori_loop(0, ncb, recur, 0)

    o = o_s[...]
    ms = _dot2(o * o, e) * (1.0 / Dh)
    o = o * lax.rsqrt(ms + EPS) * nw_ref[...]
    o_ref[0] = (o * _silu(z_ref[0])).astype(o_ref.dtype)


def _gdn_call(qkv, z, sm, conv_w, a_log, dt_bias, norm_w):
    B, S, _ = qkv.shape
    TB = min(512, S)
    W, H, Dh = GDN_WIDTH, GDN_HEADS, GDN_HEAD_DIM
    hp = jnp.zeros((SUBLANE, SMALL_W), F32)
    hp = hp.at[0, :H].set(-jnp.exp(a_log.astype(F32))).at[1, :H].set(dt_bias.astype(F32))
    nw = jnp.tile(norm_w.astype(F32), H).reshape(1, W)
    const = lambda shape: pl.BlockSpec(shape, lambda b, i: tuple(0 for _ in shape))
    blk = lambda n: pl.BlockSpec((1, TB, n), lambda b, i: (b, i, 0))
    f = lambda *shape: pltpu.VMEM(shape, F32)
    return pl.pallas_call(
        functools.partial(_gdn_kernel, TB=TB),
        out_shape=jax.ShapeDtypeStruct((B, S, W), BF16),
        grid=(B, S // TB),
        in_specs=[blk(GDN_CONV_DIM), blk(W), blk(SMALL_W),
                  const((CONV_WIDTH, GDN_CONV_DIM)), const((SUBLANE, SMALL_W)), const((1, W)),
                  const((W, W)), const((SMALL_W, W)), const((4 * Dh, 4 * Dh))],
        out_specs=blk(W),
        scratch_shapes=[f(SUBLANE + TB, GDN_CONV_DIM), f(Dh, W)]
        + [f(TB, W)] * 7 + [f(TB // CHUNK, SUBLANE, W), f(TB, W), f(TB // CHUNK, SUBLANE, W)] + [f(TB, W)] * 5,
        compiler_params=_cparams(("arbitrary", "arbitrary")),
        name="gdn_mixer",
    )(qkv, z, sm, conv_w.astype(F32), hp, nw, _block_ones(W, Dh), _head_expand(), _block_ones(4 * Dh, Dh))


def _ssd_kernel(xbc_ref, z_ref, sm_ref, cw_ref, cb_ref, hp_ref, dsk_ref, nw_ref, e_ref, xp_ref,
                hgm_ref, e_hd_ref, ghm_ref, o_ref,
                tail_s, st_s, xs_s, xdt_s, xdd_s, bm_s, cm_s, ea_s, el_s, ac_s, aw_s, y_s, inc_s, *, TB):
    H, P, G, N, C = SSD_HEADS, SSD_HEAD_DIM, SSD_GROUPS, SSD_STATE, CHUNK
    W, GW = SSD_WIDTH, SSD_GROUP_WIDTH
    ncb = TB // C

    @pl.when(pl.program_id(1) == 0)
    def _():
        tail_s[0:SUBLANE, :] = jnp.zeros((SUBLANE, tail_s.shape[1]), F32)
        st_s[...] = jnp.zeros_like(st_s)

    xc = _causal_conv_silu(xbc_ref[0], tail_s, cw_ref, cb_ref[...])
    xs = xc[:, 0:W]
    xp = xp_ref[...]

    sm = sm_ref[0]
    lane = lax.broadcasted_iota(jnp.int32, sm.shape, 1)
    head_lane = lane < H
    dt = jnp.where(head_lane, _softplus(_lanes_from(sm, DT_LANE) + hp_ref[1:2, :]), 0.0)
    acs, rcs = _chunk_scans(dt * hp_ref[0:1, :])
    dtx = _dot2(dt, xp)
    eax = _dot2(jnp.exp(acs), xp)
    erx = _dot2(jnp.exp(rcs), xp)
    xdt = xs * dtx
    xs_s[...] = xs
    xdt_s[...] = xdt
    xdd_s[...] = xdt * erx
    bm_s[...] = xc[:, W:W + G * N]
    cm_s[...] = xc[:, W + G * N:W + 2 * G * N]
    ea_s[...] = eax
    acx = _dot_sel3(acs, xp)
    ac_s[...] = acx
    r_w = lax.broadcasted_iota(jnp.int32, (C, W), 0)
    l_w = lax.broadcasted_iota(jnp.int32, (C, W), 1) & (P - 1)
    for c in range(ncb):
        diag_c = jnp.where(r_w == l_w, acx[c * C:(c + 1) * C, :], 0.0)
        aw_s[c] = jnp.broadcast_to(jnp.sum(diag_c, axis=0, keepdims=True), (SUBLANE, W))
        el_s[c] = jnp.broadcast_to(eax[(c + 1) * C - 1:(c + 1) * C, :], (SUBLANE, W))

    causal_w = r_w >= l_w
    mm = functools.partial(lax.dot_general, dimension_numbers=_NN, preferred_element_type=F32)

    def stacked(m, mask):
        return jnp.concatenate([m] * H, axis=0) * mask

    nloc = min(SSD_CHUNKS_PER_ITER, ncb)

    def local(it, carry):
        cs = [it * nloc + k for k in range(nloc)]
        rows = [pl.ds(pl.multiple_of(c * C, C), C) for c in cs]
        bs = [bm_s[r, :].astype(BF16) for r in rows]
        cbs = [lax.dot_general(cm_s[r, :].astype(BF16), stacked(b, hgm_ref[...]), _NT, preferred_element_type=F32)
               for r, b in zip(rows, bs)]
        segs = [jnp.exp(jnp.where(causal_w, ac_s[r, :] - aw_s[c, 0:1, :], NEG_BIG)) for r, c in zip(rows, cs)]
        for r, cb, seg in zip(rows, cbs, segs):
            y_s[r, :] = mm((cb * seg).astype(BF16), stacked(xdt_s[r, :].astype(BF16), e_hd_ref[...]))
        for r, c, b in zip(rows, cs, bs):
            inc_s[c] = lax.dot_general(b, xdd_s[r, :].astype(BF16), _TN, preferred_element_type=F32) * ghm_ref[...]
        return carry

    lax.fori_loop(0, ncb // nloc, local, 0)

    def recur(c, carry):
        rows = pl.ds(pl.multiple_of(c * C, C), C)
        st = st_s[...]
        y_s[rows, :] = y_s[rows, :] + mm(cm_s[rows, :].astype(BF16), st.astype(BF16)) * ea_s[rows, :]
        st_s[...] = st * el_s[c, 0:1, :] + inc_s[c]
        return carry

    lax.fori_loop(0, ncb, recur, 0)

    y = y_s[...] + dsk_ref[...] * xs_s[...]
    y = y * _silu(z_ref[0])
    ms = _dot2(y * y, e_ref[...]) * (1.0 / GW)
    o_ref[0] = (y * lax.rsqrt(ms + EPS) * nw_ref[...]).astype(o_ref.dtype)


def _ssd_call(xbc, z, sm, conv_w, conv_b, a_log, dt_bias, d_skip, norm_w):
    B, S, _ = xbc.shape
    TB = min(512, S)
    W, H, G, N = SSD_WIDTH, SSD_HEADS, SSD_GROUPS, SSD_STATE
    hp = jnp.zeros((SUBLANE, SMALL_W), F32)
    hp = hp.at[0, :H].set(-jnp.exp(a_log.astype(F32))).at[1, :H].set(dt_bias.astype(F32))
    dsk = jnp.repeat(d_skip.astype(F32), SSD_HEAD_DIM).reshape(1, W)
    const = lambda shape: pl.BlockSpec(shape, lambda b, i: tuple(0 for _ in shape))
    blk = lambda n: pl.BlockSpec((1, TB, n), lambda b, i: (b, i, 0))
    f = lambda *shape: pltpu.VMEM(shape, F32)
    head_of = np.arange(H * CHUNK) // CHUNK
    group_of = np.arange(G * N) // N
    hgm = (head_of[:, None] // (H // G) == group_of[None, :]).astype(np.float32)
    ghm = (group_of[:, None] == (np.arange(W) // SSD_HEAD_DIM // (H // G))[None, :]).astype(np.float32)
    return pl.pallas_call(
        functools.partial(_ssd_kernel, TB=TB),
        out_shape=jax.ShapeDtypeStruct((B, S, W), BF16),
        grid=(B, S // TB),
        in_specs=[blk(SSD_CONV_DIM), blk(W), blk(SMALL_W),
                  const((CONV_WIDTH, SSD_CONV_DIM)), const((1, SSD_CONV_DIM)), const((SUBLANE, SMALL_W)),
                  const((1, W)), const((1, W)), const((W, W)), const((SMALL_W, W)),
                  const((H * CHUNK, G * N)), const((W, W)), const((G * N, W))],
        out_specs=blk(W),
        scratch_shapes=[f(SUBLANE + TB, SSD_CONV_DIM), f(G * N, W),
                        f(TB, W), f(TB, W), f(TB, W), f(TB, G * N), f(TB, G * N), f(TB, W),
                        f(TB // CHUNK, SUBLANE, W),
                        f(TB, W), f(TB // CHUNK, SUBLANE, W), f(TB, W),
                        f(TB // CHUNK, G * N, W)],
        compiler_params=_cparams(("arbitrary", "arbitrary")),
        name="ssd_mixer",
    )(xbc, z, sm, conv_w.astype(F32), conv_b.reshape(1, -1).astype(F32), hp, dsk,
      norm_w.reshape(1, W).astype(F32), _block_ones(W, SSD_GROUP_WIDTH), _head_expand(),
      jnp.asarray(hgm, BF16), _block_ones(W, SSD_HEAD_DIM), jnp.asarray(ghm, F32))


def _out_kernel(x_ref, y1_ref, y2_ref, y3_ref, mod_ref, nw_ref, wo_ref, wr_ref, br_ref, tri_ref,
                x1_ref, h_ref, rt_ref, cnt_ref, run_s, wo_s):
    first = (pl.program_id(0) == 0) & (pl.program_id(1) == 0)

    @pl.when(first)
    def _():
        wo_s[...] = wo_ref[0].astype(BF16)

    y = (jnp.dot(y1_ref[0], wo_s[0:S5_WIDTH, :], preferred_element_type=F32)
         + jnp.dot(y2_ref[0], wo_s[S5_WIDTH:S5_WIDTH + GDN_WIDTH, :], preferred_element_type=F32)
         + jnp.dot(y3_ref[0], wo_s[S5_WIDTH + GDN_WIDTH:, :], preferred_element_type=F32))
    x1 = x_ref[0] + mod_ref[0, 2:3, :] * y
    x1_ref[0] = x1
    h = _norm_mod(x1, nw_ref[...], mod_ref[0, 4:5, :], mod_ref[0, 3:4, :])
    _store_row_tiles(h_ref, _pack_rows(h))
    lg =jnp.dot(h.astype(BF16), wr_ref[...], preferred_element_type=F32) + br_ref[...]

    lane = lax.broadcasted_iota(jnp.int32, lg.shape, 1)
    lanef = lane.astype(F32)
    big = float(4 * LANE)
    grp = (lane >= GRP_LANE) & (lane < GRP_LANE + MOE_GROUPS)
    lgm = jnp.where(grp, lg, -jnp.inf)
    m = jnp.max(lgm, axis=-1, keepdims=True)
    gidx = jnp.min(jnp.where(lgm == m, lanef - GRP_LANE, big), axis=-1, keepdims=True)
    g_w = 1.0 / jnp.sum(jnp.where(grp, jnp.exp(lg - m), 0.0), axis=-1, keepdims=True)
    in_grp = (lane < N_EXPERTS) & ((lane // EXPERTS_PER_GROUP).astype(F32) == gidx)
    le = jnp.where(in_grp, lg, -jnp.inf)
    v1 = jnp.max(le, axis=-1, keepdims=True)
    i1 = jnp.min(jnp.where(le == v1, lanef, big), axis=-1, keepdims=True)
    le2 = jnp.where(lanef == i1, -jnp.inf, le)
    v2 = jnp.max(le2, axis=-1, keepdims=True)
    i2 = jnp.min(jnp.where(le2 == v2, lanef, big), axis=-1, keepdims=True)
    e2 = jnp.exp(v2 - v1)
    w1 = g_w / (1.0 + e2)
    w2 = g_w * e2 / (1.0 + e2)

    @pl.when(first)
    def _():
        run_s[...] = jnp.zeros_like(run_s)

    chosen = jnp.where((lanef == i1) | (lanef == i2), 1.0, 0.0)
    before = jnp.dot(tri_ref[...], chosen.astype(BF16), preferred_element_type=F32) + run_s[0:1, :]
    p1 = jnp.sum(jnp.where(lanef == i1, before, 0.0), axis=-1, keepdims=True)
    p2 = jnp.sum(jnp.where(lanef == i2, before, 0.0), axis=-1, keepdims=True)
    run_s[...] = run_s[...] + jnp.sum(chosen, axis=0, keepdims=True)
    cnt_ref[...] = run_s[...]

    rt = jnp.zeros_like(lg)
    for k, val in enumerate((i1, i2, w1, w2, p1, p2)):
        rt = jnp.where(lane == k, val, rt)
    rt_ref[0] = rt


def _out_call(x, y1, y2, y3, mod_l, norm_w, w_out, w_router, b_router, l):
    B, S, D = x.shape
    tm = min(512, S)
    blk = lambda n: pl.BlockSpec((1, tm, n), lambda b, i: (b, i, 0))
    const = lambda shape: pl.BlockSpec(shape, lambda b, i: tuple(0 for _ in shape))
    tri = jnp.asarray(np.tril(np.ones((tm, tm), np.float32), -1), BF16)
    nb = S // tm
    tiles = PACK_ROWS
    return pl.pallas_call(
        _out_kernel,
        out_shape=[jax.ShapeDtypeStruct((B, S, D), F32), jax.ShapeDtypeStruct((B * S * tiles, LANE), U32),
                   jax.ShapeDtypeStruct((B, S, LANE), F32), jax.ShapeDtypeStruct((SUBLANE, LANE), F32)],
        grid=(B, nb),
        in_specs=[blk(D), blk(S5_WIDTH), blk(GDN_WIDTH), blk(SSD_WIDTH),
                  pl.BlockSpec((1, 6, D), lambda b, i: (b, 0, 0)),
                  const((1, D)), pl.BlockSpec((1, D, D), lambda b, i: (l, 0, 0)),
                  const((D, LANE)), const((1, LANE)), const((tm, tm))],
        out_specs=[blk(D), pl.BlockSpec((tm * tiles, LANE), lambda b, i: (b * nb + i, 0)),
                   blk(LANE), const((SUBLANE, LANE))],
        scratch_shapes=[pltpu.VMEM((SUBLANE, LANE), F32), pltpu.VMEM((D, D), BF16)],
        compiler_params=_cparams(("arbitrary", "arbitrary")),
        name="out_proj_router",
    )(x, y1, y2, y3, mod_l, norm_w.reshape(1, D), w_out, w_router, b_router, tri)


def _router_params(w_grp, b_grp, w_rt, b_rt):
    D = w_grp.shape[0]
    w = jnp.zeros((D, LANE), F32).at[:, 0:N_EXPERTS].set(w_rt).at[:, GRP_LANE:GRP_LANE + MOE_GROUPS].set(w_grp)
    b = jnp.zeros((1, LANE), F32).at[0, 0:N_EXPERTS].set(b_rt).at[0, GRP_LANE:GRP_LANE + MOE_GROUPS].set(b_grp)
    return w.astype(BF16), b


def _tile_copy_loop(n, fn):
    def body(t, carry):
        fn(t)
        return carry
    lax.fori_loop(0, n, body, 0, unroll=32)


def _scatter_kernel(d0_ref, d1_ref, h_ref, xs_hbm, stage, sem):
    i = pl.program_id(0)
    n = pl.num_programs(0)
    slot = lax.rem(i, 2)
    tm = d0_ref.shape[-1]

    def copy(s, t, dst):
        return pltpu.make_async_copy(stage.at[s, pl.ds(t * PACK_ROWS, PACK_ROWS)],
                                     xs_hbm.at[pl.ds(dst * PACK_ROWS, PACK_ROWS)], sem.at[s])

    def wait_slot(s):
        _tile_copy_loop(TOP_K * tm, lambda t: copy(s, 0, 0).wait())

    @pl.when(i >= 2)
    def _():
        wait_slot(slot)

    stage[slot] = h_ref[...]

    def start(t):
        copy(slot, t, d0_ref[0, 0, t]).start(priority=0)
        copy(slot, t, d1_ref[0, 0, t]).start(priority=1)
    _tile_copy_loop(tm, start)

    @pl.when(i == n - 1)
    def _():
        @pl.when(i >= 1)
        def _():
            wait_slot(1 - slot)
        wait_slot(slot)


def _scatter_call(h2t, dest0, dest1, n_rows):
    n_tiles, _, tm = dest0.shape
    idx_blk = pl.BlockSpec((1, 1, tm), lambda i: (i, 0, 0), memory_space=pltpu.SMEM)
    return pl.pallas_call(
        _scatter_kernel,
        out_shape=jax.ShapeDtypeStruct((n_rows * PACK_ROWS, LANE), U32),
        grid=(n_tiles,),
        in_specs=[idx_blk, idx_blk, pl.BlockSpec((tm * PACK_ROWS, LANE), lambda i: (i, 0))],
        out_specs=pl.BlockSpec(memory_space=pl.ANY),
        scratch_shapes=[pltpu.VMEM((2, tm * PACK_ROWS, LANE), U32), pltpu.SemaphoreType.DMA((2,))],
        compiler_params=_cparams(("arbitrary",)),
        name="moe_scatter",
    )(dest0, dest1, h2t)


def _expert_kernel(nused_ref, blke_ref, nvalid_ref, xs_ref, wg_ref, wu_ref, wd_ref, ys_ref, wg_s, wu_s, wd_s):
    i = pl.program_id(0)

    @pl.when(i < nused_ref[0])
    def _():
        @pl.when((i == 0) | (blke_ref[i] != blke_ref[jnp.maximum(i - 1, 0)]))
        def _():
            wg_s[...] = wg_ref[0, 0].astype(BF16)
            wu_s[...] = wu_ref[0, 0].astype(BF16)
            wd_s[...] = wd_ref[0, 0].astype(BF16)

        row = lax.broadcasted_iota(jnp.int32, (MOE_ROWS, 1), 0)
        words = jnp.where(row < nvalid_ref[i], _load_row_tiles(xs_ref, MOE_ROWS), U32(0))
        xb = _unpack_rows(words).astype(BF16)
        hid = _silu(jnp.dot(xb, wg_s[...], preferred_element_type=F32)) * jnp.dot(xb, wu_s[...], preferred_element_type=F32)
        _store_row_tiles(ys_ref, _pack_rows(jnp.dot(hid.astype(BF16), wd_s[...], preferred_element_type=F32)))

    @pl.when(i >= nused_ref[0])
    def _():
        ys_ref[...] = jnp.zeros_like(ys_ref)


def _expert_call(xs, n_used, blk_e, n_valid, w_gate, w_up, w_down, l):
    D = w_gate.shape[2]
    n_blk = blk_e.shape[0]
    rows_blk = pl.BlockSpec((MOE_ROWS * PACK_ROWS, LANE), lambda i, nu, be, nv: (i, 0))
    grid_spec = pltpu.PrefetchScalarGridSpec(
        num_scalar_prefetch=3,
        grid=(n_blk,),
        in_specs=[rows_blk,
                  pl.BlockSpec((1, 1, D, D_EXPERT), lambda i, nu, be, nv: (l, be[i], 0, 0)),
                  pl.BlockSpec((1, 1, D, D_EXPERT), lambda i, nu, be, nv: (l, be[i], 0, 0)),
                  pl.BlockSpec((1, 1, D_EXPERT, D), lambda i, nu, be, nv: (l, be[i], 0, 0))],
        out_specs=rows_blk,
        scratch_shapes=[pltpu.VMEM((D, D_EXPERT), BF16), pltpu.VMEM((D, D_EXPERT), BF16),
                        pltpu.VMEM((D_EXPERT, D), BF16)],
    )
    return pl.pallas_call(
        _expert_kernel,
        out_shape=jax.ShapeDtypeStruct(xs.shape, U32),
        grid_spec=grid_spec,
        compiler_params=_cparams(("arbitrary",)),
        name="expert_mlp",
    )(n_used, blk_e, n_valid, xs, w_gate, w_up, w_down)


def _dest_kernel(rt_ref, ps_ref, o_ref):
    n_sub, _, tm = o_ref.shape
    for j in range(n_sub):
        rt = rt_ref[j * tm:(j + 1) * tm, :]
        lanef = lax.broadcasted_iota(jnp.int32, rt.shape, 1).astype(F32)
        d = jnp.zeros_like(rt)
        for k in range(TOP_K):
            start = jnp.sum(jnp.where(lanef == rt[:, k:k + 1], ps_ref[...], 0.0), axis=-1, keepdims=True)
            d = jnp.where(lanef == k, start + rt[:, 4 + k:5 + k], d)
        o_ref[j] = d.T[0:SUBLANE, :].astype(jnp.int32)


def _dest_call(route, pstart_row, tm):
    N = route.shape[0]
    n_sub = min(DEST_TILES_PER_STEP, N // tm)
    return pl.pallas_call(
        _dest_kernel,
        out_shape=jax.ShapeDtypeStruct((N // tm, SUBLANE, tm), jnp.int32),
        grid=(N // (tm * n_sub),),
        in_specs=[pl.BlockSpec((tm * n_sub, LANE), lambda i: (i, 0)), pl.BlockSpec((1, LANE), lambda i: (0, 0))],
        out_specs=pl.BlockSpec((n_sub, SUBLANE, tm), lambda i: (i, 0, 0)),
        compiler_params=_cparams(("arbitrary",)),
        name="moe_dest",
    )(route, pstart_row)


def _dispatch(route, counts, N, tm):
    L_pad = N * TOP_K + N_EXPERTS * MOE_ROWS
    n_blk = L_pad // MOE_ROWS
    counts = counts.astype(jnp.int32)
    padded = ((counts + MOE_ROWS - 1) // MOE_ROWS) * MOE_ROWS
    pend = jnp.cumsum(padded)
    pstart = pend - padded
    n_used = (pend[-1] // MOE_ROWS).astype(jnp.int32).reshape(1)
    blk_row0 = jnp.arange(n_blk, dtype=jnp.int32) * MOE_ROWS
    blk_e = jnp.minimum(jnp.sum((pend[None, :] <= blk_row0[:, None]).astype(jnp.int32), axis=1), N_EXPERTS - 1)
    n_valid = jnp.clip((pstart + counts)[blk_e] - blk_row0, 0, MOE_ROWS).astype(jnp.int32)
    pstart_row = jnp.zeros((1, LANE), F32).at[0, 0:N_EXPERTS].set(pstart.astype(F32))
    dest = _dest_call(route, pstart_row, tm)
    return n_used, blk_e.astype(jnp.int32), n_valid, dest[:, 0:1, :], dest[:, 1:2, :], L_pad


def _combine_kernel(d0_ref, d1_ref, d0n_ref, d1n_ref, x_ref, rt_ref, mod_ref, nf_ref, ys_hbm, o_ref,
                    buf, sem, *, final):
    i = pl.program_id(0)
    n = pl.num_programs(0)
    slot = lax.rem(i, 2)
    tm = x_ref.shape[0]

    def copy(s, k, t, src):
        return pltpu.make_async_copy(ys_hbm.at[pl.ds(src * PACK_ROWS, PACK_ROWS)],
                                     buf.at[s, k, pl.ds(t * PACK_ROWS, PACK_ROWS)], sem.at[s])

    def start_tile(s, a_ref, b_ref):
        def start(t):
            copy(s, 0, t, a_ref[0, 0, t]).start(priority=0)
            copy(s, 1, t, b_ref[0, 0, t]).start(priority=1)
        _tile_copy_loop(tm, start)

    @pl.when(i == 0)
    def _():
        start_tile(0, d0_ref, d1_ref)

    @pl.when(i + 1 < n)
    def _():
        start_tile(1 - slot, d0n_ref, d1n_ref)

    _tile_copy_loop(TOP_K * tm, lambda t: copy(slot, 0, 0, 0).wait())
    rt = rt_ref[...]
    y = (rt[:, 2:3] * _unpack_rows(_load_row_tiles(buf.at[slot, 0], tm))
         + rt[:, 3:4] * _unpack_rows(_load_row_tiles(buf.at[slot, 1], tm)))
    x2 = x_ref[...] + mod_ref[0, 5:6, :] * y
    if final:
        ms = jnp.mean(x2 * x2, axis=-1, keepdims=True)
        x2 = x2 * lax.rsqrt(ms + EPS) * nf_ref[...]
    o_ref[...] = x2


def _combine_call(x1, ys, route, dest0, dest1, mod_l, norm_final, final):
    B, S, D = x1.shape
    N = B * S
    n_tiles, _, tm = dest0.shape
    per_b = S // tm
    idx_blk = lambda fn: pl.BlockSpec((1, 1, tm), fn, memory_space=pltpu.SMEM)
    cur = lambda i: (i, 0, 0)
    nxt = lambda i: (jnp.minimum(i + 1, n_tiles - 1), 0, 0)
    out = pl.pallas_call(
        functools.partial(_combine_kernel, final=final),
        out_shape=jax.ShapeDtypeStruct((N, D), F32),
        grid=(n_tiles,),
        in_specs=[idx_blk(cur), idx_blk(cur), idx_blk(nxt), idx_blk(nxt),
                  pl.BlockSpec((tm, D), lambda i: (i, 0)),
                  pl.BlockSpec((tm, LANE), lambda i: (i, 0)),
                  pl.BlockSpec((1, 6, D), lambda i: (i // per_b, 0, 0)),
                  pl.BlockSpec((1, D), lambda i: (0, 0)),
                  pl.BlockSpec(memory_space=pl.ANY)],
        out_specs=pl.BlockSpec((tm, D), lambda i: (i, 0)),
        scratch_shapes=[pltpu.VMEM((2, TOP_K, tm * PACK_ROWS, LANE), U32), pltpu.SemaphoreType.DMA((2,))],
        compiler_params=_cparams(("arbitrary",)),
        name="moe_combine",
    )(dest0, dest1, dest0, dest1, x1.reshape(N, D), route.reshape(N, LANE), mod_l, norm_final.reshape(1, D), ys)
    return out.reshape(B, S, D)


def _layer(x, mod_l, p, big, l, final, norm_final):
    B, S, D = x.shape
    N = B * S
    s5_u, g_qkv, g_z, s_z, s_xbc, small = _proj_call(x, mod_l, p["norm_mix"], big["w_in"], l)
    bm, cm, lam_rows = _s5_params(p["s5_a_re"], p["s5_a_im"], p["s5_b_re"], p["s5_b_im"],
                                  p["s5_c_re"], p["s5_c_im"], p["s5_log_dt"])
    y_s5 = _s5_call(s5_u, bm, cm, lam_rows, p["s5_d"], p["s5_w_glu"], p["s5_norm"])
    y_gdn = _gdn_call(g_qkv, g_z, small, p["gdn_conv_w"], p["gdn_a_log"], p["gdn_dt_bias"], p["gdn_norm"])
    y_ssd = _ssd_call(s_xbc, s_z, small, p["ssd_conv_w"], p["ssd_conv_b"], p["ssd_a_log"], p["ssd_dt_bias"],
                      p["ssd_d"], p["ssd_norm"])
    w_router, b_router = _router_params(p["moe_w_grp"], p["moe_b_grp"], p["moe_w_rt"], p["moe_b_rt"])
    x1, h2, route, counts = _out_call(x, y_s5, y_gdn, y_ssd, mod_l, p["norm_ffn"], big["w_out"],
                                      w_router, b_router, l)
    n_used, blk_e, n_valid, dest0, dest1, n_rows = _dispatch(route.reshape(N, LANE), counts[0, 0:N_EXPERTS], N,
                                                             min(MOE_TOKEN_TILE, S))
    xs = _scatter_call(h2, dest0, dest1, n_rows)
    ys = _expert_call(xs, n_used, blk_e, n_valid, big["moe_w_gate"], big["moe_w_up"], big["moe_w_down"], l)
    return _combine_call(x1, ys, route, dest0, dest1, mod_l, norm_final, final)


def kernel(x, c, w_ada, b_ada, norm_mix, norm_ffn, w_in, w_out, s5_a_re, s5_a_im, s5_b_re, s5_b_im, s5_c_re, s5_c_im, s5_d, s5_log_dt, s5_w_glu, s5_norm, gdn_conv_w, gdn_a_log, gdn_dt_bias, gdn_norm, ssd_conv_w, ssd_conv_b, ssd_a_log, ssd_dt_bias, ssd_d, ssd_norm, moe_w_grp, moe_b_grp, moe_w_rt, moe_b_rt, moe_w_gate, moe_w_up, moe_w_down, norm_final):
    stacked = dict(norm_mix=norm_mix, norm_ffn=norm_ffn, s5_a_re=s5_a_re, s5_a_im=s5_a_im,
                   s5_b_re=s5_b_re, s5_b_im=s5_b_im, s5_c_re=s5_c_re, s5_c_im=s5_c_im, s5_d=s5_d,
                   s5_log_dt=s5_log_dt, s5_w_glu=s5_w_glu, s5_norm=s5_norm, gdn_conv_w=gdn_conv_w,
                   gdn_a_log=gdn_a_log, gdn_dt_bias=gdn_dt_bias, gdn_norm=gdn_norm, ssd_conv_w=ssd_conv_w,
                   ssd_conv_b=ssd_conv_b, ssd_a_log=ssd_a_log, ssd_dt_bias=ssd_dt_bias, ssd_d=ssd_d,
                   ssd_norm=ssd_norm, moe_w_grp=moe_w_grp, moe_b_grp=moe_b_grp, moe_w_rt=moe_w_rt,
                   moe_b_rt=moe_b_rt)
    big = dict(w_in=_w_in_prep_call(w_in), w_out=w_out, moe_w_gate=moe_w_gate, moe_w_up=moe_w_up,
               moe_w_down=moe_w_down)
    L = w_in.shape[0]
    B, S, D = x.shape
    mod = _mod_call(c, w_ada, b_ada).reshape(L, B, 6, D)
    for l in range(L):
        p = {k: v[l] for k, v in stacked.items()}
        x = _layer(x, mod[l], p, big, l, l == L - 1, norm_final)
    return x
```

```python
import functools

import numpy as np
import jax
import jax.numpy as jnp
from jax import lax
from jax.experimental import pallas as pl
from jax.experimental.pallas import tpu as pltpu

F32 = jnp.float32
BF16 = jnp.bfloat16

D_MODEL = 1024
DEPTH = 4
EPS = 1e-6
CONV_WIDTH = 4
CHUNK = 64
S5_WIDTH = 256
S5_CH = 16
S5_GROUPS = 16
S5_STATE = 64
S5_LANES = S5_GROUPS * S5_STATE
GDN_WIDTH = 384
GDN_HEAD_DIM = 64
GDN_HEADS = 6
GDN_CONV_DIM = 3 * GDN_WIDTH
SSD_WIDTH = 384
SSD_HEAD_DIM = 64
SSD_HEADS = 6
SSD_GROUPS = 2
SSD_STATE = 128
SSD_GROUP_WIDTH = SSD_WIDTH // SSD_GROUPS
SSD_CONV_DIM = SSD_WIDTH + 2 * SSD_GROUPS * SSD_STATE
PROJ_SIZES = (S5_WIDTH, GDN_CONV_DIM, GDN_WIDTH, GDN_HEADS, GDN_HEADS, SSD_WIDTH, SSD_CONV_DIM, SSD_HEADS)
MOE_GROUPS = 4
EXPERTS_PER_GROUP = 8
N_EXPERTS = 32
TOP_K = 2
D_EXPERT = 256

LANE = 128
SUBLANE = 8
SMALL_W = LANE
A_LANE, B_LANE, DT_LANE = 0, GDN_HEADS, 2 * GDN_HEADS
GRP_LANE = N_EXPERTS
MOE_ROWS = 512
MOE_TOKEN_TILE = 512
DEST_TILES_PER_STEP = 8
SSD_CHUNKS_PER_ITER = 4
GDN_PAIRS_PER_ITER = 4
VMEM_LIMIT = 56 * 1024 * 1024
NEG_BIG = -1e30


def _cparams(sem):
    return pltpu.CompilerParams(dimension_semantics=sem, vmem_limit_bytes=VMEM_LIMIT)


def _split(a):
    hi = a.astype(BF16)
    lo = (a - hi.astype(F32)).astype(BF16)
    return hi, lo


_NN = (((1,), (0,)), ((), ()))
_NT = (((1,), (1,)), ((), ()))
_TN = (((0,), (0,)), ((), ()))


def _dot(a, b, dims=_NN):
    return lax.dot_general(a.astype(BF16), b.astype(BF16), dims, preferred_element_type=F32)


def _dot2(a, b_bf16):
    hi, lo = _split(a)
    return (lax.dot_general(hi, b_bf16, _NN, preferred_element_type=F32)
            + lax.dot_general(lo, b_bf16, _NN, preferred_element_type=F32))


def _dot_sel3(a, b_bf16):
    h1 = a.astype(BF16)
    r1 = a - h1.astype(F32)
    h2 = r1.astype(BF16)
    h3 = (r1 - h2.astype(F32)).astype(BF16)
    d = functools.partial(lax.dot_general, dimension_numbers=_NN, preferred_element_type=F32)
    return d(h1, b_bf16) + (d(h2, b_bf16) + d(h3, b_bf16))


def _dot3(a, b):
    ah, al = _split(a)
    bh, bl = _split(b)
    d = functools.partial(lax.dot_general, dimension_numbers=_NN, preferred_element_type=F32)
    return d(ah, bh) + (d(ah, bl) + d(al, bh))


def _silu(x):
    return x * jax.nn.sigmoid(x)


def _softplus(x):
    return jnp.maximum(x, 0.0) + jnp.log(1.0 + jnp.exp(-jnp.abs(x)))


def _norm_mod(x, w, scale, shift):
    ms = jnp.mean(x * x, axis=-1, keepdims=True)
    return (x * lax.rsqrt(ms + EPS) * w) * (1.0 + scale) + shift


def _mod_kernel(c_ref, w_ref, b_ref, o_ref):
    cond = _silu(c_ref[...])
    o_ref[0] = _dot3(cond, w_ref[0]) + b_ref[0]


def _mod_call(c, w_ada, b_ada):
    L, D, W = w_ada.shape
    B = c.shape[0]
    tn = 1536
    return pl.pallas_call(
        _mod_kernel,
        out_shape=jax.ShapeDtypeStruct((L, B, W), F32),
        grid=(L, W // tn),
        in_specs=[pl.BlockSpec((B, D), lambda l, j: (0, 0)),
                  pl.BlockSpec((1, D, tn), lambda l, j: (l, 0, j)),
                  pl.BlockSpec((1, 1, tn), lambda l, j: (l, 0, j))],
        out_specs=pl.BlockSpec((1, B, tn), lambda l, j: (l, 0, j)),
        compiler_params=_cparams(("arbitrary", "arbitrary")),
        name="adaln_mod",
    )(c, w_ada, b_ada.reshape(L, 1, W))


PROJ_OUT_W = (S5_WIDTH, GDN_CONV_DIM, GDN_WIDTH, SSD_WIDTH, SSD_CONV_DIM, SMALL_W)


def _proj_kernel(x_ref, mod_ref, nw_ref, w_ref, *o_refs):
    x = x_ref[0]
    h = _norm_mod(x, nw_ref[...], mod_ref[0, 1:2, :], mod_ref[0, 0:1, :]).astype(BF16)
    proj = jnp.dot(h, w_ref[0], preferred_element_type=F32)
    off = 0
    for o_ref in o_refs:
        n = o_ref.shape[-1]
        o_ref[0] = proj[:, off:off + n]
        off += n


_W_HEAD = S5_WIDTH + GDN_CONV_DIM + GDN_WIDTH
_W_AB = _W_HEAD + 2 * GDN_HEADS
_W_SSD = _W_AB + SSD_WIDTH + SSD_CONV_DIM
_W_END = _W_SSD + SSD_HEADS
PROJ_W = sum(PROJ_OUT_W)
W_PREP_ROWS = 256


def _w_in_prep_kernel(w_ref, o_ref):
    n_ssd = _W_SSD - _W_AB
    for r in range(0, w_ref.shape[1], W_PREP_ROWS):
        w = w_ref[0, r:r + W_PREP_ROWS, :]
        rows = slice(r, r + W_PREP_ROWS)
        o_ref[0, rows, 0:_W_HEAD] = w[:, 0:_W_HEAD].astype(BF16)
        o_ref[0, rows, _W_HEAD:_W_HEAD + n_ssd] = w[:, _W_AB:_W_SSD].astype(BF16)
        small = jnp.concatenate([w[:, _W_HEAD:_W_AB], w[:, _W_SSD:_W_END],
                                 jnp.zeros((W_PREP_ROWS, SMALL_W - (_W_AB - _W_HEAD) - (_W_END - _W_SSD)), F32)], axis=1)
        o_ref[0, rows, _W_HEAD + n_ssd:PROJ_W] = small.astype(BF16)


def _w_in_prep_call(w_in):
    L, D, W = w_in.shape
    return pl.pallas_call(
        _w_in_prep_kernel,
        out_shape=jax.ShapeDtypeStruct((L, D, PROJ_W), BF16),
        grid=(L,),
        in_specs=[pl.BlockSpec((1, D, W), lambda l: (l, 0, 0))],
        out_specs=pl.BlockSpec((1, D, PROJ_W), lambda l: (l, 0, 0)),
        compiler_params=_cparams(("arbitrary",)),
        name="w_in_prep",
    )(w_in)


def _proj_call(x, mod_l, norm_w, w_arr, l):
    B, S, D = x.shape
    tm = min(512, S)
    out_shape = [jax.ShapeDtypeStruct((B, S, n), F32) for n in PROJ_OUT_W]
    out_specs = [pl.BlockSpec((1, tm, n), lambda b, i: (b, i, 0)) for n in PROJ_OUT_W]
    return pl.pallas_call(
        _proj_kernel,
        out_shape=out_shape,
        grid=(B, S // tm),
        in_specs=[pl.BlockSpec((1, tm, D), lambda b, i: (b, i, 0)),
                  pl.BlockSpec((1, 6, D), lambda b, i: (b, 0, 0)),
                  pl.BlockSpec((1, D), lambda b, i: (0, 0)),
                  pl.BlockSpec((1, D, PROJ_W), lambda b, i: (l, 0, 0))],
        out_specs=out_specs,
        compiler_params=_cparams(("arbitrary", "arbitrary")),
        name="norm_in_proj",
    )(x, mod_l, norm_w.reshape(1, D), w_arr)


def _s5_kernel(u_ref, bm_ref, cm_ref, lam_ref, dsk_ref, glu_ref, nw_ref, perm_ref, o_ref,
               x_s, st_s, ubt_s, u_s, *, T):
    B = u_ref.shape[0]
    P = S5_LANES

    @pl.when(pl.program_id(0) == 0)
    def _():
        st_s[...] = jnp.zeros_like(st_s)

    nw = S5_WIDTH // LANE
    ubt = u_ref[...].reshape(B * T, S5_WIDTH)
    for j in range(nw):
        ubt_s[j] = ubt[:, j * LANE:(j + 1) * LANE]

    def regroup(t, carry):
        for j in range(nw):
            u_s[pl.ds(pl.multiple_of(t * B, B), B), j * LANE:(j + 1) * LANE] = ubt_s[j, pl.ds(t, B, stride=T), :]
        return carry

    lax.fori_loop(0, T, regroup, 0, unroll=8)
    u = u_s[...]
    x_s[...] = jnp.dot(u.astype(BF16), bm_ref[...], preferred_element_type=F32)
    lr = jnp.broadcast_to(lam_ref[0:1, :], (B, P))
    li = jnp.broadcast_to(lam_ref[1:2, :], (B, P))

    def step(t, carry):
        sr, si = carry
        rows = pl.ds(pl.multiple_of(t * B, B), B)
        nr = lr * sr - li * si + x_s[rows, 0:P]
        ni = lr * si + li * sr + x_s[rows, P:2 * P]
        x_s[rows, 0:P] = nr
        x_s[rows, P:2 * P] = ni
        return nr, ni

    sr, si = lax.fori_loop(0, T, step, (st_s[:, 0:P], st_s[:, P:2 * P]), unroll=4)
    st_s[:, 0:P] = sr
    st_s[:, P:2 * P] = si

    y = jnp.dot(x_s[...].astype(BF16), cm_ref[...], preferred_element_type=F32) + dsk_ref[...] * u
    y = jax.nn.gelu(y)
    y = y * jax.nn.sigmoid(jnp.dot(y.astype(BF16), glu_ref[...], preferred_element_type=F32))
    ms = jnp.mean(y * y, axis=-1, keepdims=True)
    y = (y * lax.rsqrt(ms + EPS) * nw_ref[...]).astype(BF16)
    y = jnp.dot(perm_ref[...], y, preferred_element_type=F32)
    o_ref[...] = y.reshape(B, T, S5_WIDTH).astype(o_ref.dtype)


def _s5_params(a_re, a_im, b_re, b_im, c_re, c_im, log_dt):
    G, P, CH = S5_GROUPS, S5_STATE, S5_CH
    lam = lax.complex(a_re.astype(F32), a_im.astype(F32))
    step = jnp.exp(log_dt.astype(F32))[:, None]
    lam_bar = jnp.exp(lam * step)
    b_bar = ((lam_bar - 1.0) / lam)[..., None] * lax.complex(b_re.astype(F32), b_im.astype(F32))
    eye = jnp.eye(G, dtype=F32)
    bre = jnp.einsum('gpc,gh->gchp', b_bar.real, eye).reshape(G * CH, G * P)
    bim = jnp.einsum('gpc,gh->gchp', b_bar.imag, eye).reshape(G * CH, G * P)
    bm = jnp.concatenate([bre, bim], axis=1).astype(BF16)
    cre = jnp.einsum('gcp,gh->gphc', c_re.astype(F32), eye).reshape(G * P, G * CH)
    cim = jnp.einsum('gcp,gh->gphc', c_im.astype(F32), eye).reshape(G * P, G * CH)
    cm = jnp.concatenate([cre, -cim], axis=0).astype(BF16)
    lam_rows = jnp.zeros((SUBLANE, G * P), F32)
    lam_rows = lam_rows.at[0].set(lam_bar.real.reshape(-1)).at[1].set(lam_bar.imag.reshape(-1))
    return bm, cm, lam_rows


def _s5_call(u, bm, cm, lam_rows, d_skip, w_glu, norm_w):
    B, S, W = u.shape
    T = min(128, S)
    P2 = 2 * S5_LANES
    const = lambda shape: pl.BlockSpec(shape, lambda i: tuple(0 for _ in shape))
    r = np.arange(B * T)
    perm = np.zeros((B * T, B * T), np.float32)
    perm[r, (r % T) * B + r // T] = 1.0
    return pl.pallas_call(
        functools.partial(_s5_kernel, T=T),
        out_shape=jax.ShapeDtypeStruct((B, S, W), BF16),
        grid=(S // T,),
        in_specs=[pl.BlockSpec((B, T, W), lambda i: (0, i, 0)),
                  const((W, P2)), const((P2, W)), const((SUBLANE, S5_LANES)),
                  const((1, W)), const((W, W)), const((1, W)), const((B * T, B * T))],
        out_specs=pl.BlockSpec((B, T, W), lambda i: (0, i, 0)),
        scratch_shapes=[pltpu.VMEM((B * T, P2), F32), pltpu.VMEM((B, P2), F32),
                        pltpu.VMEM((W // LANE, B * T, LANE), F32), pltpu.VMEM((B * T, W), F32)],
        compiler_params=_cparams(("arbitrary",)),
        name="s5_mixer",
    )(u, bm, cm, lam_rows, d_skip.reshape(1, W).astype(F32), w_glu.astype(BF16), norm_w.reshape(1, W).astype(F32),
      jnp.asarray(perm, BF16))


def _causal_conv_silu(x, xf_ref, cw_ref, bias):
    n = x.shape[0]
    xf_ref[SUBLANE:, :] = x
    acc = x * cw_ref[CONV_WIDTH - 1:CONV_WIDTH, :]
    for k in range(1, CONV_WIDTH):
        acc = acc + xf_ref[SUBLANE - k:SUBLANE - k + n, :] * cw_ref[CONV_WIDTH - 1 - k:CONV_WIDTH - k, :]
    xf_ref[0:SUBLANE, :] = x[n - SUBLANE:, :]
    if bias is not None:
        acc = acc + bias
    return _silu(acc)


U32 = jnp.uint32
PACK_ROWS = D_MODEL // (2 * LANE)


def _pack_rows(x):
    h = x.shape[1] // 2
    xb = x.astype(BF16).astype(F32)
    lo = lax.bitcast_convert_type(xb[:, 0:h], U32) >> 16
    hi = lax.bitcast_convert_type(xb[:, h:2 * h], U32) & U32(0xFFFF0000)
    return lo | hi


def _unpack_rows(w):
    lo = lax.bitcast_convert_type(w << 16, F32)
    hi = lax.bitcast_convert_type(w & U32(0xFFFF0000), F32)
    return jnp.concatenate([lo, hi], axis=1)


def _store_row_tiles(ref, val):
    n = val.shape[0]
    for j in range(PACK_ROWS):
        ref[pl.ds(j, n, stride=PACK_ROWS), :] = val[:, j * LANE:(j + 1) * LANE]


def _load_row_tiles(ref, n):
    return jnp.concatenate([ref[pl.ds(j, n, stride=PACK_ROWS), :] for j in range(PACK_ROWS)], axis=1)


def _lanes_from(sm, off):
    return pltpu.roll(sm, SMALL_W - off, axis=1) if off else sm


def _chunk_scans(g):
    n = g.shape[0]
    rin = lax.broadcasted_iota(jnp.int32, g.shape, 0) & (CHUNK - 1)
    pre = g
    suf = jnp.where(rin < CHUNK - 1, pltpu.roll(g, n - 1, axis=0), 0.0)
    s = 1
    while s < CHUNK:
        pre = pre + jnp.where(rin >= s, pltpu.roll(pre, s, axis=0), 0.0)
        suf = suf + jnp.where(rin + s <= CHUNK - 1, pltpu.roll(suf, n - s, axis=0), 0.0)
        s *= 2
    return pre, suf


def _head_expand():
    m = np.zeros((SMALL_W, GDN_WIDTH), np.float32)
    for h in range(GDN_HEADS):
        m[h, h * GDN_HEAD_DIM:(h + 1) * GDN_HEAD_DIM] = 1.0
    return jnp.asarray(m, BF16)


def _block_ones(width, blk):
    idx = np.arange(width) // blk
    return jnp.asarray((idx[:, None] == idx[None, :]).astype(np.float32), BF16)


def _gdn_kernel(qkv_ref, z_ref, sm_ref, cw_ref, hp_ref, nw_ref, e_ref, xp_ref, bdm_ref, o_ref,
                tail_s, st_s, kn_s, kb_s, qn_s, qd_s, kd_s, u_s, w_s, eg_s, gcx_s, gw_s,
                o_s, au_s, qe_s, ku_s, kw_s, *, TB):
    H, Dh, C = GDN_HEADS, GDN_HEAD_DIM, CHUNK
    W = GDN_WIDTH
    ncb = TB // C

    @pl.when(pl.program_id(1) == 0)
    def _():
        tail_s[0:SUBLANE, :] = jnp.zeros((SUBLANE, tail_s.shape[1]), F32)
        st_s[...] = jnp.zeros_like(st_s)

    xc = _causal_conv_silu(qkv_ref[0], tail_s, cw_ref, None)
    q, k, v = xc[:, 0:W], xc[:, W:2 * W], xc[:, 2 * W:3 * W]
    e = e_ref[...]
    xp = xp_ref[...]
    qn = q * lax.rsqrt(_dot2(q * q, e) + EPS) * (Dh ** -0.5)
    kn = k * lax.rsqrt(_dot2(k * k, e) + EPS)

    sm = sm_ref[0]
    lane = lax.broadcasted_iota(jnp.int32, sm.shape, 1)
    head_lane = lane < H
    g = jnp.where(head_lane, hp_ref[0:1, :] * _softplus(_lanes_from(sm, A_LANE) + hp_ref[1:2, :]), 0.0)
    beta = jnp.where(head_lane, jax.nn.sigmoid(_lanes_from(sm, B_LANE)), 0.0)
    gc, rc = _chunk_scans(g)
    bx = _dot2(beta, xp)
    egx = _dot2(jnp.exp(gc), xp)
    erx = _dot2(jnp.exp(rc), xp)
    kb = kn * bx
    kn_s[...] = kn
    kb_s[...] = kb
    qn_s[...] = qn
    qd_s[...] = qn * egx
    kd_s[...] = kn * erx
    u_s[...] = v * bx
    w_s[...] = kb * egx
    gcx = _dot_sel3(gc, xp)
    gcx_s[...] = gcx
    r384 = lax.broadcasted_iota(jnp.int32, (C, W), 0)
    l384 = lax.broadcasted_iota(jnp.int32, (C, W), 1) & (Dh - 1)
    for c in range(ncb):
        diag_c = jnp.where(r384 == l384, gcx[c * C:(c + 1) * C, :], 0.0)
        gw_s[c] = jnp.broadcast_to(jnp.sum(diag_c, axis=0, keepdims=True), (SUBLANE, W))
        eg_s[c] = jnp.broadcast_to(egx[(c + 1) * C - 1:(c + 1) * C, :], (SUBLANE, W))

    heads = [slice(h * Dh, (h + 1) * Dh) for h in range(H)]

    GL = 4 * Dh
    row_w = lax.broadcasted_iota(jnp.int32, (C, GL), 0)
    col_w = lax.broadcasted_iota(jnp.int32, (C, GL), 1) & (Dh - 1)
    causal_w, strict_w = row_w >= col_w, row_w > col_w
    eye_w = jnp.where(row_w == col_w, 1.0, 0.0).astype(F32)
    bdm = bdm_ref[...]
    mm = functools.partial(lax.dot_general, dimension_numbers=_NN, preferred_element_type=F32)

    def block_diag(m, mask):
        return jnp.concatenate([m] * 4, axis=0) * mask

    npair = min(GDN_PAIRS_PER_ITER, ncb // 2)

    def solve(it, carry):
        chunks = [2 * npair * it + k for k in range(2 * npair)]
        rows = [pl.ds(pl.multiple_of(c * C, C), C) for c in chunks]

        def groups(get, n):
            out = []
            for j in range(npair):
                a0, a1 = get(2 * j), get(2 * j + 1)
                out += [a0[:, 0:4 * n], a1[:, 0:4 * n],
                        jnp.concatenate([a0[:, 4 * n:6 * n], a1[:, 4 * n:6 * n]], axis=1)]
            return out

        def ungroup(ref, vals, n):
            for j in range(npair):
                ref[rows[2 * j], 0:4 * n] = vals[3 * j]
                ref[rows[2 * j + 1], 0:4 * n] = vals[3 * j + 1]
                ref[rows[2 * j], 4 * n:6 * n] = vals[3 * j + 2][:, 0:2 * n]
                ref[rows[2 * j + 1], 4 * n:6 * n] = vals[3 * j + 2][:, 2 * n:4 * n]

        kn_g = groups(lambda k: kn_s[rows[k], :], Dh)
        kb_g = groups(lambda k: kb_s[rows[k], :], Dh)
        qn_g = groups(lambda k: qn_s[rows[k], :], Dh)
        gx_g = groups(lambda k: gcx_s[rows[k], :], Dh)
        gw_g = groups(lambda k: gw_s[chunks[k], 0:1, :], Dh)
        decs = [jnp.exp(jnp.where(causal_w, gx - gw, NEG_BIG)) for gx, gw in zip(gx_g, gw_g)]
        prods = [lax.dot_general(jnp.concatenate([kb, qn], axis=0).astype(BF16), block_diag(kn.astype(BF16), bdm),
                                 _NT, preferred_element_type=F32)
                 for kn, kb, qn in zip(kn_g, kb_g, qn_g)]
        attns = [pr[C:2 * C] * dec for pr, dec in zip(prods, decs)]
        def times(lhs, p=None):
            lh, ll = _split(lhs)
            ph, pl_ = (lh[0:C], ll[0:C]) if p is None else _split(p)
            rh, rl = block_diag(ph, bdm), block_diag(pl_, bdm)
            return mm(lh, rh) + (mm(lh, rl) + mm(ll, rh))

        ps = [-jnp.where(strict_w, pr[0:C] * dec, 0.0) for pr, dec in zip(prods, decs)]
        invs = [eye_w + p for p in ps]
        ps = [times(p) for p in ps]
        for _ in range(4):
            outs = [times(jnp.concatenate([p, inv], axis=0)) for p, inv in zip(ps, invs)]
            invs = [inv + o[C:2 * C] for inv, o in zip(invs, outs)]
            ps = [o[0:C] for o in outs]
        invs = [inv + times(inv, p) for p, inv in zip(ps, invs)]
        def solved(ref):
            outs = []
            for inv, r in zip(invs, groups(lambda k: ref[rows[k], :], Dh)):
                ih, il = _split(inv)
                rh, rl = _split(r)
                rh, rl = block_diag(rh, bdm), block_diag(rl, bdm)
                outs.append(mm(ih, rh) + (mm(ih, rl) + mm(il, rh)))
            return outs

        u_g, w_g = solved(u_s), solved(w_s)
        ungroup(u_s, u_g, Dh)
        ungroup(w_s, w_g, Dh)
        at_g = [a.astype(BF16) for a in attns]
        au_g = [mm(a, block_diag(u.astype(BF16), bdm)) for a, u in zip(at_g, u_g)]
        aw_g = [mm(a, block_diag(w.astype(BF16), bdm)) for a, w in zip(at_g, w_g)]
        qd_g = groups(lambda k: qd_s[rows[k], :], Dh)
        ungroup(au_s, au_g, Dh)
        ungroup(qe_s, [qd - aw for qd, aw in zip(qd_g, aw_g)], Dh)
        for k in range(2 * npair):
            for sl in heads:
                kd_h = kd_s[rows[k], sl]
                ku_s[rows[k], sl] = _dot(kd_h, u_s[rows[k], sl], _TN)
                kw_s[rows[k], sl] = _dot(kd_h, w_s[rows[k], sl], _TN)
        return carry

    lax.fori_loop(0, ncb // (2 * npair), solve, 0)

    def recur(c, carry):
        rows = pl.ds(pl.multiple_of(c * C, C), C)
        st = st_s[...]
        sb = st.astype(BF16)
        lhs = jnp.concatenate([qe_s[rows, :], kw_s[rows, :]], axis=0).astype(BF16)
        out = jnp.concatenate(
            [mm(lhs[:, 0:GL], block_diag(sb[:, 0:GL], bdm)),
             mm(lhs[:, GL:W], jnp.concatenate([sb[:, GL:W]] * 2, axis=0) * bdm[0:W - GL, 0:W - GL])], axis=1)
        o_s[rows, :] = out[0:C] + au_s[rows, :]
        st_s[...] = st * eg_s[c, 0:1, :] + (ku_s[rows, :] - out[C:2 * C])
        return carry

    lax.fori_loop(0, ncb, recur, 0)

    o = o_s[...]
    ms = _dot2(o * o, e) * (1.0 / Dh)
    o = o * lax.rsqrt(ms + EPS) * nw_ref[...]
    o_ref[0] = (o * _silu(z_ref[0])).astype(o_ref.dtype)


def _gdn_call(qkv, z, sm, conv_w, a_log, dt_bias, norm_w):
    B, S, _ = qkv.shape
    TB = min(512, S)
    W, H, Dh = GDN_WIDTH, GDN_HEADS, GDN_HEAD_DIM
    hp = jnp.zeros((SUBLANE, SMALL_W), F32)
    hp = hp.at[0, :H].set(-jnp.exp(a_log.astype(F32))).at[1, :H].set(dt_bias.astype(F32))
    nw = jnp.tile(norm_w.astype(F32), H).reshape(1, W)
    const = lambda shape: pl.BlockSpec(shape, lambda b, i: tuple(0 for _ in shape))
    blk = lambda n: pl.BlockSpec((1, TB, n), lambda b, i: (b, i, 0))
    f = lambda *shape: pltpu.VMEM(shape, F32)
    return pl.pallas_call(
        functools.partial(_gdn_kernel, TB=TB),
        out_shape=jax.ShapeDtypeStruct((B, S, W), BF16),
        grid=(B, S // TB),
        in_specs=[blk(GDN_CONV_DIM), blk(W), blk(SMALL_W),
                  const((CONV_WIDTH, GDN_CONV_DIM)), const((SUBLANE, SMALL_W)), const((1, W)),
                  const((W, W)), const((SMALL_W, W)), const((4 * Dh, 4 * Dh))],
        out_specs=blk(W),
        scratch_shapes=[f(SUBLANE + TB, GDN_CONV_DIM), f(Dh, W)]
        + [f(TB, W)] * 7 + [f(TB // CHUNK, SUBLANE, W), f(TB, W), f(TB // CHUNK, SUBLANE, W)] + [f(TB, W)] * 5,
        compiler_params=_cparams(("arbitrary", "arbitrary")),
        name="gdn_mixer",
    )(qkv, z, sm, conv_w.astype(F32), hp, nw, _block_ones(W, Dh), _head_expand(), _block_ones(4 * Dh, Dh))


def _ssd_kernel(xbc_ref, z_ref, sm_ref, cw_ref, cb_ref, hp_ref, dsk_ref, nw_ref, e_ref, xp_ref,
                hgm_ref, e_hd_ref, ghm_ref, o_ref,
                tail_s, st_s, xs_s, xdt_s, xdd_s, bm_s, cm_s, ea_s, el_s, ac_s, aw_s, y_s, inc_s, *, TB):
    H, P, G, N, C = SSD_HEADS, SSD_HEAD_DIM, SSD_GROUPS, SSD_STATE, CHUNK
    W, GW = SSD_WIDTH, SSD_GROUP_WIDTH
    ncb = TB // C

    @pl.when(pl.program_id(1) == 0)
    def _():
        tail_s[0:SUBLANE, :] = jnp.zeros((SUBLANE, tail_s.shape[1]), F32)
        st_s[...] = jnp.zeros_like(st_s)

    xc = _causal_conv_silu(xbc_ref[0], tail_s, cw_ref, cb_ref[...])
    xs = xc[:, 0:W]
    xp = xp_ref[...]

    sm = sm_ref[0]
    lane = lax.broadcasted_iota(jnp.int32, sm.shape, 1)
    head_lane = lane < H
    dt = jnp.where(head_lane, _softplus(_lanes_from(sm, DT_LANE) + hp_ref[1:2, :]), 0.0)
    acs, rcs = _chunk_scans(dt * hp_ref[0:1, :])
    dtx = _dot2(dt, xp)
    eax = _dot2(jnp.exp(acs), xp)
    erx = _dot2(jnp.exp(rcs), xp)
    xdt = xs * dtx
    xs_s[...] = xs
    xdt_s[...] = xdt
    xdd_s[...] = xdt * erx
    bm_s[...] = xc[:, W:W + G * N]
    cm_s[...] = xc[:, W + G * N:W + 2 * G * N]
    ea_s[...] = eax
    acx = _dot_sel3(acs, xp)
    ac_s[...] = acx
    r_w = lax.broadcasted_iota(jnp.int32, (C, W), 0)
    l_w = lax.broadcasted_iota(jnp.int32, (C, W), 1) & (P - 1)
    for c in range(ncb):
        diag_c = jnp.where(r_w == l_w, acx[c * C:(c + 1) * C, :], 0.0)
        aw_s[c] = jnp.broadcast_to(jnp.sum(diag_c, axis=0, keepdims=True), (SUBLANE, W))
        el_s[c] = jnp.broadcast_to(eax[(c + 1) * C - 1:(c + 1) * C, :], (SUBLANE, W))

    causal_w = r_w >= l_w
    mm = functools.partial(lax.dot_general, dimension_numbers=_NN, preferred_element_type=F32)

    def stacked(m, mask):
        return jnp.concatenate([m] * H, axis=0) * mask

    nloc = min(SSD_CHUNKS_PER_ITER, ncb)

    def local(it, carry):
        cs = [it * nloc + k for k in range(nloc)]
        rows = [pl.ds(pl.multiple_of(c * C, C), C) for c in cs]
        bs = [bm_s[r, :].astype(BF16) for r in rows]
        cbs = [lax.dot_general(cm_s[r, :].astype(BF16), stacked(b, hgm_ref[...]), _NT, preferred_element_type=F32)
               for r, b in zip(rows, bs)]
        segs = [jnp.exp(jnp.where(causal_w, ac_s[r, :] - aw_s[c, 0:1, :], NEG_BIG)) for r, c in zip(rows, cs)]
        for r, cb, seg in zip(rows, cbs, segs):
            y_s[r, :] = mm((cb * seg).astype(BF16), stacked(xdt_s[r, :].astype(BF16), e_hd_ref[...]))
        for r, c, b in zip(rows, cs, bs):
            inc_s[c] = lax.dot_general(b, xdd_s[r, :].astype(BF16), _TN, preferred_element_type=F32) * ghm_ref[...]
        return carry

    lax.fori_loop(0, ncb // nloc, local, 0)

    def recur(c, carry):
        rows = pl.ds(pl.multiple_of(c * C, C), C)
        st = st_s[...]
        y_s[rows, :] = y_s[rows, :] + mm(cm_s[rows, :].astype(BF16), st.astype(BF16)) * ea_s[rows, :]
        st_s[...] = st * el_s[c, 0:1, :] + inc_s[c]
        return carry

    lax.fori_loop(0, ncb, recur, 0)

    y = y_s[...] + dsk_ref[...] * xs_s[...]
    y = y * _silu(z_ref[0])
    ms = _dot2(y * y, e_ref[...]) * (1.0 / GW)
    o_ref[0] = (y * lax.rsqrt(ms + EPS) * nw_ref[...]).astype(o_ref.dtype)


def _ssd_call(xbc, z, sm, conv_w, conv_b, a_log, dt_bias, d_skip, norm_w):
    B, S, _ = xbc.shape
    TB = min(512, S)
    W, H, G, N = SSD_WIDTH, SSD_HEADS, SSD_GROUPS, SSD_STATE
    hp = jnp.zeros((SUBLANE, SMALL_W), F32)
    hp = hp.at[0, :H].set(-jnp.exp(a_log.astype(F32))).at[1, :H].set(dt_bias.astype(F32))
    dsk = jnp.repeat(d_skip.astype(F32), SSD_HEAD_DIM).reshape(1, W)
    const = lambda shape: pl.BlockSpec(shape, lambda b, i: tuple(0 for _ in shape))
    blk = lambda n: pl.BlockSpec((1, TB, n), lambda b, i: (b, i, 0))
    f = lambda *shape: pltpu.VMEM(shape, F32)
    head_of = np.arange(H * CHUNK) // CHUNK
    group_of = np.arange(G * N) // N
    hgm = (head_of[:, None] // (H // G) == group_of[None, :]).astype(np.float32)
    ghm = (group_of[:, None] == (np.arange(W) // SSD_HEAD_DIM // (H // G))[None, :]).astype(np.float32)
    return pl.pallas_call(
        functools.partial(_ssd_kernel, TB=TB),
        out_shape=jax.ShapeDtypeStruct((B, S, W), BF16),
        grid=(B, S // TB),
        in_specs=[blk(SSD_CONV_DIM), blk(W), blk(SMALL_W),
                  const((CONV_WIDTH, SSD_CONV_DIM)), const((1, SSD_CONV_DIM)), const((SUBLANE, SMALL_W)),
                  const((1, W)), const((1, W)), const((W, W)), const((SMALL_W, W)),
                  const((H * CHUNK, G * N)), const((W, W)), const((G * N, W))],
        out_specs=blk(W),
        scratch_shapes=[f(SUBLANE + TB, SSD_CONV_DIM), f(G * N, W),
                        f(TB, W), f(TB, W), f(TB, W), f(TB, G * N), f(TB, G * N), f(TB, W),
                        f(TB // CHUNK, SUBLANE, W),
                        f(TB, W), f(TB // CHUNK, SUBLANE, W), f(TB, W),
                        f(TB // CHUNK, G * N, W)],
        compiler_params=_cparams(("arbitrary", "arbitrary")),
        name="ssd_mixer",
    )(xbc, z, sm, conv_w.astype(F32), conv_b.reshape(1, -1).astype(F32), hp, dsk,
      norm_w.reshape(1, W).astype(F32), _block_ones(W, SSD_GROUP_WIDTH), _head_expand(),
      jnp.asarray(hgm, BF16), _block_ones(W, SSD_HEAD_DIM), jnp.asarray(ghm, F32))


def _out_kernel(x_ref, y1_ref, y2_ref, y3_ref, mod_ref, nw_ref, wo_ref, wr_ref, br_ref, tri_ref,
                x1_ref, h_ref, rt_ref, rtt_ref, cnt_ref, run_s, wo_s):
    first = (pl.program_id(0) == 0) & (pl.program_id(1) == 0)

    @pl.when(first)
    def _():
        wo_s[...] = wo_ref[0].astype(BF16)

    y = (jnp.dot(y1_ref[0], wo_s[0:S5_WIDTH, :], preferred_element_type=F32)
         + jnp.dot(y2_ref[0], wo_s[S5_WIDTH:S5_WIDTH + GDN_WIDTH, :], preferred_element_type=F32)
         + jnp.dot(y3_ref[0], wo_s[S5_WIDTH + GDN_WIDTH:, :], preferred_element_type=F32))
    x1 = x_ref[0] + mod_ref[0, 2:3, :] * y
    x1_ref[0] = x1
    h = _norm_mod(x1, nw_ref[...], mod_ref[0, 4:5, :], mod_ref[0, 3:4, :])
    _store_row_tiles(h_ref, _pack_rows(h))
    lg =jnp.dot(h.astype(BF16), wr_ref[...], preferred_element_type=F32) + br_ref[...]

    lane = lax.broadcasted_iota(jnp.int32, lg.shape, 1)
    lanef = lane.astype(F32)
    big = float(4 * LANE)
    grp = (lane >= GRP_LANE) & (lane < GRP_LANE + MOE_GROUPS)
    lgm = jnp.where(grp, lg, -jnp.inf)
    m = jnp.max(lgm, axis=-1, keepdims=True)
    gidx = jnp.min(jnp.where(lgm == m, lanef - GRP_LANE, big), axis=-1, keepdims=True)
    g_w = 1.0 / jnp.sum(jnp.where(grp, jnp.exp(lg - m), 0.0), axis=-1, keepdims=True)
    in_grp = (lane < N_EXPERTS) & ((lane // EXPERTS_PER_GROUP).astype(F32) == gidx)
    le = jnp.where(in_grp, lg, -jnp.inf)
    v1 = jnp.max(le, axis=-1, keepdims=True)
    i1 = jnp.min(jnp.where(le == v1, lanef, big), axis=-1, keepdims=True)
    le2 = jnp.where(lanef == i1, -jnp.inf, le)
    v2 = jnp.max(le2, axis=-1, keepdims=True)
    i2 = jnp.min(jnp.where(le2 == v2, lanef, big), axis=-1, keepdims=True)
    e2 = jnp.exp(v2 - v1)
    w1 = g_w / (1.0 + e2)
    w2 = g_w * e2 / (1.0 + e2)

    @pl.when(first)
    def _():
        run_s[...] = jnp.zeros_like(run_s)

    chosen = jnp.where((lanef == i1) | (lanef == i2), 1.0, 0.0)
    before = jnp.dot(tri_ref[...], chosen.astype(BF16), preferred_element_type=F32) + run_s[0:1, :]
    p1 = jnp.sum(jnp.where(lanef == i1, before, 0.0), axis=-1, keepdims=True)
    p2 = jnp.sum(jnp.where(lanef == i2, before, 0.0), axis=-1, keepdims=True)
    run_s[...] = run_s[...] + jnp.sum(chosen, axis=0, keepdims=True)
    cnt_ref[...] = run_s[...]

    rt = jnp.zeros_like(lg)
    for k, val in enumerate((i1, i2, w1, w2, p1, p2)):
        rt = jnp.where(lane == k, val, rt)
    rt_ref[0] = rt
    rtt_ref[0] = rt.T[0:SUBLANE, :]


def _out_call(x, y1, y2, y3, mod_l, norm_w, w_out, w_router, b_router, l):
    B, S, D = x.shape
    tm = min(512, S)
    blk = lambda n: pl.BlockSpec((1, tm, n), lambda b, i: (b, i, 0))
    const = lambda shape: pl.BlockSpec(shape, lambda b, i: tuple(0 for _ in shape))
    tri = jnp.asarray(np.tril(np.ones((tm, tm), np.float32), -1), BF16)
    nb = S // tm
    tiles = PACK_ROWS
    return pl.pallas_call(
        _out_kernel,
        out_shape=[jax.ShapeDtypeStruct((B, S, D), F32), jax.ShapeDtypeStruct((B * S * tiles, LANE), U32),
                   jax.ShapeDtypeStruct((B, S, LANE), F32), jax.ShapeDtypeStruct((B * nb, SUBLANE, tm), F32),
                   jax.ShapeDtypeStruct((SUBLANE, LANE), F32)],
        grid=(B, nb),
        in_specs=[blk(D), blk(S5_WIDTH), blk(GDN_WIDTH), blk(SSD_WIDTH),
                  pl.BlockSpec((1, 6, D), lambda b, i: (b, 0, 0)),
                  const((1, D)), pl.BlockSpec((1, D, D), lambda b, i: (l, 0, 0)),
                  const((D, LANE)), const((1, LANE)), const((tm, tm))],
        out_specs=[blk(D), pl.BlockSpec((tm * tiles, LANE), lambda b, i: (b * nb + i, 0)),
                   blk(LANE), pl.BlockSpec((1, SUBLANE, tm), lambda b, i: (b * nb + i, 0, 0)),
                   const((SUBLANE, LANE))],
        scratch_shapes=[pltpu.VMEM((SUBLANE, LANE), F32), pltpu.VMEM((D, D), BF16)],
        compiler_params=_cparams(("arbitrary", "arbitrary")),
        name="out_proj_router",
    )(x, y1, y2, y3, mod_l, norm_w.reshape(1, D), w_out, w_router, b_router, tri)


def _router_params(w_grp, b_grp, w_rt, b_rt):
    D = w_grp.shape[0]
    w = jnp.zeros((D, LANE), F32).at[:, 0:N_EXPERTS].set(w_rt).at[:, GRP_LANE:GRP_LANE + MOE_GROUPS].set(w_grp)
    b = jnp.zeros((1, LANE), F32).at[0, 0:N_EXPERTS].set(b_rt).at[0, GRP_LANE:GRP_LANE + MOE_GROUPS].set(b_grp)
    return w.astype(BF16), b


def _tile_copy_loop(n, fn):
    def body(t, carry):
        fn(t)
        return carry
    lax.fori_loop(0, n, body, 0, unroll=32)


def _scatter_kernel(d0_ref, d1_ref, h_ref, xs_hbm, stage, sem):
    i = pl.program_id(0)
    n = pl.num_programs(0)
    slot = lax.rem(i, 2)
    tm = d0_ref.shape[-1]

    def copy(s, t, dst):
        return pltpu.make_async_copy(stage.at[s, pl.ds(t * PACK_ROWS, PACK_ROWS)],
                                     xs_hbm.at[pl.ds(dst * PACK_ROWS, PACK_ROWS)], sem.at[s])

    def wait_slot(s):
        _tile_copy_loop(TOP_K * tm, lambda t: copy(s, 0, 0).wait())

    @pl.when(i >= 2)
    def _():
        wait_slot(slot)

    stage[slot] = h_ref[...]

    def start(t):
        copy(slot, t, d0_ref[0, 0, t]).start(priority=0)
        copy(slot, t, d1_ref[0, 0, t]).start(priority=1)
    _tile_copy_loop(tm, start)

    @pl.when(i == n - 1)
    def _():
        @pl.when(i >= 1)
        def _():
            wait_slot(1 - slot)
        wait_slot(slot)


def _scatter_call(h2t, dest0, dest1, n_rows):
    n_tiles, _, tm = dest0.shape
    idx_blk = pl.BlockSpec((1, 1, tm), lambda i: (i, 0, 0), memory_space=pltpu.SMEM)
    return pl.pallas_call(
        _scatter_kernel,
        out_shape=jax.ShapeDtypeStruct((n_rows * PACK_ROWS, LANE), U32),
        grid=(n_tiles,),
        in_specs=[idx_blk, idx_blk, pl.BlockSpec((tm * PACK_ROWS, LANE), lambda i: (i, 0))],
        out_specs=pl.BlockSpec(memory_space=pl.ANY),
        scratch_shapes=[pltpu.VMEM((2, tm * PACK_ROWS, LANE), U32), pltpu.SemaphoreType.DMA((2,))],
        compiler_params=_cparams(("arbitrary",)),
        name="moe_scatter",
    )(dest0, dest1, h2t)


def _expert_kernel(nused_ref, blke_ref, nvalid_ref, xs_ref, wg_ref, wu_ref, wd_ref, ys_ref, wg_s, wu_s, wd_s):
    i = pl.program_id(0)

    @pl.when(i < nused_ref[0])
    def _():
        @pl.when((i == 0) | (blke_ref[i] != blke_ref[jnp.maximum(i - 1, 0)]))
        def _():
            wg_s[...] = wg_ref[0, 0].astype(BF16)
            wu_s[...] = wu_ref[0, 0].astype(BF16)
            wd_s[...] = wd_ref[0, 0].astype(BF16)

        row = lax.broadcasted_iota(jnp.int32, (MOE_ROWS, 1), 0)
        words = jnp.where(row < nvalid_ref[i], _load_row_tiles(xs_ref, MOE_ROWS), U32(0))
        xb = _unpack_rows(words).astype(BF16)
        hid = _silu(jnp.dot(xb, wg_s[...], preferred_element_type=F32)) * jnp.dot(xb, wu_s[...], preferred_element_type=F32)
        _store_row_tiles(ys_ref, _pack_rows(jnp.dot(hid.astype(BF16), wd_s[...], preferred_element_type=F32)))

    @pl.when(i >= nused_ref[0])
    def _():
        ys_ref[...] = jnp.zeros_like(ys_ref)


def _expert_call(xs, n_used, blk_e, n_valid, w_gate, w_up, w_down, l):
    D = w_gate.shape[2]
    n_blk = blk_e.shape[0]
    rows_blk = pl.BlockSpec((MOE_ROWS * PACK_ROWS, LANE), lambda i, nu, be, nv: (i, 0))
    grid_spec = pltpu.PrefetchScalarGridSpec(
        num_scalar_prefetch=3,
        grid=(n_blk,),
        in_specs=[rows_blk,
                  pl.BlockSpec((1, 1, D, D_EXPERT), lambda i, nu, be, nv: (l, be[i], 0, 0)),
                  pl.BlockSpec((1, 1, D, D_EXPERT), lambda i, nu, be, nv: (l, be[i], 0, 0)),
                  pl.BlockSpec((1, 1, D_EXPERT, D), lambda i, nu, be, nv: (l, be[i], 0, 0))],
        out_specs=rows_blk,
        scratch_shapes=[pltpu.VMEM((D, D_EXPERT), BF16), pltpu.VMEM((D, D_EXPERT), BF16),
                        pltpu.VMEM((D_EXPERT, D), BF16)],
    )
    return pl.pallas_call(
        _expert_kernel,
        out_shape=jax.ShapeDtypeStruct(xs.shape, U32),
        grid_spec=grid_spec,
        compiler_params=_cparams(("arbitrary",)),
        name="expert_mlp",
    )(n_used, blk_e, n_valid, xs, w_gate, w_up, w_down)


def _dest_kernel(rt_ref, ps_ref, o_ref):
    n_sub, _, tm = o_ref.shape
    expert = lax.broadcasted_iota(jnp.int32, (N_EXPERTS, tm), 0).astype(F32)
    starts = ps_ref[0:N_EXPERTS, :]
    for j in range(n_sub):
        rows = []
        for k in range(TOP_K):
            start = jnp.sum(jnp.where(expert == rt_ref[j, k:k + 1, :], starts, 0.0), axis=0, keepdims=True)
            rows.append(start + rt_ref[j, 4 + k:5 + k, :])
        rows.append(jnp.zeros((SUBLANE - TOP_K, tm), F32))
        o_ref[j] = jnp.concatenate(rows, axis=0).astype(jnp.int32)


def _dest_call(route_t, pstart_col):
    n_tiles, _, tm = route_t.shape
    n_sub = min(DEST_TILES_PER_STEP, n_tiles)
    blk = pl.BlockSpec((n_sub, SUBLANE, tm), lambda i: (i, 0, 0))
    return pl.pallas_call(
        _dest_kernel,
        out_shape=jax.ShapeDtypeStruct((n_tiles, SUBLANE, tm), jnp.int32),
        grid=(n_tiles // n_sub,),
        in_specs=[blk, pl.BlockSpec((LANE, 1), lambda i: (0, 0))],
        out_specs=blk,
        compiler_params=_cparams(("arbitrary",)),
        name="moe_dest",
    )(route_t, pstart_col)


def _dispatch(route_t, counts, N, tm):
    L_pad = N * TOP_K + N_EXPERTS * MOE_ROWS
    n_blk = L_pad // MOE_ROWS
    counts = counts.astype(jnp.int32)
    padded = ((counts + MOE_ROWS - 1) // MOE_ROWS) * MOE_ROWS
    pend = jnp.cumsum(padded)
    pstart = pend - padded
    n_used = (pend[-1] // MOE_ROWS).astype(jnp.int32).reshape(1)
    blk_row0 = jnp.arange(n_blk, dtype=jnp.int32) * MOE_ROWS
    blk_e = jnp.minimum(jnp.sum((pend[None, :] <= blk_row0[:, None]).astype(jnp.int32), axis=1), N_EXPERTS - 1)
    n_valid = jnp.clip((pstart + counts)[blk_e] - blk_row0, 0, MOE_ROWS).astype(jnp.int32)
    pstart_col = jnp.zeros((LANE, 1), F32).at[0:N_EXPERTS, 0].set(pstart.astype(F32))
    dest = _dest_call(route_t, pstart_col)
    dest0 = dest[:, 0, :].reshape(N // tm, 1, tm)
    dest1 = dest[:, 1, :].reshape(N // tm, 1, tm)
    return n_used, blk_e.astype(jnp.int32), n_valid, dest0, dest1, L_pad


def _combine_kernel(d0_ref, d1_ref, d0n_ref, d1n_ref, x_ref, rt_ref, mod_ref, nf_ref, ys_hbm, o_ref,
                    buf, sem, *, final):
    i = pl.program_id(0)
    n = pl.num_programs(0)
    slot = lax.rem(i, 2)
    tm = x_ref.shape[0]

    def copy(s, k, t, src):
        return pltpu.make_async_copy(ys_hbm.at[pl.ds(src * PACK_ROWS, PACK_ROWS)],
                                     buf.at[s, k, pl.ds(t * PACK_ROWS, PACK_ROWS)], sem.at[s])

    def start_tile(s, a_ref, b_ref):
        def start(t):
            copy(s, 0, t, a_ref[0, 0, t]).start(priority=0)
            copy(s, 1, t, b_ref[0, 0, t]).start(priority=1)
        _tile_copy_loop(tm, start)

    @pl.when(i == 0)
    def _():
        start_tile(0, d0_ref, d1_ref)

    @pl.when(i + 1 < n)
    def _():
        start_tile(1 - slot, d0n_ref, d1n_ref)

    _tile_copy_loop(TOP_K * tm, lambda t: copy(slot, 0, 0, 0).wait())
    rt = rt_ref[...]
    y = (rt[:, 2:3] * _unpack_rows(_load_row_tiles(buf.at[slot, 0], tm))
         + rt[:, 3:4] * _unpack_rows(_load_row_tiles(buf.at[slot, 1], tm)))
    x2 = x_ref[...] + mod_ref[0, 5:6, :] * y
    if final:
        ms = jnp.mean(x2 * x2, axis=-1, keepdims=True)
        x2 = x2 * lax.rsqrt(ms + EPS) * nf_ref[...]
    o_ref[...] = x2


def _combine_call(x1, ys, route, dest0, dest1, mod_l, norm_final, final):
    B, S, D = x1.shape
    N = B * S
    n_tiles, _, tm = dest0.shape
    per_b = S // tm
    idx_blk = lambda fn: pl.BlockSpec((1, 1, tm), fn, memory_space=pltpu.SMEM)
    cur = lambda i: (i, 0, 0)
    nxt = lambda i: (jnp.minimum(i + 1, n_tiles - 1), 0, 0)
    out = pl.pallas_call(
        functools.partial(_combine_kernel, final=final),
        out_shape=jax.ShapeDtypeStruct((N, D), F32),
        grid=(n_tiles,),
        in_specs=[idx_blk(cur), idx_blk(cur), idx_blk(nxt), idx_blk(nxt),
                  pl.BlockSpec((tm, D), lambda i: (i, 0)),
                  pl.BlockSpec((tm, LANE), lambda i: (i, 0)),
                  pl.BlockSpec((1, 6, D), lambda i: (i // per_b, 0, 0)),
                  pl.BlockSpec((1, D), lambda i: (0, 0)),
                  pl.BlockSpec(memory_space=pl.ANY)],
        out_specs=pl.BlockSpec((tm, D), lambda i: (i, 0)),
        scratch_shapes=[pltpu.VMEM((2, TOP_K, tm * PACK_ROWS, LANE), U32), pltpu.SemaphoreType.DMA((2,))],
        compiler_params=_cparams(("arbitrary",)),
        name="moe_combine",
    )(dest0, dest1, dest0, dest1, x1.reshape(N, D), route.reshape(N, LANE), mod_l, norm_final.reshape(1, D), ys)
    return out.reshape(B, S, D)


def _layer(x, mod_l, p, big, l, final, norm_final):
    B, S, D = x.shape
    N = B * S
    s5_u, g_qkv, g_z, s_z, s_xbc, small = _proj_call(x, mod_l, p["norm_mix"], big["w_in"], l)
    bm, cm, lam_rows = _s5_params(p["s5_a_re"], p["s5_a_im"], p["s5_b_re"], p["s5_b_im"],
                                  p["s5_c_re"], p["s5_c_im"], p["s5_log_dt"])
    y_s5 = _s5_call(s5_u, bm, cm, lam_rows, p["s5_d"], p["s5_w_glu"], p["s5_norm"])
    y_gdn = _gdn_call(g_qkv, g_z, small, p["gdn_conv_w"], p["gdn_a_log"], p["gdn_dt_bias"], p["gdn_norm"])
    y_ssd = _ssd_call(s_xbc, s_z, small, p["ssd_conv_w"], p["ssd_conv_b"], p["ssd_a_log"], p["ssd_dt_bias"],
                      p["ssd_d"], p["ssd_norm"])
    w_router, b_router = _router_params(p["moe_w_grp"], p["moe_b_grp"], p["moe_w_rt"], p["moe_b_rt"])
    x1, h2, route, route_t, counts = _out_call(x, y_s5, y_gdn, y_ssd, mod_l, p["norm_ffn"], big["w_out"],
                                               w_router, b_router, l)
    n_used, blk_e, n_valid, dest0, dest1, n_rows = _dispatch(route_t, counts[0, 0:N_EXPERTS], N,
                                                             min(MOE_TOKEN_TILE, S))
    xs = _scatter_call(h2, dest0, dest1, n_rows)
    ys = _expert_call(xs, n_used, blk_e, n_valid, big["moe_w_gate"], big["moe_w_up"], big["moe_w_down"], l)
    return _combine_call(x1, ys, route, dest0, dest1, mod_l, norm_final, final)


def kernel(x, c, w_ada, b_ada, norm_mix, norm_ffn, w_in, w_out, s5_a_re, s5_a_im, s5_b_re, s5_b_im, s5_c_re, s5_c_im, s5_d, s5_log_dt, s5_w_glu, s5_norm, gdn_conv_w, gdn_a_log, gdn_dt_bias, gdn_norm, ssd_conv_w, ssd_conv_b, ssd_a_log, ssd_dt_bias, ssd_d, ssd_norm, moe_w_grp, moe_b_grp, moe_w_rt, moe_b_rt, moe_w_gate, moe_w_up, moe_w_down, norm_final):
    stacked = dict(norm_mix=norm_mix, norm_ffn=norm_ffn, s5_a_re=s5_a_re, s5_a_im=s5_a_im,
                   s5_b_re=s5_b_re, s5_b_im=s5_b_im, s5_c_re=s5_c_re, s5_c_im=s5_c_im, s5_d=s5_d,
                   s5_log_dt=s5_log_dt, s5_w_glu=s5_w_glu, s5_norm=s5_norm, gdn_conv_w=gdn_conv_w,
                   gdn_a_log=gdn_a_log, gdn_dt_bias=gdn_dt_bias, gdn_norm=gdn_norm, ssd_conv_w=ssd_conv_w,
                   ssd_conv_b=ssd_conv_b, ssd_a_log=ssd_a_log, ssd_dt_bias=ssd_dt_bias, ssd_d=ssd_d,
                   ssd_norm=ssd_norm, moe_w_grp=moe_w_grp, moe_b_grp=moe_b_grp, moe_w_rt=moe_w_rt,
                   moe_b_rt=moe_b_rt)
    big = dict(w_in=_w_in_prep_call(w_in), w_out=w_out, moe_w_gate=moe_w_gate, moe_w_up=moe_w_up,
               moe_w_down=moe_w_down)
    L = w_in.shape[0]
    B, S, D = x.shape
    mod = _mod_call(c, w_ada, b_ada).reshape(L, B, 6, D)
    for l in range(L):
        p = {k: v[l] for k, v in stacked.items()}
        x = _layer(x, mod[l], p, big, l, l == L - 1, norm_final)
    return x
```

```python
import functools

import numpy as np
import jax
import jax.numpy as jnp
from jax import lax
from jax.experimental import pallas as pl
from jax.experimental.pallas import tpu as pltpu

F32 = jnp.float32
BF16 = jnp.bfloat16

D_MODEL = 1024
DEPTH = 4
EPS = 1e-6
CONV_WIDTH = 4
CHUNK = 64
S5_WIDTH = 256
S5_CH = 16
S5_GROUPS = 16
S5_STATE = 64
S5_LANES = S5_GROUPS * S5_STATE
GDN_WIDTH = 384
GDN_HEAD_DIM = 64
GDN_HEADS = 6
GDN_CONV_DIM = 3 * GDN_WIDTH
SSD_WIDTH = 384
SSD_HEAD_DIM = 64
SSD_HEADS = 6
SSD_GROUPS = 2
SSD_STATE = 128
SSD_GROUP_WIDTH = SSD_WIDTH // SSD_GROUPS
SSD_CONV_DIM = SSD_WIDTH + 2 * SSD_GROUPS * SSD_STATE
PROJ_SIZES = (S5_WIDTH, GDN_CONV_DIM, GDN_WIDTH, GDN_HEADS, GDN_HEADS, SSD_WIDTH, SSD_CONV_DIM, SSD_HEADS)
MOE_GROUPS = 4
EXPERTS_PER_GROUP = 8
N_EXPERTS = 32
TOP_K = 2
D_EXPERT = 256

LANE = 128
SUBLANE = 8
SMALL_W = LANE
A_LANE, B_LANE, DT_LANE = 0, GDN_HEADS, 2 * GDN_HEADS
GRP_LANE = N_EXPERTS
ROUTER_ROWS = 40
MOE_ROWS = 512
MOE_TOKEN_TILE = 512
DEST_TILES_PER_STEP = 8
SSD_CHUNKS_PER_ITER = 4
GDN_PAIRS_PER_ITER = 4
VMEM_LIMIT = 56 * 1024 * 1024
NEG_BIG = -1e30


def _cparams(sem):
    return pltpu.CompilerParams(dimension_semantics=sem, vmem_limit_bytes=VMEM_LIMIT)


def _split(a):
    hi = a.astype(BF16)
    lo = (a - hi.astype(F32)).astype(BF16)
    return hi, lo


_NN = (((1,), (0,)), ((), ()))
_NT = (((1,), (1,)), ((), ()))
_TN = (((0,), (0,)), ((), ()))


def _dot(a, b, dims=_NN):
    return lax.dot_general(a.astype(BF16), b.astype(BF16), dims, preferred_element_type=F32)


def _dot2(a, b_bf16):
    hi, lo = _split(a)
    return (lax.dot_general(hi, b_bf16, _NN, preferred_element_type=F32)
            + lax.dot_general(lo, b_bf16, _NN, preferred_element_type=F32))


def _dot_sel3(a, b_bf16):
    h1 = a.astype(BF16)
    r1 = a - h1.astype(F32)
    h2 = r1.astype(BF16)
    h3 = (r1 - h2.astype(F32)).astype(BF16)
    d = functools.partial(lax.dot_general, dimension_numbers=_NN, preferred_element_type=F32)
    return d(h1, b_bf16) + (d(h2, b_bf16) + d(h3, b_bf16))


def _dot3(a, b):
    ah, al = _split(a)
    bh, bl = _split(b)
    d = functools.partial(lax.dot_general, dimension_numbers=_NN, preferred_element_type=F32)
    return d(ah, bh) + (d(ah, bl) + d(al, bh))


def _silu(x):
    return x * jax.nn.sigmoid(x)


def _softplus(x):
    return jnp.maximum(x, 0.0) + jnp.log(1.0 + jnp.exp(-jnp.abs(x)))


def _norm_mod(x, w, scale, shift):
    ms = jnp.mean(x * x, axis=-1, keepdims=True)
    return (x * lax.rsqrt(ms + EPS) * w) * (1.0 + scale) + shift


def _mod_kernel(c_ref, w_ref, b_ref, o_ref):
    cond = _silu(c_ref[...])
    o_ref[0] = _dot3(cond, w_ref[0]) + b_ref[0]


def _mod_call(c, w_ada, b_ada):
    L, D, W = w_ada.shape
    B = c.shape[0]
    tn = 1536
    return pl.pallas_call(
        _mod_kernel,
        out_shape=jax.ShapeDtypeStruct((L, B, W), F32),
        grid=(L, W // tn),
        in_specs=[pl.BlockSpec((B, D), lambda l, j: (0, 0)),
                  pl.BlockSpec((1, D, tn), lambda l, j: (l, 0, j)),
                  pl.BlockSpec((1, 1, tn), lambda l, j: (l, 0, j))],
        out_specs=pl.BlockSpec((1, B, tn), lambda l, j: (l, 0, j)),
        compiler_params=_cparams(("arbitrary", "arbitrary")),
        name="adaln_mod",
    )(c, w_ada, b_ada.reshape(L, 1, W))


PROJ_OUT_W = (S5_WIDTH, GDN_CONV_DIM, GDN_WIDTH, SSD_WIDTH, SSD_CONV_DIM, SMALL_W)


def _proj_kernel(x_ref, mod_ref, nw_ref, w_ref, *o_refs):
    x = x_ref[0]
    h = _norm_mod(x, nw_ref[...], mod_ref[0, 1:2, :], mod_ref[0, 0:1, :]).astype(BF16)
    proj = jnp.dot(h, w_ref[0], preferred_element_type=F32)
    off = 0
    for o_ref in o_refs:
        n = o_ref.shape[-1]
        o_ref[0] = proj[:, off:off + n]
        off += n


_W_HEAD = S5_WIDTH + GDN_CONV_DIM + GDN_WIDTH
_W_AB = _W_HEAD + 2 * GDN_HEADS
_W_SSD = _W_AB + SSD_WIDTH + SSD_CONV_DIM
_W_END = _W_SSD + SSD_HEADS
PROJ_W = sum(PROJ_OUT_W)
W_PREP_ROWS = 256


def _w_in_prep_kernel(w_ref, o_ref):
    n_ssd = _W_SSD - _W_AB
    for r in range(0, w_ref.shape[1], W_PREP_ROWS):
        w = w_ref[0, r:r + W_PREP_ROWS, :]
        rows = slice(r, r + W_PREP_ROWS)
        o_ref[0, rows, 0:_W_HEAD] = w[:, 0:_W_HEAD].astype(BF16)
        o_ref[0, rows, _W_HEAD:_W_HEAD + n_ssd] = w[:, _W_AB:_W_SSD].astype(BF16)
        small = jnp.concatenate([w[:, _W_HEAD:_W_AB], w[:, _W_SSD:_W_END],
                                 jnp.zeros((W_PREP_ROWS, SMALL_W - (_W_AB - _W_HEAD) - (_W_END - _W_SSD)), F32)], axis=1)
        o_ref[0, rows, _W_HEAD + n_ssd:PROJ_W] = small.astype(BF16)


def _w_in_prep_call(w_in):
    L, D, W = w_in.shape
    return pl.pallas_call(
        _w_in_prep_kernel,
        out_shape=jax.ShapeDtypeStruct((L, D, PROJ_W), BF16),
        grid=(L,),
        in_specs=[pl.BlockSpec((1, D, W), lambda l: (l, 0, 0))],
        out_specs=pl.BlockSpec((1, D, PROJ_W), lambda l: (l, 0, 0)),
        compiler_params=_cparams(("arbitrary",)),
        name="w_in_prep",
    )(w_in)


def _proj_call(x, mod_l, norm_w, w_arr, l):
    B, S, D = x.shape
    tm = min(512, S)
    out_shape = [jax.ShapeDtypeStruct((B, S, n), F32) for n in PROJ_OUT_W]
    out_specs = [pl.BlockSpec((1, tm, n), lambda b, i: (b, i, 0)) for n in PROJ_OUT_W]
    return pl.pallas_call(
        _proj_kernel,
        out_shape=out_shape,
        grid=(B, S // tm),
        in_specs=[pl.BlockSpec((1, tm, D), lambda b, i: (b, i, 0)),
                  pl.BlockSpec((1, 6, D), lambda b, i: (b, 0, 0)),
                  pl.BlockSpec((1, D), lambda b, i: (0, 0)),
                  pl.BlockSpec((1, D, PROJ_W), lambda b, i: (l, 0, 0))],
        out_specs=out_specs,
        compiler_params=_cparams(("arbitrary", "arbitrary")),
        name="norm_in_proj",
    )(x, mod_l, norm_w.reshape(1, D), w_arr)


def _s5_kernel(u_ref, bm_ref, cm_ref, lam_ref, dsk_ref, glu_ref, nw_ref, perm_ref, o_ref,
               x_s, st_s, ubt_s, u_s, *, T):
    B = u_ref.shape[0]
    P = S5_LANES

    @pl.when(pl.program_id(0) == 0)
    def _():
        st_s[...] = jnp.zeros_like(st_s)

    nw = S5_WIDTH // LANE
    ubt = u_ref[...].reshape(B * T, S5_WIDTH)
    for j in range(nw):
        ubt_s[j] = ubt[:, j * LANE:(j + 1) * LANE]

    def regroup(t, carry):
        for j in range(nw):
            u_s[pl.ds(pl.multiple_of(t * B, B), B), j * LANE:(j + 1) * LANE] = ubt_s[j, pl.ds(t, B, stride=T), :]
        return carry

    lax.fori_loop(0, T, regroup, 0, unroll=8)
    u = u_s[...]
    x_s[...] = jnp.dot(u.astype(BF16), bm_ref[...], preferred_element_type=F32)
    lr = jnp.broadcast_to(lam_ref[0:1, :], (B, P))
    li = jnp.broadcast_to(lam_ref[1:2, :], (B, P))

    def step(t, carry):
        sr, si = carry
        rows = pl.ds(pl.multiple_of(t * B, B), B)
        nr = lr * sr - li * si + x_s[rows, 0:P]
        ni = lr * si + li * sr + x_s[rows, P:2 * P]
        x_s[rows, 0:P] = nr
        x_s[rows, P:2 * P] = ni
        return nr, ni

    sr, si = lax.fori_loop(0, T, step, (st_s[:, 0:P], st_s[:, P:2 * P]), unroll=4)
    st_s[:, 0:P] = sr
    st_s[:, P:2 * P] = si

    y = jnp.dot(x_s[...].astype(BF16), cm_ref[...], preferred_element_type=F32) + dsk_ref[...] * u
    y = jax.nn.gelu(y)
    y = y * jax.nn.sigmoid(jnp.dot(y.astype(BF16), glu_ref[...], preferred_element_type=F32))
    ms = jnp.mean(y * y, axis=-1, keepdims=True)
    y = (y * lax.rsqrt(ms + EPS) * nw_ref[...]).astype(BF16)
    y = jnp.dot(perm_ref[...], y, preferred_element_type=F32)
    o_ref[...] = y.reshape(B, T, S5_WIDTH).astype(o_ref.dtype)


def _s5_params(a_re, a_im, b_re, b_im, c_re, c_im, log_dt):
    G, P, CH = S5_GROUPS, S5_STATE, S5_CH
    lam = lax.complex(a_re.astype(F32), a_im.astype(F32))
    step = jnp.exp(log_dt.astype(F32))[:, None]
    lam_bar = jnp.exp(lam * step)
    b_bar = ((lam_bar - 1.0) / lam)[..., None] * lax.complex(b_re.astype(F32), b_im.astype(F32))
    eye = jnp.eye(G, dtype=F32)
    bre = jnp.einsum('gpc,gh->gchp', b_bar.real, eye).reshape(G * CH, G * P)
    bim = jnp.einsum('gpc,gh->gchp', b_bar.imag, eye).reshape(G * CH, G * P)
    bm = jnp.concatenate([bre, bim], axis=1).astype(BF16)
    cre = jnp.einsum('gcp,gh->gphc', c_re.astype(F32), eye).reshape(G * P, G * CH)
    cim = jnp.einsum('gcp,gh->gphc', c_im.astype(F32), eye).reshape(G * P, G * CH)
    cm = jnp.concatenate([cre, -cim], axis=0).astype(BF16)
    lam_rows = jnp.zeros((SUBLANE, G * P), F32)
    lam_rows = lam_rows.at[0].set(lam_bar.real.reshape(-1)).at[1].set(lam_bar.imag.reshape(-1))
    return bm, cm, lam_rows


def _s5_call(u, bm, cm, lam_rows, d_skip, w_glu, norm_w):
    B, S, W = u.shape
    T = min(128, S)
    P2 = 2 * S5_LANES
    const = lambda shape: pl.BlockSpec(shape, lambda i: tuple(0 for _ in shape))
    r = np.arange(B * T)
    perm = np.zeros((B * T, B * T), np.float32)
    perm[r, (r % T) * B + r // T] = 1.0
    return pl.pallas_call(
        functools.partial(_s5_kernel, T=T),
        out_shape=jax.ShapeDtypeStruct((B, S, W), BF16),
        grid=(S // T,),
        in_specs=[pl.BlockSpec((B, T, W), lambda i: (0, i, 0)),
                  const((W, P2)), const((P2, W)), const((SUBLANE, S5_LANES)),
                  const((1, W)), const((W, W)), const((1, W)), const((B * T, B * T))],
        out_specs=pl.BlockSpec((B, T, W), lambda i: (0, i, 0)),
        scratch_shapes=[pltpu.VMEM((B * T, P2), F32), pltpu.VMEM((B, P2), F32),
                        pltpu.VMEM((W // LANE, B * T, LANE), F32), pltpu.VMEM((B * T, W), F32)],
        compiler_params=_cparams(("arbitrary",)),
        name="s5_mixer",
    )(u, bm, cm, lam_rows, d_skip.reshape(1, W).astype(F32), w_glu.astype(BF16), norm_w.reshape(1, W).astype(F32),
      jnp.asarray(perm, BF16))


def _causal_conv_silu(x, xf_ref, cw_ref, bias):
    n = x.shape[0]
    xf_ref[SUBLANE:, :] = x
    acc = x * cw_ref[CONV_WIDTH - 1:CONV_WIDTH, :]
    for k in range(1, CONV_WIDTH):
        acc = acc + xf_ref[SUBLANE - k:SUBLANE - k + n, :] * cw_ref[CONV_WIDTH - 1 - k:CONV_WIDTH - k, :]
    xf_ref[0:SUBLANE, :] = x[n - SUBLANE:, :]
    if bias is not None:
        acc = acc + bias
    return _silu(acc)


U32 = jnp.uint32
PACK_ROWS = D_MODEL // (2 * LANE)


def _pack_rows(x):
    h = x.shape[1] // 2
    xb = x.astype(BF16).astype(F32)
    lo = lax.bitcast_convert_type(xb[:, 0:h], U32) >> 16
    hi = lax.bitcast_convert_type(xb[:, h:2 * h], U32) & U32(0xFFFF0000)
    return lo | hi


def _unpack_rows(w):
    lo = lax.bitcast_convert_type(w << 16, F32)
    hi = lax.bitcast_convert_type(w & U32(0xFFFF0000), F32)
    return jnp.concatenate([lo, hi], axis=1)


def _store_row_tiles(ref, val):
    n = val.shape[0]
    for j in range(PACK_ROWS):
        ref[pl.ds(j, n, stride=PACK_ROWS), :] = val[:, j * LANE:(j + 1) * LANE]


def _load_row_tiles(ref, n):
    return jnp.concatenate([ref[pl.ds(j, n, stride=PACK_ROWS), :] for j in range(PACK_ROWS)], axis=1)


def _lanes_from(sm, off):
    return pltpu.roll(sm, SMALL_W - off, axis=1) if off else sm


def _chunk_scans(g):
    n = g.shape[0]
    rin = lax.broadcasted_iota(jnp.int32, g.shape, 0) & (CHUNK - 1)
    pre = g
    suf = jnp.where(rin < CHUNK - 1, pltpu.roll(g, n - 1, axis=0), 0.0)
    s = 1
    while s < CHUNK:
        pre = pre + jnp.where(rin >= s, pltpu.roll(pre, s, axis=0), 0.0)
        suf = suf + jnp.where(rin + s <= CHUNK - 1, pltpu.roll(suf, n - s, axis=0), 0.0)
        s *= 2
    return pre, suf


def _head_expand():
    m = np.zeros((SMALL_W, GDN_WIDTH), np.float32)
    for h in range(GDN_HEADS):
        m[h, h * GDN_HEAD_DIM:(h + 1) * GDN_HEAD_DIM] = 1.0
    return jnp.asarray(m, BF16)


def _block_ones(width, blk):
    idx = np.arange(width) // blk
    return jnp.asarray((idx[:, None] == idx[None, :]).astype(np.float32), BF16)


def _gdn_kernel(qkv_ref, z_ref, sm_ref, cw_ref, hp_ref, nw_ref, e_ref, xp_ref, bdm_ref, o_ref,
                tail_s, st_s, kn_s, kb_s, qn_s, qd_s, kd_s, u_s, w_s, eg_s, gcx_s, gw_s,
                o_s, au_s, qe_s, ku_s, kw_s, *, TB):
    H, Dh, C = GDN_HEADS, GDN_HEAD_DIM, CHUNK
    W = GDN_WIDTH
    ncb = TB // C

    @pl.when(pl.program_id(1) == 0)
    def _():
        tail_s[0:SUBLANE, :] = jnp.zeros((SUBLANE, tail_s.shape[1]), F32)
        st_s[...] = jnp.zeros_like(st_s)

    xc = _causal_conv_silu(qkv_ref[0], tail_s, cw_ref, None)
    q, k, v = xc[:, 0:W], xc[:, W:2 * W], xc[:, 2 * W:3 * W]
    e = e_ref[...]
    xp = xp_ref[...]
    qn = q * lax.rsqrt(_dot2(q * q, e) + EPS) * (Dh ** -0.5)
    kn = k * lax.rsqrt(_dot2(k * k, e) + EPS)

    sm = sm_ref[0]
    lane = lax.broadcasted_iota(jnp.int32, sm.shape, 1)
    head_lane = lane < H
    g = jnp.where(head_lane, hp_ref[0:1, :] * _softplus(_lanes_from(sm, A_LANE) + hp_ref[1:2, :]), 0.0)
    beta = jnp.where(head_lane, jax.nn.sigmoid(_lanes_from(sm, B_LANE)), 0.0)
    gc, rc = _chunk_scans(g)
    bx = _dot2(beta, xp)
    egx = _dot2(jnp.exp(gc), xp)
    erx = _dot2(jnp.exp(rc), xp)
    kb = kn * bx
    kn_s[...] = kn
    kb_s[...] = kb
    qn_s[...] = qn
    qd_s[...] = qn * egx
    kd_s[...] = kn * erx
    u_s[...] = v * bx
    w_s[...] = kb * egx
    gcx = _dot_sel3(gc, xp)
    gcx_s[...] = gcx
    r384 = lax.broadcasted_iota(jnp.int32, (C, W), 0)
    l384 = lax.broadcasted_iota(jnp.int32, (C, W), 1) & (Dh - 1)
    for c in range(ncb):
        diag_c = jnp.where(r384 == l384, gcx[c * C:(c + 1) * C, :], 0.0)
        gw_s[c] = jnp.broadcast_to(jnp.sum(diag_c, axis=0, keepdims=True), (SUBLANE, W))
        eg_s[c] = jnp.broadcast_to(egx[(c + 1) * C - 1:(c + 1) * C, :], (SUBLANE, W))

    heads = [slice(h * Dh, (h + 1) * Dh) for h in range(H)]

    GL = 4 * Dh
    row_w = lax.broadcasted_iota(jnp.int32, (C, GL), 0)
    col_w = lax.broadcasted_iota(jnp.int32, (C, GL), 1) & (Dh - 1)
    causal_w, strict_w = row_w >= col_w, row_w > col_w
    eye_w = jnp.where(row_w == col_w, 1.0, 0.0).astype(F32)
    bdm = bdm_ref[...]
    mm = functools.partial(lax.dot_general, dimension_numbers=_NN, preferred_element_type=F32)

    def block_diag(m, mask):
        return jnp.concatenate([m] * 4, axis=0) * mask

    npair = min(GDN_PAIRS_PER_ITER, ncb // 2)

    def solve(it, carry):
        chunks = [2 * npair * it + k for k in range(2 * npair)]
        rows = [pl.ds(pl.multiple_of(c * C, C), C) for c in chunks]

        def groups(get, n):
            out = []
            for j in range(npair):
                a0, a1 = get(2 * j), get(2 * j + 1)
                out += [a0[:, 0:4 * n], a1[:, 0:4 * n],
                        jnp.concatenate([a0[:, 4 * n:6 * n], a1[:, 4 * n:6 * n]], axis=1)]
            return out

        def ungroup(ref, vals, n):
            for j in range(npair):
                ref[rows[2 * j], 0:4 * n] = vals[3 * j]
                ref[rows[2 * j + 1], 0:4 * n] = vals[3 * j + 1]
                ref[rows[2 * j], 4 * n:6 * n] = vals[3 * j + 2][:, 0:2 * n]
                ref[rows[2 * j + 1], 4 * n:6 * n] = vals[3 * j + 2][:, 2 * n:4 * n]

        kn_g = groups(lambda k: kn_s[rows[k], :], Dh)
        kb_g = groups(lambda k: kb_s[rows[k], :], Dh)
        qn_g = groups(lambda k: qn_s[rows[k], :], Dh)
        gx_g = groups(lambda k: gcx_s[rows[k], :], Dh)
        gw_g = groups(lambda k: gw_s[chunks[k], 0:1, :], Dh)
        decs = [jnp.exp(jnp.where(causal_w, gx - gw, NEG_BIG)) for gx, gw in zip(gx_g, gw_g)]
        prods = [lax.dot_general(jnp.concatenate([kb, qn], axis=0).astype(BF16), block_diag(kn.astype(BF16), bdm),
                                 _NT, preferred_element_type=F32)
                 for kn, kb, qn in zip(kn_g, kb_g, qn_g)]
        attns = [pr[C:2 * C] * dec for pr, dec in zip(prods, decs)]
        def times(lhs, p=None):
            lh, ll = _split(lhs)
            ph, pl_ = (lh[0:C], ll[0:C]) if p is None else _split(p)
            rh, rl = block_diag(ph, bdm), block_diag(pl_, bdm)
            return mm(lh, rh) + (mm(lh, rl) + mm(ll, rh))

        ps = [-jnp.where(strict_w, pr[0:C] * dec, 0.0) for pr, dec in zip(prods, decs)]
        invs = [eye_w + p for p in ps]
        ps = [times(p) for p in ps]
        for _ in range(4):
            outs = [times(jnp.concatenate([p, inv], axis=0)) for p, inv in zip(ps, invs)]
            invs = [inv + o[C:2 * C] for inv, o in zip(invs, outs)]
            ps = [o[0:C] for o in outs]
        invs = [inv + times(inv, p) for p, inv in zip(ps, invs)]
        def solved(ref):
            outs = []
            for inv, r in zip(invs, groups(lambda k: ref[rows[k], :], Dh)):
                ih, il = _split(inv)
                rh, rl = _split(r)
                rh, rl = block_diag(rh, bdm), block_diag(rl, bdm)
                outs.append(mm(ih, rh) + (mm(ih, rl) + mm(il, rh)))
            return outs

        u_g, w_g = solved(u_s), solved(w_s)
        ungroup(u_s, u_g, Dh)
        ungroup(w_s, w_g, Dh)
        at_g = [a.astype(BF16) for a in attns]
        au_g = [mm(a, block_diag(u.astype(BF16), bdm)) for a, u in zip(at_g, u_g)]
        aw_g = [mm(a, block_diag(w.astype(BF16), bdm)) for a, w in zip(at_g, w_g)]
        qd_g = groups(lambda k: qd_s[rows[k], :], Dh)
        ungroup(au_s, au_g, Dh)
        ungroup(qe_s, [qd - aw for qd, aw in zip(qd_g, aw_g)], Dh)
        for k in range(2 * npair):
            for sl in heads:
                kd_h = kd_s[rows[k], sl]
                ku_s[rows[k], sl] = _dot(kd_h, u_s[rows[k], sl], _TN)
                kw_s[rows[k], sl] = _dot(kd_h, w_s[rows[k], sl], _TN)
        return carry

    lax.fori_loop(0, ncb // (2 * npair), solve, 0)

    def recur(c, carry):
        rows = pl.ds(pl.multiple_of(c * C, C), C)
        st = st_s[...]
        sb = st.astype(BF16)
        lhs = jnp.concatenate([qe_s[rows, :], kw_s[rows, :]], axis=0).astype(BF16)
        out = jnp.concatenate(
            [mm(lhs[:, 0:GL], block_diag(sb[:, 0:GL], bdm)),
             mm(lhs[:, GL:W], jnp.concatenate([sb[:, GL:W]] * 2, axis=0) * bdm[0:W - GL, 0:W - GL])], axis=1)
        o_s[rows, :] = out[0:C] + au_s[rows, :]
        st_s[...] = st * eg_s[c, 0:1, :] + (ku_s[rows, :] - out[C:2 * C])
        return carry

    lax.fori_loop(0, ncb, recur, 0)

    o = o_s[...]
    ms = _dot2(o * o, e) * (1.0 / Dh)
    o = o * lax.rsqrt(ms + EPS) * nw_ref[...]
    o_ref[0] = (o * _silu(z_ref[0])).astype(o_ref.dtype)


def _gdn_call(qkv, z, sm, conv_w, a_log, dt_bias, norm_w):
    B, S, _ = qkv.shape
    TB = min(512, S)
    W, H, Dh = GDN_WIDTH, GDN_HEADS, GDN_HEAD_DIM
    hp = jnp.zeros((SUBLANE, SMALL_W), F32)
    hp = hp.at[0, :H].set(-jnp.exp(a_log.astype(F32))).at[1, :H].set(dt_bias.astype(F32))
    nw = jnp.tile(norm_w.astype(F32), H).reshape(1, W)
    const = lambda shape: pl.BlockSpec(shape, lambda b, i: tuple(0 for _ in shape))
    blk = lambda n: pl.BlockSpec((1, TB, n), lambda b, i: (b, i, 0))
    f = lambda *shape: pltpu.VMEM(shape, F32)
    return pl.pallas_call(
        functools.partial(_gdn_kernel, TB=TB),
        out_shape=jax.ShapeDtypeStruct((B, S, W), BF16),
        grid=(B, S // TB),
        in_specs=[blk(GDN_CONV_DIM), blk(W), blk(SMALL_W),
                  const((CONV_WIDTH, GDN_CONV_DIM)), const((SUBLANE, SMALL_W)), const((1, W)),
                  const((W, W)), const((SMALL_W, W)), const((4 * Dh, 4 * Dh))],
        out_specs=blk(W),
        scratch_shapes=[f(SUBLANE + TB, GDN_CONV_DIM), f(Dh, W)]
        + [f(TB, W)] * 7 + [f(TB // CHUNK, SUBLANE, W), f(TB, W), f(TB // CHUNK, SUBLANE, W)] + [f(TB, W)] * 5,
        compiler_params=_cparams(("arbitrary", "arbitrary")),
        name="gdn_mixer",
    )(qkv, z, sm, conv_w.astype(F32), hp, nw, _block_ones(W, Dh), _head_expand(), _block_ones(4 * Dh, Dh))


def _ssd_kernel(xbc_ref, z_ref, sm_ref, cw_ref, cb_ref, hp_ref, dsk_ref, nw_ref, e_ref, xp_ref,
                hgm_ref, e_hd_ref, ghm_ref, o_ref,
                tail_s, st_s, xs_s, xdt_s, xdd_s, bm_s, cm_s, ea_s, el_s, ac_s, aw_s, y_s, inc_s, *, TB):
    H, P, G, N, C = SSD_HEADS, SSD_HEAD_DIM, SSD_GROUPS, SSD_STATE, CHUNK
    W, GW = SSD_WIDTH, SSD_GROUP_WIDTH
    ncb = TB // C

    @pl.when(pl.program_id(1) == 0)
    def _():
        tail_s[0:SUBLANE, :] = jnp.zeros((SUBLANE, tail_s.shape[1]), F32)
        st_s[...] = jnp.zeros_like(st_s)

    xc = _causal_conv_silu(xbc_ref[0], tail_s, cw_ref, cb_ref[...])
    xs = xc[:, 0:W]
    xp = xp_ref[...]

    sm = sm_ref[0]
    lane = lax.broadcasted_iota(jnp.int32, sm.shape, 1)
    head_lane = lane < H
    dt = jnp.where(head_lane, _softplus(_lanes_from(sm, DT_LANE) + hp_ref[1:2, :]), 0.0)
    acs, rcs = _chunk_scans(dt * hp_ref[0:1, :])
    dtx = _dot2(dt, xp)
    eax = _dot2(jnp.exp(acs), xp)
    erx = _dot2(jnp.exp(rcs), xp)
    xdt = xs * dtx
    xs_s[...] = xs
    xdt_s[...] = xdt
    xdd_s[...] = xdt * erx
    bm_s[...] = xc[:, W:W + G * N]
    cm_s[...] = xc[:, W + G * N:W + 2 * G * N]
    ea_s[...] = eax
    acx = _dot_sel3(acs, xp)
    ac_s[...] = acx
    r_w = lax.broadcasted_iota(jnp.int32, (C, W), 0)
    l_w = lax.broadcasted_iota(jnp.int32, (C, W), 1) & (P - 1)
    for c in range(ncb):
        diag_c = jnp.where(r_w == l_w, acx[c * C:(c + 1) * C, :], 0.0)
        aw_s[c] = jnp.broadcast_to(jnp.sum(diag_c, axis=0, keepdims=True), (SUBLANE, W))
        el_s[c] = jnp.broadcast_to(eax[(c + 1) * C - 1:(c + 1) * C, :], (SUBLANE, W))

    causal_w = r_w >= l_w
    mm = functools.partial(lax.dot_general, dimension_numbers=_NN, preferred_element_type=F32)

    def stacked(m, mask):
        return jnp.concatenate([m] * H, axis=0) * mask

    nloc = min(SSD_CHUNKS_PER_ITER, ncb)

    def local(it, carry):
        cs = [it * nloc + k for k in range(nloc)]
        rows = [pl.ds(pl.multiple_of(c * C, C), C) for c in cs]
        bs = [bm_s[r, :].astype(BF16) for r in rows]
        cbs = [lax.dot_general(cm_s[r, :].astype(BF16), stacked(b, hgm_ref[...]), _NT, preferred_element_type=F32)
               for r, b in zip(rows, bs)]
        segs = [jnp.exp(jnp.where(causal_w, ac_s[r, :] - aw_s[c, 0:1, :], NEG_BIG)) for r, c in zip(rows, cs)]
        for r, cb, seg in zip(rows, cbs, segs):
            y_s[r, :] = mm((cb * seg).astype(BF16), stacked(xdt_s[r, :].astype(BF16), e_hd_ref[...]))
        for r, c, b in zip(rows, cs, bs):
            inc_s[c] = lax.dot_general(b, xdd_s[r, :].astype(BF16), _TN, preferred_element_type=F32) * ghm_ref[...]
        return carry

    lax.fori_loop(0, ncb // nloc, local, 0)

    def recur(c, carry):
        rows = pl.ds(pl.multiple_of(c * C, C), C)
        st = st_s[...]
        y_s[rows, :] = y_s[rows, :] + mm(cm_s[rows, :].astype(BF16), st.astype(BF16)) * ea_s[rows, :]
        st_s[...] = st * el_s[c, 0:1, :] + inc_s[c]
        return carry

    lax.fori_loop(0, ncb, recur, 0)

    y = y_s[...] + dsk_ref[...] * xs_s[...]
    y = y * _silu(z_ref[0])
    ms = _dot2(y * y, e_ref[...]) * (1.0 / GW)
    o_ref[0] = (y * lax.rsqrt(ms + EPS) * nw_ref[...]).astype(o_ref.dtype)


def _ssd_call(xbc, z, sm, conv_w, conv_b, a_log, dt_bias, d_skip, norm_w):
    B, S, _ = xbc.shape
    TB = min(512, S)
    W, H, G, N = SSD_WIDTH, SSD_HEADS, SSD_GROUPS, SSD_STATE
    hp = jnp.zeros((SUBLANE, SMALL_W), F32)
    hp = hp.at[0, :H].set(-jnp.exp(a_log.astype(F32))).at[1, :H].set(dt_bias.astype(F32))
    dsk = jnp.repeat(d_skip.astype(F32), SSD_HEAD_DIM).reshape(1, W)
    const = lambda shape: pl.BlockSpec(shape, lambda b, i: tuple(0 for _ in shape))
    blk = lambda n: pl.BlockSpec((1, TB, n), lambda b, i: (b, i, 0))
    f = lambda *shape: pltpu.VMEM(shape, F32)
    head_of = np.arange(H * CHUNK) // CHUNK
    group_of = np.arange(G * N) // N
    hgm = (head_of[:, None] // (H // G) == group_of[None, :]).astype(np.float32)
    ghm = (group_of[:, None] == (np.arange(W) // SSD_HEAD_DIM // (H // G))[None, :]).astype(np.float32)
    return pl.pallas_call(
        functools.partial(_ssd_kernel, TB=TB),
        out_shape=jax.ShapeDtypeStruct((B, S, W), BF16),
        grid=(B, S // TB),
        in_specs=[blk(SSD_CONV_DIM), blk(W), blk(SMALL_W),
                  const((CONV_WIDTH, SSD_CONV_DIM)), const((1, SSD_CONV_DIM)), const((SUBLANE, SMALL_W)),
                  const((1, W)), const((1, W)), const((W, W)), const((SMALL_W, W)),
                  const((H * CHUNK, G * N)), const((W, W)), const((G * N, W))],
        out_specs=blk(W),
        scratch_shapes=[f(SUBLANE + TB, SSD_CONV_DIM), f(G * N, W),
                        f(TB, W), f(TB, W), f(TB, W), f(TB, G * N), f(TB, G * N), f(TB, W),
                        f(TB // CHUNK, SUBLANE, W),
                        f(TB, W), f(TB // CHUNK, SUBLANE, W), f(TB, W),
                        f(TB // CHUNK, G * N, W)],
        compiler_params=_cparams(("arbitrary", "arbitrary")),
        name="ssd_mixer",
    )(xbc, z, sm, conv_w.astype(F32), conv_b.reshape(1, -1).astype(F32), hp, dsk,
      norm_w.reshape(1, W).astype(F32), _block_ones(W, SSD_GROUP_WIDTH), _head_expand(),
      jnp.asarray(hgm, BF16), _block_ones(W, SSD_HEAD_DIM), jnp.asarray(ghm, F32))


def _out_kernel(x_ref, y1_ref, y2_ref, y3_ref, mod_ref, nw_ref, wo_ref, wr_ref, br_ref, tri_ref,
                x1_ref, h_ref, rt_ref, rtt_ref, cnt_ref, run_s, wo_s):
    first = (pl.program_id(0) == 0) & (pl.program_id(1) == 0)

    @pl.when(first)
    def _():
        wo_s[...] = wo_ref[0].astype(BF16)

    y = (jnp.dot(y1_ref[0], wo_s[0:S5_WIDTH, :], preferred_element_type=F32)
         + jnp.dot(y2_ref[0], wo_s[S5_WIDTH:S5_WIDTH + GDN_WIDTH, :], preferred_element_type=F32)
         + jnp.dot(y3_ref[0], wo_s[S5_WIDTH + GDN_WIDTH:, :], preferred_element_type=F32))
    x1 = x_ref[0] + mod_ref[0, 2:3, :] * y
    x1_ref[0] = x1
    h = _norm_mod(x1, nw_ref[...], mod_ref[0, 4:5, :], mod_ref[0, 3:4, :])
    _store_row_tiles(h_ref, _pack_rows(h))
    lg = lax.dot_general(wr_ref[...], h.astype(BF16), _NT, preferred_element_type=F32) + br_ref[...]
    tm = lg.shape[1]
    row = lax.broadcasted_iota(jnp.int32, lg.shape, 0)
    rowf = row.astype(F32)
    big = float(4 * LANE)
    grp = (row >= GRP_LANE) & (row < GRP_LANE + MOE_GROUPS)
    lgm = jnp.where(grp, lg, -jnp.inf)
    m = jnp.max(lgm, axis=0, keepdims=True)
    gidx = jnp.min(jnp.where(lgm == m, rowf - GRP_LANE, big), axis=0, keepdims=True)
    g_w = 1.0 / jnp.sum(jnp.where(grp, jnp.exp(lg - m), 0.0), axis=0, keepdims=True)
    in_grp = (row < N_EXPERTS) & ((row // EXPERTS_PER_GROUP).astype(F32) == gidx)
    le = jnp.where(in_grp, lg, -jnp.inf)
    v1 = jnp.max(le, axis=0, keepdims=True)
    i1 = jnp.min(jnp.where(le == v1, rowf, big), axis=0, keepdims=True)
    le2 = jnp.where(rowf == i1, -jnp.inf, le)
    v2 = jnp.max(le2, axis=0, keepdims=True)
    i2 = jnp.min(jnp.where(le2 == v2, rowf, big), axis=0, keepdims=True)
    e2 = jnp.exp(v2 - v1)
    w1 = g_w / (1.0 + e2)
    w2 = g_w * e2 / (1.0 + e2)

    @pl.when(first)
    def _():
        run_s[...] = jnp.zeros_like(run_s)

    chosen = jnp.where((rowf == i1) | (rowf == i2), 1.0, 0.0)
    before = jnp.dot(chosen.astype(BF16), tri_ref[...], preferred_element_type=F32) + run_s[:, 0:1]
    p1 = jnp.sum(jnp.where(rowf == i1, before, 0.0), axis=0, keepdims=True)
    p2 = jnp.sum(jnp.where(rowf == i2, before, 0.0), axis=0, keepdims=True)
    run_s[...] = run_s[...] + jnp.sum(chosen, axis=1, keepdims=True)
    cnt_ref[...] = run_s[...]

    vals = jnp.concatenate([i1, i2, w1, w2, p1, p2, jnp.zeros((LANE - 6, tm), F32)], axis=0)
    rtt_ref[0] = vals[0:SUBLANE, :]
    rt_ref[0] = vals.T


def _out_call(x, y1, y2, y3, mod_l, norm_w, w_out, w_router, b_router, l):
    B, S, D = x.shape
    tm = min(512, S)
    blk = lambda n: pl.BlockSpec((1, tm, n), lambda b, i: (b, i, 0))
    const = lambda shape: pl.BlockSpec(shape, lambda b, i: tuple(0 for _ in shape))
    tri = jnp.asarray(np.triu(np.ones((tm, tm), np.float32), 1), BF16)
    nb = S // tm
    tiles = PACK_ROWS
    return pl.pallas_call(
        _out_kernel,
        out_shape=[jax.ShapeDtypeStruct((B, S, D), F32), jax.ShapeDtypeStruct((B * S * tiles, LANE), U32),
                   jax.ShapeDtypeStruct((B, S, LANE), F32), jax.ShapeDtypeStruct((B * nb, SUBLANE, tm), F32),
                   jax.ShapeDtypeStruct((ROUTER_ROWS, LANE), F32)],
        grid=(B, nb),
        in_specs=[blk(D), blk(S5_WIDTH), blk(GDN_WIDTH), blk(SSD_WIDTH),
                  pl.BlockSpec((1, 6, D), lambda b, i: (b, 0, 0)),
                  const((1, D)), pl.BlockSpec((1, D, D), lambda b, i: (l, 0, 0)),
                  const((ROUTER_ROWS, D)), const((ROUTER_ROWS, 1)), const((tm, tm))],
        out_specs=[blk(D), pl.BlockSpec((tm * tiles, LANE), lambda b, i: (b * nb + i, 0)),
                   blk(LANE), pl.BlockSpec((1, SUBLANE, tm), lambda b, i: (b * nb + i, 0, 0)),
                   const((ROUTER_ROWS, LANE))],
        scratch_shapes=[pltpu.VMEM((ROUTER_ROWS, LANE), F32), pltpu.VMEM((D, D), BF16)],
        compiler_params=_cparams(("arbitrary", "arbitrary")),
        name="out_proj_router",
    )(x, y1, y2, y3, mod_l, norm_w.reshape(1, D), w_out, w_router, b_router, tri)


def _router_params(w_grp, b_grp, w_rt, b_rt):
    D = w_grp.shape[0]
    w = jnp.zeros((ROUTER_ROWS, D), F32).at[0:N_EXPERTS].set(w_rt.T).at[GRP_LANE:GRP_LANE + MOE_GROUPS].set(w_grp.T)
    b = jnp.zeros((ROUTER_ROWS, 1), F32).at[0:N_EXPERTS, 0].set(b_rt).at[GRP_LANE:GRP_LANE + MOE_GROUPS, 0].set(b_grp)
    return w.astype(BF16), b


def _tile_copy_loop(n, fn):
    def body(t, carry):
        fn(t)
        return carry
    lax.fori_loop(0, n, body, 0, unroll=32)


def _scatter_kernel(d0_ref, d1_ref, h_ref, xs_hbm, stage, sem):
    i = pl.program_id(0)
    n = pl.num_programs(0)
    slot = lax.rem(i, 2)
    tm = d0_ref.shape[-1]

    def copy(s, t, dst):
        return pltpu.make_async_copy(stage.at[s, pl.ds(t * PACK_ROWS, PACK_ROWS)],
                                     xs_hbm.at[pl.ds(dst * PACK_ROWS, PACK_ROWS)], sem.at[s])

    def wait_slot(s):
        _tile_copy_loop(TOP_K * tm, lambda t: copy(s, 0, 0).wait())

    @pl.when(i >= 2)
    def _():
        wait_slot(slot)

    stage[slot] = h_ref[...]

    def start(t):
        copy(slot, t, d0_ref[0, 0, t]).start(priority=0)
        copy(slot, t, d1_ref[0, 0, t]).start(priority=1)
    _tile_copy_loop(tm, start)

    @pl.when(i == n - 1)
    def _():
        @pl.when(i >= 1)
        def _():
            wait_slot(1 - slot)
        wait_slot(slot)


def _scatter_call(h2t, dest0, dest1, n_rows):
    n_tiles, _, tm = dest0.shape
    idx_blk = pl.BlockSpec((1, 1, tm), lambda i: (i, 0, 0), memory_space=pltpu.SMEM)
    return pl.pallas_call(
        _scatter_kernel,
        out_shape=jax.ShapeDtypeStruct((n_rows * PACK_ROWS, LANE), U32),
        grid=(n_tiles,),
        in_specs=[idx_blk, idx_blk, pl.BlockSpec((tm * PACK_ROWS, LANE), lambda i: (i, 0))],
        out_specs=pl.BlockSpec(memory_space=pl.ANY),
        scratch_shapes=[pltpu.VMEM((2, tm * PACK_ROWS, LANE), U32), pltpu.SemaphoreType.DMA((2,))],
        compiler_params=_cparams(("arbitrary",)),
        name="moe_scatter",
    )(dest0, dest1, h2t)


def _expert_kernel(nused_ref, blke_ref, nvalid_ref, xs_ref, wg_ref, wu_ref, wd_ref, ys_ref, wg_s, wu_s, wd_s):
    i = pl.program_id(0)

    @pl.when(i < nused_ref[0])
    def _():
        @pl.when((i == 0) | (blke_ref[i] != blke_ref[jnp.maximum(i - 1, 0)]))
        def _():
            wg_s[...] = wg_ref[0, 0].astype(BF16)
            wu_s[...] = wu_ref[0, 0].astype(BF16)
            wd_s[...] = wd_ref[0, 0].astype(BF16)

        row = lax.broadcasted_iota(jnp.int32, (MOE_ROWS, 1), 0)
        words = jnp.where(row < nvalid_ref[i], _load_row_tiles(xs_ref, MOE_ROWS), U32(0))
        xb = _unpack_rows(words).astype(BF16)
        hid = _silu(jnp.dot(xb, wg_s[...], preferred_element_type=F32)) * jnp.dot(xb, wu_s[...], preferred_element_type=F32)
        _store_row_tiles(ys_ref, _pack_rows(jnp.dot(hid.astype(BF16), wd_s[...], preferred_element_type=F32)))

    @pl.when(i >= nused_ref[0])
    def _():
        ys_ref[...] = jnp.zeros_like(ys_ref)


def _expert_call(xs, n_used, blk_e, n_valid, w_gate, w_up, w_down, l):
    D = w_gate.shape[2]
    n_blk = blk_e.shape[0]
    rows_blk = pl.BlockSpec((MOE_ROWS * PACK_ROWS, LANE), lambda i, nu, be, nv: (i, 0))
    grid_spec = pltpu.PrefetchScalarGridSpec(
        num_scalar_prefetch=3,
        grid=(n_blk,),
        in_specs=[rows_blk,
                  pl.BlockSpec((1, 1, D, D_EXPERT), lambda i, nu, be, nv: (l, be[i], 0, 0)),
                  pl.BlockSpec((1, 1, D, D_EXPERT), lambda i, nu, be, nv: (l, be[i], 0, 0)),
                  pl.BlockSpec((1, 1, D_EXPERT, D), lambda i, nu, be, nv: (l, be[i], 0, 0))],
        out_specs=rows_blk,
        scratch_shapes=[pltpu.VMEM((D, D_EXPERT), BF16), pltpu.VMEM((D, D_EXPERT), BF16),
                        pltpu.VMEM((D_EXPERT, D), BF16)],
    )
    return pl.pallas_call(
        _expert_kernel,
        out_shape=jax.ShapeDtypeStruct(xs.shape, U32),
        grid_spec=grid_spec,
        compiler_params=_cparams(("arbitrary",)),
        name="expert_mlp",
    )(n_used, blk_e, n_valid, xs, w_gate, w_up, w_down)


def _dest_kernel(rt_ref, ps_ref, o_ref):
    n_sub, _, tm = o_ref.shape
    expert = lax.broadcasted_iota(jnp.int32, (N_EXPERTS, tm), 0).astype(F32)
    starts = ps_ref[0:N_EXPERTS, :]
    for j in range(n_sub):
        rows = []
        for k in range(TOP_K):
            start = jnp.sum(jnp.where(expert == rt_ref[j, k:k + 1, :], starts, 0.0), axis=0, keepdims=True)
            rows.append(start + rt_ref[j, 4 + k:5 + k, :])
        rows.append(jnp.zeros((SUBLANE - TOP_K, tm), F32))
        o_ref[j] = jnp.concatenate(rows, axis=0).astype(jnp.int32)


def _dest_call(route_t, pstart_col):
    n_tiles, _, tm = route_t.shape
    n_sub = min(DEST_TILES_PER_STEP, n_tiles)
    blk = pl.BlockSpec((n_sub, SUBLANE, tm), lambda i: (i, 0, 0))
    return pl.pallas_call(
        _dest_kernel,
        out_shape=jax.ShapeDtypeStruct((n_tiles, SUBLANE, tm), jnp.int32),
        grid=(n_tiles // n_sub,),
        in_specs=[blk, pl.BlockSpec((LANE, 1), lambda i: (0, 0))],
        out_specs=blk,
        compiler_params=_cparams(("arbitrary",)),
        name="moe_dest",
    )(route_t, pstart_col)


def _dispatch(route_t, counts, N, tm):
    L_pad = N * TOP_K + N_EXPERTS * MOE_ROWS
    n_blk = L_pad // MOE_ROWS
    counts = counts.astype(jnp.int32)
    padded = ((counts + MOE_ROWS - 1) // MOE_ROWS) * MOE_ROWS
    pend = jnp.cumsum(padded)
    pstart = pend - padded
    n_used = (pend[-1] // MOE_ROWS).astype(jnp.int32).reshape(1)
    blk_row0 = jnp.arange(n_blk, dtype=jnp.int32) * MOE_ROWS
    blk_e = jnp.minimum(jnp.sum((pend[None, :] <= blk_row0[:, None]).astype(jnp.int32), axis=1), N_EXPERTS - 1)
    n_valid = jnp.clip((pstart + counts)[blk_e] - blk_row0, 0, MOE_ROWS).astype(jnp.int32)
    pstart_col = jnp.zeros((LANE, 1), F32).at[0:N_EXPERTS, 0].set(pstart.astype(F32))
    dest = _dest_call(route_t, pstart_col)
    dest0 = dest[:, 0, :].reshape(N // tm, 1, tm)
    dest1 = dest[:, 1, :].reshape(N // tm, 1, tm)
    return n_used, blk_e.astype(jnp.int32), n_valid, dest0, dest1, L_pad


def _combine_kernel(d0_ref, d1_ref, d0n_ref, d1n_ref, x_ref, rt_ref, mod_ref, nf_ref, ys_hbm, o_ref,
                    buf, sem, *, final):
    i = pl.program_id(0)
    n = pl.num_programs(0)
    slot = lax.rem(i, 2)
    tm = x_ref.shape[0]

    def copy(s, k, t, src):
        return pltpu.make_async_copy(ys_hbm.at[pl.ds(src * PACK_ROWS, PACK_ROWS)],
                                     buf.at[s, k, pl.ds(t * PACK_ROWS, PACK_ROWS)], sem.at[s])

    def start_tile(s, a_ref, b_ref):
        def start(t):
            copy(s, 0, t, a_ref[0, 0, t]).start(priority=0)
            copy(s, 1, t, b_ref[0, 0, t]).start(priority=1)
        _tile_copy_loop(tm, start)

    @pl.when(i == 0)
    def _():
        start_tile(0, d0_ref, d1_ref)

    @pl.when(i + 1 < n)
    def _():
        start_tile(1 - slot, d0n_ref, d1n_ref)

    _tile_copy_loop(TOP_K * tm, lambda t: copy(slot, 0, 0, 0).wait())
    rt = rt_ref[...]
    y = (rt[:, 2:3] * _unpack_rows(_load_row_tiles(buf.at[slot, 0], tm))
         + rt[:, 3:4] * _unpack_rows(_load_row_tiles(buf.at[slot, 1], tm)))
    x2 = x_ref[...] + mod_ref[0, 5:6, :] * y
    if final:
        ms = jnp.mean(x2 * x2, axis=-1, keepdims=True)
        x2 = x2 * lax.rsqrt(ms + EPS) * nf_ref[...]
    o_ref[...] = x2


def _combine_call(x1, ys, route, dest0, dest1, mod_l, norm_final, final):
    B, S, D = x1.shape
    N = B * S
    n_tiles, _, tm = dest0.shape
    per_b = S // tm
    idx_blk = lambda fn: pl.BlockSpec((1, 1, tm), fn, memory_space=pltpu.SMEM)
    cur = lambda i: (i, 0, 0)
    nxt = lambda i: (jnp.minimum(i + 1, n_tiles - 1), 0, 0)
    out = pl.pallas_call(
        functools.partial(_combine_kernel, final=final),
        out_shape=jax.ShapeDtypeStruct((N, D), F32),
        grid=(n_tiles,),
        in_specs=[idx_blk(cur), idx_blk(cur), idx_blk(nxt), idx_blk(nxt),
                  pl.BlockSpec((tm, D), lambda i: (i, 0)),
                  pl.BlockSpec((tm, LANE), lambda i: (i, 0)),
                  pl.BlockSpec((1, 6, D), lambda i: (i // per_b, 0, 0)),
                  pl.BlockSpec((1, D), lambda i: (0, 0)),
                  pl.BlockSpec(memory_space=pl.ANY)],
        out_specs=pl.BlockSpec((tm, D), lambda i: (i, 0)),
        scratch_shapes=[pltpu.VMEM((2, TOP_K, tm * PACK_ROWS, LANE), U32), pltpu.SemaphoreType.DMA((2,))],
        compiler_params=_cparams(("arbitrary",)),
        name="moe_combine",
    )(dest0, dest1, dest0, dest1, x1.reshape(N, D), route.reshape(N, LANE), mod_l, norm_final.reshape(1, D), ys)
    return out.reshape(B, S, D)


def _layer(x, mod_l, p, big, l, final, norm_final):
    B, S, D = x.shape
    N = B * S
    s5_u, g_qkv, g_z, s_z, s_xbc, small = _proj_call(x, mod_l, p["norm_mix"], big["w_in"], l)
    bm, cm, lam_rows = _s5_params(p["s5_a_re"], p["s5_a_im"], p["s5_b_re"], p["s5_b_im"],
                                  p["s5_c_re"], p["s5_c_im"], p["s5_log_dt"])
    y_s5 = _s5_call(s5_u, bm, cm, lam_rows, p["s5_d"], p["s5_w_glu"], p["s5_norm"])
    y_gdn = _gdn_call(g_qkv, g_z, small, p["gdn_conv_w"], p["gdn_a_log"], p["gdn_dt_bias"], p["gdn_norm"])
    y_ssd = _ssd_call(s_xbc, s_z, small, p["ssd_conv_w"], p["ssd_conv_b"], p["ssd_a_log"], p["ssd_dt_bias"],
                      p["ssd_d"], p["ssd_norm"])
    w_router, b_router = _router_params(p["moe_w_grp"], p["moe_b_grp"], p["moe_w_rt"], p["moe_b_rt"])
    x1, h2, route, route_t, counts = _out_call(x, y_s5, y_gdn, y_ssd, mod_l, p["norm_ffn"], big["w_out"],
                                               w_router, b_router, l)
    n_used, blk_e, n_valid, dest0, dest1, n_rows = _dispatch(route_t, counts[0:N_EXPERTS, 0], N,
                                                             min(MOE_TOKEN_TILE, S))
    xs = _scatter_call(h2, dest0, dest1, n_rows)
    ys = _expert_call(xs, n_used, blk_e, n_valid, big["moe_w_gate"], big["moe_w_up"], big["moe_w_down"], l)
    return _combine_call(x1, ys, route, dest0, dest1, mod_l, norm_final, final)


def kernel(x, c, w_ada, b_ada, norm_mix, norm_ffn, w_in, w_out, s5_a_re, s5_a_im, s5_b_re, s5_b_im, s5_c_re, s5_c_im, s5_d, s5_log_dt, s5_w_glu, s5_norm, gdn_conv_w, gdn_a_log, gdn_dt_bias, gdn_norm, ssd_conv_w, ssd_conv_b, ssd_a_log, ssd_dt_bias, ssd_d, ssd_norm, moe_w_grp, moe_b_grp, moe_w_rt, moe_b_rt, moe_w_gate, moe_w_up, moe_w_down, norm_final):
    stacked = dict(norm_mix=norm_mix, norm_ffn=norm_ffn, s5_a_re=s5_a_re, s5_a_im=s5_a_im,
                   s5_b_re=s5_b_re, s5_b_im=s5_b_im, s5_c_re=s5_c_re, s5_c_im=s5_c_im, s5_d=s5_d,
                   s5_log_dt=s5_log_dt, s5_w_glu=s5_w_glu, s5_norm=s5_norm, gdn_conv_w=gdn_conv_w,
                   gdn_a_log=gdn_a_log, gdn_dt_bias=gdn_dt_bias, gdn_norm=gdn_norm, ssd_conv_w=ssd_conv_w,
                   ssd_conv_b=ssd_conv_b, ssd_a_log=ssd_a_log, ssd_dt_bias=ssd_dt_bias, ssd_d=ssd_d,
                   ssd_norm=ssd_norm, moe_w_grp=moe_w_grp, moe_b_grp=moe_b_grp, moe_w_rt=moe_w_rt,
                   moe_b_rt=moe_b_rt)
    big = dict(w_in=_w_in_prep_call(w_in), w_out=w_out, moe_w_gate=moe_w_gate, moe_w_up=moe_w_up,
               moe_w_down=moe_w_down)
    L = w_in.shape[0]
    B, S, D = x.shape
    mod = _mod_call(c, w_ada, b_ada).reshape(L, B, 6, D)
    for l in range(L):
        p = {k: v[l] for k, v in stacked.items()}
        x = _layer(x, mod[l], p, big, l, l == L - 1, norm_final)
    return x
```

```python
import functools

import numpy as np
import jax
import jax.numpy as jnp
from jax import lax
from jax.experimental import pallas as pl
from jax.experimental.pallas import tpu as pltpu

F32 = jnp.float32
BF16 = jnp.bfloat16

D_MODEL = 1024
DEPTH = 4
EPS = 1e-6
CONV_WIDTH = 4
CHUNK = 64
S5_WIDTH = 256
S5_CH = 16
S5_GROUPS = 16
S5_STATE = 64
S5_LANES = S5_GROUPS * S5_STATE
GDN_WIDTH = 384
GDN_HEAD_DIM = 64
GDN_HEADS = 6
GDN_CONV_DIM = 3 * GDN_WIDTH
SSD_WIDTH = 384
SSD_HEAD_DIM = 64
SSD_HEADS = 6
SSD_GROUPS = 2
SSD_STATE = 128
SSD_GROUP_WIDTH = SSD_WIDTH // SSD_GROUPS
SSD_CONV_DIM = SSD_WIDTH + 2 * SSD_GROUPS * SSD_STATE
PROJ_SIZES = (S5_WIDTH, GDN_CONV_DIM, GDN_WIDTH, GDN_HEADS, GDN_HEADS, SSD_WIDTH, SSD_CONV_DIM, SSD_HEADS)
MOE_GROUPS = 4
EXPERTS_PER_GROUP = 8
N_EXPERTS = 32
TOP_K = 2
D_EXPERT = 256

LANE = 128
SUBLANE = 8
SMALL_W = LANE
A_LANE, B_LANE, DT_LANE = 0, GDN_HEADS, 2 * GDN_HEADS
GRP_LANE = N_EXPERTS
ROUTER_ROWS = 40
MOE_ROWS = 512
MOE_TOKEN_TILE = 512
DEST_TILES_PER_STEP = 8
SSD_CHUNKS_PER_ITER = 8
GDN_PAIRS_PER_ITER = 4
VMEM_LIMIT = 56 * 1024 * 1024
NEG_BIG = -1e30


def _cparams(sem):
    return pltpu.CompilerParams(dimension_semantics=sem, vmem_limit_bytes=VMEM_LIMIT)


def _split(a):
    hi = a.astype(BF16)
    lo = (a - hi.astype(F32)).astype(BF16)
    return hi, lo


_NN = (((1,), (0,)), ((), ()))
_NT = (((1,), (1,)), ((), ()))
_TN = (((0,), (0,)), ((), ()))


def _dot(a, b, dims=_NN):
    return lax.dot_general(a.astype(BF16), b.astype(BF16), dims, preferred_element_type=F32)


def _dot2(a, b_bf16):
    hi, lo = _split(a)
    return (lax.dot_general(hi, b_bf16, _NN, preferred_element_type=F32)
            + lax.dot_general(lo, b_bf16, _NN, preferred_element_type=F32))


def _dot_sel3(a, b_bf16):
    h1 = a.astype(BF16)
    r1 = a - h1.astype(F32)
    h2 = r1.astype(BF16)
    h3 = (r1 - h2.astype(F32)).astype(BF16)
    d = functools.partial(lax.dot_general, dimension_numbers=_NN, preferred_element_type=F32)
    return d(h1, b_bf16) + (d(h2, b_bf16) + d(h3, b_bf16))


def _dot3(a, b):
    ah, al = _split(a)
    bh, bl = _split(b)
    d = functools.partial(lax.dot_general, dimension_numbers=_NN, preferred_element_type=F32)
    return d(ah, bh) + (d(ah, bl) + d(al, bh))


def _silu(x):
    return x * jax.nn.sigmoid(x)


def _softplus(x):
    return jnp.maximum(x, 0.0) + jnp.log(1.0 + jnp.exp(-jnp.abs(x)))


def _norm_mod(x, w, scale, shift):
    ms = jnp.mean(x * x, axis=-1, keepdims=True)
    return (x * lax.rsqrt(ms + EPS) * w) * (1.0 + scale) + shift


def _mod_kernel(c_ref, w_ref, b_ref, o_ref):
    cond = _silu(c_ref[...])
    o_ref[0] = _dot3(cond, w_ref[0]) + b_ref[0]


def _mod_call(c, w_ada, b_ada):
    L, D, W = w_ada.shape
    B = c.shape[0]
    tn = 1536
    return pl.pallas_call(
        _mod_kernel,
        out_shape=jax.ShapeDtypeStruct((L, B, W), F32),
        grid=(L, W // tn),
        in_specs=[pl.BlockSpec((B, D), lambda l, j: (0, 0)),
                  pl.BlockSpec((1, D, tn), lambda l, j: (l, 0, j)),
                  pl.BlockSpec((1, 1, tn), lambda l, j: (l, 0, j))],
        out_specs=pl.BlockSpec((1, B, tn), lambda l, j: (l, 0, j)),
        compiler_params=_cparams(("arbitrary", "arbitrary")),
        name="adaln_mod",
    )(c, w_ada, b_ada.reshape(L, 1, W))


PROJ_OUT_W = (S5_WIDTH, GDN_CONV_DIM, GDN_WIDTH, SSD_WIDTH, SSD_CONV_DIM, SMALL_W)


def _proj_kernel(x_ref, mod_ref, nw_ref, w_ref, *o_refs):
    x = x_ref[0]
    h = _norm_mod(x, nw_ref[...], mod_ref[0, 1:2, :], mod_ref[0, 0:1, :]).astype(BF16)
    proj = jnp.dot(h, w_ref[0], preferred_element_type=F32)
    off = 0
    for o_ref in o_refs:
        n = o_ref.shape[-1]
        o_ref[0] = proj[:, off:off + n]
        off += n


_W_HEAD = S5_WIDTH + GDN_CONV_DIM + GDN_WIDTH
_W_AB = _W_HEAD + 2 * GDN_HEADS
_W_SSD = _W_AB + SSD_WIDTH + SSD_CONV_DIM
_W_END = _W_SSD + SSD_HEADS
PROJ_W = sum(PROJ_OUT_W)
W_PREP_ROWS = 256


def _w_in_prep_kernel(w_ref, o_ref):
    n_ssd = _W_SSD - _W_AB
    for r in range(0, w_ref.shape[1], W_PREP_ROWS):
        w = w_ref[0, r:r + W_PREP_ROWS, :]
        rows = slice(r, r + W_PREP_ROWS)
        o_ref[0, rows, 0:_W_HEAD] = w[:, 0:_W_HEAD].astype(BF16)
        o_ref[0, rows, _W_HEAD:_W_HEAD + n_ssd] = w[:, _W_AB:_W_SSD].astype(BF16)
        small = jnp.concatenate([w[:, _W_HEAD:_W_AB], w[:, _W_SSD:_W_END],
                                 jnp.zeros((W_PREP_ROWS, SMALL_W - (_W_AB - _W_HEAD) - (_W_END - _W_SSD)), F32)], axis=1)
        o_ref[0, rows, _W_HEAD + n_ssd:PROJ_W] = small.astype(BF16)


def _w_in_prep_call(w_in):
    L, D, W = w_in.shape
    return pl.pallas_call(
        _w_in_prep_kernel,
        out_shape=jax.ShapeDtypeStruct((L, D, PROJ_W), BF16),
        grid=(L,),
        in_specs=[pl.BlockSpec((1, D, W), lambda l: (l, 0, 0))],
        out_specs=pl.BlockSpec((1, D, PROJ_W), lambda l: (l, 0, 0)),
        compiler_params=_cparams(("arbitrary",)),
        name="w_in_prep",
    )(w_in)


def _proj_call(x, mod_l, norm_w, w_arr, l):
    B, S, D = x.shape
    tm = min(512, S)
    out_shape = [jax.ShapeDtypeStruct((B, S, n), F32) for n in PROJ_OUT_W]
    out_specs = [pl.BlockSpec((1, tm, n), lambda b, i: (b, i, 0)) for n in PROJ_OUT_W]
    return pl.pallas_call(
        _proj_kernel,
        out_shape=out_shape,
        grid=(B, S // tm),
        in_specs=[pl.BlockSpec((1, tm, D), lambda b, i: (b, i, 0)),
                  pl.BlockSpec((1, 6, D), lambda b, i: (b, 0, 0)),
                  pl.BlockSpec((1, D), lambda b, i: (0, 0)),
                  pl.BlockSpec((1, D, PROJ_W), lambda b, i: (l, 0, 0))],
        out_specs=out_specs,
        compiler_params=_cparams(("arbitrary", "arbitrary")),
        name="norm_in_proj",
    )(x, mod_l, norm_w.reshape(1, D), w_arr)


def _s5_kernel(u_ref, bm_ref, cm_ref, lam_ref, dsk_ref, glu_ref, nw_ref, perm_ref, o_ref,
               x_s, st_s, ubt_s, u_s, *, T):
    B = u_ref.shape[0]
    P = S5_LANES

    @pl.when(pl.program_id(0) == 0)
    def _():
        st_s[...] = jnp.zeros_like(st_s)

    nw = S5_WIDTH // LANE
    ubt = u_ref[...].reshape(B * T, S5_WIDTH)
    for j in range(nw):
        ubt_s[j] = ubt[:, j * LANE:(j + 1) * LANE]

    def regroup(t, carry):
        for j in range(nw):
            u_s[pl.ds(pl.multiple_of(t * B, B), B), j * LANE:(j + 1) * LANE] = ubt_s[j, pl.ds(t, B, stride=T), :]
        return carry

    lax.fori_loop(0, T, regroup, 0, unroll=8)
    u = u_s[...]
    x_s[...] = jnp.dot(u.astype(BF16), bm_ref[...], preferred_element_type=F32)
    lr = jnp.broadcast_to(lam_ref[0:1, :], (B, P))
    li = jnp.broadcast_to(lam_ref[1:2, :], (B, P))

    def step(t, carry):
        sr, si = carry
        rows = pl.ds(pl.multiple_of(t * B, B), B)
        nr = lr * sr - li * si + x_s[rows, 0:P]
        ni = lr * si + li * sr + x_s[rows, P:2 * P]
        x_s[rows, 0:P] = nr
        x_s[rows, P:2 * P] = ni
        return nr, ni

    sr, si = lax.fori_loop(0, T, step, (st_s[:, 0:P], st_s[:, P:2 * P]), unroll=4)
    st_s[:, 0:P] = sr
    st_s[:, P:2 * P] = si

    y = jnp.dot(x_s[...].astype(BF16), cm_ref[...], preferred_element_type=F32) + dsk_ref[...] * u
    y = jax.nn.gelu(y)
    y = y * jax.nn.sigmoid(jnp.dot(y.astype(BF16), glu_ref[...], preferred_element_type=F32))
    ms = jnp.mean(y * y, axis=-1, keepdims=True)
    y = (y * lax.rsqrt(ms + EPS) * nw_ref[...]).astype(BF16)
    y = jnp.dot(perm_ref[...], y, preferred_element_type=F32)
    o_ref[...] = y.reshape(B, T, S5_WIDTH).astype(o_ref.dtype)


def _s5_params(a_re, a_im, b_re, b_im, c_re, c_im, log_dt):
    G, P, CH = S5_GROUPS, S5_STATE, S5_CH
    lam = lax.complex(a_re.astype(F32), a_im.astype(F32))
    step = jnp.exp(log_dt.astype(F32))[:, None]
    lam_bar = jnp.exp(lam * step)
    b_bar = ((lam_bar - 1.0) / lam)[..., None] * lax.complex(b_re.astype(F32), b_im.astype(F32))
    eye = jnp.eye(G, dtype=F32)
    bre = jnp.einsum('gpc,gh->gchp', b_bar.real, eye).reshape(G * CH, G * P)
    bim = jnp.einsum('gpc,gh->gchp', b_bar.imag, eye).reshape(G * CH, G * P)
    bm = jnp.concatenate([bre, bim], axis=1).astype(BF16)
    cre = jnp.einsum('gcp,gh->gphc', c_re.astype(F32), eye).reshape(G * P, G * CH)
    cim = jnp.einsum('gcp,gh->gphc', c_im.astype(F32), eye).reshape(G * P, G * CH)
    cm = jnp.concatenate([cre, -cim], axis=0).astype(BF16)
    lam_rows = jnp.zeros((SUBLANE, G * P), F32)
    lam_rows = lam_rows.at[0].set(lam_bar.real.reshape(-1)).at[1].set(lam_bar.imag.reshape(-1))
    return bm, cm, lam_rows


def _s5_call(u, bm, cm, lam_rows, d_skip, w_glu, norm_w):
    B, S, W = u.shape
    T = min(128, S)
    P2 = 2 * S5_LANES
    const = lambda shape: pl.BlockSpec(shape, lambda i: tuple(0 for _ in shape))
    r = np.arange(B * T)
    perm = np.zeros((B * T, B * T), np.float32)
    perm[r, (r % T) * B + r // T] = 1.0
    return pl.pallas_call(
        functools.partial(_s5_kernel, T=T),
        out_shape=jax.ShapeDtypeStruct((B, S, W), BF16),
        grid=(S // T,),
        in_specs=[pl.BlockSpec((B, T, W), lambda i: (0, i, 0)),
                  const((W, P2)), const((P2, W)), const((SUBLANE, S5_LANES)),
                  const((1, W)), const((W, W)), const((1, W)), const((B * T, B * T))],
        out_specs=pl.BlockSpec((B, T, W), lambda i: (0, i, 0)),
        scratch_shapes=[pltpu.VMEM((B * T, P2), F32), pltpu.VMEM((B, P2), F32),
                        pltpu.VMEM((W // LANE, B * T, LANE), F32), pltpu.VMEM((B * T, W), F32)],
        compiler_params=_cparams(("arbitrary",)),
        name="s5_mixer",
    )(u, bm, cm, lam_rows, d_skip.reshape(1, W).astype(F32), w_glu.astype(BF16), norm_w.reshape(1, W).astype(F32),
      jnp.asarray(perm, BF16))


def _causal_conv_silu(x, xf_ref, cw_ref, bias):
    n = x.shape[0]
    xf_ref[SUBLANE:, :] = x
    acc = x * cw_ref[CONV_WIDTH - 1:CONV_WIDTH, :]
    for k in range(1, CONV_WIDTH):
        acc = acc + xf_ref[SUBLANE - k:SUBLANE - k + n, :] * cw_ref[CONV_WIDTH - 1 - k:CONV_WIDTH - k, :]
    xf_ref[0:SUBLANE, :] = x[n - SUBLANE:, :]
    if bias is not None:
        acc = acc + bias
    return _silu(acc)


U32 = jnp.uint32
PACK_ROWS = D_MODEL // (2 * LANE)


def _pack_rows(x):
    h = x.shape[1] // 2
    xb = x.astype(BF16).astype(F32)
    lo = lax.bitcast_convert_type(xb[:, 0:h], U32) >> 16
    hi = lax.bitcast_convert_type(xb[:, h:2 * h], U32) & U32(0xFFFF0000)
    return lo | hi


def _unpack_rows(w):
    lo = lax.bitcast_convert_type(w << 16, F32)
    hi = lax.bitcast_convert_type(w & U32(0xFFFF0000), F32)
    return jnp.concatenate([lo, hi], axis=1)


def _store_row_tiles(ref, val):
    n = val.shape[0]
    for j in range(PACK_ROWS):
        ref[pl.ds(j, n, stride=PACK_ROWS), :] = val[:, j * LANE:(j + 1) * LANE]


def _load_row_tiles(ref, n):
    return jnp.concatenate([ref[pl.ds(j, n, stride=PACK_ROWS), :] for j in range(PACK_ROWS)], axis=1)


def _lanes_from(sm, off):
    return pltpu.roll(sm, SMALL_W - off, axis=1) if off else sm


def _chunk_scans(g):
    n = g.shape[0]
    rin = lax.broadcasted_iota(jnp.int32, g.shape, 0) & (CHUNK - 1)
    pre = g
    suf = jnp.where(rin < CHUNK - 1, pltpu.roll(g, n - 1, axis=0), 0.0)
    s = 1
    while s < CHUNK:
        pre = pre + jnp.where(rin >= s, pltpu.roll(pre, s, axis=0), 0.0)
        suf = suf + jnp.where(rin + s <= CHUNK - 1, pltpu.roll(suf, n - s, axis=0), 0.0)
        s *= 2
    return pre, suf


def _head_expand():
    m = np.zeros((SMALL_W, GDN_WIDTH), np.float32)
    for h in range(GDN_HEADS):
        m[h, h * GDN_HEAD_DIM:(h + 1) * GDN_HEAD_DIM] = 1.0
    return jnp.asarray(m, BF16)


def _block_ones(width, blk):
    idx = np.arange(width) // blk
    return jnp.asarray((idx[:, None] == idx[None, :]).astype(np.float32), BF16)


def _gdn_kernel(qkv_ref, z_ref, sm_ref, cw_ref, hp_ref, nw_ref, e_ref, xp_ref, bdm_ref, o_ref,
                tail_s, st_s, kn_s, kb_s, qn_s, qd_s, kd_s, u_s, w_s, eg_s, gcx_s, gw_s,
                o_s, au_s, qe_s, ku_s, kw_s, *, TB):
    H, Dh, C = GDN_HEADS, GDN_HEAD_DIM, CHUNK
    W = GDN_WIDTH
    ncb = TB // C

    @pl.when(pl.program_id(1) == 0)
    def _():
        tail_s[0:SUBLANE, :] = jnp.zeros((SUBLANE, tail_s.shape[1]), F32)
        st_s[...] = jnp.zeros_like(st_s)

    xc = _causal_conv_silu(qkv_ref[0], tail_s, cw_ref, None)
    q, k, v = xc[:, 0:W], xc[:, W:2 * W], xc[:, 2 * W:3 * W]
    e = e_ref[...]
    xp = xp_ref[...]
    qn = q * lax.rsqrt(_dot2(q * q, e) + EPS) * (Dh ** -0.5)
    kn = k * lax.rsqrt(_dot2(k * k, e) + EPS)

    sm = sm_ref[0]
    lane = lax.broadcasted_iota(jnp.int32, sm.shape, 1)
    head_lane = lane < H
    g = jnp.where(head_lane, hp_ref[0:1, :] * _softplus(_lanes_from(sm, A_LANE) + hp_ref[1:2, :]), 0.0)
    beta = jnp.where(head_lane, jax.nn.sigmoid(_lanes_from(sm, B_LANE)), 0.0)
    gc, rc = _chunk_scans(g)
    bx = _dot2(beta, xp)
    egx = _dot2(jnp.exp(gc), xp)
    erx = _dot2(jnp.exp(rc), xp)
    kb = kn * bx
    kn_s[...] = kn
    kb_s[...] = kb
    qn_s[...] = qn
    qd_s[...] = qn * egx
    kd_s[...] = kn * erx
    u_s[...] = v * bx
    w_s[...] = kb * egx
    gcx = _dot_sel3(gc, xp)
    gcx_s[...] = gcx
    r384 = lax.broadcasted_iota(jnp.int32, (C, W), 0)
    l384 = lax.broadcasted_iota(jnp.int32, (C, W), 1) & (Dh - 1)
    for c in range(ncb):
        diag_c = jnp.where(r384 == l384, gcx[c * C:(c + 1) * C, :], 0.0)
        gw_s[c] = jnp.broadcast_to(jnp.sum(diag_c, axis=0, keepdims=True), (SUBLANE, W))
        eg_s[c] = jnp.broadcast_to(egx[(c + 1) * C - 1:(c + 1) * C, :], (SUBLANE, W))

    heads = [slice(h * Dh, (h + 1) * Dh) for h in range(H)]

    GL = 4 * Dh
    row_w = lax.broadcasted_iota(jnp.int32, (C, GL), 0)
    col_w = lax.broadcasted_iota(jnp.int32, (C, GL), 1) & (Dh - 1)
    causal_w, strict_w = row_w >= col_w, row_w > col_w
    eye_w = jnp.where(row_w == col_w, 1.0, 0.0).astype(F32)
    bdm = bdm_ref[...]
    mm = functools.partial(lax.dot_general, dimension_numbers=_NN, preferred_element_type=F32)

    def block_diag(m, mask):
        return jnp.concatenate([m] * 4, axis=0) * mask

    npair = min(GDN_PAIRS_PER_ITER, ncb // 2)

    def solve(it, carry):
        chunks = [2 * npair * it + k for k in range(2 * npair)]
        rows = [pl.ds(pl.multiple_of(c * C, C), C) for c in chunks]

        def groups(get, n):
            out = []
            for j in range(npair):
                a0, a1 = get(2 * j), get(2 * j + 1)
                out += [a0[:, 0:4 * n], a1[:, 0:4 * n],
                        jnp.concatenate([a0[:, 4 * n:6 * n], a1[:, 4 * n:6 * n]], axis=1)]
            return out

        def ungroup(ref, vals, n):
            for j in range(npair):
                ref[rows[2 * j], 0:4 * n] = vals[3 * j]
                ref[rows[2 * j + 1], 0:4 * n] = vals[3 * j + 1]
                ref[rows[2 * j], 4 * n:6 * n] = vals[3 * j + 2][:, 0:2 * n]
                ref[rows[2 * j + 1], 4 * n:6 * n] = vals[3 * j + 2][:, 2 * n:4 * n]

        kn_g = groups(lambda k: kn_s[rows[k], :], Dh)
        kb_g = groups(lambda k: kb_s[rows[k], :], Dh)
        qn_g = groups(lambda k: qn_s[rows[k], :], Dh)
        gx_g = groups(lambda k: gcx_s[rows[k], :], Dh)
        gw_g = groups(lambda k: gw_s[chunks[k], 0:1, :], Dh)
        decs = [jnp.exp(jnp.where(causal_w, gx - gw, NEG_BIG)) for gx, gw in zip(gx_g, gw_g)]
        prods = [lax.dot_general(jnp.concatenate([kb, qn], axis=0).astype(BF16), block_diag(kn.astype(BF16), bdm),
                                 _NT, preferred_element_type=F32)
                 for kn, kb, qn in zip(kn_g, kb_g, qn_g)]
        attns = [pr[C:2 * C] * dec for pr, dec in zip(prods, decs)]
        def times(lhs, p=None):
            lh, ll = _split(lhs)
            ph, pl_ = (lh[0:C], ll[0:C]) if p is None else _split(p)
            rh, rl = block_diag(ph, bdm), block_diag(pl_, bdm)
            return mm(lh, rh) + (mm(lh, rl) + mm(ll, rh))

        ps = [-jnp.where(strict_w, pr[0:C] * dec, 0.0) for pr, dec in zip(prods, decs)]
        invs = [eye_w + p for p in ps]
        ps = [times(p) for p in ps]
        for _ in range(4):
            outs = [times(jnp.concatenate([p, inv], axis=0)) for p, inv in zip(ps, invs)]
            invs = [inv + o[C:2 * C] for inv, o in zip(invs, outs)]
            ps = [o[0:C] for o in outs]
        invs = [inv + times(inv, p) for p, inv in zip(ps, invs)]
        def solved(ref):
            outs = []
            for inv, r in zip(invs, groups(lambda k: ref[rows[k], :], Dh)):
                ih, il = _split(inv)
                rh, rl = _split(r)
                rh, rl = block_diag(rh, bdm), block_diag(rl, bdm)
                outs.append(mm(ih, rh) + (mm(ih, rl) + mm(il, rh)))
            return outs

        u_g, w_g = solved(u_s), solved(w_s)
        ungroup(u_s, u_g, Dh)
        ungroup(w_s, w_g, Dh)
        at_g = [a.astype(BF16) for a in attns]
        au_g = [mm(a, block_diag(u.astype(BF16), bdm)) for a, u in zip(at_g, u_g)]
        aw_g = [mm(a, block_diag(w.astype(BF16), bdm)) for a, w in zip(at_g, w_g)]
        qd_g = groups(lambda k: qd_s[rows[k], :], Dh)
        ungroup(au_s, au_g, Dh)
        ungroup(qe_s, [qd - aw for qd, aw in zip(qd_g, aw_g)], Dh)
        for k in range(2 * npair):
            for sl in heads:
                kd_h = kd_s[rows[k], sl]
                ku_s[rows[k], sl] = _dot(kd_h, u_s[rows[k], sl], _TN)
                kw_s[rows[k], sl] = _dot(kd_h, w_s[rows[k], sl], _TN)
        return carry

    lax.fori_loop(0, ncb // (2 * npair), solve, 0)

    def recur(c, carry):
        rows = pl.ds(pl.multiple_of(c * C, C), C)
        st = st_s[...]
        sb = st.astype(BF16)
        lhs = jnp.concatenate([qe_s[rows, :], kw_s[rows, :]], axis=0).astype(BF16)
        out = jnp.concatenate(
            [mm(lhs[:, 0:GL], block_diag(sb[:, 0:GL], bdm)),
             mm(lhs[:, GL:W], jnp.concatenate([sb[:, GL:W]] * 2, axis=0) * bdm[0:W - GL, 0:W - GL])], axis=1)
        o_s[rows, :] = out[0:C] + au_s[rows, :]
        st_s[...] = st * eg_s[c, 0:1, :] + (ku_s[rows, :] - out[C:2 * C])
        return carry

    lax.fori_loop(0, ncb, recur, 0)

    o = o_s[...]
    ms = _dot2(o * o, e) * (1.0 / Dh)
    o = o * lax.rsqrt(ms + EPS) * nw_ref[...]
    o_ref[0] = (o * _silu(z_ref[0])).astype(o_ref.dtype)


def _gdn_call(qkv, z, sm, conv_w, a_log, dt_bias, norm_w):
    B, S, _ = qkv.shape
    TB = min(512, S)
    W, H, Dh = GDN_WIDTH, GDN_HEADS, GDN_HEAD_DIM
    hp = jnp.zeros((SUBLANE, SMALL_W), F32)
    hp = hp.at[0, :H].set(-jnp.exp(a_log.astype(F32))).at[1, :H].set(dt_bias.astype(F32))
    nw = jnp.tile(norm_w.astype(F32), H).reshape(1, W)
    const = lambda shape: pl.BlockSpec(shape, lambda b, i: tuple(0 for _ in shape))
    blk = lambda n: pl.BlockSpec((1, TB, n), lambda b, i: (b, i, 0))
    f = lambda *shape: pltpu.VMEM(shape, F32)
    return pl.pallas_call(
        functools.partial(_gdn_kernel, TB=TB),
        out_shape=jax.ShapeDtypeStruct((B, S, W), BF16),
        grid=(B, S // TB),
        in_specs=[blk(GDN_CONV_DIM), blk(W), blk(SMALL_W),
                  const((CONV_WIDTH, GDN_CONV_DIM)), const((SUBLANE, SMALL_W)), const((1, W)),
                  const((W, W)), const((SMALL_W, W)), const((4 * Dh, 4 * Dh))],
        out_specs=blk(W),
        scratch_shapes=[f(SUBLANE + TB, GDN_CONV_DIM), f(Dh, W)]
        + [f(TB, W)] * 7 + [f(TB // CHUNK, SUBLANE, W), f(TB, W), f(TB // CHUNK, SUBLANE, W)] + [f(TB, W)] * 5,
        compiler_params=_cparams(("arbitrary", "arbitrary")),
        name="gdn_mixer",
    )(qkv, z, sm, conv_w.astype(F32), hp, nw, _block_ones(W, Dh), _head_expand(), _block_ones(4 * Dh, Dh))


def _ssd_kernel(xbc_ref, z_ref, sm_ref, cw_ref, cb_ref, hp_ref, dsk_ref, nw_ref, e_ref, xp_ref,
                hgm_ref, e_hd_ref, ghm_ref, o_ref,
                tail_s, st_s, xs_s, xdt_s, xdd_s, bm_s, cm_s, ea_s, el_s, ac_s, aw_s, y_s, inc_s, *, TB):
    H, P, G, N, C = SSD_HEADS, SSD_HEAD_DIM, SSD_GROUPS, SSD_STATE, CHUNK
    W, GW = SSD_WIDTH, SSD_GROUP_WIDTH
    ncb = TB // C

    @pl.when(pl.program_id(1) == 0)
    def _():
        tail_s[0:SUBLANE, :] = jnp.zeros((SUBLANE, tail_s.shape[1]), F32)
        st_s[...] = jnp.zeros_like(st_s)

    xc = _causal_conv_silu(xbc_ref[0], tail_s, cw_ref, cb_ref[...])
    xs = xc[:, 0:W]
    xp = xp_ref[...]

    sm = sm_ref[0]
    lane = lax.broadcasted_iota(jnp.int32, sm.shape, 1)
    head_lane = lane < H
    dt = jnp.where(head_lane, _softplus(_lanes_from(sm, DT_LANE) + hp_ref[1:2, :]), 0.0)
    acs, rcs = _chunk_scans(dt * hp_ref[0:1, :])
    dtx = _dot2(dt, xp)
    eax = _dot2(jnp.exp(acs), xp)
    erx = _dot2(jnp.exp(rcs), xp)
    xdt = xs * dtx
    xs_s[...] = xs
    xdt_s[...] = xdt
    xdd_s[...] = xdt * erx
    bm_s[...] = xc[:, W:W + G * N]
    cm_s[...] = xc[:, W + G * N:W + 2 * G * N]
    ea_s[...] = eax
    acx = _dot_sel3(acs, xp)
    ac_s[...] = acx
    r_w = lax.broadcasted_iota(jnp.int32, (C, W), 0)
    l_w = lax.broadcasted_iota(jnp.int32, (C, W), 1) & (P - 1)
    for c in range(ncb):
        diag_c = jnp.where(r_w == l_w, acx[c * C:(c + 1) * C, :], 0.0)
        aw_s[c] = jnp.broadcast_to(jnp.sum(diag_c, axis=0, keepdims=True), (SUBLANE, W))
        el_s[c] = jnp.broadcast_to(eax[(c + 1) * C - 1:(c + 1) * C, :], (SUBLANE, W))

    causal_w = r_w >= l_w
    mm = functools.partial(lax.dot_general, dimension_numbers=_NN, preferred_element_type=F32)

    def stacked(m, mask):
        return jnp.concatenate([m] * H, axis=0) * mask

    nloc = min(SSD_CHUNKS_PER_ITER, ncb)

    def local(it, carry):
        cs = [it * nloc + k for k in range(nloc)]
        rows = [pl.ds(pl.multiple_of(c * C, C), C) for c in cs]
        bs = [bm_s[r, :].astype(BF16) for r in rows]
        cbs = [lax.dot_general(cm_s[r, :].astype(BF16), stacked(b, hgm_ref[...]), _NT, preferred_element_type=F32)
               for r, b in zip(rows, bs)]
        segs = [jnp.exp(jnp.where(causal_w, ac_s[r, :] - aw_s[c, 0:1, :], NEG_BIG)) for r, c in zip(rows, cs)]
        for r, cb, seg in zip(rows, cbs, segs):
            y_s[r, :] = mm((cb * seg).astype(BF16), stacked(xdt_s[r, :].astype(BF16), e_hd_ref[...]))
        for r, c, b in zip(rows, cs, bs):
            inc_s[c] = lax.dot_general(b, xdd_s[r, :].astype(BF16), _TN, preferred_element_type=F32) * ghm_ref[...]
        return carry

    lax.fori_loop(0, ncb // nloc, local, 0)

    def recur(c, carry):
        rows = pl.ds(pl.multiple_of(c * C, C), C)
        st = st_s[...]
        y_s[rows, :] = y_s[rows, :] + mm(cm_s[rows, :].astype(BF16), st.astype(BF16)) * ea_s[rows, :]
        st_s[...] = st * el_s[c, 0:1, :] + inc_s[c]
        return carry

    lax.fori_loop(0, ncb, recur, 0)

    y = y_s[...] + dsk_ref[...] * xs_s[...]
    y = y * _silu(z_ref[0])
    ms = _dot2(y * y, e_ref[...]) * (1.0 / GW)
    o_ref[0] = (y * lax.rsqrt(ms + EPS) * nw_ref[...]).astype(o_ref.dtype)


def _ssd_call(xbc, z, sm, conv_w, conv_b, a_log, dt_bias, d_skip, norm_w):
    B, S, _ = xbc.shape
    TB = min(512, S)
    W, H, G, N = SSD_WIDTH, SSD_HEADS, SSD_GROUPS, SSD_STATE
    hp = jnp.zeros((SUBLANE, SMALL_W), F32)
    hp = hp.at[0, :H].set(-jnp.exp(a_log.astype(F32))).at[1, :H].set(dt_bias.astype(F32))
    dsk = jnp.repeat(d_skip.astype(F32), SSD_HEAD_DIM).reshape(1, W)
    const = lambda shape: pl.BlockSpec(shape, lambda b, i: tuple(0 for _ in shape))
    blk = lambda n: pl.BlockSpec((1, TB, n), lambda b, i: (b, i, 0))
    f = lambda *shape: pltpu.VMEM(shape, F32)
    head_of = np.arange(H * CHUNK) // CHUNK
    group_of = np.arange(G * N) // N
    hgm = (head_of[:, None] // (H // G) == group_of[None, :]).astype(np.float32)
    ghm = (group_of[:, None] == (np.arange(W) // SSD_HEAD_DIM // (H // G))[None, :]).astype(np.float32)
    return pl.pallas_call(
        functools.partial(_ssd_kernel, TB=TB),
        out_shape=jax.ShapeDtypeStruct((B, S, W), BF16),
        grid=(B, S // TB),
        in_specs=[blk(SSD_CONV_DIM), blk(W), blk(SMALL_W),
                  const((CONV_WIDTH, SSD_CONV_DIM)), const((1, SSD_CONV_DIM)), const((SUBLANE, SMALL_W)),
                  const((1, W)), const((1, W)), const((W, W)), const((SMALL_W, W)),
                  const((H * CHUNK, G * N)), const((W, W)), const((G * N, W))],
        out_specs=blk(W),
        scratch_shapes=[f(SUBLANE + TB, SSD_CONV_DIM), f(G * N, W),
                        f(TB, W), f(TB, W), f(TB, W), f(TB, G * N), f(TB, G * N), f(TB, W),
                        f(TB // CHUNK, SUBLANE, W),
                        f(TB, W), f(TB // CHUNK, SUBLANE, W), f(TB, W),
                        f(TB // CHUNK, G * N, W)],
        compiler_params=_cparams(("arbitrary", "arbitrary")),
        name="ssd_mixer",
    )(xbc, z, sm, conv_w.astype(F32), conv_b.reshape(1, -1).astype(F32), hp, dsk,
      norm_w.reshape(1, W).astype(F32), _block_ones(W, SSD_GROUP_WIDTH), _head_expand(),
      jnp.asarray(hgm, BF16), _block_ones(W, SSD_HEAD_DIM), jnp.asarray(ghm, F32))


def _out_kernel(x_ref, y1_ref, y2_ref, y3_ref, mod_ref, nw_ref, wo_ref, wr_ref, br_ref, tri_ref,
                x1_ref, h_ref, rt_ref, rtt_ref, cnt_ref, run_s, wo_s):
    first = (pl.program_id(0) == 0) & (pl.program_id(1) == 0)

    @pl.when(first)
    def _():
        wo_s[...] = wo_ref[0].astype(BF16)

    y = (jnp.dot(y1_ref[0], wo_s[0:S5_WIDTH, :], preferred_element_type=F32)
         + jnp.dot(y2_ref[0], wo_s[S5_WIDTH:S5_WIDTH + GDN_WIDTH, :], preferred_element_type=F32)
         + jnp.dot(y3_ref[0], wo_s[S5_WIDTH + GDN_WIDTH:, :], preferred_element_type=F32))
    x1 = x_ref[0] + mod_ref[0, 2:3, :] * y
    x1_ref[0] = x1
    h = _norm_mod(x1, nw_ref[...], mod_ref[0, 4:5, :], mod_ref[0, 3:4, :])
    _store_row_tiles(h_ref, _pack_rows(h))
    lg = lax.dot_general(wr_ref[...], h.astype(BF16), _NT, preferred_element_type=F32) + br_ref[...]
    tm = lg.shape[1]
    row = lax.broadcasted_iota(jnp.int32, lg.shape, 0)
    rowf = row.astype(F32)
    big = float(4 * LANE)
    grp = (row >= GRP_LANE) & (row < GRP_LANE + MOE_GROUPS)
    lgm = jnp.where(grp, lg, -jnp.inf)
    m = jnp.max(lgm, axis=0, keepdims=True)
    gidx = jnp.min(jnp.where(lgm == m, rowf - GRP_LANE, big), axis=0, keepdims=True)
    g_w = 1.0 / jnp.sum(jnp.where(grp, jnp.exp(lg - m), 0.0), axis=0, keepdims=True)
    in_grp = (row < N_EXPERTS) & ((row // EXPERTS_PER_GROUP).astype(F32) == gidx)
    le = jnp.where(in_grp, lg, -jnp.inf)
    v1 = jnp.max(le, axis=0, keepdims=True)
    i1 = jnp.min(jnp.where(le == v1, rowf, big), axis=0, keepdims=True)
    le2 = jnp.where(rowf == i1, -jnp.inf, le)
    v2 = jnp.max(le2, axis=0, keepdims=True)
    i2 = jnp.min(jnp.where(le2 == v2, rowf, big), axis=0, keepdims=True)
    e2 = jnp.exp(v2 - v1)
    w1 = g_w / (1.0 + e2)
    w2 = g_w * e2 / (1.0 + e2)

    @pl.when(first)
    def _():
        run_s[...] = jnp.zeros_like(run_s)

    chosen = jnp.where((rowf == i1) | (rowf == i2), 1.0, 0.0)
    before = jnp.dot(chosen.astype(BF16), tri_ref[...], preferred_element_type=F32) + run_s[:, 0:1]
    p1 = jnp.sum(jnp.where(rowf == i1, before, 0.0), axis=0, keepdims=True)
    p2 = jnp.sum(jnp.where(rowf == i2, before, 0.0), axis=0, keepdims=True)
    run_s[...] = run_s[...] + jnp.sum(chosen, axis=1, keepdims=True)
    cnt_ref[...] = run_s[...]

    vals = jnp.concatenate([i1, i2, w1, w2, p1, p2, jnp.zeros((LANE - 6, tm), F32)], axis=0)
    rtt_ref[0] = vals[0:SUBLANE, :]
    rt_ref[0] = vals.T


def _out_call(x, y1, y2, y3, mod_l, norm_w, w_out, w_router, b_router, l):
    B, S, D = x.shape
    tm = min(512, S)
    blk = lambda n: pl.BlockSpec((1, tm, n), lambda b, i: (b, i, 0))
    const = lambda shape: pl.BlockSpec(shape, lambda b, i: tuple(0 for _ in shape))
    tri = jnp.asarray(np.triu(np.ones((tm, tm), np.float32), 1), BF16)
    nb = S // tm
    tiles = PACK_ROWS
    return pl.pallas_call(
        _out_kernel,
        out_shape=[jax.ShapeDtypeStruct((B, S, D), F32), jax.ShapeDtypeStruct((B * S * tiles, LANE), U32),
                   jax.ShapeDtypeStruct((B, S, LANE), F32), jax.ShapeDtypeStruct((B * nb, SUBLANE, tm), F32),
                   jax.ShapeDtypeStruct((ROUTER_ROWS, LANE), F32)],
        grid=(B, nb),
        in_specs=[blk(D), blk(S5_WIDTH), blk(GDN_WIDTH), blk(SSD_WIDTH),
                  pl.BlockSpec((1, 6, D), lambda b, i: (b, 0, 0)),
                  const((1, D)), pl.BlockSpec((1, D, D), lambda b, i: (l, 0, 0)),
                  const((ROUTER_ROWS, D)), const((ROUTER_ROWS, 1)), const((tm, tm))],
        out_specs=[blk(D), pl.BlockSpec((tm * tiles, LANE), lambda b, i: (b * nb + i, 0)),
                   blk(LANE), pl.BlockSpec((1, SUBLANE, tm), lambda b, i: (b * nb + i, 0, 0)),
                   const((ROUTER_ROWS, LANE))],
        scratch_shapes=[pltpu.VMEM((ROUTER_ROWS, LANE), F32), pltpu.VMEM((D, D), BF16)],
        compiler_params=_cparams(("arbitrary", "arbitrary")),
        name="out_proj_router",
    )(x, y1, y2, y3, mod_l, norm_w.reshape(1, D), w_out, w_router, b_router, tri)


def _router_params(w_grp, b_grp, w_rt, b_rt):
    D = w_grp.shape[0]
    w = jnp.zeros((ROUTER_ROWS, D), F32).at[0:N_EXPERTS].set(w_rt.T).at[GRP_LANE:GRP_LANE + MOE_GROUPS].set(w_grp.T)
    b = jnp.zeros((ROUTER_ROWS, 1), F32).at[0:N_EXPERTS, 0].set(b_rt).at[GRP_LANE:GRP_LANE + MOE_GROUPS, 0].set(b_grp)
    return w.astype(BF16), b


def _tile_copy_loop(n, fn):
    def body(t, carry):
        fn(t)
        return carry
    lax.fori_loop(0, n, body, 0, unroll=32)


def _scatter_kernel(d0_ref, d1_ref, h_ref, xs_hbm, stage, sem):
    i = pl.program_id(0)
    n = pl.num_programs(0)
    slot = lax.rem(i, 2)
    tm = d0_ref.shape[-1]

    def copy(s, t, dst):
        return pltpu.make_async_copy(stage.at[s, pl.ds(t * PACK_ROWS, PACK_ROWS)],
                                     xs_hbm.at[pl.ds(dst * PACK_ROWS, PACK_ROWS)], sem.at[s])

    def wait_slot(s):
        _tile_copy_loop(TOP_K * tm, lambda t: copy(s, 0, 0).wait())

    @pl.when(i >= 2)
    def _():
        wait_slot(slot)

    stage[slot] = h_ref[...]

    def start(t):
        copy(slot, t, d0_ref[0, 0, t]).start(priority=0)
        copy(slot, t, d1_ref[0, 0, t]).start(priority=1)
    _tile_copy_loop(tm, start)

    @pl.when(i == n - 1)
    def _():
        @pl.when(i >= 1)
        def _():
            wait_slot(1 - slot)
        wait_slot(slot)


def _scatter_call(h2t, dest0, dest1, n_rows):
    n_tiles, _, tm = dest0.shape
    idx_blk = pl.BlockSpec((1, 1, tm), lambda i: (i, 0, 0), memory_space=pltpu.SMEM)
    return pl.pallas_call(
        _scatter_kernel,
        out_shape=jax.ShapeDtypeStruct((n_rows * PACK_ROWS, LANE), U32),
        grid=(n_tiles,),
        in_specs=[idx_blk, idx_blk, pl.BlockSpec((tm * PACK_ROWS, LANE), lambda i: (i, 0))],
        out_specs=pl.BlockSpec(memory_space=pl.ANY),
        scratch_shapes=[pltpu.VMEM((2, tm * PACK_ROWS, LANE), U32), pltpu.SemaphoreType.DMA((2,))],
        compiler_params=_cparams(("arbitrary",)),
        name="moe_scatter",
    )(dest0, dest1, h2t)


def _expert_kernel(nused_ref, blke_ref, nvalid_ref, xs_ref, wg_ref, wu_ref, wd_ref, ys_ref, wg_s, wu_s, wd_s):
    i = pl.program_id(0)

    @pl.when(i < nused_ref[0])
    def _():
        @pl.when((i == 0) | (blke_ref[i] != blke_ref[jnp.maximum(i - 1, 0)]))
        def _():
            wg_s[...] = wg_ref[0, 0].astype(BF16)
            wu_s[...] = wu_ref[0, 0].astype(BF16)
            wd_s[...] = wd_ref[0, 0].astype(BF16)

        row = lax.broadcasted_iota(jnp.int32, (MOE_ROWS, 1), 0)
        words = jnp.where(row < nvalid_ref[i], _load_row_tiles(xs_ref, MOE_ROWS), U32(0))
        xb = _unpack_rows(words).astype(BF16)
        hid = _silu(jnp.dot(xb, wg_s[...], preferred_element_type=F32)) * jnp.dot(xb, wu_s[...], preferred_element_type=F32)
        _store_row_tiles(ys_ref, _pack_rows(jnp.dot(hid.astype(BF16), wd_s[...], preferred_element_type=F32)))

    @pl.when(i >= nused_ref[0])
    def _():
        ys_ref[...] = jnp.zeros_like(ys_ref)


def _expert_call(xs, n_used, blk_e, n_valid, w_gate, w_up, w_down, l):
    D = w_gate.shape[2]
    n_blk = blk_e.shape[0]
    rows_blk = pl.BlockSpec((MOE_ROWS * PACK_ROWS, LANE), lambda i, nu, be, nv: (i, 0))
    grid_spec = pltpu.PrefetchScalarGridSpec(
        num_scalar_prefetch=3,
        grid=(n_blk,),
        in_specs=[rows_blk,
                  pl.BlockSpec((1, 1, D, D_EXPERT), lambda i, nu, be, nv: (l, be[i], 0, 0)),
                  pl.BlockSpec((1, 1, D, D_EXPERT), lambda i, nu, be, nv: (l, be[i], 0, 0)),
                  pl.BlockSpec((1, 1, D_EXPERT, D), lambda i, nu, be, nv: (l, be[i], 0, 0))],
        out_specs=rows_blk,
        scratch_shapes=[pltpu.VMEM((D, D_EXPERT), BF16), pltpu.VMEM((D, D_EXPERT), BF16),
                        pltpu.VMEM((D_EXPERT, D), BF16)],
    )
    return pl.pallas_call(
        _expert_kernel,
        out_shape=jax.ShapeDtypeStruct(xs.shape, U32),
        grid_spec=grid_spec,
        compiler_params=_cparams(("arbitrary",)),
        name="expert_mlp",
    )(n_used, blk_e, n_valid, xs, w_gate, w_up, w_down)


def _dest_kernel(rt_ref, ps_ref, o_ref):
    n_sub, _, tm = o_ref.shape
    expert = lax.broadcasted_iota(jnp.int32, (N_EXPERTS, tm), 0).astype(F32)
    starts = ps_ref[0:N_EXPERTS, :]
    for j in range(n_sub):
        rows = []
        for k in range(TOP_K):
            start = jnp.sum(jnp.where(expert == rt_ref[j, k:k + 1, :], starts, 0.0), axis=0, keepdims=True)
            rows.append(start + rt_ref[j, 4 + k:5 + k, :])
        rows.append(jnp.zeros((SUBLANE - TOP_K, tm), F32))
        o_ref[j] = jnp.concatenate(rows, axis=0).astype(jnp.int32)


def _dest_call(route_t, pstart_col):
    n_tiles, _, tm = route_t.shape
    n_sub = min(DEST_TILES_PER_STEP, n_tiles)
    blk = pl.BlockSpec((n_sub, SUBLANE, tm), lambda i: (i, 0, 0))
    return pl.pallas_call(
        _dest_kernel,
        out_shape=jax.ShapeDtypeStruct((n_tiles, SUBLANE, tm), jnp.int32),
        grid=(n_tiles // n_sub,),
        in_specs=[blk, pl.BlockSpec((LANE, 1), lambda i: (0, 0))],
        out_specs=blk,
        compiler_params=_cparams(("arbitrary",)),
        name="moe_dest",
    )(route_t, pstart_col)


def _dispatch(route_t, counts, N, tm):
    L_pad = N * TOP_K + N_EXPERTS * MOE_ROWS
    n_blk = L_pad // MOE_ROWS
    counts = counts.astype(jnp.int32)
    padded = ((counts + MOE_ROWS - 1) // MOE_ROWS) * MOE_ROWS
    pend = jnp.cumsum(padded)
    pstart = pend - padded
    n_used = (pend[-1] // MOE_ROWS).astype(jnp.int32).reshape(1)
    blk_row0 = jnp.arange(n_blk, dtype=jnp.int32) * MOE_ROWS
    blk_e = jnp.minimum(jnp.sum((pend[None, :] <= blk_row0[:, None]).astype(jnp.int32), axis=1), N_EXPERTS - 1)
    n_valid = jnp.clip((pstart + counts)[blk_e] - blk_row0, 0, MOE_ROWS).astype(jnp.int32)
    pstart_col = jnp.zeros((LANE, 1), F32).at[0:N_EXPERTS, 0].set(pstart.astype(F32))
    dest = _dest_call(route_t, pstart_col)
    dest0 = dest[:, 0, :].reshape(N // tm, 1, tm)
    dest1 = dest[:, 1, :].reshape(N // tm, 1, tm)
    return n_used, blk_e.astype(jnp.int32), n_valid, dest0, dest1, L_pad


def _combine_kernel(d0_ref, d1_ref, d0n_ref, d1n_ref, x_ref, rt_ref, mod_ref, nf_ref, ys_hbm, o_ref,
                    buf, sem, *, final):
    i = pl.program_id(0)
    n = pl.num_programs(0)
    slot = lax.rem(i, 2)
    tm = x_ref.shape[0]

    def copy(s, k, t, src):
        return pltpu.make_async_copy(ys_hbm.at[pl.ds(src * PACK_ROWS, PACK_ROWS)],
                                     buf.at[s, k, pl.ds(t * PACK_ROWS, PACK_ROWS)], sem.at[s])

    def start_tile(s, a_ref, b_ref):
        def start(t):
            copy(s, 0, t, a_ref[0, 0, t]).start(priority=0)
            copy(s, 1, t, b_ref[0, 0, t]).start(priority=1)
        _tile_copy_loop(tm, start)

    @pl.when(i == 0)
    def _():
        start_tile(0, d0_ref, d1_ref)

    @pl.when(i + 1 < n)
    def _():
        start_tile(1 - slot, d0n_ref, d1n_ref)

    _tile_copy_loop(TOP_K * tm, lambda t: copy(slot, 0, 0, 0).wait())
    rt = rt_ref[...]
    y = (rt[:, 2:3] * _unpack_rows(_load_row_tiles(buf.at[slot, 0], tm))
         + rt[:, 3:4] * _unpack_rows(_load_row_tiles(buf.at[slot, 1], tm)))
    x2 = x_ref[...] + mod_ref[0, 5:6, :] * y
    if final:
        ms = jnp.mean(x2 * x2, axis=-1, keepdims=True)
        x2 = x2 * lax.rsqrt(ms + EPS) * nf_ref[...]
    o_ref[...] = x2


def _combine_call(x1, ys, route, dest0, dest1, mod_l, norm_final, final):
    B, S, D = x1.shape
    N = B * S
    n_tiles, _, tm = dest0.shape
    per_b = S // tm
    idx_blk = lambda fn: pl.BlockSpec((1, 1, tm), fn, memory_space=pltpu.SMEM)
    cur = lambda i: (i, 0, 0)
    nxt = lambda i: (jnp.minimum(i + 1, n_tiles - 1), 0, 0)
    out = pl.pallas_call(
        functools.partial(_combine_kernel, final=final),
        out_shape=jax.ShapeDtypeStruct((N, D), F32),
        grid=(n_tiles,),
        in_specs=[idx_blk(cur), idx_blk(cur), idx_blk(nxt), idx_blk(nxt),
                  pl.BlockSpec((tm, D), lambda i: (i, 0)),
                  pl.BlockSpec((tm, LANE), lambda i: (i, 0)),
                  pl.BlockSpec((1, 6, D), lambda i: (i // per_b, 0, 0)),
                  pl.BlockSpec((1, D), lambda i: (0, 0)),
                  pl.BlockSpec(memory_space=pl.ANY)],
        out_specs=pl.BlockSpec((tm, D), lambda i: (i, 0)),
        scratch_shapes=[pltpu.VMEM((2, TOP_K, tm * PACK_ROWS, LANE), U32), pltpu.SemaphoreType.DMA((2,))],
        compiler_params=_cparams(("arbitrary",)),
        name="moe_combine",
    )(dest0, dest1, dest0, dest1, x1.reshape(N, D), route.reshape(N, LANE), mod_l, norm_final.reshape(1, D), ys)
    return out.reshape(B, S, D)


def _layer(x, mod_l, p, big, l, final, norm_final):
    B, S, D = x.shape
    N = B * S
    s5_u, g_qkv, g_z, s_z, s_xbc, small = _proj_call(x, mod_l, p["norm_mix"], big["w_in"], l)
    bm, cm, lam_rows = _s5_params(p["s5_a_re"], p["s5_a_im"], p["s5_b_re"], p["s5_b_im"],
                                  p["s5_c_re"], p["s5_c_im"], p["s5_log_dt"])
    y_s5 = _s5_call(s5_u, bm, cm, lam_rows, p["s5_d"], p["s5_w_glu"], p["s5_norm"])
    y_gdn = _gdn_call(g_qkv, g_z, small, p["gdn_conv_w"], p["gdn_a_log"], p["gdn_dt_bias"], p["gdn_norm"])
    y_ssd = _ssd_call(s_xbc, s_z, small, p["ssd_conv_w"], p["ssd_conv_b"], p["ssd_a_log"], p["ssd_dt_bias"],
                      p["ssd_d"], p["ssd_norm"])
    w_router, b_router = _router_params(p["moe_w_grp"], p["moe_b_grp"], p["moe_w_rt"], p["moe_b_rt"])
    x1, h2, route, route_t, counts = _out_call(x, y_s5, y_gdn, y_ssd, mod_l, p["norm_ffn"], big["w_out"],
                                               w_router, b_router, l)
    n_used, blk_e, n_valid, dest0, dest1, n_rows = _dispatch(route_t, counts[0:N_EXPERTS, 0], N,
                                                             min(MOE_TOKEN_TILE, S))
    xs = _scatter_call(h2, dest0, dest1, n_rows)
    ys = _expert_call(xs, n_used, blk_e, n_valid, big["moe_w_gate"], big["moe_w_up"], big["moe_w_down"], l)
    return _combine_call(x1, ys, route, dest0, dest1, mod_l, norm_final, final)


def kernel(x, c, w_ada, b_ada, norm_mix, norm_ffn, w_in, w_out, s5_a_re, s5_a_im, s5_b_re, s5_b_im, s5_c_re, s5_c_im, s5_d, s5_log_dt, s5_w_glu, s5_norm, gdn_conv_w, gdn_a_log, gdn_dt_bias, gdn_norm, ssd_conv_w, ssd_conv_b, ssd_a_log, ssd_dt_bias, ssd_d, ssd_norm, moe_w_grp, moe_b_grp, moe_w_rt, moe_b_rt, moe_w_gate, moe_w_up, moe_w_down, norm_final):
    stacked = dict(norm_mix=norm_mix, norm_ffn=norm_ffn, s5_a_re=s5_a_re, s5_a_im=s5_a_im,
                   s5_b_re=s5_b_re, s5_b_im=s5_b_im, s5_c_re=s5_c_re, s5_c_im=s5_c_im, s5_d=s5_d,
                   s5_log_dt=s5_log_dt, s5_w_glu=s5_w_glu, s5_norm=s5_norm, gdn_conv_w=gdn_conv_w,
                   gdn_a_log=gdn_a_log, gdn_dt_bias=gdn_dt_bias, gdn_norm=gdn_norm, ssd_conv_w=ssd_conv_w,
                   ssd_conv_b=ssd_conv_b, ssd_a_log=ssd_a_log, ssd_dt_bias=ssd_dt_bias, ssd_d=ssd_d,
                   ssd_norm=ssd_norm, moe_w_grp=moe_w_grp, moe_b_grp=moe_b_grp, moe_w_rt=moe_w_rt,
                   moe_b_rt=moe_b_rt)
    big = dict(w_in=_w_in_prep_call(w_in), w_out=w_out, moe_w_gate=moe_w_gate, moe_w_up=moe_w_up,
               moe_w_down=moe_w_down)
    L = w_in.shape[0]
    B, S, D = x.shape
    mod = _mod_call(c, w_ada, b_ada).reshape(L, B, 6, D)
    for l in range(L):
        p = {k: v[l] for k, v in stacked.items()}
        x = _layer(x, mod[l], p, big, l, l == L - 1, norm_final)
    return x
```

```python
import functools

import numpy as np
import jax
import jax.numpy as jnp
from jax import lax
from jax.experimental import pallas as pl
from jax.experimental.pallas import tpu as pltpu

F32 = jnp.float32
BF16 = jnp.bfloat16

D_MODEL = 1024
DEPTH = 4
EPS = 1e-6
CONV_WIDTH = 4
CHUNK = 64
S5_WIDTH = 256
S5_CH = 16
S5_GROUPS = 16
S5_STATE = 64
S5_LANES = S5_GROUPS * S5_STATE
GDN_WIDTH = 384
GDN_HEAD_DIM = 64
GDN_HEADS = 6
GDN_CONV_DIM = 3 * GDN_WIDTH
SSD_WIDTH = 384
SSD_HEAD_DIM = 64
SSD_HEADS = 6
SSD_GROUPS = 2
SSD_STATE = 128
SSD_GROUP_WIDTH = SSD_WIDTH // SSD_GROUPS
SSD_CONV_DIM = SSD_WIDTH + 2 * SSD_GROUPS * SSD_STATE
PROJ_SIZES = (S5_WIDTH, GDN_CONV_DIM, GDN_WIDTH, GDN_HEADS, GDN_HEADS, SSD_WIDTH, SSD_CONV_DIM, SSD_HEADS)
MOE_GROUPS = 4
EXPERTS_PER_GROUP = 8
N_EXPERTS = 32
TOP_K = 2
D_EXPERT = 256

LANE = 128
SUBLANE = 8
SMALL_W = LANE
A_LANE, B_LANE, DT_LANE = 0, GDN_HEADS, 2 * GDN_HEADS
GRP_LANE = N_EXPERTS
ROUTER_ROWS = 40
MOE_ROWS = 512
MOE_TOKEN_TILE = 512
DEST_TILES_PER_STEP = 8
SSD_CHUNKS_PER_ITER = 8
GDN_PAIRS_PER_ITER = 4
VMEM_LIMIT = 56 * 1024 * 1024
NEG_BIG = -1e30


def _cparams(sem):
    return pltpu.CompilerParams(dimension_semantics=sem, vmem_limit_bytes=VMEM_LIMIT)


def _split(a):
    hi = a.astype(BF16)
    lo = (a - hi.astype(F32)).astype(BF16)
    return hi, lo


_NN = (((1,), (0,)), ((), ()))
_NT = (((1,), (1,)), ((), ()))
_TN = (((0,), (0,)), ((), ()))


def _dot(a, b, dims=_NN):
    return lax.dot_general(a.astype(BF16), b.astype(BF16), dims, preferred_element_type=F32)


def _dot2(a, b_bf16):
    hi, lo = _split(a)
    return (lax.dot_general(hi, b_bf16, _NN, preferred_element_type=F32)
            + lax.dot_general(lo, b_bf16, _NN, preferred_element_type=F32))


def _dot_sel3(a, b_bf16):
    h1 = a.astype(BF16)
    r1 = a - h1.astype(F32)
    h2 = r1.astype(BF16)
    h3 = (r1 - h2.astype(F32)).astype(BF16)
    d = functools.partial(lax.dot_general, dimension_numbers=_NN, preferred_element_type=F32)
    return d(h1, b_bf16) + (d(h2, b_bf16) + d(h3, b_bf16))


def _dot3(a, b):
    ah, al = _split(a)
    bh, bl = _split(b)
    d = functools.partial(lax.dot_general, dimension_numbers=_NN, preferred_element_type=F32)
    return d(ah, bh) + (d(ah, bl) + d(al, bh))


def _silu(x):
    return x * jax.nn.sigmoid(x)


def _softplus(x):
    return jnp.maximum(x, 0.0) + jnp.log(1.0 + jnp.exp(-jnp.abs(x)))


def _norm_mod(x, w, scale, shift):
    ms = jnp.mean(x * x, axis=-1, keepdims=True)
    return (x * lax.rsqrt(ms + EPS) * w) * (1.0 + scale) + shift


def _mod_kernel(c_ref, w_ref, b_ref, o_ref):
    cond = _silu(c_ref[...])
    o_ref[0] = _dot3(cond, w_ref[0]) + b_ref[0]


def _mod_call(c, w_ada, b_ada):
    L, D, W = w_ada.shape
    B = c.shape[0]
    tn = 1536
    return pl.pallas_call(
        _mod_kernel,
        out_shape=jax.ShapeDtypeStruct((L, B, W), F32),
        grid=(L, W // tn),
        in_specs=[pl.BlockSpec((B, D), lambda l, j: (0, 0)),
                  pl.BlockSpec((1, D, tn), lambda l, j: (l, 0, j)),
                  pl.BlockSpec((1, 1, tn), lambda l, j: (l, 0, j))],
        out_specs=pl.BlockSpec((1, B, tn), lambda l, j: (l, 0, j)),
        compiler_params=_cparams(("arbitrary", "arbitrary")),
        name="adaln_mod",
    )(c, w_ada, b_ada.reshape(L, 1, W))


PROJ_OUT_W = (S5_WIDTH, GDN_CONV_DIM, GDN_WIDTH, SSD_WIDTH, SSD_CONV_DIM, SMALL_W)


def _proj_kernel(x_ref, mod_ref, nw_ref, w_ref, *o_refs):
    x = x_ref[0]
    h = _norm_mod(x, nw_ref[...], mod_ref[0, 1:2, :], mod_ref[0, 0:1, :]).astype(BF16)
    proj = jnp.dot(h, w_ref[0], preferred_element_type=F32)
    off = 0
    for o_ref in o_refs:
        n = o_ref.shape[-1]
        o_ref[0] = proj[:, off:off + n]
        off += n


_W_HEAD = S5_WIDTH + GDN_CONV_DIM + GDN_WIDTH
_W_AB = _W_HEAD + 2 * GDN_HEADS
_W_SSD = _W_AB + SSD_WIDTH + SSD_CONV_DIM
_W_END = _W_SSD + SSD_HEADS
PROJ_W = sum(PROJ_OUT_W)
W_PREP_ROWS = 256


def _w_in_prep_kernel(w_ref, o_ref):
    n_ssd = _W_SSD - _W_AB
    for r in range(0, w_ref.shape[1], W_PREP_ROWS):
        w = w_ref[0, r:r + W_PREP_ROWS, :]
        rows = slice(r, r + W_PREP_ROWS)
        o_ref[0, rows, 0:_W_HEAD] = w[:, 0:_W_HEAD].astype(BF16)
        o_ref[0, rows, _W_HEAD:_W_HEAD + n_ssd] = w[:, _W_AB:_W_SSD].astype(BF16)
        small = jnp.concatenate([w[:, _W_HEAD:_W_AB], w[:, _W_SSD:_W_END],
                                 jnp.zeros((W_PREP_ROWS, SMALL_W - (_W_AB - _W_HEAD) - (_W_END - _W_SSD)), F32)], axis=1)
        o_ref[0, rows, _W_HEAD + n_ssd:PROJ_W] = small.astype(BF16)


def _w_in_prep_call(w_in):
    L, D, W = w_in.shape
    return pl.pallas_call(
        _w_in_prep_kernel,
        out_shape=jax.ShapeDtypeStruct((L, D, PROJ_W), BF16),
        grid=(L,),
        in_specs=[pl.BlockSpec((1, D, W), lambda l: (l, 0, 0))],
        out_specs=pl.BlockSpec((1, D, PROJ_W), lambda l: (l, 0, 0)),
        compiler_params=_cparams(("arbitrary",)),
        name="w_in_prep",
    )(w_in)


def _proj_call(x, mod_l, norm_w, w_arr, l):
    B, S, D = x.shape
    tm = min(512, S)
    out_shape = [jax.ShapeDtypeStruct((B, S, n), F32) for n in PROJ_OUT_W]
    out_specs = [pl.BlockSpec((1, tm, n), lambda b, i: (b, i, 0)) for n in PROJ_OUT_W]
    return pl.pallas_call(
        _proj_kernel,
        out_shape=out_shape,
        grid=(B, S // tm),
        in_specs=[pl.BlockSpec((1, tm, D), lambda b, i: (b, i, 0)),
                  pl.BlockSpec((1, 6, D), lambda b, i: (b, 0, 0)),
                  pl.BlockSpec((1, D), lambda b, i: (0, 0)),
                  pl.BlockSpec((1, D, PROJ_W), lambda b, i: (l, 0, 0))],
        out_specs=out_specs,
        compiler_params=_cparams(("arbitrary", "arbitrary")),
        name="norm_in_proj",
    )(x, mod_l, norm_w.reshape(1, D), w_arr)


def _s5_kernel(u_ref, bm_ref, cm_ref, lam_ref, dsk_ref, glu_ref, nw_ref, perm_ref, o_ref,
               x_s, st_s, ubt_s, u_s, *, T):
    B = u_ref.shape[0]
    P = S5_LANES

    @pl.when(pl.program_id(0) == 0)
    def _():
        st_s[...] = jnp.zeros_like(st_s)

    nw = S5_WIDTH // LANE
    ubt = u_ref[...].reshape(B * T, S5_WIDTH)
    for j in range(nw):
        ubt_s[j] = ubt[:, j * LANE:(j + 1) * LANE]

    def regroup(t, carry):
        for j in range(nw):
            u_s[pl.ds(pl.multiple_of(t * B, B), B), j * LANE:(j + 1) * LANE] = ubt_s[j, pl.ds(t, B, stride=T), :]
        return carry

    lax.fori_loop(0, T, regroup, 0, unroll=8)
    u = u_s[...]
    x_s[...] = jnp.dot(u.astype(BF16), bm_ref[...], preferred_element_type=F32)
    lr = jnp.broadcast_to(lam_ref[0:1, :], (B, P))
    li = jnp.broadcast_to(lam_ref[1:2, :], (B, P))

    def step(t, carry):
        sr, si = carry
        rows = pl.ds(pl.multiple_of(t * B, B), B)
        nr = lr * sr - li * si + x_s[rows, 0:P]
        ni = lr * si + li * sr + x_s[rows, P:2 * P]
        x_s[rows, 0:P] = nr
        x_s[rows, P:2 * P] = ni
        return nr, ni

    sr, si = lax.fori_loop(0, T, step, (st_s[:, 0:P], st_s[:, P:2 * P]), unroll=4)
    st_s[:, 0:P] = sr
    st_s[:, P:2 * P] = si

    y = jnp.dot(x_s[...].astype(BF16), cm_ref[...], preferred_element_type=F32) + dsk_ref[...] * u
    y = jax.nn.gelu(y)
    y = y * jax.nn.sigmoid(jnp.dot(y.astype(BF16), glu_ref[...], preferred_element_type=F32))
    ms = jnp.mean(y * y, axis=-1, keepdims=True)
    y = (y * lax.rsqrt(ms + EPS) * nw_ref[...]).astype(BF16)
    y = jnp.dot(perm_ref[...], y, preferred_element_type=F32)
    o_ref[...] = y.reshape(B, T, S5_WIDTH).astype(o_ref.dtype)


def _s5_params(a_re, a_im, b_re, b_im, c_re, c_im, log_dt):
    G, P, CH = S5_GROUPS, S5_STATE, S5_CH
    lam = lax.complex(a_re.astype(F32), a_im.astype(F32))
    step = jnp.exp(log_dt.astype(F32))[:, None]
    lam_bar = jnp.exp(lam * step)
    b_bar = ((lam_bar - 1.0) / lam)[..., None] * lax.complex(b_re.astype(F32), b_im.astype(F32))
    eye = jnp.eye(G, dtype=F32)
    bre = jnp.einsum('gpc,gh->gchp', b_bar.real, eye).reshape(G * CH, G * P)
    bim = jnp.einsum('gpc,gh->gchp', b_bar.imag, eye).reshape(G * CH, G * P)
    bm = jnp.concatenate([bre, bim], axis=1).astype(BF16)
    cre = jnp.einsum('gcp,gh->gphc', c_re.astype(F32), eye).reshape(G * P, G * CH)
    cim = jnp.einsum('gcp,gh->gphc', c_im.astype(F32), eye).reshape(G * P, G * CH)
    cm = jnp.concatenate([cre, -cim], axis=0).astype(BF16)
    lam_rows = jnp.zeros((SUBLANE, G * P), F32)
    lam_rows = lam_rows.at[0].set(lam_bar.real.reshape(-1)).at[1].set(lam_bar.imag.reshape(-1))
    return bm, cm, lam_rows


def _s5_call(u, bm, cm, lam_rows, d_skip, w_glu, norm_w):
    B, S, W = u.shape
    T = min(128, S)
    P2 = 2 * S5_LANES
    const = lambda shape: pl.BlockSpec(shape, lambda i: tuple(0 for _ in shape))
    r = np.arange(B * T)
    perm = np.zeros((B * T, B * T), np.float32)
    perm[r, (r % T) * B + r // T] = 1.0
    return pl.pallas_call(
        functools.partial(_s5_kernel, T=T),
        out_shape=jax.ShapeDtypeStruct((B, S, W), BF16),
        grid=(S // T,),
        in_specs=[pl.BlockSpec((B, T, W), lambda i: (0, i, 0)),
                  const((W, P2)), const((P2, W)), const((SUBLANE, S5_LANES)),
                  const((1, W)), const((W, W)), const((1, W)), const((B * T, B * T))],
        out_specs=pl.BlockSpec((B, T, W), lambda i: (0, i, 0)),
        scratch_shapes=[pltpu.VMEM((B * T, P2), F32), pltpu.VMEM((B, P2), F32),
                        pltpu.VMEM((W // LANE, B * T, LANE), F32), pltpu.VMEM((B * T, W), F32)],
        compiler_params=_cparams(("arbitrary",)),
        name="s5_mixer",
    )(u, bm, cm, lam_rows, d_skip.reshape(1, W).astype(F32), w_glu.astype(BF16), norm_w.reshape(1, W).astype(F32),
      jnp.asarray(perm, BF16))


def _causal_conv_silu(x, xf_ref, cw_ref, bias):
    n = x.shape[0]
    xf_ref[SUBLANE:, :] = x
    acc = x * cw_ref[CONV_WIDTH - 1:CONV_WIDTH, :]
    for k in range(1, CONV_WIDTH):
        acc = acc + xf_ref[SUBLANE - k:SUBLANE - k + n, :] * cw_ref[CONV_WIDTH - 1 - k:CONV_WIDTH - k, :]
    xf_ref[0:SUBLANE, :] = x[n - SUBLANE:, :]
    if bias is not None:
        acc = acc + bias
    return _silu(acc)


U32 = jnp.uint32
PACK_ROWS = D_MODEL // (2 * LANE)


def _pack_rows(x):
    h = x.shape[1] // 2
    xb = x.astype(BF16).astype(F32)
    lo = lax.bitcast_convert_type(xb[:, 0:h], U32) >> 16
    hi = lax.bitcast_convert_type(xb[:, h:2 * h], U32) & U32(0xFFFF0000)
    return lo | hi


def _unpack_rows(w):
    lo = lax.bitcast_convert_type(w << 16, F32)
    hi = lax.bitcast_convert_type(w & U32(0xFFFF0000), F32)
    return jnp.concatenate([lo, hi], axis=1)


def _store_row_tiles(ref, val):
    n = val.shape[0]
    for j in range(PACK_ROWS):
        ref[pl.ds(j, n, stride=PACK_ROWS), :] = val[:, j * LANE:(j + 1) * LANE]


def _load_row_tiles(ref, n):
    return jnp.concatenate([ref[pl.ds(j, n, stride=PACK_ROWS), :] for j in range(PACK_ROWS)], axis=1)


def _lanes_from(sm, off):
    return pltpu.roll(sm, SMALL_W - off, axis=1) if off else sm


def _chunk_scans(g):
    n = g.shape[0]
    rin = lax.broadcasted_iota(jnp.int32, g.shape, 0) & (CHUNK - 1)
    pre = g
    suf = jnp.where(rin < CHUNK - 1, pltpu.roll(g, n - 1, axis=0), 0.0)
    s = 1
    while s < CHUNK:
        pre = pre + jnp.where(rin >= s, pltpu.roll(pre, s, axis=0), 0.0)
        suf = suf + jnp.where(rin + s <= CHUNK - 1, pltpu.roll(suf, n - s, axis=0), 0.0)
        s *= 2
    return pre, suf


def _head_expand():
    m = np.zeros((SMALL_W, GDN_WIDTH), np.float32)
    for h in range(GDN_HEADS):
        m[h, h * GDN_HEAD_DIM:(h + 1) * GDN_HEAD_DIM] = 1.0
    return jnp.asarray(m, BF16)


def _block_ones(width, blk):
    idx = np.arange(width) // blk
    return jnp.asarray((idx[:, None] == idx[None, :]).astype(np.float32), BF16)


def _gdn_kernel(qkv_ref, z_ref, sm_ref, cw_ref, hp_ref, nw_ref, e_ref, xp_ref, bdm_ref, o_ref,
                tail_s, st_s, kn_s, kb_s, qn_s, qd_s, kd_s, u_s, w_s, eg_s, gcx_s, gw_s,
                o_s, au_s, qe_s, ku_s, kw_s, *, TB):
    H, Dh, C = GDN_HEADS, GDN_HEAD_DIM, CHUNK
    W = GDN_WIDTH
    ncb = TB // C

    @pl.when(pl.program_id(1) == 0)
    def _():
        tail_s[0:SUBLANE, :] = jnp.zeros((SUBLANE, tail_s.shape[1]), F32)
        st_s[...] = jnp.zeros_like(st_s)

    xc = _causal_conv_silu(qkv_ref[0], tail_s, cw_ref, None)
    q, k, v = xc[:, 0:W], xc[:, W:2 * W], xc[:, 2 * W:3 * W]
    e = e_ref[...]
    xp = xp_ref[...]
    qn = q * lax.rsqrt(_dot2(q * q, e) + EPS) * (Dh ** -0.5)
    kn = k * lax.rsqrt(_dot2(k * k, e) + EPS)

    sm = sm_ref[0]
    lane = lax.broadcasted_iota(jnp.int32, sm.shape, 1)
    head_lane = lane < H
    g = jnp.where(head_lane, hp_ref[0:1, :] * _softplus(_lanes_from(sm, A_LANE) + hp_ref[1:2, :]), 0.0)
    beta = jnp.where(head_lane, jax.nn.sigmoid(_lanes_from(sm, B_LANE)), 0.0)
    gc, rc = _chunk_scans(g)
    bx = _dot2(beta, xp)
    egx = _dot2(jnp.exp(gc), xp)
    erx = _dot2(jnp.exp(rc), xp)
    kb = kn * bx
    kn_s[...] = kn
    kb_s[...] = kb
    qn_s[...] = qn
    qd_s[...] = qn * egx
    kd_s[...] = kn * erx
    u_s[...] = v * bx
    w_s[...] = kb * egx
    gcx = _dot_sel3(gc, xp)
    gcx_s[...] = gcx
    r384 = lax.broadcasted_iota(jnp.int32, (C, W), 0)
    l384 = lax.broadcasted_iota(jnp.int32, (C, W), 1) & (Dh - 1)
    for c in range(ncb):
        diag_c = jnp.where(r384 == l384, gcx[c * C:(c + 1) * C, :], 0.0)
        gw_s[c] = jnp.broadcast_to(jnp.sum(diag_c, axis=0, keepdims=True), (SUBLANE, W))
        eg_s[c] = jnp.broadcast_to(egx[(c + 1) * C - 1:(c + 1) * C, :], (SUBLANE, W))

    heads = [slice(h * Dh, (h + 1) * Dh) for h in range(H)]

    GL = 4 * Dh
    row_w = lax.broadcasted_iota(jnp.int32, (C, GL), 0)
    col_w = lax.broadcasted_iota(jnp.int32, (C, GL), 1) & (Dh - 1)
    causal_w, strict_w = row_w >= col_w, row_w > col_w
    eye_w = jnp.where(row_w == col_w, 1.0, 0.0).astype(F32)
    bdm = bdm_ref[...]
    mm = functools.partial(lax.dot_general, dimension_numbers=_NN, preferred_element_type=F32)

    def block_diag(m, mask):
        return jnp.concatenate([m] * 4, axis=0) * mask

    npair = min(GDN_PAIRS_PER_ITER, ncb // 2)

    def solve(it, carry):
        chunks = [2 * npair * it + k for k in range(2 * npair)]
        rows = [pl.ds(pl.multiple_of(c * C, C), C) for c in chunks]

        def groups(get, n):
            out = []
            for j in range(npair):
                a0, a1 = get(2 * j), get(2 * j + 1)
                out += [a0[:, 0:4 * n], a1[:, 0:4 * n],
                        jnp.concatenate([a0[:, 4 * n:6 * n], a1[:, 4 * n:6 * n]], axis=1)]
            return out

        def ungroup(ref, vals, n):
            for j in range(npair):
                ref[rows[2 * j], 0:4 * n] = vals[3 * j]
                ref[rows[2 * j + 1], 0:4 * n] = vals[3 * j + 1]
                ref[rows[2 * j], 4 * n:6 * n] = vals[3 * j + 2][:, 0:2 * n]
                ref[rows[2 * j + 1], 4 * n:6 * n] = vals[3 * j + 2][:, 2 * n:4 * n]

        kn_g = groups(lambda k: kn_s[rows[k], :], Dh)
        kb_g = groups(lambda k: kb_s[rows[k], :], Dh)
        qn_g = groups(lambda k: qn_s[rows[k], :], Dh)
        gx_g = groups(lambda k: gcx_s[rows[k], :], Dh)
        gw_g = groups(lambda k: gw_s[chunks[k], 0:1, :], Dh)
        decs = [jnp.exp(jnp.where(causal_w, gx - gw, NEG_BIG)) for gx, gw in zip(gx_g, gw_g)]
        prods = [lax.dot_general(jnp.concatenate([kb, qn], axis=0).astype(BF16), block_diag(kn.astype(BF16), bdm),
                                 _NT, preferred_element_type=F32)
                 for kn, kb, qn in zip(kn_g, kb_g, qn_g)]
        attns = [pr[C:2 * C] * dec for pr, dec in zip(prods, decs)]
        def times(lhs, p=None):
            lh, ll = _split(lhs)
            ph, pl_ = (lh[0:C], ll[0:C]) if p is None else _split(p)
            rh, rl = block_diag(ph, bdm), block_diag(pl_, bdm)
            return mm(lh, rh) + (mm(lh, rl) + mm(ll, rh))

        ps = [-jnp.where(strict_w, pr[0:C] * dec, 0.0) for pr, dec in zip(prods, decs)]
        invs = [eye_w + p for p in ps]
        ps = [times(p) for p in ps]
        for _ in range(4):
            outs = [times(jnp.concatenate([p, inv], axis=0)) for p, inv in zip(ps, invs)]
            invs = [inv + o[C:2 * C] for inv, o in zip(invs, outs)]
            ps = [o[0:C] for o in outs]
        invs = [inv + times(inv, p) for p, inv in zip(ps, invs)]
        def solved(ref):
            outs = []
            for inv, r in zip(invs, groups(lambda k: ref[rows[k], :], Dh)):
                ih, il = _split(inv)
                rh, rl = _split(r)
                rh, rl = block_diag(rh, bdm), block_diag(rl, bdm)
                outs.append(mm(ih, rh) + (mm(ih, rl) + mm(il, rh)))
            return outs

        u_g, w_g = solved(u_s), solved(w_s)
        ungroup(u_s, u_g, Dh)
        ungroup(w_s, w_g, Dh)
        at_g = [a.astype(BF16) for a in attns]
        au_g = [mm(a, block_diag(u.astype(BF16), bdm)) for a, u in zip(at_g, u_g)]
        aw_g = [mm(a, block_diag(w.astype(BF16), bdm)) for a, w in zip(at_g, w_g)]
        qd_g = groups(lambda k: qd_s[rows[k], :], Dh)
        ungroup(au_s, au_g, Dh)
        ungroup(qe_s, [qd - aw for qd, aw in zip(qd_g, aw_g)], Dh)
        for k in range(2 * npair):
            for sl in heads:
                kd_h = kd_s[rows[k], sl]
                ku_s[rows[k], sl] = _dot(kd_h, u_s[rows[k], sl], _TN)
                kw_s[rows[k], sl] = _dot(kd_h, w_s[rows[k], sl], _TN)
        return carry

    lax.fori_loop(0, ncb // (2 * npair), solve, 0)

    def recur(c, carry):
        rows = pl.ds(pl.multiple_of(c * C, C), C)
        st = st_s[...]
        sb = st.astype(BF16)
        lhs = jnp.concatenate([qe_s[rows, :], kw_s[rows, :]], axis=0).astype(BF16)
        out = jnp.concatenate(
            [mm(lhs[:, 0:GL], block_diag(sb[:, 0:GL], bdm)),
             mm(lhs[:, GL:W], jnp.concatenate([sb[:, GL:W]] * 2, axis=0) * bdm[0:W - GL, 0:W - GL])], axis=1)
        o_s[rows, :] = out[0:C] + au_s[rows, :]
        st_s[...] = st * eg_s[c, 0:1, :] + (ku_s[rows, :] - out[C:2 * C])
        return carry

    lax.fori_loop(0, ncb, recur, 0, unroll=2)

    o = o_s[...]
    ms = _dot2(o * o, e) * (1.0 / Dh)
    o = o * lax.rsqrt(ms + EPS) * nw_ref[...]
    o_ref[0] = (o * _silu(z_ref[0])).astype(o_ref.dtype)


def _gdn_call(qkv, z, sm, conv_w, a_log, dt_bias, norm_w):
    B, S, _ = qkv.shape
    TB = min(512, S)
    W, H, Dh = GDN_WIDTH, GDN_HEADS, GDN_HEAD_DIM
    hp = jnp.zeros((SUBLANE, SMALL_W), F32)
    hp = hp.at[0, :H].set(-jnp.exp(a_log.astype(F32))).at[1, :H].set(dt_bias.astype(F32))
    nw = jnp.tile(norm_w.astype(F32), H).reshape(1, W)
    const = lambda shape: pl.BlockSpec(shape, lambda b, i: tuple(0 for _ in shape))
    blk = lambda n: pl.BlockSpec((1, TB, n), lambda b, i: (b, i, 0))
    f = lambda *shape: pltpu.VMEM(shape, F32)
    return pl.pallas_call(
        functools.partial(_gdn_kernel, TB=TB),
        out_shape=jax.ShapeDtypeStruct((B, S, W), BF16),
        grid=(B, S // TB),
        in_specs=[blk(GDN_CONV_DIM), blk(W), blk(SMALL_W),
                  const((CONV_WIDTH, GDN_CONV_DIM)), const((SUBLANE, SMALL_W)), const((1, W)),
                  const((W, W)), const((SMALL_W, W)), const((4 * Dh, 4 * Dh))],
        out_specs=blk(W),
        scratch_shapes=[f(SUBLANE + TB, GDN_CONV_DIM), f(Dh, W)]
        + [f(TB, W)] * 7 + [f(TB // CHUNK, SUBLANE, W), f(TB, W), f(TB // CHUNK, SUBLANE, W)] + [f(TB, W)] * 5,
        compiler_params=_cparams(("arbitrary", "arbitrary")),
        name="gdn_mixer",
    )(qkv, z, sm, conv_w.astype(F32), hp, nw, _block_ones(W, Dh), _head_expand(), _block_ones(4 * Dh, Dh))


def _ssd_kernel(xbc_ref, z_ref, sm_ref, cw_ref, cb_ref, hp_ref, dsk_ref, nw_ref, e_ref, xp_ref,
                hgm_ref, e_hd_ref, ghm_ref, o_ref,
                tail_s, st_s, xs_s, xdt_s, xdd_s, bm_s, cm_s, ea_s, el_s, ac_s, aw_s, y_s, inc_s, *, TB):
    H, P, G, N, C = SSD_HEADS, SSD_HEAD_DIM, SSD_GROUPS, SSD_STATE, CHUNK
    W, GW = SSD_WIDTH, SSD_GROUP_WIDTH
    ncb = TB // C

    @pl.when(pl.program_id(1) == 0)
    def _():
        tail_s[0:SUBLANE, :] = jnp.zeros((SUBLANE, tail_s.shape[1]), F32)
        st_s[...] = jnp.zeros_like(st_s)

    xc = _causal_conv_silu(xbc_ref[0], tail_s, cw_ref, cb_ref[...])
    xs = xc[:, 0:W]
    xp = xp_ref[...]

    sm = sm_ref[0]
    lane = lax.broadcasted_iota(jnp.int32, sm.shape, 1)
    head_lane = lane < H
    dt = jnp.where(head_lane, _softplus(_lanes_from(sm, DT_LANE) + hp_ref[1:2, :]), 0.0)
    acs, rcs = _chunk_scans(dt * hp_ref[0:1, :])
    dtx = _dot2(dt, xp)
    eax = _dot2(jnp.exp(acs), xp)
    erx = _dot2(jnp.exp(rcs), xp)
    xdt = xs * dtx
    xs_s[...] = xs
    xdt_s[...] = xdt
    xdd_s[...] = xdt * erx
    bm_s[...] = xc[:, W:W + G * N]
    cm_s[...] = xc[:, W + G * N:W + 2 * G * N]
    ea_s[...] = eax
    acx = _dot_sel3(acs, xp)
    ac_s[...] = acx
    r_w = lax.broadcasted_iota(jnp.int32, (C, W), 0)
    l_w = lax.broadcasted_iota(jnp.int32, (C, W), 1) & (P - 1)
    for c in range(ncb):
        diag_c = jnp.where(r_w == l_w, acx[c * C:(c + 1) * C, :], 0.0)
        aw_s[c] = jnp.broadcast_to(jnp.sum(diag_c, axis=0, keepdims=True), (SUBLANE, W))
        el_s[c] = jnp.broadcast_to(eax[(c + 1) * C - 1:(c + 1) * C, :], (SUBLANE, W))

    causal_w = r_w >= l_w
    mm = functools.partial(lax.dot_general, dimension_numbers=_NN, preferred_element_type=F32)

    def stacked(m, mask):
        return jnp.concatenate([m] * H, axis=0) * mask

    nloc = min(SSD_CHUNKS_PER_ITER, ncb)

    def local(it, carry):
        cs = [it * nloc + k for k in range(nloc)]
        rows = [pl.ds(pl.multiple_of(c * C, C), C) for c in cs]
        bs = [bm_s[r, :].astype(BF16) for r in rows]
        cbs = [lax.dot_general(cm_s[r, :].astype(BF16), stacked(b, hgm_ref[...]), _NT, preferred_element_type=F32)
               for r, b in zip(rows, bs)]
        segs = [jnp.exp(jnp.where(causal_w, ac_s[r, :] - aw_s[c, 0:1, :], NEG_BIG)) for r, c in zip(rows, cs)]
        for r, cb, seg in zip(rows, cbs, segs):
            y_s[r, :] = mm((cb * seg).astype(BF16), stacked(xdt_s[r, :].astype(BF16), e_hd_ref[...]))
        for r, c, b in zip(rows, cs, bs):
            inc_s[c] = lax.dot_general(b, xdd_s[r, :].astype(BF16), _TN, preferred_element_type=F32) * ghm_ref[...]
        return carry

    lax.fori_loop(0, ncb // nloc, local, 0)

    def recur(c, carry):
        rows = pl.ds(pl.multiple_of(c * C, C), C)
        st = st_s[...]
        y_s[rows, :] = y_s[rows, :] + mm(cm_s[rows, :].astype(BF16), st.astype(BF16)) * ea_s[rows, :]
        st_s[...] = st * el_s[c, 0:1, :] + inc_s[c]
        return carry

    lax.fori_loop(0, ncb, recur, 0, unroll=2)

    y = y_s[...] + dsk_ref[...] * xs_s[...]
    y = y * _silu(z_ref[0])
    ms = _dot2(y * y, e_ref[...]) * (1.0 / GW)
    o_ref[0] = (y * lax.rsqrt(ms + EPS) * nw_ref[...]).astype(o_ref.dtype)


def _ssd_call(xbc, z, sm, conv_w, conv_b, a_log, dt_bias, d_skip, norm_w):
    B, S, _ = xbc.shape
    TB = min(512, S)
    W, H, G, N = SSD_WIDTH, SSD_HEADS, SSD_GROUPS, SSD_STATE
    hp = jnp.zeros((SUBLANE, SMALL_W), F32)
    hp = hp.at[0, :H].set(-jnp.exp(a_log.astype(F32))).at[1, :H].set(dt_bias.astype(F32))
    dsk = jnp.repeat(d_skip.astype(F32), SSD_HEAD_DIM).reshape(1, W)
    const = lambda shape: pl.BlockSpec(shape, lambda b, i: tuple(0 for _ in shape))
    blk = lambda n: pl.BlockSpec((1, TB, n), lambda b, i: (b, i, 0))
    f = lambda *shape: pltpu.VMEM(shape, F32)
    head_of = np.arange(H * CHUNK) // CHUNK
    group_of = np.arange(G * N) // N
    hgm = (head_of[:, None] // (H // G) == group_of[None, :]).astype(np.float32)
    ghm = (group_of[:, None] == (np.arange(W) // SSD_HEAD_DIM // (H // G))[None, :]).astype(np.float32)
    return pl.pallas_call(
        functools.partial(_ssd_kernel, TB=TB),
        out_shape=jax.ShapeDtypeStruct((B, S, W), BF16),
        grid=(B, S // TB),
        in_specs=[blk(SSD_CONV_DIM), blk(W), blk(SMALL_W),
                  const((CONV_WIDTH, SSD_CONV_DIM)), const((1, SSD_CONV_DIM)), const((SUBLANE, SMALL_W)),
                  const((1, W)), const((1, W)), const((W, W)), const((SMALL_W, W)),
                  const((H * CHUNK, G * N)), const((W, W)), const((G * N, W))],
        out_specs=blk(W),
        scratch_shapes=[f(SUBLANE + TB, SSD_CONV_DIM), f(G * N, W),
                        f(TB, W), f(TB, W), f(TB, W), f(TB, G * N), f(TB, G * N), f(TB, W),
                        f(TB // CHUNK, SUBLANE, W),
                        f(TB, W), f(TB // CHUNK, SUBLANE, W), f(TB, W),
                        f(TB // CHUNK, G * N, W)],
        compiler_params=_cparams(("arbitrary", "arbitrary")),
        name="ssd_mixer",
    )(xbc, z, sm, conv_w.astype(F32), conv_b.reshape(1, -1).astype(F32), hp, dsk,
      norm_w.reshape(1, W).astype(F32), _block_ones(W, SSD_GROUP_WIDTH), _head_expand(),
      jnp.asarray(hgm, BF16), _block_ones(W, SSD_HEAD_DIM), jnp.asarray(ghm, F32))


def _out_kernel(x_ref, y1_ref, y2_ref, y3_ref, mod_ref, nw_ref, wo_ref, wr_ref, br_ref, tri_ref,
                x1_ref, h_ref, rt_ref, rtt_ref, cnt_ref, run_s, wo_s):
    first = (pl.program_id(0) == 0) & (pl.program_id(1) == 0)

    @pl.when(first)
    def _():
        wo_s[...] = wo_ref[0].astype(BF16)

    y = (jnp.dot(y1_ref[0], wo_s[0:S5_WIDTH, :], preferred_element_type=F32)
         + jnp.dot(y2_ref[0], wo_s[S5_WIDTH:S5_WIDTH + GDN_WIDTH, :], preferred_element_type=F32)
         + jnp.dot(y3_ref[0], wo_s[S5_WIDTH + GDN_WIDTH:, :], preferred_element_type=F32))
    x1 = x_ref[0] + mod_ref[0, 2:3, :] * y
    x1_ref[0] = x1
    h = _norm_mod(x1, nw_ref[...], mod_ref[0, 4:5, :], mod_ref[0, 3:4, :])
    _store_row_tiles(h_ref, _pack_rows(h))
    lg = lax.dot_general(wr_ref[...], h.astype(BF16), _NT, preferred_element_type=F32) + br_ref[...]
    tm = lg.shape[1]
    row = lax.broadcasted_iota(jnp.int32, lg.shape, 0)
    rowf = row.astype(F32)
    big = float(4 * LANE)
    grp = (row >= GRP_LANE) & (row < GRP_LANE + MOE_GROUPS)
    lgm = jnp.where(grp, lg, -jnp.inf)
    m = jnp.max(lgm, axis=0, keepdims=True)
    gidx = jnp.min(jnp.where(lgm == m, rowf - GRP_LANE, big), axis=0, keepdims=True)
    g_w = 1.0 / jnp.sum(jnp.where(grp, jnp.exp(lg - m), 0.0), axis=0, keepdims=True)
    in_grp = (row < N_EXPERTS) & ((row // EXPERTS_PER_GROUP).astype(F32) == gidx)
    le = jnp.where(in_grp, lg, -jnp.inf)
    v1 = jnp.max(le, axis=0, keepdims=True)
    i1 = jnp.min(jnp.where(le == v1, rowf, big), axis=0, keepdims=True)
    le2 = jnp.where(rowf == i1, -jnp.inf, le)
    v2 = jnp.max(le2, axis=0, keepdims=True)
    i2 = jnp.min(jnp.where(le2 == v2, rowf, big), axis=0, keepdims=True)
    e2 = jnp.exp(v2 - v1)
    w1 = g_w / (1.0 + e2)
    w2 = g_w * e2 / (1.0 + e2)

    @pl.when(first)
    def _():
        run_s[...] = jnp.zeros_like(run_s)

    chosen = jnp.where((rowf == i1) | (rowf == i2), 1.0, 0.0)
    before = jnp.dot(chosen.astype(BF16), tri_ref[...], preferred_element_type=F32) + run_s[:, 0:1]
    p1 = jnp.sum(jnp.where(rowf == i1, before, 0.0), axis=0, keepdims=True)
    p2 = jnp.sum(jnp.where(rowf == i2, before, 0.0), axis=0, keepdims=True)
    run_s[...] = run_s[...] + jnp.sum(chosen, axis=1, keepdims=True)
    cnt_ref[...] = run_s[...]

    vals = jnp.concatenate([i1, i2, w1, w2, p1, p2, jnp.zeros((LANE - 6, tm), F32)], axis=0)
    rtt_ref[0] = vals[0:SUBLANE, :]
    rt_ref[0] = vals.T


def _out_call(x, y1, y2, y3, mod_l, norm_w, w_out, w_router, b_router, l):
    B, S, D = x.shape
    tm = min(512, S)
    blk = lambda n: pl.BlockSpec((1, tm, n), lambda b, i: (b, i, 0))
    const = lambda shape: pl.BlockSpec(shape, lambda b, i: tuple(0 for _ in shape))
    tri = jnp.asarray(np.triu(np.ones((tm, tm), np.float32), 1), BF16)
    nb = S // tm
    tiles = PACK_ROWS
    return pl.pallas_call(
        _out_kernel,
        out_shape=[jax.ShapeDtypeStruct((B, S, D), F32), jax.ShapeDtypeStruct((B * S * tiles, LANE), U32),
                   jax.ShapeDtypeStruct((B, S, LANE), F32), jax.ShapeDtypeStruct((B * nb, SUBLANE, tm), F32),
                   jax.ShapeDtypeStruct((ROUTER_ROWS, LANE), F32)],
        grid=(B, nb),
        in_specs=[blk(D), blk(S5_WIDTH), blk(GDN_WIDTH), blk(SSD_WIDTH),
                  pl.BlockSpec((1, 6, D), lambda b, i: (b, 0, 0)),
                  const((1, D)), pl.BlockSpec((1, D, D), lambda b, i: (l, 0, 0)),
                  const((ROUTER_ROWS, D)), const((ROUTER_ROWS, 1)), const((tm, tm))],
        out_specs=[blk(D), pl.BlockSpec((tm * tiles, LANE), lambda b, i: (b * nb + i, 0)),
                   blk(LANE), pl.BlockSpec((1, SUBLANE, tm), lambda b, i: (b * nb + i, 0, 0)),
                   const((ROUTER_ROWS, LANE))],
        scratch_shapes=[pltpu.VMEM((ROUTER_ROWS, LANE), F32), pltpu.VMEM((D, D), BF16)],
        compiler_params=_cparams(("arbitrary", "arbitrary")),
        name="out_proj_router",
    )(x, y1, y2, y3, mod_l, norm_w.reshape(1, D), w_out, w_router, b_router, tri)


def _router_params(w_grp, b_grp, w_rt, b_rt):
    D = w_grp.shape[0]
    w = jnp.zeros((ROUTER_ROWS, D), F32).at[0:N_EXPERTS].set(w_rt.T).at[GRP_LANE:GRP_LANE + MOE_GROUPS].set(w_grp.T)
    b = jnp.zeros((ROUTER_ROWS, 1), F32).at[0:N_EXPERTS, 0].set(b_rt).at[GRP_LANE:GRP_LANE + MOE_GROUPS, 0].set(b_grp)
    return w.astype(BF16), b


def _tile_copy_loop(n, fn):
    def body(t, carry):
        fn(t)
        return carry
    lax.fori_loop(0, n, body, 0, unroll=32)


def _scatter_kernel(d0_ref, d1_ref, h_ref, xs_hbm, stage, sem):
    i = pl.program_id(0)
    n = pl.num_programs(0)
    slot = lax.rem(i, 2)
    tm = d0_ref.shape[-1]

    def copy(s, t, dst):
        return pltpu.make_async_copy(stage.at[s, pl.ds(t * PACK_ROWS, PACK_ROWS)],
                                     xs_hbm.at[pl.ds(dst * PACK_ROWS, PACK_ROWS)], sem.at[s])

    def wait_slot(s):
        _tile_copy_loop(TOP_K * tm, lambda t: copy(s, 0, 0).wait())

    @pl.when(i >= 2)
    def _():
        wait_slot(slot)

    stage[slot] = h_ref[...]

    def start(t):
        copy(slot, t, d0_ref[0, 0, t]).start(priority=0)
        copy(slot, t, d1_ref[0, 0, t]).start(priority=1)
    _tile_copy_loop(tm, start)

    @pl.when(i == n - 1)
    def _():
        @pl.when(i >= 1)
        def _():
            wait_slot(1 - slot)
        wait_slot(slot)


def _scatter_call(h2t, dest0, dest1, n_rows):
    n_tiles, _, tm = dest0.shape
    idx_blk = pl.BlockSpec((1, 1, tm), lambda i: (i, 0, 0), memory_space=pltpu.SMEM)
    return pl.pallas_call(
        _scatter_kernel,
        out_shape=jax.ShapeDtypeStruct((n_rows * PACK_ROWS, LANE), U32),
        grid=(n_tiles,),
        in_specs=[idx_blk, idx_blk, pl.BlockSpec((tm * PACK_ROWS, LANE), lambda i: (i, 0))],
        out_specs=pl.BlockSpec(memory_space=pl.ANY),
        scratch_shapes=[pltpu.VMEM((2, tm * PACK_ROWS, LANE), U32), pltpu.SemaphoreType.DMA((2,))],
        compiler_params=_cparams(("arbitrary",)),
        name="moe_scatter",
    )(dest0, dest1, h2t)


def _expert_kernel(nused_ref, blke_ref, nvalid_ref, xs_ref, wg_ref, wu_ref, wd_ref, ys_ref, wg_s, wu_s, wd_s):
    i = pl.program_id(0)

    @pl.when(i < nused_ref[0])
    def _():
        @pl.when((i == 0) | (blke_ref[i] != blke_ref[jnp.maximum(i - 1, 0)]))
        def _():
            wg_s[...] = wg_ref[0, 0].astype(BF16)
            wu_s[...] = wu_ref[0, 0].astype(BF16)
            wd_s[...] = wd_ref[0, 0].astype(BF16)

        row = lax.broadcasted_iota(jnp.int32, (MOE_ROWS, 1), 0)
        words = jnp.where(row < nvalid_ref[i], _load_row_tiles(xs_ref, MOE_ROWS), U32(0))
        xb = _unpack_rows(words).astype(BF16)
        hid = _silu(jnp.dot(xb, wg_s[...], preferred_element_type=F32)) * jnp.dot(xb, wu_s[...], preferred_element_type=F32)
        _store_row_tiles(ys_ref, _pack_rows(jnp.dot(hid.astype(BF16), wd_s[...], preferred_element_type=F32)))

    @pl.when(i >= nused_ref[0])
    def _():
        ys_ref[...] = jnp.zeros_like(ys_ref)


def _expert_call(xs, n_used, blk_e, n_valid, w_gate, w_up, w_down, l):
    D = w_gate.shape[2]
    n_blk = blk_e.shape[0]
    rows_blk = pl.BlockSpec((MOE_ROWS * PACK_ROWS, LANE), lambda i, nu, be, nv: (i, 0))
    grid_spec = pltpu.PrefetchScalarGridSpec(
        num_scalar_prefetch=3,
        grid=(n_blk,),
        in_specs=[rows_blk,
                  pl.BlockSpec((1, 1, D, D_EXPERT), lambda i, nu, be, nv: (l, be[i], 0, 0)),
                  pl.BlockSpec((1, 1, D, D_EXPERT), lambda i, nu, be, nv: (l, be[i], 0, 0)),
                  pl.BlockSpec((1, 1, D_EXPERT, D), lambda i, nu, be, nv: (l, be[i], 0, 0))],
        out_specs=rows_blk,
        scratch_shapes=[pltpu.VMEM((D, D_EXPERT), BF16), pltpu.VMEM((D, D_EXPERT), BF16),
                        pltpu.VMEM((D_EXPERT, D), BF16)],
    )
    return pl.pallas_call(
        _expert_kernel,
        out_shape=jax.ShapeDtypeStruct(xs.shape, U32),
        grid_spec=grid_spec,
        compiler_params=_cparams(("arbitrary",)),
        name="expert_mlp",
    )(n_used, blk_e, n_valid, xs, w_gate, w_up, w_down)


def _dest_kernel(rt_ref, ps_ref, o_ref):
    n_sub, _, tm = o_ref.shape
    expert = lax.broadcasted_iota(jnp.int32, (N_EXPERTS, tm), 0).astype(F32)
    starts = ps_ref[0:N_EXPERTS, :]
    for j in range(n_sub):
        rows = []
        for k in range(TOP_K):
            start = jnp.sum(jnp.where(expert == rt_ref[j, k:k + 1, :], starts, 0.0), axis=0, keepdims=True)
            rows.append(start + rt_ref[j, 4 + k:5 + k, :])
        rows.append(jnp.zeros((SUBLANE - TOP_K, tm), F32))
        o_ref[j] = jnp.concatenate(rows, axis=0).astype(jnp.int32)


def _dest_call(route_t, pstart_col):
    n_tiles, _, tm = route_t.shape
    n_sub = min(DEST_TILES_PER_STEP, n_tiles)
    blk = pl.BlockSpec((n_sub, SUBLANE, tm), lambda i: (i, 0, 0))
    return pl.pallas_call(
        _dest_kernel,
        out_shape=jax.ShapeDtypeStruct((n_tiles, SUBLANE, tm), jnp.int32),
        grid=(n_tiles // n_sub,),
        in_specs=[blk, pl.BlockSpec((LANE, 1), lambda i: (0, 0))],
        out_specs=blk,
        compiler_params=_cparams(("arbitrary",)),
        name="moe_dest",
    )(route_t, pstart_col)


def _dispatch(route_t, counts, N, tm):
    L_pad = N * TOP_K + N_EXPERTS * MOE_ROWS
    n_blk = L_pad // MOE_ROWS
    counts = counts.astype(jnp.int32)
    padded = ((counts + MOE_ROWS - 1) // MOE_ROWS) * MOE_ROWS
    pend = jnp.cumsum(padded)
    pstart = pend - padded
    n_used = (pend[-1] // MOE_ROWS).astype(jnp.int32).reshape(1)
    blk_row0 = jnp.arange(n_blk, dtype=jnp.int32) * MOE_ROWS
    blk_e = jnp.minimum(jnp.sum((pend[None, :] <= blk_row0[:, None]).astype(jnp.int32), axis=1), N_EXPERTS - 1)
    n_valid = jnp.clip((pstart + counts)[blk_e] - blk_row0, 0, MOE_ROWS).astype(jnp.int32)
    pstart_col = jnp.zeros((LANE, 1), F32).at[0:N_EXPERTS, 0].set(pstart.astype(F32))
    dest = _dest_call(route_t, pstart_col)
    dest0 = dest[:, 0, :].reshape(N // tm, 1, tm)
    dest1 = dest[:, 1, :].reshape(N // tm, 1, tm)
    return n_used, blk_e.astype(jnp.int32), n_valid, dest0, dest1, L_pad


def _combine_kernel(d0_ref, d1_ref, d0n_ref, d1n_ref, x_ref, rt_ref, mod_ref, nf_ref, ys_hbm, o_ref,
                    buf, sem, *, final):
    i = pl.program_id(0)
    n = pl.num_programs(0)
    slot = lax.rem(i, 2)
    tm = x_ref.shape[0]

    def copy(s, k, t, src):
        return pltpu.make_async_copy(ys_hbm.at[pl.ds(src * PACK_ROWS, PACK_ROWS)],
                                     buf.at[s, k, pl.ds(t * PACK_ROWS, PACK_ROWS)], sem.at[s])

    def start_tile(s, a_ref, b_ref):
        def start(t):
            copy(s, 0, t, a_ref[0, 0, t]).start(priority=0)
            copy(s, 1, t, b_ref[0, 0, t]).start(priority=1)
        _tile_copy_loop(tm, start)

    @pl.when(i == 0)
    def _():
        start_tile(0, d0_ref, d1_ref)

    @pl.when(i + 1 < n)
    def _():
        start_tile(1 - slot, d0n_ref, d1n_ref)

    _tile_copy_loop(TOP_K * tm, lambda t: copy(slot, 0, 0, 0).wait())
    rt = rt_ref[...]
    y = (rt[:, 2:3] * _unpack_rows(_load_row_tiles(buf.at[slot, 0], tm))
         + rt[:, 3:4] * _unpack_rows(_load_row_tiles(buf.at[slot, 1], tm)))
    x2 = x_ref[...] + mod_ref[0, 5:6, :] * y
    if final:
        ms = jnp.mean(x2 * x2, axis=-1, keepdims=True)
        x2 = x2 * lax.rsqrt(ms + EPS) * nf_ref[...]
    o_ref[...] = x2


def _combine_call(x1, ys, route, dest0, dest1, mod_l, norm_final, final):
    B, S, D = x1.shape
    N = B * S
    n_tiles, _, tm = dest0.shape
    per_b = S // tm
    idx_blk = lambda fn: pl.BlockSpec((1, 1, tm), fn, memory_space=pltpu.SMEM)
    cur = lambda i: (i, 0, 0)
    nxt = lambda i: (jnp.minimum(i + 1, n_tiles - 1), 0, 0)
    out = pl.pallas_call(
        functools.partial(_combine_kernel, final=final),
        out_shape=jax.ShapeDtypeStruct((N, D), F32),
        grid=(n_tiles,),
        in_specs=[idx_blk(cur), idx_blk(cur), idx_blk(nxt), idx_blk(nxt),
                  pl.BlockSpec((tm, D), lambda i: (i, 0)),
                  pl.BlockSpec((tm, LANE), lambda i: (i, 0)),
                  pl.BlockSpec((1, 6, D), lambda i: (i // per_b, 0, 0)),
                  pl.BlockSpec((1, D), lambda i: (0, 0)),
                  pl.BlockSpec(memory_space=pl.ANY)],
        out_specs=pl.BlockSpec((tm, D), lambda i: (i, 0)),
        scratch_shapes=[pltpu.VMEM((2, TOP_K, tm * PACK_ROWS, LANE), U32), pltpu.SemaphoreType.DMA((2,))],
        compiler_params=_cparams(("arbitrary",)),
        name="moe_combine",
    )(dest0, dest1, dest0, dest1, x1.reshape(N, D), route.reshape(N, LANE), mod_l, norm_final.reshape(1, D), ys)
    return out.reshape(B, S, D)


def _layer(x, mod_l, p, big, l, final, norm_final):
    B, S, D = x.shape
    N = B * S
    s5_u, g_qkv, g_z, s_z, s_xbc, small = _proj_call(x, mod_l, p["norm_mix"], big["w_in"], l)
    bm, cm, lam_rows = _s5_params(p["s5_a_re"], p["s5_a_im"], p["s5_b_re"], p["s5_b_im"],
                                  p["s5_c_re"], p["s5_c_im"], p["s5_log_dt"])
    y_s5 = _s5_call(s5_u, bm, cm, lam_rows, p["s5_d"], p["s5_w_glu"], p["s5_norm"])
    y_gdn = _gdn_call(g_qkv, g_z, small, p["gdn_conv_w"], p["gdn_a_log"], p["gdn_dt_bias"], p["gdn_norm"])
    y_ssd = _ssd_call(s_xbc, s_z, small, p["ssd_conv_w"], p["ssd_conv_b"], p["ssd_a_log"], p["ssd_dt_bias"],
                      p["ssd_d"], p["ssd_norm"])
    w_router, b_router = _router_params(p["moe_w_grp"], p["moe_b_grp"], p["moe_w_rt"], p["moe_b_rt"])
    x1, h2, route, route_t, counts = _out_call(x, y_s5, y_gdn, y_ssd, mod_l, p["norm_ffn"], big["w_out"],
                                               w_router, b_router, l)
    n_used, blk_e, n_valid, dest0, dest1, n_rows = _dispatch(route_t, counts[0:N_EXPERTS, 0], N,
                                                             min(MOE_TOKEN_TILE, S))
    xs = _scatter_call(h2, dest0, dest1, n_rows)
    ys = _expert_call(xs, n_used, blk_e, n_valid, big["moe_w_gate"], big["moe_w_up"], big["moe_w_down"], l)
    return _combine_call(x1, ys, route, dest0, dest1, mod_l, norm_final, final)


def kernel(x, c, w_ada, b_ada, norm_mix, norm_ffn, w_in, w_out, s5_a_re, s5_a_im, s5_b_re, s5_b_im, s5_c_re, s5_c_im, s5_d, s5_log_dt, s5_w_glu, s5_norm, gdn_conv_w, gdn_a_log, gdn_dt_bias, gdn_norm, ssd_conv_w, ssd_conv_b, ssd_a_log, ssd_dt_bias, ssd_d, ssd_norm, moe_w_grp, moe_b_grp, moe_w_rt, moe_b_rt, moe_w_gate, moe_w_up, moe_w_down, norm_final):
    stacked = dict(norm_mix=norm_mix, norm_ffn=norm_ffn, s5_a_re=s5_a_re, s5_a_im=s5_a_im,
                   s5_b_re=s5_b_re, s5_b_im=s5_b_im, s5_c_re=s5_c_re, s5_c_im=s5_c_im, s5_d=s5_d,
                   s5_log_dt=s5_log_dt, s5_w_glu=s5_w_glu, s5_norm=s5_norm, gdn_conv_w=gdn_conv_w,
                   gdn_a_log=gdn_a_log, gdn_dt_bias=gdn_dt_bias, gdn_norm=gdn_norm, ssd_conv_w=ssd_conv_w,
                   ssd_conv_b=ssd_conv_b, ssd_a_log=ssd_a_log, ssd_dt_bias=ssd_dt_bias, ssd_d=ssd_d,
                   ssd_norm=ssd_norm, moe_w_grp=moe_w_grp, moe_b_grp=moe_b_grp, moe_w_rt=moe_w_rt,
                   moe_b_rt=moe_b_rt)
    big = dict(w_in=_w_in_prep_call(w_in), w_out=w_out, moe_w_gate=moe_w_gate, moe_w_up=moe_w_up,
               moe_w_down=moe_w_down)
    L = w_in.shape[0]
    B, S, D = x.shape
    mod = _mod_call(c, w_ada, b_ada).reshape(L, B, 6, D)
    for l in range(L):
        p = {k: v[l] for k, v in stacked.items()}
        x = _layer(x, mod[l], p, big, l, l == L - 1, norm_final)
    return x
```

```python
import functools

import numpy as np
import jax
import jax.numpy as jnp
from jax import lax
from jax.experimental import pallas as pl
from jax.experimental.pallas import tpu as pltpu

F32 = jnp.float32
BF16 = jnp.bfloat16

D_MODEL = 1024
DEPTH = 4
EPS = 1e-6
CONV_WIDTH = 4
CHUNK = 64
S5_WIDTH = 256
S5_CH = 16
S5_GROUPS = 16
S5_STATE = 64
S5_LANES = S5_GROUPS * S5_STATE
GDN_WIDTH = 384
GDN_HEAD_DIM = 64
GDN_HEADS = 6
GDN_CONV_DIM = 3 * GDN_WIDTH
SSD_WIDTH = 384
SSD_HEAD_DIM = 64
SSD_HEADS = 6
SSD_GROUPS = 2
SSD_STATE = 128
SSD_GROUP_WIDTH = SSD_WIDTH // SSD_GROUPS
SSD_CONV_DIM = SSD_WIDTH + 2 * SSD_GROUPS * SSD_STATE
PROJ_SIZES = (S5_WIDTH, GDN_CONV_DIM, GDN_WIDTH, GDN_HEADS, GDN_HEADS, SSD_WIDTH, SSD_CONV_DIM, SSD_HEADS)
MOE_GROUPS = 4
EXPERTS_PER_GROUP = 8
N_EXPERTS = 32
TOP_K = 2
D_EXPERT = 256

LANE = 128
SUBLANE = 8
SMALL_W = LANE
A_LANE, B_LANE, DT_LANE = 0, GDN_HEADS, 2 * GDN_HEADS
GRP_LANE = N_EXPERTS
ROUTER_ROWS = 40
MOE_ROWS = 512
MOE_TOKEN_TILE = 1024
DEST_TILES_PER_STEP = 8
SSD_CHUNKS_PER_ITER = 8
GDN_PAIRS_PER_ITER = 4
VMEM_LIMIT = 56 * 1024 * 1024
NEG_BIG = -1e30


def _cparams(sem):
    return pltpu.CompilerParams(dimension_semantics=sem, vmem_limit_bytes=VMEM_LIMIT)


def _split(a):
    hi = a.astype(BF16)
    lo = (a - hi.astype(F32)).astype(BF16)
    return hi, lo


_NN = (((1,), (0,)), ((), ()))
_NT = (((1,), (1,)), ((), ()))
_TN = (((0,), (0,)), ((), ()))


def _dot(a, b, dims=_NN):
    return lax.dot_general(a.astype(BF16), b.astype(BF16), dims, preferred_element_type=F32)


def _dot2(a, b_bf16):
    hi, lo = _split(a)
    return (lax.dot_general(hi, b_bf16, _NN, preferred_element_type=F32)
            + lax.dot_general(lo, b_bf16, _NN, preferred_element_type=F32))


def _dot_sel3(a, b_bf16):
    h1 = a.astype(BF16)
    r1 = a - h1.astype(F32)
    h2 = r1.astype(BF16)
    h3 = (r1 - h2.astype(F32)).astype(BF16)
    d = functools.partial(lax.dot_general, dimension_numbers=_NN, preferred_element_type=F32)
    return d(h1, b_bf16) + (d(h2, b_bf16) + d(h3, b_bf16))


def _dot3(a, b):
    ah, al = _split(a)
    bh, bl = _split(b)
    d = functools.partial(lax.dot_general, dimension_numbers=_NN, preferred_element_type=F32)
    return d(ah, bh) + (d(ah, bl) + d(al, bh))


def _silu(x):
    return x * jax.nn.sigmoid(x)


def _softplus(x):
    return jnp.maximum(x, 0.0) + jnp.log(1.0 + jnp.exp(-jnp.abs(x)))


def _norm_mod(x, w, scale, shift):
    ms = jnp.mean(x * x, axis=-1, keepdims=True)
    return (x * lax.rsqrt(ms + EPS) * w) * (1.0 + scale) + shift


def _mod_kernel(c_ref, w_ref, b_ref, o_ref):
    cond = _silu(c_ref[...])
    o_ref[0] = _dot3(cond, w_ref[0]) + b_ref[0]


def _mod_call(c, w_ada, b_ada):
    L, D, W = w_ada.shape
    B = c.shape[0]
    tn = 1536
    return pl.pallas_call(
        _mod_kernel,
        out_shape=jax.ShapeDtypeStruct((L, B, W), F32),
        grid=(L, W // tn),
        in_specs=[pl.BlockSpec((B, D), lambda l, j: (0, 0)),
                  pl.BlockSpec((1, D, tn), lambda l, j: (l, 0, j)),
                  pl.BlockSpec((1, 1, tn), lambda l, j: (l, 0, j))],
        out_specs=pl.BlockSpec((1, B, tn), lambda l, j: (l, 0, j)),
        compiler_params=_cparams(("arbitrary", "arbitrary")),
        name="adaln_mod",
    )(c, w_ada, b_ada.reshape(L, 1, W))


PROJ_OUT_W = (S5_WIDTH, GDN_CONV_DIM, GDN_WIDTH, SSD_WIDTH, SSD_CONV_DIM, SMALL_W)


def _proj_kernel(x_ref, mod_ref, nw_ref, w_ref, *o_refs):
    x = x_ref[0]
    h = _norm_mod(x, nw_ref[...], mod_ref[0, 1:2, :], mod_ref[0, 0:1, :]).astype(BF16)
    proj = jnp.dot(h, w_ref[0], preferred_element_type=F32)
    off = 0
    for o_ref in o_refs:
        n = o_ref.shape[-1]
        o_ref[0] = proj[:, off:off + n]
        off += n


_W_HEAD = S5_WIDTH + GDN_CONV_DIM + GDN_WIDTH
_W_AB = _W_HEAD + 2 * GDN_HEADS
_W_SSD = _W_AB + SSD_WIDTH + SSD_CONV_DIM
_W_END = _W_SSD + SSD_HEADS
PROJ_W = sum(PROJ_OUT_W)
W_PREP_ROWS = 256


def _w_in_prep_kernel(w_ref, o_ref):
    n_ssd = _W_SSD - _W_AB
    for r in range(0, w_ref.shape[1], W_PREP_ROWS):
        w = w_ref[0, r:r + W_PREP_ROWS, :]
        rows = slice(r, r + W_PREP_ROWS)
        o_ref[0, rows, 0:_W_HEAD] = w[:, 0:_W_HEAD].astype(BF16)
        o_ref[0, rows, _W_HEAD:_W_HEAD + n_ssd] = w[:, _W_AB:_W_SSD].astype(BF16)
        small = jnp.concatenate([w[:, _W_HEAD:_W_AB], w[:, _W_SSD:_W_END],
                                 jnp.zeros((W_PREP_ROWS, SMALL_W - (_W_AB - _W_HEAD) - (_W_END - _W_SSD)), F32)], axis=1)
        o_ref[0, rows, _W_HEAD + n_ssd:PROJ_W] = small.astype(BF16)


def _w_in_prep_call(w_in):
    L, D, W = w_in.shape
    return pl.pallas_call(
        _w_in_prep_kernel,
        out_shape=jax.ShapeDtypeStruct((L, D, PROJ_W), BF16),
        grid=(L,),
        in_specs=[pl.BlockSpec((1, D, W), lambda l: (l, 0, 0))],
        out_specs=pl.BlockSpec((1, D, PROJ_W), lambda l: (l, 0, 0)),
        compiler_params=_cparams(("arbitrary",)),
        name="w_in_prep",
    )(w_in)


def _proj_call(x, mod_l, norm_w, w_arr, l):
    B, S, D = x.shape
    tm = min(512, S)
    out_shape = [jax.ShapeDtypeStruct((B, S, n), F32) for n in PROJ_OUT_W]
    out_specs = [pl.BlockSpec((1, tm, n), lambda b, i: (b, i, 0)) for n in PROJ_OUT_W]
    return pl.pallas_call(
        _proj_kernel,
        out_shape=out_shape,
        grid=(B, S // tm),
        in_specs=[pl.BlockSpec((1, tm, D), lambda b, i: (b, i, 0)),
                  pl.BlockSpec((1, 6, D), lambda b, i: (b, 0, 0)),
                  pl.BlockSpec((1, D), lambda b, i: (0, 0)),
                  pl.BlockSpec((1, D, PROJ_W), lambda b, i: (l, 0, 0))],
        out_specs=out_specs,
        compiler_params=_cparams(("arbitrary", "arbitrary")),
        name="norm_in_proj",
    )(x, mod_l, norm_w.reshape(1, D), w_arr)


def _s5_kernel(u_ref, bm_ref, cm_ref, lam_ref, dsk_ref, glu_ref, nw_ref, perm_ref, o_ref,
               x_s, st_s, ubt_s, u_s, *, T):
    B = u_ref.shape[0]
    P = S5_LANES

    @pl.when(pl.program_id(0) == 0)
    def _():
        st_s[...] = jnp.zeros_like(st_s)

    nw = S5_WIDTH // LANE
    ubt = u_ref[...].reshape(B * T, S5_WIDTH)
    for j in range(nw):
        ubt_s[j] = ubt[:, j * LANE:(j + 1) * LANE]

    def regroup(t, carry):
        for j in range(nw):
            u_s[pl.ds(pl.multiple_of(t * B, B), B), j * LANE:(j + 1) * LANE] = ubt_s[j, pl.ds(t, B, stride=T), :]
        return carry

    lax.fori_loop(0, T, regroup, 0, unroll=8)
    u = u_s[...]
    x_s[...] = jnp.dot(u.astype(BF16), bm_ref[...], preferred_element_type=F32)
    lr = jnp.broadcast_to(lam_ref[0:1, :], (B, P))
    li = jnp.broadcast_to(lam_ref[1:2, :], (B, P))

    def step(t, carry):
        sr, si = carry
        rows = pl.ds(pl.multiple_of(t * B, B), B)
        nr = lr * sr - li * si + x_s[rows, 0:P]
        ni = lr * si + li * sr + x_s[rows, P:2 * P]
        x_s[rows, 0:P] = nr
        x_s[rows, P:2 * P] = ni
        return nr, ni

    sr, si = lax.fori_loop(0, T, step, (st_s[:, 0:P], st_s[:, P:2 * P]), unroll=4)
    st_s[:, 0:P] = sr
    st_s[:, P:2 * P] = si

    y = jnp.dot(x_s[...].astype(BF16), cm_ref[...], preferred_element_type=F32) + dsk_ref[...] * u
    y = jax.nn.gelu(y)
    y = y * jax.nn.sigmoid(jnp.dot(y.astype(BF16), glu_ref[...], preferred_element_type=F32))
    ms = jnp.mean(y * y, axis=-1, keepdims=True)
    y = (y * lax.rsqrt(ms + EPS) * nw_ref[...]).astype(BF16)
    y = jnp.dot(perm_ref[...], y, preferred_element_type=F32)
    o_ref[...] = y.reshape(B, T, S5_WIDTH).astype(o_ref.dtype)


def _s5_params(a_re, a_im, b_re, b_im, c_re, c_im, log_dt):
    G, P, CH = S5_GROUPS, S5_STATE, S5_CH
    lam = lax.complex(a_re.astype(F32), a_im.astype(F32))
    step = jnp.exp(log_dt.astype(F32))[:, None]
    lam_bar = jnp.exp(lam * step)
    b_bar = ((lam_bar - 1.0) / lam)[..., None] * lax.complex(b_re.astype(F32), b_im.astype(F32))
    eye = jnp.eye(G, dtype=F32)
    bre = jnp.einsum('gpc,gh->gchp', b_bar.real, eye).reshape(G * CH, G * P)
    bim = jnp.einsum('gpc,gh->gchp', b_bar.imag, eye).reshape(G * CH, G * P)
    bm = jnp.concatenate([bre, bim], axis=1).astype(BF16)
    cre = jnp.einsum('gcp,gh->gphc', c_re.astype(F32), eye).reshape(G * P, G * CH)
    cim = jnp.einsum('gcp,gh->gphc', c_im.astype(F32), eye).reshape(G * P, G * CH)
    cm = jnp.concatenate([cre, -cim], axis=0).astype(BF16)
    lam_rows = jnp.zeros((SUBLANE, G * P), F32)
    lam_rows = lam_rows.at[0].set(lam_bar.real.reshape(-1)).at[1].set(lam_bar.imag.reshape(-1))
    return bm, cm, lam_rows


def _s5_call(u, bm, cm, lam_rows, d_skip, w_glu, norm_w):
    B, S, W = u.shape
    T = min(128, S)
    P2 = 2 * S5_LANES
    const = lambda shape: pl.BlockSpec(shape, lambda i: tuple(0 for _ in shape))
    r = np.arange(B * T)
    perm = np.zeros((B * T, B * T), np.float32)
    perm[r, (r % T) * B + r // T] = 1.0
    return pl.pallas_call(
        functools.partial(_s5_kernel, T=T),
        out_shape=jax.ShapeDtypeStruct((B, S, W), BF16),
        grid=(S // T,),
        in_specs=[pl.BlockSpec((B, T, W), lambda i: (0, i, 0)),
                  const((W, P2)), const((P2, W)), const((SUBLANE, S5_LANES)),
                  const((1, W)), const((W, W)), const((1, W)), const((B * T, B * T))],
        out_specs=pl.BlockSpec((B, T, W), lambda i: (0, i, 0)),
        scratch_shapes=[pltpu.VMEM((B * T, P2), F32), pltpu.VMEM((B, P2), F32),
                        pltpu.VMEM((W // LANE, B * T, LANE), F32), pltpu.VMEM((B * T, W), F32)],
        compiler_params=_cparams(("arbitrary",)),
        name="s5_mixer",
    )(u, bm, cm, lam_rows, d_skip.reshape(1, W).astype(F32), w_glu.astype(BF16), norm_w.reshape(1, W).astype(F32),
      jnp.asarray(perm, BF16))


def _causal_conv_silu(x, xf_ref, cw_ref, bias):
    n = x.shape[0]
    xf_ref[SUBLANE:, :] = x
    acc = x * cw_ref[CONV_WIDTH - 1:CONV_WIDTH, :]
    for k in range(1, CONV_WIDTH):
        acc = acc + xf_ref[SUBLANE - k:SUBLANE - k + n, :] * cw_ref[CONV_WIDTH - 1 - k:CONV_WIDTH - k, :]
    xf_ref[0:SUBLANE, :] = x[n - SUBLANE:, :]
    if bias is not None:
        acc = acc + bias
    return _silu(acc)


U32 = jnp.uint32
PACK_ROWS = D_MODEL // (2 * LANE)


def _pack_rows(x):
    h = x.shape[1] // 2
    xb = x.astype(BF16).astype(F32)
    lo = lax.bitcast_convert_type(xb[:, 0:h], U32) >> 16
    hi = lax.bitcast_convert_type(xb[:, h:2 * h], U32) & U32(0xFFFF0000)
    return lo | hi


def _unpack_rows(w):
    lo = lax.bitcast_convert_type(w << 16, F32)
    hi = lax.bitcast_convert_type(w & U32(0xFFFF0000), F32)
    return jnp.concatenate([lo, hi], axis=1)


def _store_row_tiles(ref, val):
    n = val.shape[0]
    for j in range(PACK_ROWS):
        ref[pl.ds(j, n, stride=PACK_ROWS), :] = val[:, j * LANE:(j + 1) * LANE]


def _load_row_tiles(ref, n):
    return jnp.concatenate([ref[pl.ds(j, n, stride=PACK_ROWS), :] for j in range(PACK_ROWS)], axis=1)


def _lanes_from(sm, off):
    return pltpu.roll(sm, SMALL_W - off, axis=1) if off else sm


def _chunk_scans(g):
    n = g.shape[0]
    rin = lax.broadcasted_iota(jnp.int32, g.shape, 0) & (CHUNK - 1)
    pre = g
    suf = jnp.where(rin < CHUNK - 1, pltpu.roll(g, n - 1, axis=0), 0.0)
    s = 1
    while s < CHUNK:
        pre = pre + jnp.where(rin >= s, pltpu.roll(pre, s, axis=0), 0.0)
        suf = suf + jnp.where(rin + s <= CHUNK - 1, pltpu.roll(suf, n - s, axis=0), 0.0)
        s *= 2
    return pre, suf


def _head_expand():
    m = np.zeros((SMALL_W, GDN_WIDTH), np.float32)
    for h in range(GDN_HEADS):
        m[h, h * GDN_HEAD_DIM:(h + 1) * GDN_HEAD_DIM] = 1.0
    return jnp.asarray(m, BF16)


def _block_ones(width, blk):
    idx = np.arange(width) // blk
    return jnp.asarray((idx[:, None] == idx[None, :]).astype(np.float32), BF16)


def _gdn_kernel(qkv_ref, z_ref, sm_ref, cw_ref, hp_ref, nw_ref, e_ref, xp_ref, bdm_ref, o_ref,
                tail_s, st_s, kn_s, kb_s, qn_s, qd_s, kd_s, u_s, w_s, eg_s, gcx_s, gw_s,
                o_s, au_s, qe_s, ku_s, kw_s, *, TB):
    H, Dh, C = GDN_HEADS, GDN_HEAD_DIM, CHUNK
    W = GDN_WIDTH
    ncb = TB // C

    @pl.when(pl.program_id(1) == 0)
    def _():
        tail_s[0:SUBLANE, :] = jnp.zeros((SUBLANE, tail_s.shape[1]), F32)
        st_s[...] = jnp.zeros_like(st_s)

    xc = _causal_conv_silu(qkv_ref[0], tail_s, cw_ref, None)
    q, k, v = xc[:, 0:W], xc[:, W:2 * W], xc[:, 2 * W:3 * W]
    e = e_ref[...]
    xp = xp_ref[...]
    qn = q * lax.rsqrt(_dot2(q * q, e) + EPS) * (Dh ** -0.5)
    kn = k * lax.rsqrt(_dot2(k * k, e) + EPS)

    sm = sm_ref[0]
    lane = lax.broadcasted_iota(jnp.int32, sm.shape, 1)
    head_lane = lane < H
    g = jnp.where(head_lane, hp_ref[0:1, :] * _softplus(_lanes_from(sm, A_LANE) + hp_ref[1:2, :]), 0.0)
    beta = jnp.where(head_lane, jax.nn.sigmoid(_lanes_from(sm, B_LANE)), 0.0)
    gc, rc = _chunk_scans(g)
    bx = _dot2(beta, xp)
    egx = _dot2(jnp.exp(gc), xp)
    erx = _dot2(jnp.exp(rc), xp)
    kb = kn * bx
    kn_s[...] = kn
    kb_s[...] = kb
    qn_s[...] = qn
    qd_s[...] = qn * egx
    kd_s[...] = kn * erx
    u_s[...] = v * bx
    w_s[...] = kb * egx
    gcx = _dot_sel3(gc, xp)
    gcx_s[...] = gcx
    r384 = lax.broadcasted_iota(jnp.int32, (C, W), 0)
    l384 = lax.broadcasted_iota(jnp.int32, (C, W), 1) & (Dh - 1)
    for c in range(ncb):
        diag_c = jnp.where(r384 == l384, gcx[c * C:(c + 1) * C, :], 0.0)
        gw_s[c] = jnp.broadcast_to(jnp.sum(diag_c, axis=0, keepdims=True), (SUBLANE, W))
        eg_s[c] = jnp.broadcast_to(egx[(c + 1) * C - 1:(c + 1) * C, :], (SUBLANE, W))

    heads = [slice(h * Dh, (h + 1) * Dh) for h in range(H)]

    GL = 4 * Dh
    row_w = lax.broadcasted_iota(jnp.int32, (C, GL), 0)
    col_w = lax.broadcasted_iota(jnp.int32, (C, GL), 1) & (Dh - 1)
    causal_w, strict_w = row_w >= col_w, row_w > col_w
    eye_w = jnp.where(row_w == col_w, 1.0, 0.0).astype(F32)
    bdm = bdm_ref[...]
    mm = functools.partial(lax.dot_general, dimension_numbers=_NN, preferred_element_type=F32)

    def block_diag(m, mask):
        return jnp.concatenate([m] * 4, axis=0) * mask

    npair = min(GDN_PAIRS_PER_ITER, ncb // 2)

    def solve(it, carry):
        chunks = [2 * npair * it + k for k in range(2 * npair)]
        rows = [pl.ds(pl.multiple_of(c * C, C), C) for c in chunks]

        def groups(get, n):
            out = []
            for j in range(npair):
                a0, a1 = get(2 * j), get(2 * j + 1)
                out += [a0[:, 0:4 * n], a1[:, 0:4 * n],
                        jnp.concatenate([a0[:, 4 * n:6 * n], a1[:, 4 * n:6 * n]], axis=1)]
            return out

        def ungroup(ref, vals, n):
            for j in range(npair):
                ref[rows[2 * j], 0:4 * n] = vals[3 * j]
                ref[rows[2 * j + 1], 0:4 * n] = vals[3 * j + 1]
                ref[rows[2 * j], 4 * n:6 * n] = vals[3 * j + 2][:, 0:2 * n]
                ref[rows[2 * j + 1], 4 * n:6 * n] = vals[3 * j + 2][:, 2 * n:4 * n]

        kn_g = groups(lambda k: kn_s[rows[k], :], Dh)
        kb_g = groups(lambda k: kb_s[rows[k], :], Dh)
        qn_g = groups(lambda k: qn_s[rows[k], :], Dh)
        gx_g = groups(lambda k: gcx_s[rows[k], :], Dh)
        gw_g = groups(lambda k: gw_s[chunks[k], 0:1, :], Dh)
        decs = [jnp.exp(jnp.where(causal_w, gx - gw, NEG_BIG)) for gx, gw in zip(gx_g, gw_g)]
        prods = [lax.dot_general(jnp.concatenate([kb, qn], axis=0).astype(BF16), block_diag(kn.astype(BF16), bdm),
                                 _NT, preferred_element_type=F32)
                 for kn, kb, qn in zip(kn_g, kb_g, qn_g)]
        attns = [pr[C:2 * C] * dec for pr, dec in zip(prods, decs)]
        def times(lhs, p=None):
            lh, ll = _split(lhs)
            ph, pl_ = (lh[0:C], ll[0:C]) if p is None else _split(p)
            rh, rl = block_diag(ph, bdm), block_diag(pl_, bdm)
            return mm(lh, rh) + (mm(lh, rl) + mm(ll, rh))

        ps = [-jnp.where(strict_w, pr[0:C] * dec, 0.0) for pr, dec in zip(prods, decs)]
        invs = [eye_w + p for p in ps]
        ps = [times(p) for p in ps]
        for _ in range(4):
            outs = [times(jnp.concatenate([p, inv], axis=0)) for p, inv in zip(ps, invs)]
            invs = [inv + o[C:2 * C] for inv, o in zip(invs, outs)]
            ps = [o[0:C] for o in outs]
        invs = [inv + times(inv, p) for p, inv in zip(ps, invs)]
        def solved(ref):
            outs = []
            for inv, r in zip(invs, groups(lambda k: ref[rows[k], :], Dh)):
                ih, il = _split(inv)
                rh, rl = _split(r)
                rh, rl = block_diag(rh, bdm), block_diag(rl, bdm)
                outs.append(mm(ih, rh) + (mm(ih, rl) + mm(il, rh)))
            return outs

        u_g, w_g = solved(u_s), solved(w_s)
        ungroup(u_s, u_g, Dh)
        ungroup(w_s, w_g, Dh)
        at_g = [a.astype(BF16) for a in attns]
        au_g = [mm(a, block_diag(u.astype(BF16), bdm)) for a, u in zip(at_g, u_g)]
        aw_g = [mm(a, block_diag(w.astype(BF16), bdm)) for a, w in zip(at_g, w_g)]
        qd_g = groups(lambda k: qd_s[rows[k], :], Dh)
        ungroup(au_s, au_g, Dh)
        ungroup(qe_s, [qd - aw for qd, aw in zip(qd_g, aw_g)], Dh)
        for k in range(2 * npair):
            for sl in heads:
                kd_h = kd_s[rows[k], sl]
                ku_s[rows[k], sl] = _dot(kd_h, u_s[rows[k], sl], _TN)
                kw_s[rows[k], sl] = _dot(kd_h, w_s[rows[k], sl], _TN)
        return carry

    lax.fori_loop(0, ncb // (2 * npair), solve, 0)

    def recur(c, carry):
        rows = pl.ds(pl.multiple_of(c * C, C), C)
        st = st_s[...]
        sb = st.astype(BF16)
        lhs = jnp.concatenate([qe_s[rows, :], kw_s[rows, :]], axis=0).astype(BF16)
        out = jnp.concatenate(
            [mm(lhs[:, 0:GL], block_diag(sb[:, 0:GL], bdm)),
             mm(lhs[:, GL:W], jnp.concatenate([sb[:, GL:W]] * 2, axis=0) * bdm[0:W - GL, 0:W - GL])], axis=1)
        o_s[rows, :] = out[0:C] + au_s[rows, :]
        st_s[...] = st * eg_s[c, 0:1, :] + (ku_s[rows, :] - out[C:2 * C])
        return carry

    lax.fori_loop(0, ncb, recur, 0)

    o = o_s[...]
    ms = _dot2(o * o, e) * (1.0 / Dh)
    o = o * lax.rsqrt(ms + EPS) * nw_ref[...]
    o_ref[0] = (o * _silu(z_ref[0])).astype(o_ref.dtype)


def _gdn_call(qkv, z, sm, conv_w, a_log, dt_bias, norm_w):
    B, S, _ = qkv.shape
    TB = min(512, S)
    W, H, Dh = GDN_WIDTH, GDN_HEADS, GDN_HEAD_DIM
    hp = jnp.zeros((SUBLANE, SMALL_W), F32)
    hp = hp.at[0, :H].set(-jnp.exp(a_log.astype(F32))).at[1, :H].set(dt_bias.astype(F32))
    nw = jnp.tile(norm_w.astype(F32), H).reshape(1, W)
    const = lambda shape: pl.BlockSpec(shape, lambda b, i: tuple(0 for _ in shape))
    blk = lambda n: pl.BlockSpec((1, TB, n), lambda b, i: (b, i, 0))
    f = lambda *shape: pltpu.VMEM(shape, F32)
    return pl.pallas_call(
        functools.partial(_gdn_kernel, TB=TB),
        out_shape=jax.ShapeDtypeStruct((B, S, W), BF16),
        grid=(B, S // TB),
        in_specs=[blk(GDN_CONV_DIM), blk(W), blk(SMALL_W),
                  const((CONV_WIDTH, GDN_CONV_DIM)), const((SUBLANE, SMALL_W)), const((1, W)),
                  const((W, W)), const((SMALL_W, W)), const((4 * Dh, 4 * Dh))],
        out_specs=blk(W),
        scratch_shapes=[f(SUBLANE + TB, GDN_CONV_DIM), f(Dh, W)]
        + [f(TB, W)] * 7 + [f(TB // CHUNK, SUBLANE, W), f(TB, W), f(TB // CHUNK, SUBLANE, W)] + [f(TB, W)] * 5,
        compiler_params=_cparams(("arbitrary", "arbitrary")),
        name="gdn_mixer",
    )(qkv, z, sm, conv_w.astype(F32), hp, nw, _block_ones(W, Dh), _head_expand(), _block_ones(4 * Dh, Dh))


def _ssd_kernel(xbc_ref, z_ref, sm_ref, cw_ref, cb_ref, hp_ref, dsk_ref, nw_ref, e_ref, xp_ref,
                hgm_ref, e_hd_ref, ghm_ref, o_ref,
                tail_s, st_s, xs_s, xdt_s, xdd_s, bm_s, cm_s, ea_s, el_s, ac_s, aw_s, y_s, inc_s, *, TB):
    H, P, G, N, C = SSD_HEADS, SSD_HEAD_DIM, SSD_GROUPS, SSD_STATE, CHUNK
    W, GW = SSD_WIDTH, SSD_GROUP_WIDTH
    ncb = TB // C

    @pl.when(pl.program_id(1) == 0)
    def _():
        tail_s[0:SUBLANE, :] = jnp.zeros((SUBLANE, tail_s.shape[1]), F32)
        st_s[...] = jnp.zeros_like(st_s)

    xc = _causal_conv_silu(xbc_ref[0], tail_s, cw_ref, cb_ref[...])
    xs = xc[:, 0:W]
    xp = xp_ref[...]

    sm = sm_ref[0]
    lane = lax.broadcasted_iota(jnp.int32, sm.shape, 1)
    head_lane = lane < H
    dt = jnp.where(head_lane, _softplus(_lanes_from(sm, DT_LANE) + hp_ref[1:2, :]), 0.0)
    acs, rcs = _chunk_scans(dt * hp_ref[0:1, :])
    dtx = _dot2(dt, xp)
    eax = _dot2(jnp.exp(acs), xp)
    erx = _dot2(jnp.exp(rcs), xp)
    xdt = xs * dtx
    xs_s[...] = xs
    xdt_s[...] = xdt
    xdd_s[...] = xdt * erx
    bm_s[...] = xc[:, W:W + G * N]
    cm_s[...] = xc[:, W + G * N:W + 2 * G * N]
    ea_s[...] = eax
    acx = _dot_sel3(acs, xp)
    ac_s[...] = acx
    r_w = lax.broadcasted_iota(jnp.int32, (C, W), 0)
    l_w = lax.broadcasted_iota(jnp.int32, (C, W), 1) & (P - 1)
    for c in range(ncb):
        diag_c = jnp.where(r_w == l_w, acx[c * C:(c + 1) * C, :], 0.0)
        aw_s[c] = jnp.broadcast_to(jnp.sum(diag_c, axis=0, keepdims=True), (SUBLANE, W))
        el_s[c] = jnp.broadcast_to(eax[(c + 1) * C - 1:(c + 1) * C, :], (SUBLANE, W))

    causal_w = r_w >= l_w
    mm = functools.partial(lax.dot_general, dimension_numbers=_NN, preferred_element_type=F32)

    def stacked(m, mask):
        return jnp.concatenate([m] * H, axis=0) * mask

    nloc = min(SSD_CHUNKS_PER_ITER, ncb)

    def local(it, carry):
        cs = [it * nloc + k for k in range(nloc)]
        rows = [pl.ds(pl.multiple_of(c * C, C), C) for c in cs]
        bs = [bm_s[r, :].astype(BF16) for r in rows]
        cbs = [lax.dot_general(cm_s[r, :].astype(BF16), stacked(b, hgm_ref[...]), _NT, preferred_element_type=F32)
               for r, b in zip(rows, bs)]
        segs = [jnp.exp(jnp.where(causal_w, ac_s[r, :] - aw_s[c, 0:1, :], NEG_BIG)) for r, c in zip(rows, cs)]
        for r, cb, seg in zip(rows, cbs, segs):
            y_s[r, :] = mm((cb * seg).astype(BF16), stacked(xdt_s[r, :].astype(BF16), e_hd_ref[...]))
        for r, c, b in zip(rows, cs, bs):
            inc_s[c] = lax.dot_general(b, xdd_s[r, :].astype(BF16), _TN, preferred_element_type=F32) * ghm_ref[...]
        return carry

    lax.fori_loop(0, ncb // nloc, local, 0)

    def recur(c, carry):
        rows = pl.ds(pl.multiple_of(c * C, C), C)
        st = st_s[...]
        y_s[rows, :] = y_s[rows, :] + mm(cm_s[rows, :].astype(BF16), st.astype(BF16)) * ea_s[rows, :]
        st_s[...] = st * el_s[c, 0:1, :] + inc_s[c]
        return carry

    lax.fori_loop(0, ncb, recur, 0, unroll=2)

    y = y_s[...] + dsk_ref[...] * xs_s[...]
    y = y * _silu(z_ref[0])
    ms = _dot2(y * y, e_ref[...]) * (1.0 / GW)
    o_ref[0] = (y * lax.rsqrt(ms + EPS) * nw_ref[...]).astype(o_ref.dtype)


def _ssd_call(xbc, z, sm, conv_w, conv_b, a_log, dt_bias, d_skip, norm_w):
    B, S, _ = xbc.shape
    TB = min(512, S)
    W, H, G, N = SSD_WIDTH, SSD_HEADS, SSD_GROUPS, SSD_STATE
    hp = jnp.zeros((SUBLANE, SMALL_W), F32)
    hp = hp.at[0, :H].set(-jnp.exp(a_log.astype(F32))).at[1, :H].set(dt_bias.astype(F32))
    dsk = jnp.repeat(d_skip.astype(F32), SSD_HEAD_DIM).reshape(1, W)
    const = lambda shape: pl.BlockSpec(shape, lambda b, i: tuple(0 for _ in shape))
    blk = lambda n: pl.BlockSpec((1, TB, n), lambda b, i: (b, i, 0))
    f = lambda *shape: pltpu.VMEM(shape, F32)
    head_of = np.arange(H * CHUNK) // CHUNK
    group_of = np.arange(G * N) // N
    hgm = (head_of[:, None] // (H // G) == group_of[None, :]).astype(np.float32)
    ghm = (group_of[:, None] == (np.arange(W) // SSD_HEAD_DIM // (H // G))[None, :]).astype(np.float32)
    return pl.pallas_call(
        functools.partial(_ssd_kernel, TB=TB),
        out_shape=jax.ShapeDtypeStruct((B, S, W), BF16),
        grid=(B, S // TB),
        in_specs=[blk(SSD_CONV_DIM), blk(W), blk(SMALL_W),
                  const((CONV_WIDTH, SSD_CONV_DIM)), const((1, SSD_CONV_DIM)), const((SUBLANE, SMALL_W)),
                  const((1, W)), const((1, W)), const((W, W)), const((SMALL_W, W)),
                  const((H * CHUNK, G * N)), const((W, W)), const((G * N, W))],
        out_specs=blk(W),
        scratch_shapes=[f(SUBLANE + TB, SSD_CONV_DIM), f(G * N, W),
                        f(TB, W), f(TB, W), f(TB, W), f(TB, G * N), f(TB, G * N), f(TB, W),
                        f(TB // CHUNK, SUBLANE, W),
                        f(TB, W), f(TB // CHUNK, SUBLANE, W), f(TB, W),
                        f(TB // CHUNK, G * N, W)],
        compiler_params=_cparams(("arbitrary", "arbitrary")),
        name="ssd_mixer",
    )(xbc, z, sm, conv_w.astype(F32), conv_b.reshape(1, -1).astype(F32), hp, dsk,
      norm_w.reshape(1, W).astype(F32), _block_ones(W, SSD_GROUP_WIDTH), _head_expand(),
      jnp.asarray(hgm, BF16), _block_ones(W, SSD_HEAD_DIM), jnp.asarray(ghm, F32))


def _out_kernel(x_ref, y1_ref, y2_ref, y3_ref, mod_ref, nw_ref, wo_ref, wr_ref, br_ref, tri_ref,
                x1_ref, h_ref, rt_ref, rtt_ref, cnt_ref, run_s, wo_s):
    first = (pl.program_id(0) == 0) & (pl.program_id(1) == 0)

    @pl.when(first)
    def _():
        wo_s[...] = wo_ref[0].astype(BF16)

    y = (jnp.dot(y1_ref[0], wo_s[0:S5_WIDTH, :], preferred_element_type=F32)
         + jnp.dot(y2_ref[0], wo_s[S5_WIDTH:S5_WIDTH + GDN_WIDTH, :], preferred_element_type=F32)
         + jnp.dot(y3_ref[0], wo_s[S5_WIDTH + GDN_WIDTH:, :], preferred_element_type=F32))
    x1 = x_ref[0] + mod_ref[0, 2:3, :] * y
    x1_ref[0] = x1
    h = _norm_mod(x1, nw_ref[...], mod_ref[0, 4:5, :], mod_ref[0, 3:4, :])
    _store_row_tiles(h_ref, _pack_rows(h))
    lg = lax.dot_general(wr_ref[...], h.astype(BF16), _NT, preferred_element_type=F32) + br_ref[...]
    tm = lg.shape[1]
    row = lax.broadcasted_iota(jnp.int32, lg.shape, 0)
    rowf = row.astype(F32)
    big = float(4 * LANE)
    grp = (row >= GRP_LANE) & (row < GRP_LANE + MOE_GROUPS)
    lgm = jnp.where(grp, lg, -jnp.inf)
    m = jnp.max(lgm, axis=0, keepdims=True)
    gidx = jnp.min(jnp.where(lgm == m, rowf - GRP_LANE, big), axis=0, keepdims=True)
    g_w = 1.0 / jnp.sum(jnp.where(grp, jnp.exp(lg - m), 0.0), axis=0, keepdims=True)
    in_grp = (row < N_EXPERTS) & ((row // EXPERTS_PER_GROUP).astype(F32) == gidx)
    le = jnp.where(in_grp, lg, -jnp.inf)
    v1 = jnp.max(le, axis=0, keepdims=True)
    i1 = jnp.min(jnp.where(le == v1, rowf, big), axis=0, keepdims=True)
    le2 = jnp.where(rowf == i1, -jnp.inf, le)
    v2 = jnp.max(le2, axis=0, keepdims=True)
    i2 = jnp.min(jnp.where(le2 == v2, rowf, big), axis=0, keepdims=True)
    e2 = jnp.exp(v2 - v1)
    w1 = g_w / (1.0 + e2)
    w2 = g_w * e2 / (1.0 + e2)

    @pl.when(first)
    def _():
        run_s[...] = jnp.zeros_like(run_s)

    chosen = jnp.where((rowf == i1) | (rowf == i2), 1.0, 0.0)
    before = jnp.dot(chosen.astype(BF16), tri_ref[...], preferred_element_type=F32) + run_s[:, 0:1]
    p1 = jnp.sum(jnp.where(rowf == i1, before, 0.0), axis=0, keepdims=True)
    p2 = jnp.sum(jnp.where(rowf == i2, before, 0.0), axis=0, keepdims=True)
    run_s[...] = run_s[...] + jnp.sum(chosen, axis=1, keepdims=True)
    cnt_ref[...] = run_s[...]

    vals = jnp.concatenate([i1, i2, w1, w2, p1, p2, jnp.zeros((LANE - 6, tm), F32)], axis=0)
    rtt_ref[0] = vals[0:SUBLANE, :]
    rt_ref[0] = vals.T


def _out_call(x, y1, y2, y3, mod_l, norm_w, w_out, w_router, b_router, l):
    B, S, D = x.shape
    tm = min(512, S)
    blk = lambda n: pl.BlockSpec((1, tm, n), lambda b, i: (b, i, 0))
    const = lambda shape: pl.BlockSpec(shape, lambda b, i: tuple(0 for _ in shape))
    tri = jnp.asarray(np.triu(np.ones((tm, tm), np.float32), 1), BF16)
    nb = S // tm
    tiles = PACK_ROWS
    return pl.pallas_call(
        _out_kernel,
        out_shape=[jax.ShapeDtypeStruct((B, S, D), F32), jax.ShapeDtypeStruct((B * S * tiles, LANE), U32),
                   jax.ShapeDtypeStruct((B, S, LANE), F32), jax.ShapeDtypeStruct((B * nb, SUBLANE, tm), F32),
                   jax.ShapeDtypeStruct((ROUTER_ROWS, LANE), F32)],
        grid=(B, nb),
        in_specs=[blk(D), blk(S5_WIDTH), blk(GDN_WIDTH), blk(SSD_WIDTH),
                  pl.BlockSpec((1, 6, D), lambda b, i: (b, 0, 0)),
                  const((1, D)), pl.BlockSpec((1, D, D), lambda b, i: (l, 0, 0)),
                  const((ROUTER_ROWS, D)), const((ROUTER_ROWS, 1)), const((tm, tm))],
        out_specs=[blk(D), pl.BlockSpec((tm * tiles, LANE), lambda b, i: (b * nb + i, 0)),
                   blk(LANE), pl.BlockSpec((1, SUBLANE, tm), lambda b, i: (b * nb + i, 0, 0)),
                   const((ROUTER_ROWS, LANE))],
        scratch_shapes=[pltpu.VMEM((ROUTER_ROWS, LANE), F32), pltpu.VMEM((D, D), BF16)],
        compiler_params=_cparams(("arbitrary", "arbitrary")),
        name="out_proj_router",
    )(x, y1, y2, y3, mod_l, norm_w.reshape(1, D), w_out, w_router, b_router, tri)


def _router_params(w_grp, b_grp, w_rt, b_rt):
    D = w_grp.shape[0]
    w = jnp.zeros((ROUTER_ROWS, D), F32).at[0:N_EXPERTS].set(w_rt.T).at[GRP_LANE:GRP_LANE + MOE_GROUPS].set(w_grp.T)
    b = jnp.zeros((ROUTER_ROWS, 1), F32).at[0:N_EXPERTS, 0].set(b_rt).at[GRP_LANE:GRP_LANE + MOE_GROUPS, 0].set(b_grp)
    return w.astype(BF16), b


def _tile_copy_loop(n, fn):
    def body(t, carry):
        fn(t)
        return carry
    lax.fori_loop(0, n, body, 0, unroll=32)


def _scatter_kernel(d0_ref, d1_ref, h_ref, xs_hbm, stage, sem):
    i = pl.program_id(0)
    n = pl.num_programs(0)
    slot = lax.rem(i, 2)
    tm = d0_ref.shape[-1]

    def copy(s, t, dst):
        return pltpu.make_async_copy(stage.at[s, pl.ds(t * PACK_ROWS, PACK_ROWS)],
                                     xs_hbm.at[pl.ds(dst * PACK_ROWS, PACK_ROWS)], sem.at[s])

    def wait_slot(s):
        _tile_copy_loop(TOP_K * tm, lambda t: copy(s, 0, 0).wait())

    @pl.when(i >= 2)
    def _():
        wait_slot(slot)

    stage[slot] = h_ref[...]

    def start(t):
        copy(slot, t, d0_ref[0, 0, t]).start(priority=0)
        copy(slot, t, d1_ref[0, 0, t]).start(priority=1)
    _tile_copy_loop(tm, start)

    @pl.when(i == n - 1)
    def _():
        @pl.when(i >= 1)
        def _():
            wait_slot(1 - slot)
        wait_slot(slot)


def _scatter_call(h2t, dest0, dest1, n_rows):
    n_tiles, _, tm = dest0.shape
    idx_blk = pl.BlockSpec((1, 1, tm), lambda i: (i, 0, 0), memory_space=pltpu.SMEM)
    return pl.pallas_call(
        _scatter_kernel,
        out_shape=jax.ShapeDtypeStruct((n_rows * PACK_ROWS, LANE), U32),
        grid=(n_tiles,),
        in_specs=[idx_blk, idx_blk, pl.BlockSpec((tm * PACK_ROWS, LANE), lambda i: (i, 0))],
        out_specs=pl.BlockSpec(memory_space=pl.ANY),
        scratch_shapes=[pltpu.VMEM((2, tm * PACK_ROWS, LANE), U32), pltpu.SemaphoreType.DMA((2,))],
        compiler_params=_cparams(("arbitrary",)),
        name="moe_scatter",
    )(dest0, dest1, h2t)


def _expert_kernel(nused_ref, blke_ref, nvalid_ref, xs_ref, wg_ref, wu_ref, wd_ref, ys_ref, wg_s, wu_s, wd_s):
    i = pl.program_id(0)

    @pl.when(i < nused_ref[0])
    def _():
        @pl.when((i == 0) | (blke_ref[i] != blke_ref[jnp.maximum(i - 1, 0)]))
        def _():
            wg_s[...] = wg_ref[0, 0].astype(BF16)
            wu_s[...] = wu_ref[0, 0].astype(BF16)
            wd_s[...] = wd_ref[0, 0].astype(BF16)

        row = lax.broadcasted_iota(jnp.int32, (MOE_ROWS, 1), 0)
        words = jnp.where(row < nvalid_ref[i], _load_row_tiles(xs_ref, MOE_ROWS), U32(0))
        xb = _unpack_rows(words).astype(BF16)
        hid = _silu(jnp.dot(xb, wg_s[...], preferred_element_type=F32)) * jnp.dot(xb, wu_s[...], preferred_element_type=F32)
        _store_row_tiles(ys_ref, _pack_rows(jnp.dot(hid.astype(BF16), wd_s[...], preferred_element_type=F32)))

    @pl.when(i >= nused_ref[0])
    def _():
        ys_ref[...] = jnp.zeros_like(ys_ref)


def _expert_call(xs, n_used, blk_e, n_valid, w_gate, w_up, w_down, l):
    D = w_gate.shape[2]
    n_blk = blk_e.shape[0]
    rows_blk = pl.BlockSpec((MOE_ROWS * PACK_ROWS, LANE), lambda i, nu, be, nv: (i, 0))
    grid_spec = pltpu.PrefetchScalarGridSpec(
        num_scalar_prefetch=3,
        grid=(n_blk,),
        in_specs=[rows_blk,
                  pl.BlockSpec((1, 1, D, D_EXPERT), lambda i, nu, be, nv: (l, be[i], 0, 0)),
                  pl.BlockSpec((1, 1, D, D_EXPERT), lambda i, nu, be, nv: (l, be[i], 0, 0)),
                  pl.BlockSpec((1, 1, D_EXPERT, D), lambda i, nu, be, nv: (l, be[i], 0, 0))],
        out_specs=rows_blk,
        scratch_shapes=[pltpu.VMEM((D, D_EXPERT), BF16), pltpu.VMEM((D, D_EXPERT), BF16),
                        pltpu.VMEM((D_EXPERT, D), BF16)],
    )
    return pl.pallas_call(
        _expert_kernel,
        out_shape=jax.ShapeDtypeStruct(xs.shape, U32),
        grid_spec=grid_spec,
        compiler_params=_cparams(("arbitrary",)),
        name="expert_mlp",
    )(n_used, blk_e, n_valid, xs, w_gate, w_up, w_down)


def _dest_kernel(rt_ref, ps_ref, o_ref):
    n_sub, _, tm = o_ref.shape
    expert = lax.broadcasted_iota(jnp.int32, (N_EXPERTS, tm), 0).astype(F32)
    starts = ps_ref[0:N_EXPERTS, :]
    for j in range(n_sub):
        rows = []
        for k in range(TOP_K):
            start = jnp.sum(jnp.where(expert == rt_ref[j, k:k + 1, :], starts, 0.0), axis=0, keepdims=True)
            rows.append(start + rt_ref[j, 4 + k:5 + k, :])
        rows.append(jnp.zeros((SUBLANE - TOP_K, tm), F32))
        o_ref[j] = jnp.concatenate(rows, axis=0).astype(jnp.int32)


def _dest_call(route_t, pstart_col):
    n_tiles, _, tm = route_t.shape
    n_sub = min(DEST_TILES_PER_STEP, n_tiles)
    blk = pl.BlockSpec((n_sub, SUBLANE, tm), lambda i: (i, 0, 0))
    return pl.pallas_call(
        _dest_kernel,
        out_shape=jax.ShapeDtypeStruct((n_tiles, SUBLANE, tm), jnp.int32),
        grid=(n_tiles // n_sub,),
        in_specs=[blk, pl.BlockSpec((LANE, 1), lambda i: (0, 0))],
        out_specs=blk,
        compiler_params=_cparams(("arbitrary",)),
        name="moe_dest",
    )(route_t, pstart_col)


def _dispatch(route_t, counts, N, tm):
    L_pad = N * TOP_K + N_EXPERTS * MOE_ROWS
    n_blk = L_pad // MOE_ROWS
    counts = counts.astype(jnp.int32)
    padded = ((counts + MOE_ROWS - 1) // MOE_ROWS) * MOE_ROWS
    pend = jnp.cumsum(padded)
    pstart = pend - padded
    n_used = (pend[-1] // MOE_ROWS).astype(jnp.int32).reshape(1)
    blk_row0 = jnp.arange(n_blk, dtype=jnp.int32) * MOE_ROWS
    blk_e = jnp.minimum(jnp.sum((pend[None, :] <= blk_row0[:, None]).astype(jnp.int32), axis=1), N_EXPERTS - 1)
    n_valid = jnp.clip((pstart + counts)[blk_e] - blk_row0, 0, MOE_ROWS).astype(jnp.int32)
    pstart_col = jnp.zeros((LANE, 1), F32).at[0:N_EXPERTS, 0].set(pstart.astype(F32))
    dest = _dest_call(route_t, pstart_col)
    dest0 = dest[:, 0, :].reshape(N // tm, 1, tm)
    dest1 = dest[:, 1, :].reshape(N // tm, 1, tm)
    return n_used, blk_e.astype(jnp.int32), n_valid, dest0, dest1, L_pad


def _combine_kernel(d0_ref, d1_ref, d0n_ref, d1n_ref, x_ref, rt_ref, mod_ref, nf_ref, ys_hbm, o_ref,
                    buf, sem, *, final):
    i = pl.program_id(0)
    n = pl.num_programs(0)
    slot = lax.rem(i, 2)
    tm = x_ref.shape[0]

    def copy(s, k, t, src):
        return pltpu.make_async_copy(ys_hbm.at[pl.ds(src * PACK_ROWS, PACK_ROWS)],
                                     buf.at[s, k, pl.ds(t * PACK_ROWS, PACK_ROWS)], sem.at[s])

    def start_tile(s, a_ref, b_ref):
        def start(t):
            copy(s, 0, t, a_ref[0, 0, t]).start(priority=0)
            copy(s, 1, t, b_ref[0, 0, t]).start(priority=1)
        _tile_copy_loop(tm, start)

    @pl.when(i == 0)
    def _():
        start_tile(0, d0_ref, d1_ref)

    @pl.when(i + 1 < n)
    def _():
        start_tile(1 - slot, d0n_ref, d1n_ref)

    _tile_copy_loop(TOP_K * tm, lambda t: copy(slot, 0, 0, 0).wait())
    rt = rt_ref[...]
    y = (rt[:, 2:3] * _unpack_rows(_load_row_tiles(buf.at[slot, 0], tm))
         + rt[:, 3:4] * _unpack_rows(_load_row_tiles(buf.at[slot, 1], tm)))
    x2 = x_ref[...] + mod_ref[0, 5:6, :] * y
    if final:
        ms = jnp.mean(x2 * x2, axis=-1, keepdims=True)
        x2 = x2 * lax.rsqrt(ms + EPS) * nf_ref[...]
    o_ref[...] = x2


def _combine_call(x1, ys, route, dest0, dest1, mod_l, norm_final, final):
    B, S, D = x1.shape
    N = B * S
    n_tiles, _, tm = dest0.shape
    per_b = S // tm
    idx_blk = lambda fn: pl.BlockSpec((1, 1, tm), fn, memory_space=pltpu.SMEM)
    cur = lambda i: (i, 0, 0)
    nxt = lambda i: (jnp.minimum(i + 1, n_tiles - 1), 0, 0)
    out = pl.pallas_call(
        functools.partial(_combine_kernel, final=final),
        out_shape=jax.ShapeDtypeStruct((N, D), F32),
        grid=(n_tiles,),
        in_specs=[idx_blk(cur), idx_blk(cur), idx_blk(nxt), idx_blk(nxt),
                  pl.BlockSpec((tm, D), lambda i: (i, 0)),
                  pl.BlockSpec((tm, LANE), lambda i: (i, 0)),
                  pl.BlockSpec((1, 6, D), lambda i: (i // per_b, 0, 0)),
                  pl.BlockSpec((1, D), lambda i: (0, 0)),
                  pl.BlockSpec(memory_space=pl.ANY)],
        out_specs=pl.BlockSpec((tm, D), lambda i: (i, 0)),
        scratch_shapes=[pltpu.VMEM((2, TOP_K, tm * PACK_ROWS, LANE), U32), pltpu.SemaphoreType.DMA((2,))],
        compiler_params=_cparams(("arbitrary",)),
        name="moe_combine",
    )(dest0, dest1, dest0, dest1, x1.reshape(N, D), route.reshape(N, LANE), mod_l, norm_final.reshape(1, D), ys)
    return out.reshape(B, S, D)


def _layer(x, mod_l, p, big, l, final, norm_final):
    B, S, D = x.shape
    N = B * S
    s5_u, g_qkv, g_z, s_z, s_xbc, small = _proj_call(x, mod_l, p["norm_mix"], big["w_in"], l)
    bm, cm, lam_rows = _s5_params(p["s5_a_re"], p["s5_a_im"], p["s5_b_re"], p["s5_b_im"],
                                  p["s5_c_re"], p["s5_c_im"], p["s5_log_dt"])
    y_s5 = _s5_call(s5_u, bm, cm, lam_rows, p["s5_d"], p["s5_w_glu"], p["s5_norm"])
    y_gdn = _gdn_call(g_qkv, g_z, small, p["gdn_conv_w"], p["gdn_a_log"], p["gdn_dt_bias"], p["gdn_norm"])
    y_ssd = _ssd_call(s_xbc, s_z, small, p["ssd_conv_w"], p["ssd_conv_b"], p["ssd_a_log"], p["ssd_dt_bias"],
                      p["ssd_d"], p["ssd_norm"])
    w_router, b_router = _router_params(p["moe_w_grp"], p["moe_b_grp"], p["moe_w_rt"], p["moe_b_rt"])
    x1, h2, route, route_t, counts = _out_call(x, y_s5, y_gdn, y_ssd, mod_l, p["norm_ffn"], big["w_out"],
                                               w_router, b_router, l)
    n_used, blk_e, n_valid, dest0, dest1, n_rows = _dispatch(route_t, counts[0:N_EXPERTS, 0], N,
                                                             min(MOE_TOKEN_TILE, S))
    xs = _scatter_call(h2, dest0, dest1, n_rows)
    ys = _expert_call(xs, n_used, blk_e, n_valid, big["moe_w_gate"], big["moe_w_up"], big["moe_w_down"], l)
    return _combine_call(x1, ys, route, dest0, dest1, mod_l, norm_final, final)


def kernel(x, c, w_ada, b_ada, norm_mix, norm_ffn, w_in, w_out, s5_a_re, s5_a_im, s5_b_re, s5_b_im, s5_c_re, s5_c_im, s5_d, s5_log_dt, s5_w_glu, s5_norm, gdn_conv_w, gdn_a_log, gdn_dt_bias, gdn_norm, ssd_conv_w, ssd_conv_b, ssd_a_log, ssd_dt_bias, ssd_d, ssd_norm, moe_w_grp, moe_b_grp, moe_w_rt, moe_b_rt, moe_w_gate, moe_w_up, moe_w_down, norm_final):
    stacked = dict(norm_mix=norm_mix, norm_ffn=norm_ffn, s5_a_re=s5_a_re, s5_a_im=s5_a_im,
                   s5_b_re=s5_b_re, s5_b_im=s5_b_im, s5_c_re=s5_c_re, s5_c_im=s5_c_im, s5_d=s5_d,
                   s5_log_dt=s5_log_dt, s5_w_glu=s5_w_glu, s5_norm=s5_norm, gdn_conv_w=gdn_conv_w,
                   gdn_a_log=gdn_a_log, gdn_dt_bias=gdn_dt_bias, gdn_norm=gdn_norm, ssd_conv_w=ssd_conv_w,
                   ssd_conv_b=ssd_conv_b, ssd_a_log=ssd_a_log, ssd_dt_bias=ssd_dt_bias, ssd_d=ssd_d,
                   ssd_norm=ssd_norm, moe_w_grp=moe_w_grp, moe_b_grp=moe_b_grp, moe_w_rt=moe_w_rt,
                   moe_b_rt=moe_b_rt)
    big = dict(w_in=_w_in_prep_call(w_in), w_out=w_out, moe_w_gate=moe_w_gate, moe_w_up=moe_w_up,
               moe_w_down=moe_w_down)
    L = w_in.shape[0]
    B, S, D = x.shape
    mod = _mod_call(c, w_ada, b_ada).reshape(L, B, 6, D)
    for l in range(L):
        p = {k: v[l] for k, v in stacked.items()}
        x = _layer(x, mod[l], p, big, l, l == L - 1, norm_final)
    return x
```

```python
import functools

import numpy as np
import jax
import jax.numpy as jnp
from jax import lax
from jax.experimental import pallas as pl
from jax.experimental.pallas import tpu as pltpu

F32 = jnp.float32
BF16 = jnp.bfloat16

D_MODEL = 1024
DEPTH = 4
EPS = 1e-6
CONV_WIDTH = 4
CHUNK = 64
S5_WIDTH = 256
S5_CH = 16
S5_GROUPS = 16
S5_STATE = 64
S5_LANES = S5_GROUPS * S5_STATE
GDN_WIDTH = 384
GDN_HEAD_DIM = 64
GDN_HEADS = 6
GDN_CONV_DIM = 3 * GDN_WIDTH
SSD_WIDTH = 384
SSD_HEAD_DIM = 64
SSD_HEADS = 6
SSD_GROUPS = 2
SSD_STATE = 128
SSD_GROUP_WIDTH = SSD_WIDTH // SSD_GROUPS
SSD_CONV_DIM = SSD_WIDTH + 2 * SSD_GROUPS * SSD_STATE
PROJ_SIZES = (S5_WIDTH, GDN_CONV_DIM, GDN_WIDTH, GDN_HEADS, GDN_HEADS, SSD_WIDTH, SSD_CONV_DIM, SSD_HEADS)
MOE_GROUPS = 4
EXPERTS_PER_GROUP = 8
N_EXPERTS = 32
TOP_K = 2
D_EXPERT = 256

LANE = 128
SUBLANE = 8
SMALL_W = LANE
A_LANE, B_LANE, DT_LANE = 0, GDN_HEADS, 2 * GDN_HEADS
GRP_LANE = N_EXPERTS
ROUTER_ROWS = 40
MOE_ROWS = 512
MOE_TOKEN_TILE = 512
DEST_TILES_PER_STEP = 8
SSD_CHUNKS_PER_ITER = 8
GDN_PAIRS_PER_ITER = 4
VMEM_LIMIT = 56 * 1024 * 1024
NEG_BIG = -1e30


def _cparams(sem):
    return pltpu.CompilerParams(dimension_semantics=sem, vmem_limit_bytes=VMEM_LIMIT)


def _split(a):
    hi = a.astype(BF16)
    lo = (a - hi.astype(F32)).astype(BF16)
    return hi, lo


_NN = (((1,), (0,)), ((), ()))
_NT = (((1,), (1,)), ((), ()))
_TN = (((0,), (0,)), ((), ()))


def _dot(a, b, dims=_NN):
    return lax.dot_general(a.astype(BF16), b.astype(BF16), dims, preferred_element_type=F32)


def _dot2(a, b_bf16):
    hi, lo = _split(a)
    return (lax.dot_general(hi, b_bf16, _NN, preferred_element_type=F32)
            + lax.dot_general(lo, b_bf16, _NN, preferred_element_type=F32))


def _dot_sel3(a, b_bf16):
    h1 = a.astype(BF16)
    r1 = a - h1.astype(F32)
    h2 = r1.astype(BF16)
    h3 = (r1 - h2.astype(F32)).astype(BF16)
    d = functools.partial(lax.dot_general, dimension_numbers=_NN, preferred_element_type=F32)
    return d(h1, b_bf16) + (d(h2, b_bf16) + d(h3, b_bf16))


def _dot3(a, b):
    ah, al = _split(a)
    bh, bl = _split(b)
    d = functools.partial(lax.dot_general, dimension_numbers=_NN, preferred_element_type=F32)
    return d(ah, bh) + (d(ah, bl) + d(al, bh))


def _silu(x):
    return x * jax.nn.sigmoid(x)


def _softplus(x):
    return jnp.maximum(x, 0.0) + jnp.log(1.0 + jnp.exp(-jnp.abs(x)))


def _norm_mod(x, w, scale, shift):
    ms = jnp.mean(x * x, axis=-1, keepdims=True)
    return (x * lax.rsqrt(ms + EPS) * w) * (1.0 + scale) + shift


def _mod_kernel(c_ref, w_ref, b_ref, o_ref):
    cond = _silu(c_ref[...])
    o_ref[0] = _dot3(cond, w_ref[0]) + b_ref[0]


def _mod_call(c, w_ada, b_ada):
    L, D, W = w_ada.shape
    B = c.shape[0]
    tn = 1536
    return pl.pallas_call(
        _mod_kernel,
        out_shape=jax.ShapeDtypeStruct((L, B, W), F32),
        grid=(L, W // tn),
        in_specs=[pl.BlockSpec((B, D), lambda l, j: (0, 0)),
                  pl.BlockSpec((1, D, tn), lambda l, j: (l, 0, j)),
                  pl.BlockSpec((1, 1, tn), lambda l, j: (l, 0, j))],
        out_specs=pl.BlockSpec((1, B, tn), lambda l, j: (l, 0, j)),
        compiler_params=_cparams(("arbitrary", "arbitrary")),
        name="adaln_mod",
    )(c, w_ada, b_ada.reshape(L, 1, W))


PROJ_OUT_W = (S5_WIDTH, GDN_CONV_DIM, GDN_WIDTH, SSD_WIDTH, SSD_CONV_DIM, SMALL_W)


def _proj_kernel(x_ref, mod_ref, nw_ref, w_ref, *o_refs):
    x = x_ref[0]
    h = _norm_mod(x, nw_ref[...], mod_ref[0, 1:2, :], mod_ref[0, 0:1, :]).astype(BF16)
    proj = jnp.dot(h, w_ref[0], preferred_element_type=F32)
    off = 0
    for o_ref in o_refs:
        n = o_ref.shape[-1]
        o_ref[0] = proj[:, off:off + n]
        off += n


_W_HEAD = S5_WIDTH + GDN_CONV_DIM + GDN_WIDTH
_W_AB = _W_HEAD + 2 * GDN_HEADS
_W_SSD = _W_AB + SSD_WIDTH + SSD_CONV_DIM
_W_END = _W_SSD + SSD_HEADS
PROJ_W = sum(PROJ_OUT_W)
W_PREP_ROWS = 256


def _w_in_prep_kernel(w_ref, o_ref):
    n_ssd = _W_SSD - _W_AB
    for r in range(0, w_ref.shape[1], W_PREP_ROWS):
        w = w_ref[0, r:r + W_PREP_ROWS, :]
        rows = slice(r, r + W_PREP_ROWS)
        o_ref[0, rows, 0:_W_HEAD] = w[:, 0:_W_HEAD].astype(BF16)
        o_ref[0, rows, _W_HEAD:_W_HEAD + n_ssd] = w[:, _W_AB:_W_SSD].astype(BF16)
        small = jnp.concatenate([w[:, _W_HEAD:_W_AB], w[:, _W_SSD:_W_END],
                                 jnp.zeros((W_PREP_ROWS, SMALL_W - (_W_AB - _W_HEAD) - (_W_END - _W_SSD)), F32)], axis=1)
        o_ref[0, rows, _W_HEAD + n_ssd:PROJ_W] = small.astype(BF16)


def _w_in_prep_call(w_in):
    L, D, W = w_in.shape
    return pl.pallas_call(
        _w_in_prep_kernel,
        out_shape=jax.ShapeDtypeStruct((L, D, PROJ_W), BF16),
        grid=(L,),
        in_specs=[pl.BlockSpec((1, D, W), lambda l: (l, 0, 0))],
        out_specs=pl.BlockSpec((1, D, PROJ_W), lambda l: (l, 0, 0)),
        compiler_params=_cparams(("arbitrary",)),
        name="w_in_prep",
    )(w_in)


def _proj_call(x, mod_l, norm_w, w_arr, l):
    B, S, D = x.shape
    tm = min(512, S)
    out_shape = [jax.ShapeDtypeStruct((B, S, n), F32) for n in PROJ_OUT_W]
    out_specs = [pl.BlockSpec((1, tm, n), lambda b, i: (b, i, 0)) for n in PROJ_OUT_W]
    return pl.pallas_call(
        _proj_kernel,
        out_shape=out_shape,
        grid=(B, S // tm),
        in_specs=[pl.BlockSpec((1, tm, D), lambda b, i: (b, i, 0)),
                  pl.BlockSpec((1, 6, D), lambda b, i: (b, 0, 0)),
                  pl.BlockSpec((1, D), lambda b, i: (0, 0)),
                  pl.BlockSpec((1, D, PROJ_W), lambda b, i: (l, 0, 0))],
        out_specs=out_specs,
        compiler_params=_cparams(("arbitrary", "arbitrary")),
        name="norm_in_proj",
    )(x, mod_l, norm_w.reshape(1, D), w_arr)


def _s5_kernel(u_ref, bm_ref, cm_ref, lam_ref, dsk_ref, glu_ref, nw_ref, perm_ref, o_ref,
               x_s, st_s, ubt_s, u_s, *, T):
    B = u_ref.shape[0]
    P = S5_LANES

    @pl.when(pl.program_id(0) == 0)
    def _():
        st_s[...] = jnp.zeros_like(st_s)

    nw = S5_WIDTH // LANE
    ubt = u_ref[...].reshape(B * T, S5_WIDTH)
    for j in range(nw):
        ubt_s[j] = ubt[:, j * LANE:(j + 1) * LANE]

    def regroup(t, carry):
        for j in range(nw):
            u_s[pl.ds(pl.multiple_of(t * B, B), B), j * LANE:(j + 1) * LANE] = ubt_s[j, pl.ds(t, B, stride=T), :]
        return carry

    lax.fori_loop(0, T, regroup, 0, unroll=8)
    u = u_s[...]
    x_s[...] = jnp.dot(u.astype(BF16), bm_ref[...], preferred_element_type=F32)
    lr = jnp.broadcast_to(lam_ref[0:1, :], (B, P))
    li = jnp.broadcast_to(lam_ref[1:2, :], (B, P))

    def step(t, carry):
        sr, si = carry
        rows = pl.ds(pl.multiple_of(t * B, B), B)
        nr = lr * sr - li * si + x_s[rows, 0:P]
        ni = lr * si + li * sr + x_s[rows, P:2 * P]
        x_s[rows, 0:P] = nr
        x_s[rows, P:2 * P] = ni
        return nr, ni

    sr, si = lax.fori_loop(0, T, step, (st_s[:, 0:P], st_s[:, P:2 * P]), unroll=4)
    st_s[:, 0:P] = sr
    st_s[:, P:2 * P] = si

    y = jnp.dot(x_s[...].astype(BF16), cm_ref[...], preferred_element_type=F32) + dsk_ref[...] * u
    y = jax.nn.gelu(y)
    y = y * jax.nn.sigmoid(jnp.dot(y.astype(BF16), glu_ref[...], preferred_element_type=F32))
    ms = jnp.mean(y * y, axis=-1, keepdims=True)
    y = (y * lax.rsqrt(ms + EPS) * nw_ref[...]).astype(BF16)
    y = jnp.dot(perm_ref[...], y, preferred_element_type=F32)
    o_ref[...] = y.reshape(B, T, S5_WIDTH).astype(o_ref.dtype)


def _s5_params(a_re, a_im, b_re, b_im, c_re, c_im, log_dt):
    G, P, CH = S5_GROUPS, S5_STATE, S5_CH
    lam = lax.complex(a_re.astype(F32), a_im.astype(F32))
    step = jnp.exp(log_dt.astype(F32))[:, None]
    lam_bar = jnp.exp(lam * step)
    b_bar = ((lam_bar - 1.0) / lam)[..., None] * lax.complex(b_re.astype(F32), b_im.astype(F32))
    eye = jnp.eye(G, dtype=F32)
    bre = jnp.einsum('gpc,gh->gchp', b_bar.real, eye).reshape(G * CH, G * P)
    bim = jnp.einsum('gpc,gh->gchp', b_bar.imag, eye).reshape(G * CH, G * P)
    bm = jnp.concatenate([bre, bim], axis=1).astype(BF16)
    cre = jnp.einsum('gcp,gh->gphc', c_re.astype(F32), eye).reshape(G * P, G * CH)
    cim = jnp.einsum('gcp,gh->gphc', c_im.astype(F32), eye).reshape(G * P, G * CH)
    cm = jnp.concatenate([cre, -cim], axis=0).astype(BF16)
    lam_rows = jnp.zeros((SUBLANE, G * P), F32)
    lam_rows = lam_rows.at[0].set(lam_bar.real.reshape(-1)).at[1].set(lam_bar.imag.reshape(-1))
    return bm, cm, lam_rows


def _s5_call(u, bm, cm, lam_rows, d_skip, w_glu, norm_w):
    B, S, W = u.shape
    T = min(128, S)
    P2 = 2 * S5_LANES
    const = lambda shape: pl.BlockSpec(shape, lambda i: tuple(0 for _ in shape))
    r = np.arange(B * T)
    perm = np.zeros((B * T, B * T), np.float32)
    perm[r, (r % T) * B + r // T] = 1.0
    return pl.pallas_call(
        functools.partial(_s5_kernel, T=T),
        out_shape=jax.ShapeDtypeStruct((B, S, W), BF16),
        grid=(S // T,),
        in_specs=[pl.BlockSpec((B, T, W), lambda i: (0, i, 0)),
                  const((W, P2)), const((P2, W)), const((SUBLANE, S5_LANES)),
                  const((1, W)), const((W, W)), const((1, W)), const((B * T, B * T))],
        out_specs=pl.BlockSpec((B, T, W), lambda i: (0, i, 0)),
        scratch_shapes=[pltpu.VMEM((B * T, P2), F32), pltpu.VMEM((B, P2), F32),
                        pltpu.VMEM((W // LANE, B * T, LANE), F32), pltpu.VMEM((B * T, W), F32)],
        compiler_params=_cparams(("arbitrary",)),
        name="s5_mixer",
    )(u, bm, cm, lam_rows, d_skip.reshape(1, W).astype(F32), w_glu.astype(BF16), norm_w.reshape(1, W).astype(F32),
      jnp.asarray(perm, BF16))


def _causal_conv_silu(x, xf_ref, cw_ref, bias):
    n = x.shape[0]
    xf_ref[SUBLANE:, :] = x
    acc = x * cw_ref[CONV_WIDTH - 1:CONV_WIDTH, :]
    for k in range(1, CONV_WIDTH):
        acc = acc + xf_ref[SUBLANE - k:SUBLANE - k + n, :] * cw_ref[CONV_WIDTH - 1 - k:CONV_WIDTH - k, :]
    xf_ref[0:SUBLANE, :] = x[n - SUBLANE:, :]
    if bias is not None:
        acc = acc + bias
    return _silu(acc)


U32 = jnp.uint32
PACK_ROWS = D_MODEL // (2 * LANE)


def _pack_rows(x):
    h = x.shape[1] // 2
    xb = x.astype(BF16).astype(F32)
    lo = lax.bitcast_convert_type(xb[:, 0:h], U32) >> 16
    hi = lax.bitcast_convert_type(xb[:, h:2 * h], U32) & U32(0xFFFF0000)
    return lo | hi


def _unpack_rows(w):
    lo = lax.bitcast_convert_type(w << 16, F32)
    hi = lax.bitcast_convert_type(w & U32(0xFFFF0000), F32)
    return jnp.concatenate([lo, hi], axis=1)


def _store_row_tiles(ref, val):
    n = val.shape[0]
    for j in range(PACK_ROWS):
        ref[pl.ds(j, n, stride=PACK_ROWS), :] = val[:, j * LANE:(j + 1) * LANE]


def _load_row_tiles(ref, n):
    return jnp.concatenate([ref[pl.ds(j, n, stride=PACK_ROWS), :] for j in range(PACK_ROWS)], axis=1)


def _lanes_from(sm, off):
    return pltpu.roll(sm, SMALL_W - off, axis=1) if off else sm


def _chunk_scans(g):
    n = g.shape[0]
    rin = lax.broadcasted_iota(jnp.int32, g.shape, 0) & (CHUNK - 1)
    pre = g
    suf = jnp.where(rin < CHUNK - 1, pltpu.roll(g, n - 1, axis=0), 0.0)
    s = 1
    while s < CHUNK:
        pre = pre + jnp.where(rin >= s, pltpu.roll(pre, s, axis=0), 0.0)
        suf = suf + jnp.where(rin + s <= CHUNK - 1, pltpu.roll(suf, n - s, axis=0), 0.0)
        s *= 2
    return pre, suf


def _head_expand():
    m = np.zeros((SMALL_W, GDN_WIDTH), np.float32)
    for h in range(GDN_HEADS):
        m[h, h * GDN_HEAD_DIM:(h + 1) * GDN_HEAD_DIM] = 1.0
    return jnp.asarray(m, BF16)


def _block_ones(width, blk):
    idx = np.arange(width) // blk
    return jnp.asarray((idx[:, None] == idx[None, :]).astype(np.float32), BF16)


def _gdn_kernel(qkv_ref, z_ref, sm_ref, cw_ref, hp_ref, nw_ref, e_ref, xp_ref, bdm_ref, o_ref,
                tail_s, st_s, kn_s, kb_s, qn_s, qd_s, kd_s, u_s, w_s, eg_s, gcx_s, gw_s,
                o_s, au_s, qe_s, ku_s, kw_s, *, TB):
    H, Dh, C = GDN_HEADS, GDN_HEAD_DIM, CHUNK
    W = GDN_WIDTH
    ncb = TB // C

    @pl.when(pl.program_id(1) == 0)
    def _():
        tail_s[0:SUBLANE, :] = jnp.zeros((SUBLANE, tail_s.shape[1]), F32)
        st_s[...] = jnp.zeros_like(st_s)

    xc = _causal_conv_silu(qkv_ref[0], tail_s, cw_ref, None)
    q, k, v = xc[:, 0:W], xc[:, W:2 * W], xc[:, 2 * W:3 * W]
    e = e_ref[...]
    xp = xp_ref[...]
    qn = q * lax.rsqrt(_dot2(q * q, e) + EPS) * (Dh ** -0.5)
    kn = k * lax.rsqrt(_dot2(k * k, e) + EPS)

    sm = sm_ref[0]
    lane = lax.broadcasted_iota(jnp.int32, sm.shape, 1)
    head_lane = lane < H
    g = jnp.where(head_lane, hp_ref[0:1, :] * _softplus(_lanes_from(sm, A_LANE) + hp_ref[1:2, :]), 0.0)
    beta = jnp.where(head_lane, jax.nn.sigmoid(_lanes_from(sm, B_LANE)), 0.0)
    gc, rc = _chunk_scans(g)
    bx = _dot2(beta, xp)
    egx = _dot2(jnp.exp(gc), xp)
    erx = _dot2(jnp.exp(rc), xp)
    kb = kn * bx
    kn_s[...] = kn
    kb_s[...] = kb
    qn_s[...] = qn
    qd_s[...] = qn * egx
    kd_s[...] = kn * erx
    u_s[...] = v * bx
    w_s[...] = kb * egx
    gcx = _dot_sel3(gc, xp)
    gcx_s[...] = gcx
    r384 = lax.broadcasted_iota(jnp.int32, (C, W), 0)
    l384 = lax.broadcasted_iota(jnp.int32, (C, W), 1) & (Dh - 1)
    for c in range(ncb):
        diag_c = jnp.where(r384 == l384, gcx[c * C:(c + 1) * C, :], 0.0)
        gw_s[c] = jnp.broadcast_to(jnp.sum(diag_c, axis=0, keepdims=True), (SUBLANE, W))
        eg_s[c] = jnp.broadcast_to(egx[(c + 1) * C - 1:(c + 1) * C, :], (SUBLANE, W))

    heads = [slice(h * Dh, (h + 1) * Dh) for h in range(H)]

    GL = 4 * Dh
    row_w = lax.broadcasted_iota(jnp.int32, (C, GL), 0)
    col_w = lax.broadcasted_iota(jnp.int32, (C, GL), 1) & (Dh - 1)
    causal_w, strict_w = row_w >= col_w, row_w > col_w
    eye_w = jnp.where(row_w == col_w, 1.0, 0.0).astype(F32)
    bdm = bdm_ref[...]
    mm = functools.partial(lax.dot_general, dimension_numbers=_NN, preferred_element_type=F32)

    def block_diag(m, mask):
        return jnp.concatenate([m] * 4, axis=0) * mask

    npair = min(GDN_PAIRS_PER_ITER, ncb // 2)

    def solve(it, carry):
        chunks = [2 * npair * it + k for k in range(2 * npair)]
        rows = [pl.ds(pl.multiple_of(c * C, C), C) for c in chunks]

        def groups(get, n):
            out = []
            for j in range(npair):
                a0, a1 = get(2 * j), get(2 * j + 1)
                out += [a0[:, 0:4 * n], a1[:, 0:4 * n],
                        jnp.concatenate([a0[:, 4 * n:6 * n], a1[:, 4 * n:6 * n]], axis=1)]
            return out

        def ungroup(ref, vals, n):
            for j in range(npair):
                ref[rows[2 * j], 0:4 * n] = vals[3 * j]
                ref[rows[2 * j + 1], 0:4 * n] = vals[3 * j + 1]
                ref[rows[2 * j], 4 * n:6 * n] = vals[3 * j + 2][:, 0:2 * n]
                ref[rows[2 * j + 1], 4 * n:6 * n] = vals[3 * j + 2][:, 2 * n:4 * n]

        kn_g = groups(lambda k: kn_s[rows[k], :], Dh)
        kb_g = groups(lambda k: kb_s[rows[k], :], Dh)
        qn_g = groups(lambda k: qn_s[rows[k], :], Dh)
        gx_g = groups(lambda k: gcx_s[rows[k], :], Dh)
        gw_g = groups(lambda k: gw_s[chunks[k], 0:1, :], Dh)
        decs = [jnp.exp(jnp.where(causal_w, gx - gw, NEG_BIG)) for gx, gw in zip(gx_g, gw_g)]
        prods = [lax.dot_general(jnp.concatenate([kb, qn], axis=0).astype(BF16), block_diag(kn.astype(BF16), bdm),
                                 _NT, preferred_element_type=F32)
                 for kn, kb, qn in zip(kn_g, kb_g, qn_g)]
        attns = [pr[C:2 * C] * dec for pr, dec in zip(prods, decs)]
        def times(lhs, p=None):
            lh, ll = _split(lhs)
            ph, pl_ = (lh[0:C], ll[0:C]) if p is None else _split(p)
            rh, rl = block_diag(ph, bdm), block_diag(pl_, bdm)
            return mm(lh, rh) + (mm(lh, rl) + mm(ll, rh))

        ps = [-jnp.where(strict_w, pr[0:C] * dec, 0.0) for pr, dec in zip(prods, decs)]
        invs = [eye_w + p for p in ps]
        ps = [times(p) for p in ps]
        for _ in range(4):
            outs = [times(jnp.concatenate([p, inv], axis=0)) for p, inv in zip(ps, invs)]
            invs = [inv + o[C:2 * C] for inv, o in zip(invs, outs)]
            ps = [o[0:C] for o in outs]
        invs = [inv + times(inv, p) for p, inv in zip(ps, invs)]
        def solved(ref):
            outs = []
            for inv, r in zip(invs, groups(lambda k: ref[rows[k], :], Dh)):
                ih, il = _split(inv)
                rh, rl = _split(r)
                rh, rl = block_diag(rh, bdm), block_diag(rl, bdm)
                outs.append(mm(ih, rh) + (mm(ih, rl) + mm(il, rh)))
            return outs

        u_g, w_g = solved(u_s), solved(w_s)
        ungroup(u_s, u_g, Dh)
        ungroup(w_s, w_g, Dh)
        at_g = [a.astype(BF16) for a in attns]
        au_g = [mm(a, block_diag(u.astype(BF16), bdm)) for a, u in zip(at_g, u_g)]
        aw_g = [mm(a, block_diag(w.astype(BF16), bdm)) for a, w in zip(at_g, w_g)]
        qd_g = groups(lambda k: qd_s[rows[k], :], Dh)
        ungroup(au_s, au_g, Dh)
        ungroup(qe_s, [qd - aw for qd, aw in zip(qd_g, aw_g)], Dh)
        for k in range(2 * npair):
            for sl in heads:
                kd_h = kd_s[rows[k], sl]
                ku_s[rows[k], sl] = _dot(kd_h, u_s[rows[k], sl], _TN)
                kw_s[rows[k], sl] = _dot(kd_h, w_s[rows[k], sl], _TN)
        return carry

    lax.fori_loop(0, ncb // (2 * npair), solve, 0)

    def recur(c, carry):
        rows = pl.ds(pl.multiple_of(c * C, C), C)
        st = st_s[...]
        sb = st.astype(BF16)
        lhs = jnp.concatenate([qe_s[rows, :], kw_s[rows, :]], axis=0).astype(BF16)
        out = jnp.concatenate(
            [mm(lhs[:, 0:GL], block_diag(sb[:, 0:GL], bdm)),
             mm(lhs[:, GL:W], jnp.concatenate([sb[:, GL:W]] * 2, axis=0) * bdm[0:W - GL, 0:W - GL])], axis=1)
        o_s[rows, :] = out[0:C] + au_s[rows, :]
        st_s[...] = st * eg_s[c, 0:1, :] + (ku_s[rows, :] - out[C:2 * C])
        return carry

    lax.fori_loop(0, ncb, recur, 0)

    o = o_s[...]
    ms = _dot2(o * o, e) * (1.0 / Dh)
    o = o * lax.rsqrt(ms + EPS) * nw_ref[...]
    o_ref[0] = (o * _silu(z_ref[0])).astype(o_ref.dtype)


def _gdn_call(qkv, z, sm, conv_w, a_log, dt_bias, norm_w):
    B, S, _ = qkv.shape
    TB = min(512, S)
    W, H, Dh = GDN_WIDTH, GDN_HEADS, GDN_HEAD_DIM
    hp = jnp.zeros((SUBLANE, SMALL_W), F32)
    hp = hp.at[0, :H].set(-jnp.exp(a_log.astype(F32))).at[1, :H].set(dt_bias.astype(F32))
    nw = jnp.tile(norm_w.astype(F32), H).reshape(1, W)
    const = lambda shape: pl.BlockSpec(shape, lambda b, i: tuple(0 for _ in shape))
    blk = lambda n: pl.BlockSpec((1, TB, n), lambda b, i: (b, i, 0))
    f = lambda *shape: pltpu.VMEM(shape, F32)
    return pl.pallas_call(
        functools.partial(_gdn_kernel, TB=TB),
        out_shape=jax.ShapeDtypeStruct((B, S, W), BF16),
        grid=(B, S // TB),
        in_specs=[blk(GDN_CONV_DIM), blk(W), blk(SMALL_W),
                  const((CONV_WIDTH, GDN_CONV_DIM)), const((SUBLANE, SMALL_W)), const((1, W)),
                  const((W, W)), const((SMALL_W, W)), const((4 * Dh, 4 * Dh))],
        out_specs=blk(W),
        scratch_shapes=[f(SUBLANE + TB, GDN_CONV_DIM), f(Dh, W)]
        + [f(TB, W)] * 7 + [f(TB // CHUNK, SUBLANE, W), f(TB, W), f(TB // CHUNK, SUBLANE, W)] + [f(TB, W)] * 5,
        compiler_params=_cparams(("arbitrary", "arbitrary")),
        name="gdn_mixer",
    )(qkv, z, sm, conv_w.astype(F32), hp, nw, _block_ones(W, Dh), _head_expand(), _block_ones(4 * Dh, Dh))


def _ssd_kernel(xbc_ref, z_ref, sm_ref, cw_ref, cb_ref, hp_ref, dsk_ref, nw_ref, e_ref, xp_ref,
                hgm_ref, e_hd_ref, ghm_ref, o_ref,
                tail_s, st_s, xs_s, xdt_s, xdd_s, bm_s, cm_s, ea_s, el_s, ac_s, aw_s, y_s, inc_s, *, TB):
    H, P, G, N, C = SSD_HEADS, SSD_HEAD_DIM, SSD_GROUPS, SSD_STATE, CHUNK
    W, GW = SSD_WIDTH, SSD_GROUP_WIDTH
    ncb = TB // C

    @pl.when(pl.program_id(1) == 0)
    def _():
        tail_s[0:SUBLANE, :] = jnp.zeros((SUBLANE, tail_s.shape[1]), F32)
        st_s[...] = jnp.zeros_like(st_s)

    xc = _causal_conv_silu(xbc_ref[0], tail_s, cw_ref, cb_ref[...])
    xs = xc[:, 0:W]
    xp = xp_ref[...]

    sm = sm_ref[0]
    lane = lax.broadcasted_iota(jnp.int32, sm.shape, 1)
    head_lane = lane < H
    dt = jnp.where(head_lane, _softplus(_lanes_from(sm, DT_LANE) + hp_ref[1:2, :]), 0.0)
    acs, rcs = _chunk_scans(dt * hp_ref[0:1, :])
    dtx = _dot2(dt, xp)
    eax = _dot2(jnp.exp(acs), xp)
    erx = _dot2(jnp.exp(rcs), xp)
    xdt = xs * dtx
    xs_s[...] = xs
    xdt_s[...] = xdt
    xdd_s[...] = xdt * erx
    bm_s[...] = xc[:, W:W + G * N]
    cm_s[...] = xc[:, W + G * N:W + 2 * G * N]
    ea_s[...] = eax
    acx = _dot_sel3(acs, xp)
    ac_s[...] = acx
    r_w = lax.broadcasted_iota(jnp.int32, (C, W), 0)
    l_w = lax.broadcasted_iota(jnp.int32, (C, W), 1) & (P - 1)
    for c in range(ncb):
        diag_c = jnp.where(r_w == l_w, acx[c * C:(c + 1) * C, :], 0.0)
        aw_s[c] = jnp.broadcast_to(jnp.sum(diag_c, axis=0, keepdims=True), (SUBLANE, W))
        el_s[c] = jnp.broadcast_to(eax[(c + 1) * C - 1:(c + 1) * C, :], (SUBLANE, W))

    causal_w = r_w >= l_w
    mm = functools.partial(lax.dot_general, dimension_numbers=_NN, preferred_element_type=F32)

    def stacked(m, mask):
        return jnp.concatenate([m] * H, axis=0) * mask

    nloc = min(SSD_CHUNKS_PER_ITER, ncb)

    def local(it, carry):
        cs = [it * nloc + k for k in range(nloc)]
        rows = [pl.ds(pl.multiple_of(c * C, C), C) for c in cs]
        bs = [bm_s[r, :].astype(BF16) for r in rows]
        cbs = [lax.dot_general(cm_s[r, :].astype(BF16), stacked(b, hgm_ref[...]), _NT, preferred_element_type=F32)
               for r, b in zip(rows, bs)]
        segs = [jnp.exp(jnp.where(causal_w, ac_s[r, :] - aw_s[c, 0:1, :], NEG_BIG)) for r, c in zip(rows, cs)]
        for r, cb, seg in zip(rows, cbs, segs):
            y_s[r, :] = mm((cb * seg).astype(BF16), stacked(xdt_s[r, :].astype(BF16), e_hd_ref[...]))
        for r, c, b in zip(rows, cs, bs):
            inc_s[c] = lax.dot_general(b, xdd_s[r, :].astype(BF16), _TN, preferred_element_type=F32) * ghm_ref[...]
        return carry

    lax.fori_loop(0, ncb // nloc, local, 0)

    def recur(c, carry):
        rows = pl.ds(pl.multiple_of(c * C, C), C)
        st = st_s[...]
        y_s[rows, :] = y_s[rows, :] + mm(cm_s[rows, :].astype(BF16), st.astype(BF16)) * ea_s[rows, :]
        st_s[...] = st * el_s[c, 0:1, :] + inc_s[c]
        return carry

    lax.fori_loop(0, ncb, recur, 0, unroll=2)

    y = y_s[...] + dsk_ref[...] * xs_s[...]
    y = y * _silu(z_ref[0])
    ms = _dot2(y * y, e_ref[...]) * (1.0 / GW)
    o_ref[0] = (y * lax.rsqrt(ms + EPS) * nw_ref[...]).astype(o_ref.dtype)


def _ssd_call(xbc, z, sm, conv_w, conv_b, a_log, dt_bias, d_skip, norm_w):
    B, S, _ = xbc.shape
    TB = min(512, S)
    W, H, G, N = SSD_WIDTH, SSD_HEADS, SSD_GROUPS, SSD_STATE
    hp = jnp.zeros((SUBLANE, SMALL_W), F32)
    hp = hp.at[0, :H].set(-jnp.exp(a_log.astype(F32))).at[1, :H].set(dt_bias.astype(F32))
    dsk = jnp.repeat(d_skip.astype(F32), SSD_HEAD_DIM).reshape(1, W)
    const = lambda shape: pl.BlockSpec(shape, lambda b, i: tuple(0 for _ in shape))
    blk = lambda n: pl.BlockSpec((1, TB, n), lambda b, i: (b, i, 0))
    f = lambda *shape: pltpu.VMEM(shape, F32)
    head_of = np.arange(H * CHUNK) // CHUNK
    group_of = np.arange(G * N) // N
    hgm = (head_of[:, None] // (H // G) == group_of[None, :]).astype(np.float32)
    ghm = (group_of[:, None] == (np.arange(W) // SSD_HEAD_DIM // (H // G))[None, :]).astype(np.float32)
    return pl.pallas_call(
        functools.partial(_ssd_kernel, TB=TB),
        out_shape=jax.ShapeDtypeStruct((B, S, W), BF16),
        grid=(B, S // TB),
        in_specs=[blk(SSD_CONV_DIM), blk(W), blk(SMALL_W),
                  const((CONV_WIDTH, SSD_CONV_DIM)), const((1, SSD_CONV_DIM)), const((SUBLANE, SMALL_W)),
                  const((1, W)), const((1, W)), const((W, W)), const((SMALL_W, W)),
                  const((H * CHUNK, G * N)), const((W, W)), const((G * N, W))],
        out_specs=blk(W),
        scratch_shapes=[f(SUBLANE + TB, SSD_CONV_DIM), f(G * N, W),
                        f(TB, W), f(TB, W), f(TB, W), f(TB, G * N), f(TB, G * N), f(TB, W),
                        f(TB // CHUNK, SUBLANE, W),
                        f(TB, W), f(TB // CHUNK, SUBLANE, W), f(TB, W),
                        f(TB // CHUNK, G * N, W)],
        compiler_params=_cparams(("arbitrary", "arbitrary")),
        name="ssd_mixer",
    )(xbc, z, sm, conv_w.astype(F32), conv_b.reshape(1, -1).astype(F32), hp, dsk,
      norm_w.reshape(1, W).astype(F32), _block_ones(W, SSD_GROUP_WIDTH), _head_expand(),
      jnp.asarray(hgm, BF16), _block_ones(W, SSD_HEAD_DIM), jnp.asarray(ghm, F32))


def _out_kernel(x_ref, y1_ref, y2_ref, y3_ref, mod_ref, nw_ref, wo_ref, wr_ref, br_ref, tri_ref,
                x1_ref, h_ref, rt_ref, rtt_ref, cnt_ref, run_s, wo_s):
    first = (pl.program_id(0) == 0) & (pl.program_id(1) == 0)

    @pl.when(first)
    def _():
        wo_s[...] = wo_ref[0].astype(BF16)

    y = (jnp.dot(y1_ref[0], wo_s[0:S5_WIDTH, :], preferred_element_type=F32)
         + jnp.dot(y2_ref[0], wo_s[S5_WIDTH:S5_WIDTH + GDN_WIDTH, :], preferred_element_type=F32)
         + jnp.dot(y3_ref[0], wo_s[S5_WIDTH + GDN_WIDTH:, :], preferred_element_type=F32))
    x1 = x_ref[0] + mod_ref[0, 2:3, :] * y
    x1_ref[0] = x1
    h = _norm_mod(x1, nw_ref[...], mod_ref[0, 4:5, :], mod_ref[0, 3:4, :])
    _store_row_tiles(h_ref, _pack_rows(h))
    lg = lax.dot_general(wr_ref[...], h.astype(BF16), _NT, preferred_element_type=F32) + br_ref[...]
    tm = lg.shape[1]
    row = lax.broadcasted_iota(jnp.int32, lg.shape, 0)
    rowf = row.astype(F32)
    big = float(4 * LANE)
    grp = (row >= GRP_LANE) & (row < GRP_LANE + MOE_GROUPS)
    lgm = jnp.where(grp, lg, -jnp.inf)
    m = jnp.max(lgm, axis=0, keepdims=True)
    gidx = jnp.min(jnp.where(lgm == m, rowf - GRP_LANE, big), axis=0, keepdims=True)
    g_w = 1.0 / jnp.sum(jnp.where(grp, jnp.exp(lg - m), 0.0), axis=0, keepdims=True)
    in_grp = (row < N_EXPERTS) & ((row // EXPERTS_PER_GROUP).astype(F32) == gidx)
    le = jnp.where(in_grp, lg, -jnp.inf)
    v1 = jnp.max(le, axis=0, keepdims=True)
    i1 = jnp.min(jnp.where(le == v1, rowf, big), axis=0, keepdims=True)
    le2 = jnp.where(rowf == i1, -jnp.inf, le)
    v2 = jnp.max(le2, axis=0, keepdims=True)
    i2 = jnp.min(jnp.where(le2 == v2, rowf, big), axis=0, keepdims=True)
    e2 = jnp.exp(v2 - v1)
    w1 = g_w / (1.0 + e2)
    w2 = g_w * e2 / (1.0 + e2)

    @pl.when(first)
    def _():
        run_s[...] = jnp.zeros_like(run_s)

    chosen = jnp.where((rowf == i1) | (rowf == i2), 1.0, 0.0)
    before = jnp.dot(chosen.astype(BF16), tri_ref[...], preferred_element_type=F32) + run_s[:, 0:1]
    p1 = jnp.sum(jnp.where(rowf == i1, before, 0.0), axis=0, keepdims=True)
    p2 = jnp.sum(jnp.where(rowf == i2, before, 0.0), axis=0, keepdims=True)
    run_s[...] = run_s[...] + jnp.sum(chosen, axis=1, keepdims=True)
    cnt_ref[...] = run_s[...]

    vals = jnp.concatenate([i1, i2, w1, w2, p1, p2, jnp.zeros((LANE - 6, tm), F32)], axis=0)
    rtt_ref[0] = vals[0:SUBLANE, :]
    rt_ref[0] = vals.T


def _out_call(x, y1, y2, y3, mod_l, norm_w, w_out, w_router, b_router, l):
    B, S, D = x.shape
    tm = min(512, S)
    blk = lambda n: pl.BlockSpec((1, tm, n), lambda b, i: (b, i, 0))
    const = lambda shape: pl.BlockSpec(shape, lambda b, i: tuple(0 for _ in shape))
    tri = jnp.asarray(np.triu(np.ones((tm, tm), np.float32), 1), BF16)
    nb = S // tm
    tiles = PACK_ROWS
    return pl.pallas_call(
        _out_kernel,
        out_shape=[jax.ShapeDtypeStruct((B, S, D), F32), jax.ShapeDtypeStruct((B * S * tiles, LANE), U32),
                   jax.ShapeDtypeStruct((B, S, LANE), F32), jax.ShapeDtypeStruct((B * nb, SUBLANE, tm), F32),
                   jax.ShapeDtypeStruct((ROUTER_ROWS, LANE), F32)],
        grid=(B, nb),
        in_specs=[blk(D), blk(S5_WIDTH), blk(GDN_WIDTH), blk(SSD_WIDTH),
                  pl.BlockSpec((1, 6, D), lambda b, i: (b, 0, 0)),
                  const((1, D)), pl.BlockSpec((1, D, D), lambda b, i: (l, 0, 0)),
                  const((ROUTER_ROWS, D)), const((ROUTER_ROWS, 1)), const((tm, tm))],
        out_specs=[blk(D), pl.BlockSpec((tm * tiles, LANE), lambda b, i: (b * nb + i, 0)),
                   blk(LANE), pl.BlockSpec((1, SUBLANE, tm), lambda b, i: (b * nb + i, 0, 0)),
                   const((ROUTER_ROWS, LANE))],
        scratch_shapes=[pltpu.VMEM((ROUTER_ROWS, LANE), F32), pltpu.VMEM((D, D), BF16)],
        compiler_params=_cparams(("arbitrary", "arbitrary")),
        name="out_proj_router",
    )(x, y1, y2, y3, mod_l, norm_w.reshape(1, D), w_out, w_router, b_router, tri)


def _router_params(w_grp, b_grp, w_rt, b_rt):
    D = w_grp.shape[0]
    w = jnp.zeros((ROUTER_ROWS, D), F32).at[0:N_EXPERTS].set(w_rt.T).at[GRP_LANE:GRP_LANE + MOE_GROUPS].set(w_grp.T)
    b = jnp.zeros((ROUTER_ROWS, 1), F32).at[0:N_EXPERTS, 0].set(b_rt).at[GRP_LANE:GRP_LANE + MOE_GROUPS, 0].set(b_grp)
    return w.astype(BF16), b


def _tile_copy_loop(n, fn):
    def body(t, carry):
        fn(t)
        return carry
    lax.fori_loop(0, n, body, 0, unroll=32)


def _scatter_kernel(d0_ref, d1_ref, h_ref, xs_hbm, stage, sem):
    i = pl.program_id(0)
    n = pl.num_programs(0)
    slot = lax.rem(i, 2)
    tm = d0_ref.shape[-1]

    def copy(s, t, dst):
        return pltpu.make_async_copy(stage.at[s, pl.ds(t * PACK_ROWS, PACK_ROWS)],
                                     xs_hbm.at[pl.ds(dst * PACK_ROWS, PACK_ROWS)], sem.at[s])

    def wait_slot(s):
        _tile_copy_loop(TOP_K * tm, lambda t: copy(s, 0, 0).wait())

    @pl.when(i >= 2)
    def _():
        wait_slot(slot)

    stage[slot] = h_ref[...]

    def start(t):
        copy(slot, t, d0_ref[0, 0, t]).start(priority=0)
        copy(slot, t, d1_ref[0, 0, t]).start(priority=1)
    _tile_copy_loop(tm, start)

    @pl.when(i == n - 1)
    def _():
        @pl.when(i >= 1)
        def _():
            wait_slot(1 - slot)
        wait_slot(slot)


def _scatter_call(h2t, dest0, dest1, n_rows):
    n_tiles, _, tm = dest0.shape
    idx_blk = pl.BlockSpec((1, 1, tm), lambda i: (i, 0, 0), memory_space=pltpu.SMEM)
    return pl.pallas_call(
        _scatter_kernel,
        out_shape=jax.ShapeDtypeStruct((n_rows * PACK_ROWS, LANE), U32),
        grid=(n_tiles,),
        in_specs=[idx_blk, idx_blk, pl.BlockSpec((tm * PACK_ROWS, LANE), lambda i: (i, 0))],
        out_specs=pl.BlockSpec(memory_space=pl.ANY),
        scratch_shapes=[pltpu.VMEM((2, tm * PACK_ROWS, LANE), U32), pltpu.SemaphoreType.DMA((2,))],
        compiler_params=_cparams(("arbitrary",)),
        name="moe_scatter",
    )(dest0, dest1, h2t)


def _expert_kernel(nused_ref, blke_ref, nvalid_ref, xs_ref, wg_ref, wu_ref, wd_ref, ys_ref, wg_s, wu_s, wd_s):
    i = pl.program_id(0)

    @pl.when(i < nused_ref[0])
    def _():
        @pl.when((i == 0) | (blke_ref[i] != blke_ref[jnp.maximum(i - 1, 0)]))
        def _():
            wg_s[...] = wg_ref[0, 0].astype(BF16)
            wu_s[...] = wu_ref[0, 0].astype(BF16)
            wd_s[...] = wd_ref[0, 0].astype(BF16)

        row = lax.broadcasted_iota(jnp.int32, (MOE_ROWS, 1), 0)
        words = jnp.where(row < nvalid_ref[i], _load_row_tiles(xs_ref, MOE_ROWS), U32(0))
        xb = _unpack_rows(words).astype(BF16)
        hid = _silu(jnp.dot(xb, wg_s[...], preferred_element_type=F32)) * jnp.dot(xb, wu_s[...], preferred_element_type=F32)
        _store_row_tiles(ys_ref, _pack_rows(jnp.dot(hid.astype(BF16), wd_s[...], preferred_element_type=F32)))

    @pl.when(i >= nused_ref[0])
    def _():
        ys_ref[...] = jnp.zeros_like(ys_ref)


def _expert_call(xs, n_used, blk_e, n_valid, w_gate, w_up, w_down, l):
    D = w_gate.shape[2]
    n_blk = blk_e.shape[0]
    rows_blk = pl.BlockSpec((MOE_ROWS * PACK_ROWS, LANE), lambda i, nu, be, nv: (i, 0))
    grid_spec = pltpu.PrefetchScalarGridSpec(
        num_scalar_prefetch=3,
        grid=(n_blk,),
        in_specs=[rows_blk,
                  pl.BlockSpec((1, 1, D, D_EXPERT), lambda i, nu, be, nv: (l, be[i], 0, 0)),
                  pl.BlockSpec((1, 1, D, D_EXPERT), lambda i, nu, be, nv: (l, be[i], 0, 0)),
                  pl.BlockSpec((1, 1, D_EXPERT, D), lambda i, nu, be, nv: (l, be[i], 0, 0))],
        out_specs=rows_blk,
        scratch_shapes=[pltpu.VMEM((D, D_EXPERT), BF16), pltpu.VMEM((D, D_EXPERT), BF16),
                        pltpu.VMEM((D_EXPERT, D), BF16)],
    )
    return pl.pallas_call(
        _expert_kernel,
        out_shape=jax.ShapeDtypeStruct(xs.shape, U32),
        grid_spec=grid_spec,
        compiler_params=_cparams(("arbitrary",)),
        name="expert_mlp",
    )(n_used, blk_e, n_valid, xs, w_gate, w_up, w_down)


def _dest_kernel(rt_ref, ps_ref, o_ref):
    n_sub, _, tm = o_ref.shape
    expert = lax.broadcasted_iota(jnp.int32, (N_EXPERTS, tm), 0).astype(F32)
    starts = ps_ref[0:N_EXPERTS, :]
    for j in range(n_sub):
        rows = []
        for k in range(TOP_K):
            start = jnp.sum(jnp.where(expert == rt_ref[j, k:k + 1, :], starts, 0.0), axis=0, keepdims=True)
            rows.append(start + rt_ref[j, 4 + k:5 + k, :])
        rows.append(jnp.zeros((SUBLANE - TOP_K, tm), F32))
        o_ref[j] = jnp.concatenate(rows, axis=0).astype(jnp.int32)


def _dest_call(route_t, pstart_col):
    n_tiles, _, tm = route_t.shape
    n_sub = min(DEST_TILES_PER_STEP, n_tiles)
    blk = pl.BlockSpec((n_sub, SUBLANE, tm), lambda i: (i, 0, 0))
    return pl.pallas_call(
        _dest_kernel,
        out_shape=jax.ShapeDtypeStruct((n_tiles, SUBLANE, tm), jnp.int32),
        grid=(n_tiles // n_sub,),
        in_specs=[blk, pl.BlockSpec((LANE, 1), lambda i: (0, 0))],
        out_specs=blk,
        compiler_params=_cparams(("arbitrary",)),
        name="moe_dest",
    )(route_t, pstart_col)


def _dispatch(route_t, counts, N, tm):
    L_pad = N * TOP_K + N_EXPERTS * MOE_ROWS
    n_blk = L_pad // MOE_ROWS
    counts = counts.astype(jnp.int32)
    padded = ((counts + MOE_ROWS - 1) // MOE_ROWS) * MOE_ROWS
    pend = jnp.cumsum(padded)
    pstart = pend - padded
    n_used = (pend[-1] // MOE_ROWS).astype(jnp.int32).reshape(1)
    blk_row0 = jnp.arange(n_blk, dtype=jnp.int32) * MOE_ROWS
    blk_e = jnp.minimum(jnp.sum((pend[None, :] <= blk_row0[:, None]).astype(jnp.int32), axis=1), N_EXPERTS - 1)
    n_valid = jnp.clip((pstart + counts)[blk_e] - blk_row0, 0, MOE_ROWS).astype(jnp.int32)
    pstart_col = jnp.zeros((LANE, 1), F32).at[0:N_EXPERTS, 0].set(pstart.astype(F32))
    dest = _dest_call(route_t, pstart_col)
    dest0 = dest[:, 0, :].reshape(N // tm, 1, tm)
    dest1 = dest[:, 1, :].reshape(N // tm, 1, tm)
    return n_used, blk_e.astype(jnp.int32), n_valid, dest0, dest1, L_pad


def _combine_kernel(d0_ref, d1_ref, d0n_ref, d1n_ref, x_ref, rt_ref, mod_ref, nf_ref, ys_hbm, o_ref,
                    buf, sem, *, final):
    i = pl.program_id(0)
    n = pl.num_programs(0)
    slot = lax.rem(i, 2)
    tm = x_ref.shape[0]

    def copy(s, k, t, src):
        return pltpu.make_async_copy(ys_hbm.at[pl.ds(src * PACK_ROWS, PACK_ROWS)],
                                     buf.at[s, k, pl.ds(t * PACK_ROWS, PACK_ROWS)], sem.at[s])

    def start_tile(s, a_ref, b_ref):
        def start(t):
            copy(s, 0, t, a_ref[0, 0, t]).start(priority=0)
            copy(s, 1, t, b_ref[0, 0, t]).start(priority=1)
        _tile_copy_loop(tm, start)

    @pl.when(i == 0)
    def _():
        start_tile(0, d0_ref, d1_ref)

    @pl.when(i + 1 < n)
    def _():
        start_tile(1 - slot, d0n_ref, d1n_ref)

    _tile_copy_loop(TOP_K * tm, lambda t: copy(slot, 0, 0, 0).wait())
    rt = rt_ref[...]
    y = (rt[:, 2:3] * _unpack_rows(_load_row_tiles(buf.at[slot, 0], tm))
         + rt[:, 3:4] * _unpack_rows(_load_row_tiles(buf.at[slot, 1], tm)))
    x2 = x_ref[...] + mod_ref[0, 5:6, :] * y
    if final:
        ms = jnp.mean(x2 * x2, axis=-1, keepdims=True)
        x2 = x2 * lax.rsqrt(ms + EPS) * nf_ref[...]
    o_ref[...] = x2


def _combine_call(x1, ys, route, dest0, dest1, mod_l, norm_final, final):
    B, S, D = x1.shape
    N = B * S
    n_tiles, _, tm = dest0.shape
    per_b = S // tm
    idx_blk = lambda fn: pl.BlockSpec((1, 1, tm), fn, memory_space=pltpu.SMEM)
    cur = lambda i: (i, 0, 0)
    nxt = lambda i: (jnp.minimum(i + 1, n_tiles - 1), 0, 0)
    out = pl.pallas_call(
        functools.partial(_combine_kernel, final=final),
        out_shape=jax.ShapeDtypeStruct((N, D), F32),
        grid=(n_tiles,),
        in_specs=[idx_blk(cur), idx_blk(cur), idx_blk(nxt), idx_blk(nxt),
                  pl.BlockSpec((tm, D), lambda i: (i, 0)),
                  pl.BlockSpec((tm, LANE), lambda i: (i, 0)),
                  pl.BlockSpec((1, 6, D), lambda i: (i // per_b, 0, 0)),
                  pl.BlockSpec((1, D), lambda i: (0, 0)),
                  pl.BlockSpec(memory_space=pl.ANY)],
        out_specs=pl.BlockSpec((tm, D), lambda i: (i, 0)),
        scratch_shapes=[pltpu.VMEM((2, TOP_K, tm * PACK_ROWS, LANE), U32), pltpu.SemaphoreType.DMA((2,))],
        compiler_params=_cparams(("arbitrary",)),
        name="moe_combine",
    )(dest0, dest1, dest0, dest1, x1.reshape(N, D), route.reshape(N, LANE), mod_l, norm_final.reshape(1, D), ys)
    return out.reshape(B, S, D)


def _layer(x, mod_l, p, big, l, final, norm_final):
    B, S, D = x.shape
    N = B * S
    s5_u, g_qkv, g_z, s_z, s_xbc, small = _proj_call(x, mod_l, p["norm_mix"], big["w_in"], l)
    bm, cm, lam_rows = _s5_params(p["s5_a_re"], p["s5_a_im"], p["s5_b_re"], p["s5_b_im"],
                                  p["s5_c_re"], p["s5_c_im"], p["s5_log_dt"])
    y_s5 = _s5_call(s5_u, bm, cm, lam_rows, p["s5_d"], p["s5_w_glu"], p["s5_norm"])
    y_gdn = _gdn_call(g_qkv, g_z, small, p["gdn_conv_w"], p["gdn_a_log"], p["gdn_dt_bias"], p["gdn_norm"])
    y_ssd = _ssd_call(s_xbc, s_z, small, p["ssd_conv_w"], p["ssd_conv_b"], p["ssd_a_log"], p["ssd_dt_bias"],
                      p["ssd_d"], p["ssd_norm"])
    w_router, b_router = _router_params(p["moe_w_grp"], p["moe_b_grp"], p["moe_w_rt"], p["moe_b_rt"])
    x1, h2, route, route_t, counts = _out_call(x, y_s5, y_gdn, y_ssd, mod_l, p["norm_ffn"], big["w_out"],
                                               w_router, b_router, l)
    n_used, blk_e, n_valid, dest0, dest1, n_rows = _dispatch(route_t, counts[0:N_EXPERTS, 0], N,
                                                             min(MOE_TOKEN_TILE, S))
    xs = _scatter_call(h2, dest0, dest1, n_rows)
    ys = _expert_call(xs, n_used, blk_e, n_valid, big["moe_w_gate"], big["moe_w_up"], big["moe_w_down"], l)
    return _combine_call(x1, ys, route, dest0, dest1, mod_l, norm_final, final)


def kernel(x, c, w_ada, b_ada, norm_mix, norm_ffn, w_in, w_out, s5_a_re, s5_a_im, s5_b_re, s5_b_im, s5_c_re, s5_c_im, s5_d, s5_log_dt, s5_w_glu, s5_norm, gdn_conv_w, gdn_a_log, gdn_dt_bias, gdn_norm, ssd_conv_w, ssd_conv_b, ssd_a_log, ssd_dt_bias, ssd_d, ssd_norm, moe_w_grp, moe_b_grp, moe_w_rt, moe_b_rt, moe_w_gate, moe_w_up, moe_w_down, norm_final):
    stacked = dict(norm_mix=norm_mix, norm_ffn=norm_ffn, s5_a_re=s5_a_re, s5_a_im=s5_a_im,
                   s5_b_re=s5_b_re, s5_b_im=s5_b_im, s5_c_re=s5_c_re, s5_c_im=s5_c_im, s5_d=s5_d,
                   s5_log_dt=s5_log_dt, s5_w_glu=s5_w_glu, s5_norm=s5_norm, gdn_conv_w=gdn_conv_w,
                   gdn_a_log=gdn_a_log, gdn_dt_bias=gdn_dt_bias, gdn_norm=gdn_norm, ssd_conv_w=ssd_conv_w,
                   ssd_conv_b=ssd_conv_b, ssd_a_log=ssd_a_log, ssd_dt_bias=ssd_dt_bias, ssd_d=ssd_d,
                   ssd_norm=ssd_norm, moe_w_grp=moe_w_grp, moe_b_grp=moe_b_grp, moe_w_rt=moe_w_rt,
                   moe_b_rt=moe_b_rt)
    big = dict(w_in=_w_in_prep_call(w_in), w_out=w_out, moe_w_gate=moe_w_gate, moe_w_up=moe_w_up,
               moe_w_down=moe_w_down)
    L = w_in.shape[0]
    B, S, D = x.shape
    mod = _mod_call(c, w_ada, b_ada).reshape(L, B, 6, D)
    for l in range(L):
        p = {k: v[l] for k, v in stacked.items()}
        x = _layer(x, mod[l], p, big, l, l == L - 1, norm_final)
    return x
```
